```python
import jax, jax.numpy as jnp
from jax import lax
import numpy as np

D_MODEL = 2048
BATCH = 8
SEQ = 4096
DEPTH = 4

D_MIX = D_MODEL
HEAD_DIM = 128
D_ATTN = D_MIX // 2
D_GMLP = D_MIX - D_ATTN
N_HEADS = D_ATTN // HEAD_DIM
GMLP_GROUP = 128
N_GMLP_GROUPS = D_GMLP // GMLP_GROUP
CHUNK = 128
DILATED_PATTERNS = ((128, 1), (512, 4), (2048, 16))
BLK = 64
ROPE_THETA = 10000.0
D_FF = -(-(8 * D_MODEL) // (3 * 256)) * 256
D_IN = 3 * D_ATTN + 2 * D_GMLP
EPS = 1e-6
NEG = -1e30

kernel_name = "hybrid_dilated_attn_gmlp_encoder"


def rmsnorm(x, g):
    xf = x.astype(jnp.float32)
    y = xf * lax.rsqrt(jnp.mean(xf * xf, axis=-1, keepdims=True) + EPS)
    return (y * g.astype(jnp.float32)).astype(x.dtype)


def layernorm(x, g):
    xf = x.astype(jnp.float32)
    mu = jnp.mean(xf, axis=-1, keepdims=True)
    xc = xf - mu
    y = xc * lax.rsqrt(jnp.mean(xc * xc, axis=-1, keepdims=True) + EPS)
    return (y * g.astype(jnp.float32)).astype(x.dtype)


def rope_tables(seq):
    pos = jnp.arange(seq, dtype=jnp.float32)
    inv = ROPE_THETA ** (-jnp.arange(0, HEAD_DIM, 2, dtype=jnp.float32) / HEAD_DIM)
    ang = pos[:, None] * inv[None, :]
    return jnp.cos(ang), jnp.sin(ang)


def apply_rope(t, cos, sin):
    tf = t.astype(jnp.float32)
    t1, t2 = jnp.split(tf, 2, axis=-1)
    c = cos[None, :, None, :]
    s = sin[None, :, None, :]
    out = jnp.concatenate([t1 * c - t2 * s, t1 * s + t2 * c], axis=-1)
    return out.astype(t.dtype)


def _band_blocks(t, nb):
    return jnp.concatenate([t[:, :, 0:nb], t[:, :, 1:nb + 1], t[:, :, 2:nb + 2]], axis=3)


def dilated_window_branch(q, k, v, window, dilation):
    B, S, H, Dh = q.shape
    n_side = window // (2 * dilation)
    span = dilation * BLK
    s_pad = -(-S // span) * span
    L = s_pad // dilation
    nb = L // BLK

    def to_sub(t):
        t = jnp.pad(t, ((0, 0), (0, s_pad - S), (0, 0), (0, 0)))
        return t.reshape(B, L, dilation, H, Dh).transpose(0, 2, 1, 3, 4)

    qs, ks, vs = to_sub(q), to_sub(k), to_sub(v)
    valid = (jnp.arange(s_pad) < S).reshape(L, dilation).T

    qb = qs.reshape(B, dilation, nb, BLK, H, Dh)
    pad_kv = ((0, 0), (0, 0), (BLK, BLK), (0, 0), (0, 0))
    kb = _band_blocks(jnp.pad(ks, pad_kv).reshape(B, dilation, nb + 2, BLK, H, Dh), nb)
    vb = _band_blocks(jnp.pad(vs, pad_kv).reshape(B, dilation, nb + 2, BLK, H, Dh), nb)
    vk = jnp.pad(valid, ((0, 0), (BLK, BLK))).reshape(dilation, nb + 2, BLK)
    vk = jnp.concatenate([vk[:, 0:nb], vk[:, 1:nb + 1], vk[:, 2:nb + 2]], axis=2)

    scale = HEAD_DIM ** -0.5
    s = jnp.einsum('brnqhd,brnkhd->brnhqk', qb, kb,
                   preferred_element_type=jnp.float32) * scale
    rel = jnp.arange(3 * BLK)[None, :] - BLK - jnp.arange(BLK)[:, None]
    band = jnp.abs(rel) <= n_side
    mask = band[None, None, None, None] & vk[None, :, :, None, None, :]
    s = jnp.where(mask, s, NEG)
    m = jnp.max(s, axis=-1, keepdims=True)
    p = jnp.exp(s - m)
    denom = jnp.sum(p, axis=-1)
    o = jnp.einsum('brnhqk,brnkhd->brnqhd', p, vb.astype(jnp.float32))
    o = o / jnp.moveaxis(denom, -1, -2)[..., None]
    lse = jnp.moveaxis(m[..., 0] + jnp.log(denom), -1, -2)

    o = o.reshape(B, dilation, L, H, Dh).transpose(0, 2, 1, 3, 4).reshape(B, s_pad, H, Dh)[:, :S]
    lse = lse.reshape(B, dilation, L, H).transpose(0, 2, 1, 3).reshape(B, s_pad, H)[:, :S]
    return o, lse


def dilated_attention(q, k, v):
    outs, lses = [], []
    for window, dilation in DILATED_PATTERNS:
        o, l = dilated_window_branch(q, k, v, window, dilation)
        outs.append(o)
        lses.append(l)
    w = jax.nn.softmax(jnp.stack(lses, axis=0), axis=0)
    o = jnp.sum(w[..., None] * jnp.stack(outs, axis=0), axis=0)
    B, S = q.shape[0], q.shape[1]
    return o.reshape(B, S, D_ATTN).astype(q.dtype)


def chunked_spatial_gating(uv, ln_g, w_s, b_s):
    uv = jax.nn.gelu(uv, approximate=False)
    u, v = jnp.split(uv, 2, axis=-1)
    v = layernorm(v, ln_g)
    B, S, _ = v.shape
    vg = v.reshape(B, S // CHUNK, CHUNK, N_GMLP_GROUPS, GMLP_GROUP)
    mixed = jnp.einsum('gpq,bnqgc->bnpgc', w_s, vg) + b_s.T[None, None, :, :, None]
    return u * mixed.reshape(B, S, D_GMLP)


def _fwd_setup_inputs(seed: int = 0) -> dict:
    key = jax.random.key(seed)
    ks = jax.random.split(key, 14)
    f32 = jnp.float32
    nrm = lambda k, shape, scale: jax.random.normal(k, shape, f32) * scale
    gain = lambda k, shape: 1.0 + 0.02 * jax.random.normal(k, shape, f32)
    return {
        "x": jax.random.normal(ks[0], (BATCH, SEQ, D_MODEL), f32),
        "norm1_g": gain(ks[1], (DEPTH, D_MODEL)),
        "w_in": nrm(ks[2], (DEPTH, D_MODEL, D_IN), D_MODEL ** -0.5),
        "gmlp_ln_g": gain(ks[3], (DEPTH, D_GMLP)),
        "w_spatial": nrm(ks[4], (DEPTH, N_GMLP_GROUPS, CHUNK, CHUNK), CHUNK ** -0.5),
        "b_spatial": gain(ks[5], (DEPTH, N_GMLP_GROUPS, CHUNK)),
        "mix_norm_attn_g": gain(ks[6], (DEPTH, D_ATTN)),
        "mix_norm_gmlp_g": gain(ks[7], (DEPTH, D_GMLP)),
        "w_out": nrm(ks[8], (DEPTH, D_MIX, D_MODEL), D_MIX ** -0.5),
        "norm2_g": gain(ks[9], (DEPTH, D_MODEL)),
        "w_gate": nrm(ks[10], (DEPTH, D_MODEL, D_FF), D_MODEL ** -0.5),
        "w_up": nrm(ks[11], (DEPTH, D_MODEL, D_FF), D_MODEL ** -0.5),
        "w_down": nrm(ks[12], (DEPTH, D_FF, D_MODEL), D_FF ** -0.5),
        "final_g": gain(ks[13], (D_MODEL,)),
    }


def _fwd_reference(x, norm1_g, w_in, gmlp_ln_g, w_spatial, b_spatial, mix_norm_attn_g,
              mix_norm_gmlp_g, w_out, norm2_g, w_gate, w_up, w_down, final_g):
    B, S, _ = x.shape
    cos, sin = rope_tables(S)
    for l in range(DEPTH):
        h = rmsnorm(x, norm1_g[l])
        proj = jnp.einsum('bsd,de->bse', h, w_in[l])
        q = proj[..., 0:D_ATTN].reshape(B, S, N_HEADS, HEAD_DIM)
        k = proj[..., D_ATTN:2 * D_ATTN].reshape(B, S, N_HEADS, HEAD_DIM)
        v = proj[..., 2 * D_ATTN:3 * D_ATTN].reshape(B, S, N_HEADS, HEAD_DIM)
        uv = proj[..., 3 * D_ATTN:]
        q = apply_rope(q, cos, sin)
        k = apply_rope(k, cos, sin)
        a = dilated_attention(q, k, v)
        g = chunked_spatial_gating(uv, gmlp_ln_g[l], w_spatial[l], b_spatial[l])
        mix = jnp.concatenate([rmsnorm(a, mix_norm_attn_g[l]),
                               rmsnorm(g, mix_norm_gmlp_g[l])], axis=-1)
        x = x + jnp.einsum('bse,ed->bsd', mix, w_out[l])
        h = rmsnorm(x, norm2_g[l])
        ff = jax.nn.silu(jnp.einsum('bsd,df->bsf', h, w_gate[l])) * jnp.einsum('bsd,df->bsf', h, w_up[l])
        x = x + jnp.einsum('bsf,fd->bsd', ff, w_down[l])
    return rmsnorm(x, final_g)


import jax as _jax
import jax.numpy as _jnp

TWIN_FORMAT = 'train_step'
FWD_PARAMS = ['x', 'norm1_g', 'w_in', 'gmlp_ln_g', 'w_spatial', 'b_spatial', 'mix_norm_attn_g', 'mix_norm_gmlp_g', 'w_out', 'norm2_g', 'w_gate', 'w_up', 'w_down', 'final_g']
TWIN_WEIGHTS = ['norm1_g', 'w_in', 'gmlp_ln_g', 'w_spatial', 'b_spatial', 'mix_norm_attn_g', 'mix_norm_gmlp_g', 'w_out', 'norm2_g', 'w_gate', 'w_up', 'w_down', 'final_g']
TWIN_DIFF_INPUT = 'x'
TWIN_INPUTS = ['x', 'norm1_g', 'w_in', 'gmlp_ln_g', 'w_spatial', 'b_spatial', 'mix_norm_attn_g', 'mix_norm_gmlp_g', 'w_out', 'norm2_g', 'w_gate', 'w_up', 'w_down', 'final_g', 'loss_target', 'm_norm1_g', 'm_w_in', 'm_gmlp_ln_g', 'm_w_spatial', 'm_b_spatial', 'm_mix_norm_attn_g', 'm_mix_norm_gmlp_g', 'm_w_out', 'm_norm2_g', 'm_w_gate', 'm_w_up', 'm_w_down', 'm_final_g', 'v_norm1_g', 'v_w_in', 'v_gmlp_ln_g', 'v_w_spatial', 'v_b_spatial', 'v_mix_norm_attn_g', 'v_mix_norm_gmlp_g', 'v_w_out', 'v_norm2_g', 'v_w_gate', 'v_w_up', 'v_w_down', 'v_final_g']
TWIN_OUTPUTS = ['loss', 'grad_x', 'grad_norm1_g', 'grad_w_in', 'grad_gmlp_ln_g', 'grad_w_spatial', 'grad_b_spatial', 'grad_mix_norm_attn_g', 'grad_mix_norm_gmlp_g', 'grad_w_out', 'grad_norm2_g', 'grad_w_gate', 'grad_w_up', 'grad_w_down', 'grad_final_g', 'delta_norm1_g', 'delta_w_in', 'delta_gmlp_ln_g', 'delta_w_spatial', 'delta_b_spatial', 'delta_mix_norm_attn_g', 'delta_mix_norm_gmlp_g', 'delta_w_out', 'delta_norm2_g', 'delta_w_gate', 'delta_w_up', 'delta_w_down', 'delta_final_g', 'new_m_norm1_g', 'new_m_w_in', 'new_m_gmlp_ln_g', 'new_m_w_spatial', 'new_m_b_spatial', 'new_m_mix_norm_attn_g', 'new_m_mix_norm_gmlp_g', 'new_m_w_out', 'new_m_norm2_g', 'new_m_w_gate', 'new_m_w_up', 'new_m_w_down', 'new_m_final_g', 'new_v_norm1_g', 'new_v_w_in', 'new_v_gmlp_ln_g', 'new_v_w_spatial', 'new_v_b_spatial', 'new_v_mix_norm_attn_g', 'new_v_mix_norm_gmlp_g', 'new_v_w_out', 'new_v_norm2_g', 'new_v_w_gate', 'new_v_w_up', 'new_v_w_down', 'new_v_final_g']
TWIN_LEAF_KINDS = {'loss': 'loss', 'grad_x': 'grad_x', 'grad_norm1_g': 'grad_w', 'grad_w_in': 'grad_w', 'grad_gmlp_ln_g': 'grad_w', 'grad_w_spatial': 'grad_w', 'grad_b_spatial': 'grad_w', 'grad_mix_norm_attn_g': 'grad_w', 'grad_mix_norm_gmlp_g': 'grad_w', 'grad_w_out': 'grad_w', 'grad_norm2_g': 'grad_w', 'grad_w_gate': 'grad_w', 'grad_w_up': 'grad_w', 'grad_w_down': 'grad_w', 'grad_final_g': 'grad_w', 'delta_norm1_g': 'delta_w', 'delta_w_in': 'delta_w', 'delta_gmlp_ln_g': 'delta_w', 'delta_w_spatial': 'delta_w', 'delta_b_spatial': 'delta_w', 'delta_mix_norm_attn_g': 'delta_w', 'delta_mix_norm_gmlp_g': 'delta_w', 'delta_w_out': 'delta_w', 'delta_norm2_g': 'delta_w', 'delta_w_gate': 'delta_w', 'delta_w_up': 'delta_w', 'delta_w_down': 'delta_w', 'delta_final_g': 'delta_w', 'new_m_norm1_g': 'new_m', 'new_m_w_in': 'new_m', 'new_m_gmlp_ln_g': 'new_m', 'new_m_w_spatial': 'new_m', 'new_m_b_spatial': 'new_m', 'new_m_mix_norm_attn_g': 'new_m', 'new_m_mix_norm_gmlp_g': 'new_m', 'new_m_w_out': 'new_m', 'new_m_norm2_g': 'new_m', 'new_m_w_gate': 'new_m', 'new_m_w_up': 'new_m', 'new_m_w_down': 'new_m', 'new_m_final_g': 'new_m', 'new_v_norm1_g': 'new_v', 'new_v_w_in': 'new_v', 'new_v_gmlp_ln_g': 'new_v', 'new_v_w_spatial': 'new_v', 'new_v_b_spatial': 'new_v', 'new_v_mix_norm_attn_g': 'new_v', 'new_v_mix_norm_gmlp_g': 'new_v', 'new_v_w_out': 'new_v', 'new_v_norm2_g': 'new_v', 'new_v_w_gate': 'new_v', 'new_v_w_up': 'new_v', 'new_v_w_down': 'new_v', 'new_v_final_g': 'new_v'}


def _forward(args):
    return _fwd_reference(*[args[k] for k in FWD_PARAMS])


def _output_shape():
    out = _jax.eval_shape(lambda: _forward(_fwd_setup_inputs(0)))
    return out.shape, out.dtype

N_MICROBATCH = 1
ADAM_LR = 0.001
ADAM_B1 = 0.9
ADAM_B2 = 0.999
ADAM_EPS = 1e-08
ADAM_WD = 0.01
ADAM_STEP = 10
PER_EXAMPLE_BATCH_AXIS = {'x': 0, 'loss_target': 0}
SHARED_INPUTS = []
_WEIGHT_DTYPES = {'norm1_g': _jnp.float32, 'w_in': _jnp.float32, 'gmlp_ln_g': _jnp.float32, 'w_spatial': _jnp.float32, 'b_spatial': _jnp.float32, 'mix_norm_attn_g': _jnp.float32, 'mix_norm_gmlp_g': _jnp.float32, 'w_out': _jnp.float32, 'norm2_g': _jnp.float32, 'w_gate': _jnp.float32, 'w_up': _jnp.float32, 'w_down': _jnp.float32, 'final_g': _jnp.float32}
MOMENT_SCALE = {'norm1_g': 8.781980e-02, 'w_in': 5.379775e-02, 'gmlp_ln_g': 3.898702e-02, 'w_spatial': 3.929669e-02, 'b_spatial': 3.943011e-02, 'mix_norm_attn_g': 7.960396e-02, 'mix_norm_gmlp_g': 7.115145e-02, 'w_out': 7.546829e-02, 'norm2_g': 4.293502e-02, 'w_gate': 1.845511e-02, 'w_up': 1.818966e-02, 'w_down': 3.016481e-02, 'final_g': 1.641727e+01}


def _to_microbatches(a, axis):
    t = _jnp.moveaxis(a, axis, 0)
    t = t.reshape((N_MICROBATCH, t.shape[0] // N_MICROBATCH) + t.shape[1:])
    return _jnp.moveaxis(t, 1, axis + 1)


def setup_inputs(seed: int = 0) -> dict:
    inp = _fwd_setup_inputs(seed)
    key = _jax.random.fold_in(_jax.random.key(seed), 7919)
    shape, _ = _output_shape()
    out = dict(inp)
    out["loss_target"] = _jax.random.normal(_jax.random.fold_in(key, 0), shape, _jnp.float32)
    for i, name in enumerate(TWIN_WEIGHTS):
        w = inp[name].astype(_jnp.float32)
        if MOMENT_SCALE is None:
            s = _jnp.sqrt(_jnp.mean(_jnp.square(w)) + 1e-30)
        else:
            s = MOMENT_SCALE[name]
        km, kv = _jax.random.split(_jax.random.fold_in(key, i + 1))
        out[name] = w
        out["m_" + name] = s * _jax.random.normal(km, w.shape, _jnp.float32)
        out["v_" + name] = (s * s) * _jax.random.uniform(kv, w.shape, _jnp.float32, 0.5, 1.5)
    if N_MICROBATCH > 1:
        for name, axis in PER_EXAMPLE_BATCH_AXIS.items():
            out[name] = _to_microbatches(out[name], axis)
    return {'x': out['x'], 'norm1_g': out['norm1_g'], 'w_in': out['w_in'], 'gmlp_ln_g': out['gmlp_ln_g'], 'w_spatial': out['w_spatial'], 'b_spatial': out['b_spatial'], 'mix_norm_attn_g': out['mix_norm_attn_g'], 'mix_norm_gmlp_g': out['mix_norm_gmlp_g'], 'w_out': out['w_out'], 'norm2_g': out['norm2_g'], 'w_gate': out['w_gate'], 'w_up': out['w_up'], 'w_down': out['w_down'], 'final_g': out['final_g'], 'loss_target': out['loss_target'], 'm_norm1_g': out['m_norm1_g'], 'm_w_in': out['m_w_in'], 'm_gmlp_ln_g': out['m_gmlp_ln_g'], 'm_w_spatial': out['m_w_spatial'], 'm_b_spatial': out['m_b_spatial'], 'm_mix_norm_attn_g': out['m_mix_norm_attn_g'], 'm_mix_norm_gmlp_g': out['m_mix_norm_gmlp_g'], 'm_w_out': out['m_w_out'], 'm_norm2_g': out['m_norm2_g'], 'm_w_gate': out['m_w_gate'], 'm_w_up': out['m_w_up'], 'm_w_down': out['m_w_down'], 'm_final_g': out['m_final_g'], 'v_norm1_g': out['v_norm1_g'], 'v_w_in': out['v_w_in'], 'v_gmlp_ln_g': out['v_gmlp_ln_g'], 'v_w_spatial': out['v_w_spatial'], 'v_b_spatial': out['v_b_spatial'], 'v_mix_norm_attn_g': out['v_mix_norm_attn_g'], 'v_mix_norm_gmlp_g': out['v_mix_norm_gmlp_g'], 'v_w_out': out['v_w_out'], 'v_norm2_g': out['v_norm2_g'], 'v_w_gate': out['v_w_gate'], 'v_w_up': out['v_w_up'], 'v_w_down': out['v_w_down'], 'v_final_g': out['v_final_g']}


def _loss(weights, diff, rest, loss_target):
    with _jax.named_scope("forward"):
        args = {**rest, TWIN_DIFF_INPUT: diff, **{k: w.astype(_WEIGHT_DTYPES[k]) for k, w in weights.items()}}
        y = _forward(args)
    with _jax.named_scope("loss_head"):
        err = _jnp.square(y.astype(_jnp.float32) - loss_target)
        return 0.5 * _jnp.sum(_jnp.mean(err, axis=-1)) if err.ndim else 0.5 * err


def _adamw(w, g, m, v):
    m = ADAM_B1 * m + (1.0 - ADAM_B1) * g
    v = ADAM_B2 * v + (1.0 - ADAM_B2) * _jnp.square(g)
    m_hat = m / (1.0 - ADAM_B1 ** ADAM_STEP)
    v_hat = v / (1.0 - ADAM_B2 ** ADAM_STEP)
    delta = -ADAM_LR * (m_hat / (_jnp.sqrt(v_hat) + ADAM_EPS) + ADAM_WD * w)
    return delta, m, v


def reference(x, norm1_g, w_in, gmlp_ln_g, w_spatial, b_spatial, mix_norm_attn_g, mix_norm_gmlp_g, w_out, norm2_g, w_gate, w_up, w_down, final_g, loss_target, m_norm1_g, m_w_in, m_gmlp_ln_g, m_w_spatial, m_b_spatial, m_mix_norm_attn_g, m_mix_norm_gmlp_g, m_w_out, m_norm2_g, m_w_gate, m_w_up, m_w_down, m_final_g, v_norm1_g, v_w_in, v_gmlp_ln_g, v_w_spatial, v_b_spatial, v_mix_norm_attn_g, v_mix_norm_gmlp_g, v_w_out, v_norm2_g, v_w_gate, v_w_up, v_w_down, v_final_g):
    given = dict(x=x, norm1_g=norm1_g, w_in=w_in, gmlp_ln_g=gmlp_ln_g, w_spatial=w_spatial, b_spatial=b_spatial, mix_norm_attn_g=mix_norm_attn_g, mix_norm_gmlp_g=mix_norm_gmlp_g, w_out=w_out, norm2_g=norm2_g, w_gate=w_gate, w_up=w_up, w_down=w_down, final_g=final_g, loss_target=loss_target, m_norm1_g=m_norm1_g, m_w_in=m_w_in, m_gmlp_ln_g=m_gmlp_ln_g, m_w_spatial=m_w_spatial, m_b_spatial=m_b_spatial, m_mix_norm_attn_g=m_mix_norm_attn_g, m_mix_norm_gmlp_g=m_mix_norm_gmlp_g, m_w_out=m_w_out, m_norm2_g=m_norm2_g, m_w_gate=m_w_gate, m_w_up=m_w_up, m_w_down=m_w_down, m_final_g=m_final_g, v_norm1_g=v_norm1_g, v_w_in=v_w_in, v_gmlp_ln_g=v_gmlp_ln_g, v_w_spatial=v_w_spatial, v_b_spatial=v_b_spatial, v_mix_norm_attn_g=v_mix_norm_attn_g, v_mix_norm_gmlp_g=v_mix_norm_gmlp_g, v_w_out=v_w_out, v_norm2_g=v_norm2_g, v_w_gate=v_w_gate, v_w_up=v_w_up, v_w_down=v_w_down, v_final_g=v_final_g)
    weights = {n: given[n] for n in TWIN_WEIGHTS}
    shared = {n: given[n] for n in SHARED_INPUTS}
    per_example = {n: given[n] for n in ['x']}
    grad_fn = _jax.value_and_grad(_loss, argnums=(0, 1))

    def one_microbatch(ex, loss_target):
        ex = dict(ex)
        diff = ex.pop(TWIN_DIFF_INPUT)
        return grad_fn(weights, diff, {**shared, **ex}, loss_target)

    if N_MICROBATCH == 1:
        loss, (grad_w, grad_x) = one_microbatch(per_example, given["loss_target"])
    else:
        def body(carry, xs):
            loss_sum, grad_sum = carry
            l_k, (gw_k, gx_k) = one_microbatch(xs[0], xs[1])
            with _jax.named_scope("update"):
                return (loss_sum + l_k, _jax.tree.map(_jnp.add, grad_sum, gw_k)), gx_k

        init = (_jnp.zeros((), _jnp.float32), _jax.tree.map(_jnp.zeros_like, weights))
        (loss, grad_w), grad_x = _jax.lax.scan(body, init, (per_example, given["loss_target"]))
    with _jax.named_scope("update"):
        delta_w, new_m, new_v = {}, {}, {}
        for n in TWIN_WEIGHTS:
            delta_w[n], new_m[n], new_v[n] = _adamw(weights[n], grad_w[n], given["m_" + n], given["v_" + n])
    return (loss, grad_x, *[grad_w[n] for n in TWIN_WEIGHTS], *[delta_w[n] for n in TWIN_WEIGHTS],
            *[new_m[n] for n in TWIN_WEIGHTS], *[new_v[n] for n in TWIN_WEIGHTS])
```

```python
import functools

import jax
import jax.numpy as jnp
from jax import lax
from jax.experimental import pallas as pl
from jax.experimental.pallas import tpu as pltpu

F32 = jnp.float32
BF16 = jnp.bfloat16

HEAD_DIM = 128
CHUNK = 128
GROUP = 128
N_SIDE = 64
DILATIONS = (1, 4, 16)
TQ = 128
HALO = 64
EPS = 1e-6
NEG = -1e30
ROPE_THETA = 10000.0
ADAM_LR = 0.001
ADAM_B1 = 0.9
ADAM_B2 = 0.999
ADAM_EPS = 1e-08
ADAM_WD = 0.01
ADAM_STEP = 10
N_DEV = 8
N_CHIP = 4
VMEM_LIMIT_V7X = 56 * 1024 * 1024
ADAMW_BLOCK_ELEMS = 128 * 1024
MESH = pl.DeviceIdType.MESH
ANY = pl.BlockSpec(memory_space=pl.ANY)


def _params(sem=None):
    return pltpu.CompilerParams(dimension_semantics=sem, vmem_limit_bytes=VMEM_LIMIT_V7X)


def _tile(n, want):
    for t in range(min(want, n), 15, -1):
        if n % t == 0 and t % 16 == 0:
            return t
    return n


def _sds(shape, dtype):
    return jax.ShapeDtypeStruct(shape, dtype)


def _dot(a, b, dims):
    return lax.dot_general(a, b, (dims, ((), ())), preferred_element_type=F32)


NN = ((1,), (0,))
NT = ((1,), (1,))
TN = ((0,), (0,))


def _sigmoid(x):
    return 1.0 / (1.0 + jnp.exp(-x))


def _gelu(x):
    return 0.5 * x * (1.0 + lax.erf(x * 0.7071067811865476))


def _gelu_grad(x):
    cdf = 0.5 * (1.0 + lax.erf(x * 0.7071067811865476))
    pdf = 0.3989422804014327 * jnp.exp(-0.5 * x * x)
    return cdf + x * pdf


def rms_fwd(x, g):
    S, D = x.shape
    tr = _tile(S, 512)

    def body(x_ref, g_ref, o_ref):
        xv = x_ref[...]
        r = lax.rsqrt(jnp.mean(xv * xv, axis=-1, keepdims=True) + EPS)
        o_ref[...] = (xv * r * g_ref[...]).astype(o_ref.dtype)

    return pl.pallas_call(
        body, name="rms_fwd", grid=(S // tr,),
        in_specs=[pl.BlockSpec((tr, D), lambda i: (i, 0)), pl.BlockSpec((1, D), lambda i: (0, 0))],
        out_specs=pl.BlockSpec((tr, D), lambda i: (i, 0)),
        out_shape=_sds((S, D), BF16), compiler_params=_params(("parallel",)),
    )(x, g)


def _rms_bwd_math(xv, gv, dy):
    r = lax.rsqrt(jnp.mean(xv * xv, axis=-1, keepdims=True) + EPS)
    dyg = dy * gv
    dx = r * dyg - xv * (r * r * r) * jnp.mean(dyg * xv, axis=-1, keepdims=True)
    dg = jnp.sum(dy * xv * r, axis=0, keepdims=True)
    return dx, dg


def rms_bwd_res(x, g, dy, dres):
    S, D = x.shape
    tr = _tile(S, 512)

    def body(x_ref, g_ref, dy_ref, dres_ref, dx_ref, dxb_ref, dg_ref):
        dx, dg = _rms_bwd_math(x_ref[...], g_ref[...], dy_ref[...])
        tot = dres_ref[...] + dx
        dx_ref[...] = tot
        dxb_ref[...] = tot.astype(dxb_ref.dtype)

        @pl.when(pl.program_id(0) == 0)
        def _():
            dg_ref[...] = dg

        @pl.when(pl.program_id(0) > 0)
        def _():
            dg_ref[...] += dg

    row = pl.BlockSpec((tr, D), lambda i: (i, 0))
    vec = pl.BlockSpec((1, D), lambda i: (0, 0))
    return pl.pallas_call(
        body, name="rms_bwd_res", grid=(S // tr,),
        in_specs=[row, vec, row, row], out_specs=[row, row, vec],
        out_shape=[_sds((S, D), F32), _sds((S, D), BF16), _sds((1, D), F32)],
        compiler_params=_params(("arbitrary",)),
    )(x, g, dy, dres)


def loss_and_grad(x, g, target):
    S, D = x.shape
    tr = _tile(S, 512)

    def body(x_ref, g_ref, t_ref, loss_ref, dx_ref, dxb_ref, dg_ref):
        xv = x_ref[...]
        gv = g_ref[...]
        r = lax.rsqrt(jnp.mean(xv * xv, axis=-1, keepdims=True) + EPS)
        y = xv * r * gv
        diff = y - t_ref[...]
        part = 0.5 * jnp.sum(jnp.mean(diff * diff, axis=-1, keepdims=True), axis=0, keepdims=True)
        dy = diff * (1.0 / D)
        dx, dg = _rms_bwd_math(xv, gv, dy)
        dx_ref[...] = dx
        dxb_ref[...] = dx.astype(dxb_ref.dtype)

        @pl.when(pl.program_id(0) == 0)
        def _():
            dg_ref[...] = dg
            loss_ref[...] = jnp.broadcast_to(part, loss_ref.shape)

        @pl.when(pl.program_id(0) > 0)
        def _():
            dg_ref[...] += dg
            loss_ref[...] += jnp.broadcast_to(part, loss_ref.shape)

    row = pl.BlockSpec((tr, D), lambda i: (i, 0))
    vec = pl.BlockSpec((1, D), lambda i: (0, 0))
    return pl.pallas_call(
        body, name="loss_and_grad", grid=(S // tr,),
        in_specs=[row, vec, row],
        out_specs=[pl.BlockSpec((8, 128), lambda i: (0, 0)), row, row, vec],
        out_shape=[_sds((8, 128), F32), _sds((S, D), F32), _sds((S, D), BF16), _sds((1, D), F32)],
        compiler_params=_params(("arbitrary",)),
    )(x, g, target)


def _mm_body(n_pairs, dims, red_axis, n_red, has_res):
    def body(*refs):
        ins = refs[:2 * n_pairs]
        res_ref = refs[2 * n_pairs] if has_res else None
        o_ref = refs[2 * n_pairs + has_res]
        p = None
        for t in range(n_pairs):
            d = _dot(ins[2 * t][...], ins[2 * t + 1][...], dims)
            p = d if p is None else p + d
        if red_axis is None:
            if has_res:
                p = res_ref[...] + p
            o_ref[...] = p.astype(o_ref.dtype)
            return
        acc_ref = o_ref if o_ref.dtype == F32 else refs[2 * n_pairs + has_res + 1]
        r = pl.program_id(red_axis)

        @pl.when(r == 0)
        def _():
            acc_ref[...] = res_ref[...] + p if has_res else p

        @pl.when(r > 0)
        def _():
            acc_ref[...] += p

        if acc_ref is not o_ref:
            @pl.when(r == n_red - 1)
            def _():
                o_ref[...] = acc_ref[...].astype(o_ref.dtype)

    return body


def mm_in(h, w):
    S, D = h.shape
    NS, _, n = w.shape
    tm = _tile(S, 1024)
    return pl.pallas_call(
        _mm_body(1, NN, None, 1, False), name="mm_in", grid=(S // tm, NS),
        in_specs=[pl.BlockSpec((tm, D), lambda i, k: (i, 0)), pl.BlockSpec((None, D, n), lambda i, k: (k, 0, 0))],
        out_specs=pl.BlockSpec((tm, n), lambda i, k: (i, k)),
        out_shape=_sds((S, NS * n), F32), compiler_params=_params(("parallel", "parallel")),
    )(h, w)


def mm_out(x, mix_a, mix_g, w):
    S, D = x.shape
    DA = mix_a.shape[1]
    tm = _tile(S, 1024)
    tn = _tile(D, 1024)
    act = pl.BlockSpec((tm, DA), lambda i, j: (i, 0))
    return pl.pallas_call(
        _mm_body(2, NN, None, 1, True), name="mm_out", grid=(S // tm, D // tn),
        in_specs=[act, pl.BlockSpec((None, DA, tn), lambda i, j: (0, 0, j)),
                  act, pl.BlockSpec((None, DA, tn), lambda i, j: (1, 0, j)),
                  pl.BlockSpec((tm, tn), lambda i, j: (i, j))],
        out_specs=pl.BlockSpec((tm, tn), lambda i, j: (i, j)),
        out_shape=_sds((S, D), F32), compiler_params=_params(("parallel", "parallel")),
    )(mix_a, w, mix_g, w, x)


def ff_fwd(h, wg, wu):
    S, D = h.shape
    NS, _, f = wg.shape
    tm = _tile(S, 512)

    def body(h_ref, wg_ref, wu_ref, gate_ref, up_ref, ff_ref):
        hv = h_ref[...]
        g = _dot(hv, wg_ref[...], NN)
        u = _dot(hv, wu_ref[...], NN)
        gate_ref[...] = g
        up_ref[...] = u
        ff_ref[...] = (g * _sigmoid(g) * u).astype(ff_ref.dtype)

    wspec = pl.BlockSpec((None, D, f), lambda i, k: (k, 0, 0))
    ospec = pl.BlockSpec((None, tm, f), lambda i, k: (k, i, 0))
    return pl.pallas_call(
        body, name="ff_fwd", grid=(S // tm, NS),
        in_specs=[pl.BlockSpec((tm, D), lambda i, k: (i, 0)), wspec, wspec],
        out_specs=[ospec, ospec, ospec],
        out_shape=[_sds((NS, S, f), F32), _sds((NS, S, f), F32), _sds((NS, S, f), BF16)],
        compiler_params=_params(("parallel", "parallel")),
    )(h, wg, wu)


def mm_down(x, ff, wd):
    S, D = x.shape
    NS, _, f = ff.shape
    tm = _tile(S, 512)
    return pl.pallas_call(
        _mm_body(1, NN, 1, NS, True), name="mm_down", grid=(S // tm, NS),
        in_specs=[pl.BlockSpec((None, tm, f), lambda i, k: (k, i, 0)), pl.BlockSpec((None, f, D), lambda i, k: (k, 0, 0)),
                  pl.BlockSpec((tm, D), lambda i, k: (i, 0))],
        out_specs=pl.BlockSpec((tm, D), lambda i, k: (i, 0)),
        out_shape=_sds((S, D), F32), compiler_params=_params(("parallel", "arbitrary")),
    )(ff, wd, x)


def ff_bwd_act(dxb, wd, gate, up):
    S, D = dxb.shape
    NS, f, _ = wd.shape
    tm = _tile(S, 512)

    def body(dx_ref, wd_ref, gate_ref, up_ref, dgate_ref, dup_ref):
        dff = _dot(dx_ref[...], wd_ref[...], NT)
        g = gate_ref[...]
        sg = _sigmoid(g)
        dup_ref[...] = (dff * (g * sg)).astype(dup_ref.dtype)
        dgate_ref[...] = (dff * up_ref[...] * (sg * (1.0 + g * (1.0 - sg)))).astype(dgate_ref.dtype)

    aspec = pl.BlockSpec((None, tm, f), lambda i, k: (k, i, 0))
    return pl.pallas_call(
        body, name="ff_bwd_act", grid=(S // tm, NS),
        in_specs=[pl.BlockSpec((tm, D), lambda i, k: (i, 0)), pl.BlockSpec((None, f, D), lambda i, k: (k, 0, 0)), aspec, aspec],
        out_specs=[aspec, aspec],
        out_shape=[_sds((NS, S, f), BF16), _sds((NS, S, f), BF16)],
        compiler_params=_params(("parallel", "parallel")),
    )(dxb, wd, gate, up)


def dw_down(ff, dxb):
    NS, S, f = ff.shape
    D = dxb.shape[1]
    ts = _tile(S, 1024)
    return pl.pallas_call(
        _mm_body(1, TN, 1, S // ts, False), name="dw_down", grid=(NS, S // ts),
        in_specs=[pl.BlockSpec((None, ts, f), lambda k, s: (k, s, 0)), pl.BlockSpec((ts, D), lambda k, s: (s, 0))],
        out_specs=pl.BlockSpec((None, f, D), lambda k, s: (k, 0, 0)),
        out_shape=_sds((NS, f, D), BF16), scratch_shapes=[pltpu.VMEM((f, D), F32)],
        compiler_params=_params(("parallel", "arbitrary")),
    )(ff, dxb)


def dw_gate_up(h, dgate, dup):
    S, D = h.shape
    NS, _, f = dgate.shape
    ts = _tile(S, 512)
    n_red = S // ts

    def body(h_ref, dg_ref, du_ref, og_ref, ou_ref, accg_ref, accu_ref):
        hv = h_ref[...]
        pg = _dot(hv, dg_ref[...], TN)
        pu = _dot(hv, du_ref[...], TN)
        r = pl.program_id(1)

        @pl.when(r == 0)
        def _():
            accg_ref[...] = pg
            accu_ref[...] = pu

        @pl.when(r > 0)
        def _():
            accg_ref[...] += pg
            accu_ref[...] += pu

        @pl.when(r == n_red - 1)
        def _():
            og_ref[...] = accg_ref[...].astype(og_ref.dtype)
            ou_ref[...] = accu_ref[...].astype(ou_ref.dtype)

    aspec = pl.BlockSpec((None, ts, f), lambda k, s: (k, s, 0))
    ospec = pl.BlockSpec((None, D, f), lambda k, s: (k, 0, 0))
    return pl.pallas_call(
        body, name="dw_gate_up", grid=(NS, n_red),
        in_specs=[pl.BlockSpec((ts, D), lambda k, s: (s, 0)), aspec, aspec],
        out_specs=[ospec, ospec],
        out_shape=[_sds((NS, D, f), BF16), _sds((NS, D, f), BF16)],
        scratch_shapes=[pltpu.VMEM((D, f), F32), pltpu.VMEM((D, f), F32)],
        compiler_params=_params(("parallel", "arbitrary")),
    )(h, dgate, dup)


def dh_ff(dgate, dup, wg, wu):
    NS, S, f = dgate.shape
    D = wg.shape[1]
    tm = _tile(S, 512)
    aspec = pl.BlockSpec((None, tm, f), lambda i, k: (k, i, 0))
    wspec = pl.BlockSpec((None, D, f), lambda i, k: (k, 0, 0))
    return pl.pallas_call(
        _mm_body(2, NT, 1, NS, False), name="dh_ff", grid=(S // tm, NS),
        in_specs=[aspec, wspec, aspec, wspec],
        out_specs=pl.BlockSpec((tm, D), lambda i, k: (i, 0)),
        out_shape=_sds((S, D), F32), compiler_params=_params(("parallel", "arbitrary")),
    )(dgate, wg, dup, wu)


def dmix_mm(dxb, w):
    S, D = dxb.shape
    tm = _tile(S, 1024)
    tn = _tile(D, 1024)
    return pl.pallas_call(
        _mm_body(1, NT, None, 1, False), name="dmix_mm", grid=(S // tm, D // tn),
        in_specs=[pl.BlockSpec((tm, D), lambda i, j: (i, 0)), pl.BlockSpec((tn, D), lambda i, j: (j, 0))],
        out_specs=pl.BlockSpec((tm, tn), lambda i, j: (i, j)),
        out_shape=_sds((S, D), F32), compiler_params=_params(("parallel", "parallel")),
    )(dxb, w)


def dw_out(mix_a, mix_g, dxb):
    S, DA = mix_a.shape
    D = dxb.shape[1]
    ts = _tile(S, 1024)

    def half(name, m):
        return pl.pallas_call(
            _mm_body(1, TN, 0, S // ts, False), name=name, grid=(S // ts,),
            in_specs=[pl.BlockSpec((ts, DA), lambda s: (s, 0)), pl.BlockSpec((ts, D), lambda s: (s, 0))],
            out_specs=pl.BlockSpec((DA, D), lambda s: (0, 0)),
            out_shape=_sds((DA, D), BF16), scratch_shapes=[pltpu.VMEM((DA, D), F32)],
            compiler_params=_params(("arbitrary",)),
        )(m, dxb)

    return half("dw_out_a", mix_a), half("dw_out_g", mix_g)


def dw_in(h, dproj, NS):
    S, D = h.shape
    n = dproj.shape[1] // NS
    ts = _tile(S, 1024)
    return pl.pallas_call(
        _mm_body(1, TN, 1, S // ts, False), name="dw_in", grid=(NS, S // ts),
        in_specs=[pl.BlockSpec((ts, D), lambda k, s: (s, 0)), pl.BlockSpec((ts, n), lambda k, s: (s, k))],
        out_specs=pl.BlockSpec((None, D, n), lambda k, s: (k, 0, 0)),
        out_shape=_sds((NS, D, n), BF16), scratch_shapes=[pltpu.VMEM((D, n), F32)],
        compiler_params=_params(("parallel", "arbitrary")),
    )(h, dproj)


def dh_in(dproj, w):
    S = dproj.shape[0]
    NS, D, n = w.shape
    tm = _tile(S, 512)
    return pl.pallas_call(
        _mm_body(1, NT, 1, NS, False), name="dh_in", grid=(S // tm, NS),
        in_specs=[pl.BlockSpec((tm, n), lambda i, k: (i, k)), pl.BlockSpec((None, D, n), lambda i, k: (k, 0, 0))],
        out_specs=pl.BlockSpec((tm, D), lambda i, k: (i, 0)),
        out_shape=_sds((S, D), F32), compiler_params=_params(("parallel", "arbitrary")),
    )(dproj, w)


def rope_tables(S):
    pos = jnp.arange(S, dtype=F32)
    inv = ROPE_THETA ** (-jnp.arange(0, HEAD_DIM, 2, dtype=F32) / HEAD_DIM)
    ang = pos[:, None] * inv[None, :]
    cos, sin = jnp.cos(ang), jnp.sin(ang)
    return jnp.concatenate([cos, cos], axis=-1), jnp.concatenate([-sin, sin], axis=-1)


def _rot_half(t):
    return pltpu.roll(t, HEAD_DIM // 2, 1)


def rope_qkv(proj, cos2, sin2, DA):
    S = proj.shape[0]
    tr = _tile(S, 512)
    nh = DA // HEAD_DIM

    def body(q_ref, k_ref, v_ref, cos_ref, sin_ref, o_ref):
        c = cos_ref[...]
        s = sin_ref[...]
        for h in range(nh):
            sl = slice(h * HEAD_DIM, (h + 1) * HEAD_DIM)
            for j, ref in enumerate((q_ref, k_ref)):
                t = ref[:, sl]
                o_ref[:, j * DA + h * HEAD_DIM:j * DA + (h + 1) * HEAD_DIM] = (t * c + _rot_half(t) * s).astype(o_ref.dtype)
        o_ref[:, 2 * DA:3 * DA] = v_ref[...].astype(o_ref.dtype)

    tab = pl.BlockSpec((tr, HEAD_DIM), lambda i: (i, 0))
    return pl.pallas_call(
        body, name="rope_qkv", grid=(S // tr,),
        in_specs=[pl.BlockSpec((tr, DA), lambda i: (i, 0)), pl.BlockSpec((tr, DA), lambda i: (i, 1)),
                  pl.BlockSpec((tr, DA), lambda i: (i, 2)), tab, tab],
        out_specs=pl.BlockSpec((tr, 3 * DA), lambda i: (i, 0)),
        out_shape=_sds((S, 3 * DA), BF16), compiler_params=_params(("parallel",)),
    )(proj, proj, proj, cos2, sin2)


def assemble_dproj(dqs, dks, dvs, duv, cos2, sin2):
    S, DA = dqs[0].shape
    tr = _tile(S, 256)
    nh = DA // HEAD_DIM

    def body(*refs):
        dq_refs, dk_refs, dv_refs = refs[0:3], refs[3:6], refs[6:9]
        duv_ref, cos_ref, sin_ref, o_ref = refs[9:13]
        c = cos_ref[...]
        s = sin_ref[...]
        for j, trio in enumerate((dq_refs, dk_refs)):
            for h in range(nh):
                sl = slice(h * HEAD_DIM, (h + 1) * HEAD_DIM)
                t = trio[0][:, sl] + trio[1][:, sl] + trio[2][:, sl]
                o_ref[:, j * DA + h * HEAD_DIM:j * DA + (h + 1) * HEAD_DIM] = (t * c - _rot_half(t) * s).astype(o_ref.dtype)
        o_ref[:, 2 * DA:3 * DA] = (dv_refs[0][...] + dv_refs[1][...] + dv_refs[2][...]).astype(o_ref.dtype)
        o_ref[:, 3 * DA:5 * DA] = duv_ref[...]

    blk = pl.BlockSpec((tr, DA), lambda i: (i, 0))
    tab = pl.BlockSpec((tr, HEAD_DIM), lambda i: (i, 0))
    return pl.pallas_call(
        body, name="assemble_dproj", grid=(S // tr,),
        in_specs=[blk] * 9 + [pl.BlockSpec((tr, 2 * DA), lambda i: (i, 0)), tab, tab],
        out_specs=pl.BlockSpec((tr, 5 * DA), lambda i: (i, 0)),
        out_shape=_sds((S, 5 * DA), BF16), compiler_params=_params(("parallel",)),
    )(*dqs, *dks, *dvs, duv, cos2, sin2)


def _halo_specs(L, width_blocks, col):
    per = TQ // HALO
    last = L // HALO - 1
    W = width_blocks
    return [
        pl.BlockSpec((HALO, W), lambda r, i: (jnp.maximum(i * per - 1, 0), col(r))),
        pl.BlockSpec((TQ, W), lambda r, i: (i, col(r))),
        pl.BlockSpec((HALO, W), lambda r, i: (jnp.minimum((i + 1) * per, last), col(r))),
    ]


def _cat3(refs, sl):
    return jnp.concatenate([refs[0][:, sl], refs[1][:, sl], refs[2][:, sl]], axis=0)


def attn_fwd(qkv, d, DA):
    S = qkv.shape[0]
    L = S // d
    nh = DA // HEAD_DIM
    scale = HEAD_DIM ** -0.5
    TK = TQ + 2 * HALO

    def body(q_ref, kp_ref, kc_ref, kn_ref, vp_ref, vc_ref, vn_ref, o_ref, lse_ref):
        i = pl.program_id(1)
        row = lax.broadcasted_iota(jnp.int32, (TQ, TK), 0)
        col = lax.broadcasted_iota(jnp.int32, (TQ, TK), 1)
        kpos = i * TQ - HALO + col
        mask = (jnp.abs(col - HALO - row) <= N_SIDE) & (kpos >= 0) & (kpos < L)
        for h in range(nh):
            sl = slice(h * HEAD_DIM, (h + 1) * HEAD_DIM)
            k = _cat3((kp_ref, kc_ref, kn_ref), sl)
            v = _cat3((vp_ref, vc_ref, vn_ref), sl)
            s = _dot(q_ref[:, sl], k, NT) * scale
            s = jnp.where(mask, s, NEG)
            m = jnp.max(s, axis=-1, keepdims=True)
            p = jnp.exp(s - m)
            l = jnp.sum(p, axis=-1, keepdims=True)
            o = _dot(p.astype(v.dtype), v, NN) / l
            o_ref[:, sl] = o
            lse_ref[:, sl] = jnp.broadcast_to(m + jnp.log(l), (TQ, HEAD_DIM))

    out_spec = pl.BlockSpec((TQ, DA), lambda r, i: (i, r))
    o, lse = pl.pallas_call(
        body, name="attn_fwd_d%d" % d, grid=(d, L // TQ),
        in_specs=[pl.BlockSpec((TQ, DA), lambda r, i: (i, 3 * r))]
        + _halo_specs(L, DA, lambda r: 3 * r + 1) + _halo_specs(L, DA, lambda r: 3 * r + 2),
        out_specs=[out_spec, out_spec],
        out_shape=[_sds((L, d * DA), F32), _sds((L, d * DA), F32)],
        compiler_params=_params(("parallel", "parallel")),
    )(*([qkv.reshape(L, d * 3 * DA)] * 7))
    return o.reshape(S, DA), lse.reshape(S, DA)


def attn_merge(os_, lses, g):
    S, DA = os_[0].shape
    tr = _tile(S, 256)

    def body(o0, o1, o2, l0, l1, l2, g_ref, a_ref, lse_ref, mix_ref):
        a0, a1, a2 = l0[...], l1[...], l2[...]
        m = jnp.maximum(jnp.maximum(a0, a1), a2)
        e0, e1, e2 = jnp.exp(a0 - m), jnp.exp(a1 - m), jnp.exp(a2 - m)
        den = e0 + e1 + e2
        a = (e0 / den) * o0[...] + (e1 / den) * o1[...] + (e2 / den) * o2[...]
        a_ref[...] = a
        lse_ref[...] = m + jnp.log(den)
        r = lax.rsqrt(jnp.mean(a * a, axis=-1, keepdims=True) + EPS)
        mix_ref[...] = (a * r * g_ref[...]).astype(mix_ref.dtype)

    blk = pl.BlockSpec((tr, DA), lambda i: (i, 0))
    return pl.pallas_call(
        body, name="attn_merge", grid=(S // tr,),
        in_specs=[blk] * 6 + [pl.BlockSpec((1, DA), lambda i: (0, 0))],
        out_specs=[blk, blk, blk],
        out_shape=[_sds((S, DA), F32), _sds((S, DA), F32), _sds((S, DA), BF16)],
        compiler_params=_params(("parallel",)),
    )(*os_, *lses, g)


def attn_out_bwd(a, g, dmix):
    S, DA = a.shape
    tr = _tile(S, 256)
    nh = DA // HEAD_DIM

    def body(a_ref, g_ref, dy_ref, do_ref, dl_ref, dg_ref):
        av = a_ref[...]
        dx, dg = _rms_bwd_math(av, g_ref[...], dy_ref[...])
        do_ref[...] = dx.astype(do_ref.dtype)
        prod = dx * av
        for h in range(nh):
            sl = slice(h * HEAD_DIM, (h + 1) * HEAD_DIM)
            dl_ref[:, sl] = jnp.broadcast_to(jnp.sum(prod[:, sl], axis=-1, keepdims=True), (tr, HEAD_DIM))

        @pl.when(pl.program_id(0) == 0)
        def _():
            dg_ref[...] = dg

        @pl.when(pl.program_id(0) > 0)
        def _():
            dg_ref[...] += dg

    blk = pl.BlockSpec((tr, DA), lambda i: (i, 0))
    vec = pl.BlockSpec((1, DA), lambda i: (0, 0))
    return pl.pallas_call(
        body, name="attn_out_bwd", grid=(S // tr,),
        in_specs=[blk, vec, blk], out_specs=[blk, blk, vec],
        out_shape=[_sds((S, DA), BF16), _sds((S, DA), F32), _sds((1, DA), F32)],
        compiler_params=_params(("arbitrary",)),
    )(a, g, dmix)


def attn_bwd_dq(qkv, dob, lse, dl, d, DA):
    S = qkv.shape[0]
    L = S // d
    nh = DA // HEAD_DIM
    scale = HEAD_DIM ** -0.5
    TK = TQ + 2 * HALO

    def body(q_ref, kp_ref, kc_ref, kn_ref, vp_ref, vc_ref, vn_ref, do_ref, lse_ref, dl_ref, dq_ref):
        i = pl.program_id(1)
        row = lax.broadcasted_iota(jnp.int32, (TQ, TK), 0)
        col = lax.broadcasted_iota(jnp.int32, (TQ, TK), 1)
        kpos = i * TQ - HALO + col
        mask = (jnp.abs(col - HALO - row) <= N_SIDE) & (kpos >= 0) & (kpos < L)
        for h in range(nh):
            sl = slice(h * HEAD_DIM, (h + 1) * HEAD_DIM)
            k = _cat3((kp_ref, kc_ref, kn_ref), sl)
            v = _cat3((vp_ref, vc_ref, vn_ref), sl)
            s = _dot(q_ref[:, sl], k, NT) * scale
            s = jnp.where(mask, s, NEG)
            p = jnp.exp(s - lse_ref[:, sl][:, 0:1])
            dp = _dot(do_ref[:, sl], v, NT)
            ds = p * (dp - dl_ref[:, sl][:, 0:1])
            dq_ref[:, sl] = _dot(ds.astype(k.dtype), k, NN) * scale

    blk = pl.BlockSpec((TQ, DA), lambda r, i: (i, r))
    dq = pl.pallas_call(
        body, name="attn_bwd_dq_d%d" % d, grid=(d, L // TQ),
        in_specs=[pl.BlockSpec((TQ, DA), lambda r, i: (i, 3 * r))]
        + _halo_specs(L, DA, lambda r: 3 * r + 1) + _halo_specs(L, DA, lambda r: 3 * r + 2) + [blk, blk, blk],
        out_specs=blk, out_shape=_sds((L, d * DA), F32),
        compiler_params=_params(("parallel", "parallel")),
    )(*([qkv.reshape(L, d * 3 * DA)] * 7), dob.reshape(L, d * DA), lse.reshape(L, d * DA), dl.reshape(L, d * DA))
    return dq.reshape(S, DA)


def attn_bwd_dkv(qkv, dob, lse, dl, d, DA):
    S = qkv.shape[0]
    L = S // d
    nh = DA // HEAD_DIM
    scale = HEAD_DIM ** -0.5
    TR = TQ + 2 * HALO

    def body(k_ref, v_ref, qp, qc, qn, dop, doc, don, lp, lc, ln, dp_, dc_, dn_, dk_ref, dv_ref):
        j = pl.program_id(1)
        row = lax.broadcasted_iota(jnp.int32, (TR, TQ), 0)
        col = lax.broadcasted_iota(jnp.int32, (TR, TQ), 1)
        qpos = j * TQ - HALO + row
        mask = (jnp.abs(col - (row - HALO)) <= N_SIDE) & (qpos >= 0) & (qpos < L)
        for h in range(nh):
            sl = slice(h * HEAD_DIM, (h + 1) * HEAD_DIM)
            q = _cat3((qp, qc, qn), sl)
            do = _cat3((dop, doc, don), sl)
            lse_q = _cat3((lp, lc, ln), sl)
            dl_q = _cat3((dp_, dc_, dn_), sl)
            k = k_ref[:, sl]
            v = v_ref[:, sl]
            s = _dot(q, k, NT) * scale
            s = jnp.where(mask, s, NEG)
            p = jnp.exp(s - lse_q)
            dv_ref[:, sl] = _dot(p.astype(do.dtype), do, TN)
            dp = _dot(do, v, NT)
            ds = p * (dp - dl_q)
            dk_ref[:, sl] = _dot(ds.astype(q.dtype), q, TN) * scale

    blk = pl.BlockSpec((TQ, DA), lambda r, i: (i, r))
    ident = lambda r: r
    dk, dv = pl.pallas_call(
        body, name="attn_bwd_dkv_d%d" % d, grid=(d, L // TQ),
        in_specs=[pl.BlockSpec((TQ, DA), lambda r, i: (i, 3 * r + 1)), pl.BlockSpec((TQ, DA), lambda r, i: (i, 3 * r + 2))]
        + _halo_specs(L, DA, lambda r: 3 * r) + _halo_specs(L, DA, ident) + _halo_specs(L, DA, ident) + _halo_specs(L, DA, ident),
        out_specs=[blk, blk], out_shape=[_sds((L, d * DA), F32), _sds((L, d * DA), F32)],
        compiler_params=_params(("parallel", "parallel")),
    )(*([qkv.reshape(L, d * 3 * DA)] * 5), *([dob.reshape(L, d * DA)] * 3), *([lse.reshape(L, d * DA)] * 3),
      *([dl.reshape(L, d * DA)] * 3))
    return dk.reshape(S, DA), dv.reshape(S, DA)


def _gmlp_fwd_math(u_raw, v_raw, ln_g, ws_ref, bcol_ref, n_chunks, ng):
    ug = _gelu(u_raw)
    vg = _gelu(v_raw)
    mu = jnp.mean(vg, axis=-1, keepdims=True)
    xc = vg - mu
    rstd = lax.rsqrt(jnp.mean(xc * xc, axis=-1, keepdims=True) + EPS)
    xhat = xc * rstd
    vn = xhat * ln_g
    rows = []
    for n in range(n_chunks):
        cols = []
        for g in range(ng):
            blk = vn[n * CHUNK:(n + 1) * CHUNK, g * GROUP:(g + 1) * GROUP]
            cols.append(_dot(ws_ref[g].astype(BF16), blk.astype(BF16), NN) + bcol_ref[g])
        rows.append(jnp.concatenate(cols, axis=1))
    mixed = jnp.concatenate(rows, axis=0)
    return ug, xhat, rstd, vn, mixed


def gmlp_fwd(proj, ln_g, w_s, bcol, g_out, DA):
    S = proj.shape[0]
    ng = DA // GROUP
    tr = _tile(S, 2 * CHUNK)

    def body(u_ref, v_ref, ln_ref, ws_ref, bcol_ref, g_ref, G_ref, mix_ref):
        ug, _, _, _, mixed = _gmlp_fwd_math(u_ref[...], v_ref[...], ln_ref[...], ws_ref, bcol_ref, tr // CHUNK, ng)
        G = ug * mixed
        G_ref[...] = G
        r = lax.rsqrt(jnp.mean(G * G, axis=-1, keepdims=True) + EPS)
        mix_ref[...] = (G * r * g_ref[...]).astype(mix_ref.dtype)

    vec = pl.BlockSpec((1, DA), lambda i: (0, 0))
    par = pl.BlockSpec((ng, CHUNK, CHUNK), lambda i: (0, 0, 0))
    blk = pl.BlockSpec((tr, DA), lambda i: (i, 0))
    return pl.pallas_call(
        body, name="gmlp_fwd", grid=(S // tr,),
        in_specs=[pl.BlockSpec((tr, DA), lambda i: (i, 3)), pl.BlockSpec((tr, DA), lambda i: (i, 4)), vec, par, par, vec],
        out_specs=[blk, blk], out_shape=[_sds((S, DA), F32), _sds((S, DA), BF16)],
        compiler_params=_params(("parallel",)),
    )(proj, proj, ln_g, w_s, bcol, g_out)


def gmlp_bwd(proj, ln_g, w_s, w_st, bcol, g_out, dmix, DA):
    S = proj.shape[0]
    ng = DA // GROUP
    tr = _tile(S, 2 * CHUNK)
    nc = tr // CHUNK

    def body(u_ref, v_ref, ln_ref, ws_ref, wst_ref, bcol_ref, g_ref, dy_ref, duv_ref, dg_ref, dln_ref, dws_ref, db_ref):
        u_raw = u_ref[...]
        v_raw = v_ref[...]
        ln_g_v = ln_ref[...]
        ug, xhat, rstd, vn, mixed = _gmlp_fwd_math(u_raw, v_raw, ln_g_v, ws_ref, bcol_ref, nc, ng)
        G = ug * mixed
        dG, dg = _rms_bwd_math(G, g_ref[...], dy_ref[...])
        du_g = dG * mixed
        dmixed = dG * ug
        dws, dbs, rows = [], [], []
        for g in range(ng):
            dws.append(None)
            dbs.append(None)
        for n in range(nc):
            cols = []
            for g in range(ng):
                dm = dmixed[n * CHUNK:(n + 1) * CHUNK, g * GROUP:(g + 1) * GROUP]
                vb = vn[n * CHUNK:(n + 1) * CHUNK, g * GROUP:(g + 1) * GROUP]
                dmb = dm.astype(BF16)
                w = _dot(dmb, vb.astype(BF16), NT)
                b = jnp.broadcast_to(jnp.sum(dm, axis=-1, keepdims=True), (CHUNK, GROUP))
                dws[g] = w if dws[g] is None else dws[g] + w
                dbs[g] = b if dbs[g] is None else dbs[g] + b
                cols.append(_dot(wst_ref[g].astype(BF16), dmb, NN))
            rows.append(jnp.concatenate(cols, axis=1))
        dvn = jnp.concatenate(rows, axis=0)
        dln = jnp.sum(dvn * xhat, axis=0, keepdims=True)
        dxh = dvn * ln_g_v
        dvg = rstd * (dxh - jnp.mean(dxh, axis=-1, keepdims=True) - xhat * jnp.mean(dxh * xhat, axis=-1, keepdims=True))
        duv_ref[:, 0:DA] = (du_g * _gelu_grad(u_raw)).astype(duv_ref.dtype)
        duv_ref[:, DA:2 * DA] = (dvg * _gelu_grad(v_raw)).astype(duv_ref.dtype)

        first = pl.program_id(0) == 0

        @pl.when(first)
        def _():
            dg_ref[...] = dg
            dln_ref[...] = dln
            for g in range(ng):
                dws_ref[g] = dws[g]
                db_ref[g] = dbs[g]

        @pl.when(jnp.logical_not(first))
        def _():
            dg_ref[...] += dg
            dln_ref[...] += dln
            for g in range(ng):
                dws_ref[g] += dws[g]
                db_ref[g] += dbs[g]

    vec = pl.BlockSpec((1, DA), lambda i: (0, 0))
    par = pl.BlockSpec((ng, CHUNK, CHUNK), lambda i: (0, 0, 0))
    return pl.pallas_call(
        body, name="gmlp_bwd", grid=(S // tr,),
        in_specs=[pl.BlockSpec((tr, DA), lambda i: (i, 3)), pl.BlockSpec((tr, DA), lambda i: (i, 4)), vec, par, par, par, vec,
                  pl.BlockSpec((tr, DA), lambda i: (i, 1))],
        out_specs=[pl.BlockSpec((tr, 2 * DA), lambda i: (i, 0)), vec, vec, par, par],
        out_shape=[_sds((S, 2 * DA), BF16), _sds((1, DA), F32), _sds((1, DA), F32),
                   _sds((ng, CHUNK, CHUNK), F32), _sds((ng, CHUNK, CHUNK), F32)],
        compiler_params=_params(("arbitrary",)),
    )(proj, proj, ln_g, w_s, w_st, bcol, g_out, dmix)


def layer_fwd(x, sm, gw, tabs):
    D = x.shape[1]
    DA = D // 2
    cos2, sin2 = tabs
    h1 = rms_fwd(x, sm["norm1_g"])
    proj = mm_in(h1, gw["w_in"])
    qkv = rope_qkv(proj, cos2, sin2, DA)
    os_, lses = [], []
    for d in DILATIONS:
        o, l = attn_fwd(qkv, d, DA)
        os_.append(o)
        lses.append(l)
    a, lse, mix_a = attn_merge(os_, lses, sm["mix_norm_attn_g"])
    _, mix_g = gmlp_fwd(proj, sm["gmlp_ln_g"], sm["w_spatial"], sm["bcol"], sm["mix_norm_gmlp_g"], DA)
    x2 = mm_out(x, mix_a, mix_g, gw["w_out"])
    h2 = rms_fwd(x2, sm["norm2_g"])
    gate, up, ff = ff_fwd(h2, gw["w_gate"], gw["w_up"])
    x3 = mm_down(x2, ff, gw["w_down"])
    saved = dict(x=x, h1=h1, proj=proj, qkv=qkv, a=a, lse=lse, mix_a=mix_a, mix_g=mix_g, x2=x2, h2=h2, gate=gate, up=up, ff=ff)
    return x3, saved


def layer_bwd(dx, dxb, sv, sm, gw, tabs):
    D = dx.shape[1]
    DA = D // 2
    cos2, sin2 = tabs
    NS = gw["w_in"].shape[0]
    dgate, dup = ff_bwd_act(dxb, gw["w_down"], sv["gate"], sv["up"])
    g_down = dw_down(sv["ff"], dxb)
    g_gate, g_up = dw_gate_up(sv["h2"], dgate, dup)
    dh2 = dh_ff(dgate, dup, gw["w_gate"], gw["w_up"])
    dx2, dx2b, d_norm2 = rms_bwd_res(sv["x2"], sm["norm2_g"], dh2, dx)
    dmix = dmix_mm(dx2b, gw["w_out"].reshape(D, D))
    g_out_a, g_out_g = dw_out(sv["mix_a"], sv["mix_g"], dx2b)
    dob, dl, d_mix_a = attn_out_bwd(sv["a"], sm["mix_norm_attn_g"], dmix)
    duv, d_mix_g, d_ln, d_ws, d_bs = gmlp_bwd(sv["proj"], sm["gmlp_ln_g"], sm["w_spatial"], sm["w_spatial_t"], sm["bcol"],
                                              sm["mix_norm_gmlp_g"], dmix, DA)
    dqs, dks, dvs = [], [], []
    for d in DILATIONS:
        dqs.append(attn_bwd_dq(sv["qkv"], dob, sv["lse"], dl, d, DA))
        dk, dv = attn_bwd_dkv(sv["qkv"], dob, sv["lse"], dl, d, DA)
        dks.append(dk)
        dvs.append(dv)
    dproj = assemble_dproj(dqs, dks, dvs, duv, cos2, sin2)
    g_in = dw_in(sv["h1"], dproj, NS)
    dh1 = dh_in(dproj, gw["w_in"])
    dx0, dx0b, d_norm1 = rms_bwd_res(sv["x"], sm["norm1_g"], dh1, dx2)
    big = dict(w_in=g_in, w_out=jnp.concatenate([g_out_a, g_out_g], axis=0), w_gate=g_gate, w_up=g_up, w_down=g_down)
    small = dict(norm1_g=d_norm1, gmlp_ln_g=d_ln, w_spatial=d_ws, b_spatial=d_bs[:, :, 0], mix_norm_attn_g=d_mix_a,
                 mix_norm_gmlp_g=d_mix_g, norm2_g=d_norm2)
    return dx0, dx0b, big, small


def _place():
    x, y, c = lax.axis_index("x"), lax.axis_index("y"), lax.axis_index("c")
    return x, y, c, [(1 - x, y), (x, 1 - y), (1 - x, 1 - y)]


def _remote(src, dst, send_sems, recv_sems, k, to):
    return pltpu.make_async_remote_copy(src_ref=src, dst_ref=dst, send_sem=send_sems.at[k], recv_sem=recv_sems.at[k],
                                        device_id=to, device_id_type=MESH)


def gather_ici(shards):
    n = len(shards)

    def body(*refs):
        ins, outs = refs[:n], refs[n:2 * n]
        send_sems, recv_sems, local_sems = refs[2 * n:]
        x, y, c, chips = _place()
        me = 4 * x + 2 * y + c
        local = [pltpu.make_async_copy(ins[a], outs[a].at[me], local_sems.at[a]) for a in range(n)]
        for cp in local:
            cp.start()
        sent = []
        for a in range(n):
            for j, (px, py) in enumerate(chips):
                cp = _remote(ins[a], outs[a].at[me], send_sems, recv_sems, 3 * a + j, (px, py, c))
                cp.start()
                sent.append(cp)
        for a in range(n):
            for j, (px, py) in enumerate(chips):
                slot = outs[a].at[4 * px + 2 * py + c]
                _remote(slot, slot, send_sems, recv_sems, 3 * a + j, (px, py, c)).wait_recv()
        for cp in sent:
            cp.wait_send()
        for cp in local:
            cp.wait()

    return pl.pallas_call(
        body, name="gather_ici", in_specs=[ANY] * n, out_specs=[ANY] * n,
        out_shape=[_sds((N_DEV,) + s.shape, s.dtype) for s in shards],
        scratch_shapes=[pltpu.SemaphoreType.DMA((3 * n,)), pltpu.SemaphoreType.DMA((3 * n,)), pltpu.SemaphoreType.DMA((n,))],
    )(*shards)


def gather_d2d(bufs):
    n = len(bufs)

    def body(*refs):
        outs = refs[n:2 * n]
        send_sems, recv_sems = refs[2 * n:]
        x, y, c, _ = _place()
        sent = []
        for a in range(n):
            for j in range(N_CHIP):
                slot = outs[a].at[2 * j + c]
                cp = _remote(slot, slot, send_sems, recv_sems, N_CHIP * a + j, (x, y, 1 - c))
                cp.start()
                sent.append(cp)
        for a in range(n):
            for j in range(N_CHIP):
                slot = outs[a].at[2 * j + 1 - c]
                _remote(slot, slot, send_sems, recv_sems, N_CHIP * a + j, (x, y, 1 - c)).wait_recv()
        for cp in sent:
            cp.wait_send()

    return pl.pallas_call(
        body, name="gather_d2d", in_specs=[ANY] * n, out_specs=[ANY] * n,
        out_shape=[_sds(b.shape, b.dtype) for b in bufs], input_output_aliases={a: a for a in range(n)},
        scratch_shapes=[pltpu.SemaphoreType.DMA((N_CHIP * n,)), pltpu.SemaphoreType.DMA((N_CHIP * n,))],
    )(*bufs)


def all_gather(shards):
    return gather_d2d(gather_ici(shards))


def scatter_d2d(parts):
    n = len(parts)

    def body(*refs):
        ins, outs = refs[:n], refs[n:2 * n]
        send_sems, recv_sems = refs[2 * n:]
        x, y, c, _ = _place()
        sent = []
        for a in range(n):
            for j in range(N_CHIP):
                cp = _remote(ins[a].at[2 * j + 1 - c], outs[a].at[j], send_sems, recv_sems, N_CHIP * a + j, (x, y, 1 - c))
                cp.start()
                sent.append(cp)
        for a in range(n):
            for j in range(N_CHIP):
                slot = outs[a].at[j]
                _remote(slot, slot, send_sems, recv_sems, N_CHIP * a + j, (x, y, 1 - c)).wait_recv()
        for cp in sent:
            cp.wait_send()

    return pl.pallas_call(
        body, name="scatter_d2d", in_specs=[ANY] * n, out_specs=[ANY] * n,
        out_shape=[_sds((N_CHIP,) + p.shape[1:], p.dtype) for p in parts],
        scratch_shapes=[pltpu.SemaphoreType.DMA((N_CHIP * n,)), pltpu.SemaphoreType.DMA((N_CHIP * n,))],
    )(*parts)


def pair_sum(part, got, c_arr):
    _, R, C = part.shape
    tr = _tile(R, 512)
    p4 = part.reshape(N_CHIP, 2, R, C)

    def body(c_ref, p_ref, g_ref, o_ref):
        o_ref[...] = (p_ref[...].astype(F32) + g_ref[...].astype(F32)).astype(o_ref.dtype)

    return pl.pallas_call(
        body, name="pair_sum",
        grid_spec=pltpu.PrefetchScalarGridSpec(
            num_scalar_prefetch=1, grid=(N_CHIP, R // tr),
            in_specs=[pl.BlockSpec((None, None, tr, C), lambda j, i, c_ref: (j, c_ref[0], i, 0)),
                      pl.BlockSpec((None, tr, C), lambda j, i, c_ref: (j, i, 0))],
            out_specs=pl.BlockSpec((None, tr, C), lambda j, i, c_ref: (j, i, 0))),
        out_shape=_sds((N_CHIP, R, C), part.dtype), compiler_params=_params(("parallel", "parallel")),
    )(c_arr, p4, got)


def scatter_ici(sums):
    n = len(sums)

    def body(*refs):
        ins, outs = refs[:n], refs[n:2 * n]
        send_sems, recv_sems, local_sems = refs[2 * n:]
        x, y, c, chips = _place()
        mine = 2 * x + y
        local = [pltpu.make_async_copy(ins[a].at[mine], outs[a].at[mine], local_sems.at[a]) for a in range(n)]
        for cp in local:
            cp.start()
        sent = []
        for a in range(n):
            for j, (px, py) in enumerate(chips):
                cp = _remote(ins[a].at[2 * px + py], outs[a].at[mine], send_sems, recv_sems, 3 * a + j, (px, py, c))
                cp.start()
                sent.append(cp)
        for a in range(n):
            for j, (px, py) in enumerate(chips):
                slot = outs[a].at[2 * px + py]
                _remote(slot, slot, send_sems, recv_sems, 3 * a + j, (px, py, c)).wait_recv()
        for cp in sent:
            cp.wait_send()
        for cp in local:
            cp.wait()

    return pl.pallas_call(
        body, name="scatter_ici", in_specs=[ANY] * n, out_specs=[ANY] * n,
        out_shape=[_sds(s.shape, s.dtype) for s in sums],
        scratch_shapes=[pltpu.SemaphoreType.DMA((3 * n,)), pltpu.SemaphoreType.DMA((3 * n,)), pltpu.SemaphoreType.DMA((n,))],
    )(*sums)


def reduce_scatter(parts, c_arr):
    got = scatter_d2d(parts)
    return scatter_ici([pair_sum(p, g, c_arr) for p, g in zip(parts, got)])


def adamw(w, m, v, parts_per_layer):
    NL, R, C = w.shape
    P = parts_per_layer[0].shape[0]
    tr = _tile(R, max(16, ADAMW_BLOCK_ELEMS // C))
    nb = R // tr

    def body(w_ref, m_ref, v_ref, *rest):
        part_refs = rest[:NL]
        g_ref, d_ref, nm_ref, nv_ref = rest[NL:]
        layer = pl.program_id(0)
        g = None
        for q in range(NL):
            s = part_refs[q][0].astype(F32)
            for t in range(1, P):
                s = s + part_refs[q][t].astype(F32)
            s = jnp.where(layer == q, s, 0.0)
            g = s if g is None else g + s
        wv = w_ref[...]
        nm = ADAM_B1 * m_ref[...] + (1.0 - ADAM_B1) * g
        nv = ADAM_B2 * v_ref[...] + (1.0 - ADAM_B2) * (g * g)
        m_hat = nm / (1.0 - ADAM_B1 ** ADAM_STEP)
        v_hat = nv / (1.0 - ADAM_B2 ** ADAM_STEP)
        g_ref[...] = g
        d_ref[...] = -ADAM_LR * (m_hat / (jnp.sqrt(v_hat) + ADAM_EPS) + ADAM_WD * wv)
        nm_ref[...] = nm
        nv_ref[...] = nv

    def part_spec(q):
        return pl.BlockSpec((P, tr, C), lambda l, i: (0, jnp.where(l == q, i, jnp.where(l < q, 0, nb - 1)), 0))

    blk = pl.BlockSpec((None, tr, C), lambda l, i: (l, i, 0))
    return pl.pallas_call(
        body, name="adamw", grid=(NL, nb),
        in_specs=[blk, blk, blk] + [part_spec(q) for q in range(NL)],
        out_specs=[blk, blk, blk, blk], out_shape=[_sds((NL, R, C), F32)] * 4,
        compiler_params=_params(("arbitrary", "arbitrary")),
    )(w, m, v, *parts_per_layer)


SMALL = ("norm1_g", "gmlp_ln_g", "w_spatial", "b_spatial", "mix_norm_attn_g", "mix_norm_gmlp_g", "norm2_g")
BIG = ("w_in", "w_out", "w_gate", "w_up", "w_down")
LANES = 128


def _pack(layers, final):
    flat = [layer[n].reshape(-1) for layer in layers for n in SMALL] + [final.reshape(-1)]
    return jnp.concatenate(flat).reshape(-1, LANES)


def _unpack(packed, like_layers, like_final):
    flat = packed.reshape(-1)
    out, off = [], 0
    for layer in like_layers:
        d = {}
        for n in SMALL:
            size = layer[n].size
            d[n] = flat[off:off + size].reshape(layer[n].shape)
            off += size
        out.append(d)
    return out, flat[off:off + like_final.size].reshape(like_final.shape)


def kernel(x, norm1_g, w_in, gmlp_ln_g, w_spatial, b_spatial, mix_norm_attn_g, mix_norm_gmlp_g, w_out, norm2_g, w_gate, w_up, w_down, final_g, loss_target, m_norm1_g, m_w_in, m_gmlp_ln_g, m_w_spatial, m_b_spatial, m_mix_norm_attn_g, m_mix_norm_gmlp_g, m_w_out, m_norm2_g, m_w_gate, m_w_up, m_w_down, m_final_g, v_norm1_g, v_w_in, v_gmlp_ln_g, v_w_spatial, v_b_spatial, v_mix_norm_attn_g, v_mix_norm_gmlp_g, v_w_out, v_norm2_g, v_w_gate, v_w_up, v_w_down, v_final_g):
    S, D = x.shape[1], x.shape[2]
    NL = norm1_g.shape[0]
    DA = D // 2
    xs = x.reshape(S, D)
    tabs = rope_tables(S)
    c_arr = lax.axis_index("c").astype(jnp.int32).reshape(1)
    small_w = dict(norm1_g=norm1_g, gmlp_ln_g=gmlp_ln_g, w_spatial=w_spatial, b_spatial=b_spatial,
                   mix_norm_attn_g=mix_norm_attn_g, mix_norm_gmlp_g=mix_norm_gmlp_g, norm2_g=norm2_g)
    small_m = dict(norm1_g=m_norm1_g, gmlp_ln_g=m_gmlp_ln_g, w_spatial=m_w_spatial, b_spatial=m_b_spatial,
                   mix_norm_attn_g=m_mix_norm_attn_g, mix_norm_gmlp_g=m_mix_norm_gmlp_g, norm2_g=m_norm2_g)
    small_v = dict(norm1_g=v_norm1_g, gmlp_ln_g=v_gmlp_ln_g, w_spatial=v_w_spatial, b_spatial=v_b_spatial,
                   mix_norm_attn_g=v_mix_norm_attn_g, mix_norm_gmlp_g=v_mix_norm_gmlp_g, norm2_g=v_norm2_g)
    big_w = dict(w_in=w_in, w_out=w_out, w_gate=w_gate, w_up=w_up, w_down=w_down)
    big_m = dict(w_in=m_w_in, w_out=m_w_out, w_gate=m_w_gate, w_up=m_w_up, w_down=m_w_down)
    big_v = dict(w_in=v_w_in, w_out=v_w_out, w_gate=v_w_gate, w_up=v_w_up, w_down=v_w_down)

    def layer_small(l):
        ws = w_spatial[l]
        return dict(norm1_g=norm1_g[l][None], gmlp_ln_g=gmlp_ln_g[l][None], mix_norm_attn_g=mix_norm_attn_g[l][None],
                    mix_norm_gmlp_g=mix_norm_gmlp_g[l][None], norm2_g=norm2_g[l][None], w_spatial=ws,
                    w_spatial_t=jnp.swapaxes(ws, 1, 2), bcol=jnp.broadcast_to(b_spatial[l][:, :, None], ws.shape))

    def gathered(l):
        got = all_gather([big_w[n][l].astype(BF16) for n in BIG])
        gw = dict(zip(BIG, got))
        gw["w_out"] = gw["w_out"].reshape(2, DA, D)
        return gw

    h = xs
    saved, weights = [], []
    for l in range(NL):
        gw = gathered(l)
        h, sv = layer_fwd(h, layer_small(l), gw, tabs)
        saved.append(sv)
        weights.append(gw)
    loss_part, dx, dxb, d_final = loss_and_grad(h, final_g[None], loss_target.reshape(S, D))
    loss = lax.psum(loss_part[0, 0], ("x", "y", "c"))

    big_sums = [None] * NL
    small_grads = [None] * NL
    for l in reversed(range(NL)):
        dx, dxb, big, small = layer_bwd(dx, dxb, saved[l], layer_small(l), weights[l], tabs)
        parts = [big[n].reshape((N_DEV, -1, big[n].shape[-1])) for n in BIG]
        big_sums[l] = dict(zip(BIG, reduce_scatter(parts, c_arr)))
        small_grads[l] = small

    packed = _pack(small_grads, d_final)
    gathered_small = all_gather([packed])[0]
    pw = _pack([{n: small_w[n][l] for n in SMALL} for l in range(NL)], final_g)[None]
    pm = _pack([{n: small_m[n][l] for n in SMALL} for l in range(NL)], m_final_g)[None]
    pv = _pack([{n: small_v[n][l] for n in SMALL} for l in range(NL)], v_final_g)[None]
    like_layers = [{n: small_w[n][l] for n in SMALL} for l in range(NL)]
    small_out = [_unpack(t[0], like_layers, final_g) for t in adamw(pw, pm, pv, [gathered_small])]

    def small_stack(k, n):
        return jnp.stack([small_out[k][0][l][n] for l in range(NL)])

    big_out = {}
    for n in BIG:
        R, C = big_w[n].shape[1], big_w[n].shape[2]
        big_out[n] = adamw(big_w[n], big_m[n], big_v[n], [big_sums[l][n].reshape(N_CHIP, R, C) for l in range(NL)])

    order = ("norm1_g", "w_in", "gmlp_ln_g", "w_spatial", "b_spatial", "mix_norm_attn_g", "mix_norm_gmlp_g", "w_out",
             "norm2_g", "w_gate", "w_up", "w_down")
    outs = [loss, dx.reshape(x.shape)]
    for k in range(4):
        for n in order:
            outs.append(big_out[n][k] if n in BIG else small_stack(k, n))
        outs.append(small_out[k][1])
    return tuple(outs)
```

```python
import functools

import jax
import jax.numpy as jnp
from jax import lax
from jax.experimental import pallas as pl
from jax.experimental.pallas import tpu as pltpu

F32 = jnp.float32
BF16 = jnp.bfloat16

HEAD_DIM = 128
CHUNK = 128
GROUP = 128
N_SIDE = 64
DILATIONS = (1, 4, 16)
TQ = 128
HALO = 64
EPS = 1e-6
NEG = -1e30
ROPE_THETA = 10000.0
ADAM_LR = 0.001
ADAM_B1 = 0.9
ADAM_B2 = 0.999
ADAM_EPS = 1e-08
ADAM_WD = 0.01
ADAM_STEP = 10
N_DEV = 8
N_CHIP = 4
VMEM_LIMIT_V7X = 56 * 1024 * 1024
ADAMW_BLOCK_ELEMS = 128 * 1024
MESH = pl.DeviceIdType.MESH
ANY = pl.BlockSpec(memory_space=pl.ANY)


def _params(sem=None):
    return pltpu.CompilerParams(dimension_semantics=sem, vmem_limit_bytes=VMEM_LIMIT_V7X)


def _tile(n, want):
    for t in range(min(want, n), 15, -1):
        if n % t == 0 and t % 16 == 0:
            return t
    return n


def _sds(shape, dtype):
    return jax.ShapeDtypeStruct(shape, dtype)


def _dot(a, b, dims):
    return lax.dot_general(a, b, (dims, ((), ())), preferred_element_type=F32)


NN = ((1,), (0,))
NT = ((1,), (1,))
TN = ((0,), (0,))


def _sigmoid(x):
    return 1.0 / (1.0 + jnp.exp(-x))


def _gelu(x):
    return 0.5 * x * (1.0 + lax.erf(x * 0.7071067811865476))


def _gelu_grad(x):
    cdf = 0.5 * (1.0 + lax.erf(x * 0.7071067811865476))
    pdf = 0.3989422804014327 * jnp.exp(-0.5 * x * x)
    return cdf + x * pdf


def rms_fwd(x, g):
    S, D = x.shape
    tr = _tile(S, 512)

    def body(x_ref, g_ref, o_ref):
        xv = x_ref[...]
        r = lax.rsqrt(jnp.mean(xv * xv, axis=-1, keepdims=True) + EPS)
        o_ref[...] = (xv * r * g_ref[...]).astype(o_ref.dtype)

    return pl.pallas_call(
        body, name="rms_fwd", grid=(S // tr,),
        in_specs=[pl.BlockSpec((tr, D), lambda i: (i, 0)), pl.BlockSpec((1, D), lambda i: (0, 0))],
        out_specs=pl.BlockSpec((tr, D), lambda i: (i, 0)),
        out_shape=_sds((S, D), BF16), compiler_params=_params(("parallel",)),
    )(x, g)


def _rms_bwd_math(xv, gv, dy):
    r = lax.rsqrt(jnp.mean(xv * xv, axis=-1, keepdims=True) + EPS)
    dyg = dy * gv
    dx = r * dyg - xv * (r * r * r) * jnp.mean(dyg * xv, axis=-1, keepdims=True)
    dg = jnp.sum(dy * xv * r, axis=0, keepdims=True)
    return dx, dg


def rms_bwd_res(x, g, dy, dres):
    S, D = x.shape
    tr = _tile(S, 512)

    def body(x_ref, g_ref, dy_ref, dres_ref, dx_ref, dxb_ref, dg_ref):
        dx, dg = _rms_bwd_math(x_ref[...], g_ref[...], dy_ref[...])
        tot = dres_ref[...] + dx
        dx_ref[...] = tot
        dxb_ref[...] = tot.astype(dxb_ref.dtype)

        @pl.when(pl.program_id(0) == 0)
        def _():
            dg_ref[...] = dg

        @pl.when(pl.program_id(0) > 0)
        def _():
            dg_ref[...] += dg

    row = pl.BlockSpec((tr, D), lambda i: (i, 0))
    vec = pl.BlockSpec((1, D), lambda i: (0, 0))
    return pl.pallas_call(
        body, name="rms_bwd_res", grid=(S // tr,),
        in_specs=[row, vec, row, row], out_specs=[row, row, vec],
        out_shape=[_sds((S, D), F32), _sds((S, D), BF16), _sds((1, D), F32)],
        compiler_params=_params(("arbitrary",)),
    )(x, g, dy, dres)


def loss_and_grad(x, g, target):
    S, D = x.shape
    tr = _tile(S, 512)

    def body(x_ref, g_ref, t_ref, loss_ref, dx_ref, dxb_ref, dg_ref):
        xv = x_ref[...]
        gv = g_ref[...]
        r = lax.rsqrt(jnp.mean(xv * xv, axis=-1, keepdims=True) + EPS)
        y = xv * r * gv
        diff = y - t_ref[...]
        part = 0.5 * jnp.sum(jnp.mean(diff * diff, axis=-1, keepdims=True), axis=0, keepdims=True)
        dy = diff * (1.0 / D)
        dx, dg = _rms_bwd_math(xv, gv, dy)
        dx_ref[...] = dx
        dxb_ref[...] = dx.astype(dxb_ref.dtype)

        @pl.when(pl.program_id(0) == 0)
        def _():
            dg_ref[...] = dg
            loss_ref[...] = jnp.broadcast_to(part, loss_ref.shape)

        @pl.when(pl.program_id(0) > 0)
        def _():
            dg_ref[...] += dg
            loss_ref[...] += jnp.broadcast_to(part, loss_ref.shape)

    row = pl.BlockSpec((tr, D), lambda i: (i, 0))
    vec = pl.BlockSpec((1, D), lambda i: (0, 0))
    return pl.pallas_call(
        body, name="loss_and_grad", grid=(S // tr,),
        in_specs=[row, vec, row],
        out_specs=[pl.BlockSpec((8, 128), lambda i: (0, 0)), row, row, vec],
        out_shape=[_sds((8, 128), F32), _sds((S, D), F32), _sds((S, D), BF16), _sds((1, D), F32)],
        compiler_params=_params(("arbitrary",)),
    )(x, g, target)


def _mm_body(n_pairs, dims, red_axis, n_red, has_res):
    def body(*refs):
        ins = refs[:2 * n_pairs]
        res_ref = refs[2 * n_pairs] if has_res else None
        o_ref = refs[2 * n_pairs + has_res]
        p = None
        for t in range(n_pairs):
            d = _dot(ins[2 * t][...], ins[2 * t + 1][...], dims)
            p = d if p is None else p + d
        if red_axis is None:
            if has_res:
                p = res_ref[...] + p
            o_ref[...] = p.astype(o_ref.dtype)
            return
        acc_ref = o_ref if o_ref.dtype == F32 else refs[2 * n_pairs + has_res + 1]
        r = pl.program_id(red_axis)

        @pl.when(r == 0)
        def _():
            acc_ref[...] = res_ref[...] + p if has_res else p

        @pl.when(r > 0)
        def _():
            acc_ref[...] += p

        if acc_ref is not o_ref:
            @pl.when(r == n_red - 1)
            def _():
                o_ref[...] = acc_ref[...].astype(o_ref.dtype)

    return body


def mm_in(h, w):
    S, D = h.shape
    NS, _, n = w.shape
    tm = _tile(S, 1024)
    return pl.pallas_call(
        _mm_body(1, NN, None, 1, False), name="mm_in", grid=(S // tm, NS),
        in_specs=[pl.BlockSpec((tm, D), lambda i, k: (i, 0)), pl.BlockSpec((None, D, n), lambda i, k: (k, 0, 0))],
        out_specs=pl.BlockSpec((tm, n), lambda i, k: (i, k)),
        out_shape=_sds((S, NS * n), F32), compiler_params=_params(("parallel", "parallel")),
    )(h, w)


def mm_out(x, mix_a, mix_g, w):
    S, D = x.shape
    DA = mix_a.shape[1]
    tm = _tile(S, 1024)
    tn = _tile(D, 1024)
    act = pl.BlockSpec((tm, DA), lambda i, j: (i, 0))
    return pl.pallas_call(
        _mm_body(2, NN, None, 1, True), name="mm_out", grid=(S // tm, D // tn),
        in_specs=[act, pl.BlockSpec((None, DA, tn), lambda i, j: (0, 0, j)),
                  act, pl.BlockSpec((None, DA, tn), lambda i, j: (1, 0, j)),
                  pl.BlockSpec((tm, tn), lambda i, j: (i, j))],
        out_specs=pl.BlockSpec((tm, tn), lambda i, j: (i, j)),
        out_shape=_sds((S, D), F32), compiler_params=_params(("parallel", "parallel")),
    )(mix_a, w, mix_g, w, x)


def ff_fwd(h, wg, wu):
    S, D = h.shape
    NS, f, _ = wg.shape
    tm = _tile(S, 512)

    def body(h_ref, wg_ref, wu_ref, gate_ref, up_ref, ff_ref):
        hv = h_ref[...]
        g = _dot(hv, wg_ref[...], NT)
        u = _dot(hv, wu_ref[...], NT)
        gate_ref[...] = g
        up_ref[...] = u
        ff_ref[...] = (g * _sigmoid(g) * u).astype(ff_ref.dtype)

    wspec = pl.BlockSpec((None, f, D), lambda i, k: (k, 0, 0))
    ospec = pl.BlockSpec((None, tm, f), lambda i, k: (k, i, 0))
    return pl.pallas_call(
        body, name="ff_fwd", grid=(S // tm, NS),
        in_specs=[pl.BlockSpec((tm, D), lambda i, k: (i, 0)), wspec, wspec],
        out_specs=[ospec, ospec, ospec],
        out_shape=[_sds((NS, S, f), F32), _sds((NS, S, f), F32), _sds((NS, S, f), BF16)],
        compiler_params=_params(("parallel", "parallel")),
    )(h, wg, wu)


def mm_down(x, ff, wd):
    S, D = x.shape
    NS, _, f = ff.shape
    tm = _tile(S, 512)
    return pl.pallas_call(
        _mm_body(1, NN, 1, NS, True), name="mm_down", grid=(S // tm, NS),
        in_specs=[pl.BlockSpec((None, tm, f), lambda i, k: (k, i, 0)), pl.BlockSpec((None, f, D), lambda i, k: (k, 0, 0)),
                  pl.BlockSpec((tm, D), lambda i, k: (i, 0))],
        out_specs=pl.BlockSpec((tm, D), lambda i, k: (i, 0)),
        out_shape=_sds((S, D), F32), compiler_params=_params(("parallel", "arbitrary")),
    )(ff, wd, x)


def ff_bwd_act(dxb, wd, gate, up, dep):
    S, D = dxb.shape
    NS, f, _ = wd.shape
    tm = _tile(S, 512)

    def body(dx_ref, wd_ref, gate_ref, up_ref, dep_ref, dgate_ref, dup_ref):
        dff = _dot(dx_ref[...], wd_ref[...], NT)
        g = gate_ref[...]
        sg = _sigmoid(g)
        dup_ref[...] = (dff * (g * sg)).astype(dup_ref.dtype)
        dgate_ref[...] = (dff * up_ref[...] * (sg * (1.0 + g * (1.0 - sg)))).astype(dgate_ref.dtype)

    aspec = pl.BlockSpec((None, tm, f), lambda i, k: (k, i, 0))
    return pl.pallas_call(
        body, name="ff_bwd_act", grid=(S // tm, NS),
        in_specs=[pl.BlockSpec((tm, D), lambda i, k: (i, 0)), pl.BlockSpec((None, f, D), lambda i, k: (k, 0, 0)), aspec, aspec,
                  pl.BlockSpec((8, 128), lambda i, k: (0, 0))],
        out_specs=[aspec, aspec],
        out_shape=[_sds((NS, S, f), BF16), _sds((NS, S, f), BF16)],
        compiler_params=_params(("parallel", "parallel")),
    )(dxb, wd, gate, up, dep)


def dw_down(ff, dxb):
    NS, S, f = ff.shape
    D = dxb.shape[1]
    ts = _tile(S, 1024)
    return pl.pallas_call(
        _mm_body(1, TN, 1, S // ts, False), name="dw_down", grid=(NS, S // ts),
        in_specs=[pl.BlockSpec((None, ts, f), lambda k, s: (k, s, 0)), pl.BlockSpec((ts, D), lambda k, s: (s, 0))],
        out_specs=pl.BlockSpec((None, f, D), lambda k, s: (k, 0, 0)),
        out_shape=_sds((NS, f, D), BF16), scratch_shapes=[pltpu.VMEM((f, D), F32)],
        compiler_params=_params(("parallel", "arbitrary")),
    )(ff, dxb)


def dw_gate_up(h, dgate, dup):
    S, D = h.shape
    NS, _, f = dgate.shape
    ts = _tile(S, 512)
    n_red = S // ts

    def body(h_ref, dg_ref, du_ref, og_ref, ou_ref, accg_ref, accu_ref):
        hv = h_ref[...]
        pg = _dot(dg_ref[...], hv, TN)
        pu = _dot(du_ref[...], hv, TN)
        r = pl.program_id(1)

        @pl.when(r == 0)
        def _():
            accg_ref[...] = pg
            accu_ref[...] = pu

        @pl.when(r > 0)
        def _():
            accg_ref[...] += pg
            accu_ref[...] += pu

        @pl.when(r == n_red - 1)
        def _():
            og_ref[...] = accg_ref[...].astype(og_ref.dtype)
            ou_ref[...] = accu_ref[...].astype(ou_ref.dtype)

    aspec = pl.BlockSpec((None, ts, f), lambda k, s: (k, s, 0))
    ospec = pl.BlockSpec((None, f, D), lambda k, s: (k, 0, 0))
    return pl.pallas_call(
        body, name="dw_gate_up", grid=(NS, n_red),
        in_specs=[pl.BlockSpec((ts, D), lambda k, s: (s, 0)), aspec, aspec],
        out_specs=[ospec, ospec],
        out_shape=[_sds((NS, f, D), BF16), _sds((NS, f, D), BF16)],
        scratch_shapes=[pltpu.VMEM((f, D), F32), pltpu.VMEM((f, D), F32)],
        compiler_params=_params(("parallel", "arbitrary")),
    )(h, dgate, dup)


def dh_ff(dgate, dup, wg, wu):
    NS, S, f = dgate.shape
    D = wg.shape[2]
    tm = _tile(S, 512)
    aspec = pl.BlockSpec((None, tm, f), lambda i, k: (k, i, 0))
    wspec = pl.BlockSpec((None, f, D), lambda i, k: (k, 0, 0))
    return pl.pallas_call(
        _mm_body(2, NN, 1, NS, False), name="dh_ff", grid=(S // tm, NS),
        in_specs=[aspec, wspec, aspec, wspec],
        out_specs=pl.BlockSpec((tm, D), lambda i, k: (i, 0)),
        out_shape=_sds((S, D), F32), compiler_params=_params(("parallel", "arbitrary")),
    )(dgate, wg, dup, wu)


def dmix_mm(dxb, w):
    S, D = dxb.shape
    tm = _tile(S, 1024)
    tn = _tile(D, 1024)
    return pl.pallas_call(
        _mm_body(1, NT, None, 1, False), name="dmix_mm", grid=(S // tm, D // tn),
        in_specs=[pl.BlockSpec((tm, D), lambda i, j: (i, 0)), pl.BlockSpec((tn, D), lambda i, j: (j, 0))],
        out_specs=pl.BlockSpec((tm, tn), lambda i, j: (i, j)),
        out_shape=_sds((S, D), F32), compiler_params=_params(("parallel", "parallel")),
    )(dxb, w)


def dw_out(mix_a, mix_g, dxb):
    S, DA = mix_a.shape
    D = dxb.shape[1]
    ts = _tile(S, 1024)

    def half(name, m):
        return pl.pallas_call(
            _mm_body(1, TN, 0, S // ts, False), name=name, grid=(S // ts,),
            in_specs=[pl.BlockSpec((ts, DA), lambda s: (s, 0)), pl.BlockSpec((ts, D), lambda s: (s, 0))],
            out_specs=pl.BlockSpec((DA, D), lambda s: (0, 0)),
            out_shape=_sds((DA, D), BF16), scratch_shapes=[pltpu.VMEM((DA, D), F32)],
            compiler_params=_params(("arbitrary",)),
        )(m, dxb)

    return half("dw_out_a", mix_a), half("dw_out_g", mix_g)


def dw_in(h, dproj, NS):
    S, D = h.shape
    n = dproj.shape[1] // NS
    ts = _tile(S, 1024)
    return pl.pallas_call(
        _mm_body(1, TN, 1, S // ts, False), name="dw_in", grid=(NS, S // ts),
        in_specs=[pl.BlockSpec((ts, D), lambda k, s: (s, 0)), pl.BlockSpec((ts, n), lambda k, s: (s, k))],
        out_specs=pl.BlockSpec((None, D, n), lambda k, s: (k, 0, 0)),
        out_shape=_sds((NS, D, n), BF16), scratch_shapes=[pltpu.VMEM((D, n), F32)],
        compiler_params=_params(("parallel", "arbitrary")),
    )(h, dproj)


def dh_in(dproj, w):
    S = dproj.shape[0]
    NS, D, n = w.shape
    tm = _tile(S, 512)
    return pl.pallas_call(
        _mm_body(1, NT, 1, NS, False), name="dh_in", grid=(S // tm, NS),
        in_specs=[pl.BlockSpec((tm, n), lambda i, k: (i, k)), pl.BlockSpec((None, D, n), lambda i, k: (k, 0, 0))],
        out_specs=pl.BlockSpec((tm, D), lambda i, k: (i, 0)),
        out_shape=_sds((S, D), F32), compiler_params=_params(("parallel", "arbitrary")),
    )(dproj, w)


def rope_tables(S):
    pos = jnp.arange(S, dtype=F32)
    inv = ROPE_THETA ** (-jnp.arange(0, HEAD_DIM, 2, dtype=F32) / HEAD_DIM)
    ang = pos[:, None] * inv[None, :]
    cos, sin = jnp.cos(ang), jnp.sin(ang)
    return jnp.concatenate([cos, cos], axis=-1), jnp.concatenate([-sin, sin], axis=-1)


def _rot_half(t):
    return pltpu.roll(t, HEAD_DIM // 2, 1)


def rope_qkv(proj, cos2, sin2, DA):
    S = proj.shape[0]
    tr = _tile(S, 512)
    nh = DA // HEAD_DIM

    def body(q_ref, k_ref, v_ref, cos_ref, sin_ref, o_ref):
        c = cos_ref[...]
        s = sin_ref[...]
        for h in range(nh):
            sl = slice(h * HEAD_DIM, (h + 1) * HEAD_DIM)
            for j, ref in enumerate((q_ref, k_ref)):
                t = ref[:, sl]
                o_ref[:, j * DA + h * HEAD_DIM:j * DA + (h + 1) * HEAD_DIM] = (t * c + _rot_half(t) * s).astype(o_ref.dtype)
        o_ref[:, 2 * DA:3 * DA] = v_ref[...].astype(o_ref.dtype)

    tab = pl.BlockSpec((tr, HEAD_DIM), lambda i: (i, 0))
    return pl.pallas_call(
        body, name="rope_qkv", grid=(S // tr,),
        in_specs=[pl.BlockSpec((tr, DA), lambda i: (i, 0)), pl.BlockSpec((tr, DA), lambda i: (i, 1)),
                  pl.BlockSpec((tr, DA), lambda i: (i, 2)), tab, tab],
        out_specs=pl.BlockSpec((tr, 3 * DA), lambda i: (i, 0)),
        out_shape=_sds((S, 3 * DA), BF16), compiler_params=_params(("parallel",)),
    )(proj, proj, proj, cos2, sin2)


def assemble_dproj(dqs, dks, dvs, duv, cos2, sin2):
    S, DA = dqs[0].shape
    tr = _tile(S, 256)
    nh = DA // HEAD_DIM

    def body(*refs):
        dq_refs, dk_refs, dv_refs = refs[0:3], refs[3:6], refs[6:9]
        duv_ref, cos_ref, sin_ref, o_ref = refs[9:13]
        c = cos_ref[...]
        s = sin_ref[...]
        for j, trio in enumerate((dq_refs, dk_refs)):
            for h in range(nh):
                sl = slice(h * HEAD_DIM, (h + 1) * HEAD_DIM)
                t = trio[0][:, sl] + trio[1][:, sl] + trio[2][:, sl]
                o_ref[:, j * DA + h * HEAD_DIM:j * DA + (h + 1) * HEAD_DIM] = (t * c - _rot_half(t) * s).astype(o_ref.dtype)
        o_ref[:, 2 * DA:3 * DA] = (dv_refs[0][...] + dv_refs[1][...] + dv_refs[2][...]).astype(o_ref.dtype)
        o_ref[:, 3 * DA:5 * DA] = duv_ref[...]

    blk = pl.BlockSpec((tr, DA), lambda i: (i, 0))
    tab = pl.BlockSpec((tr, HEAD_DIM), lambda i: (i, 0))
    return pl.pallas_call(
        body, name="assemble_dproj", grid=(S // tr,),
        in_specs=[blk] * 9 + [pl.BlockSpec((tr, 2 * DA), lambda i: (i, 0)), tab, tab],
        out_specs=pl.BlockSpec((tr, 5 * DA), lambda i: (i, 0)),
        out_shape=_sds((S, 5 * DA), BF16), compiler_params=_params(("parallel",)),
    )(*dqs, *dks, *dvs, duv, cos2, sin2)


def _halo_specs(L, width_blocks, col):
    per = TQ // HALO
    last = L // HALO - 1
    W = width_blocks
    return [
        pl.BlockSpec((HALO, W), lambda r, i: (jnp.maximum(i * per - 1, 0), col(r))),
        pl.BlockSpec((TQ, W), lambda r, i: (i, col(r))),
        pl.BlockSpec((HALO, W), lambda r, i: (jnp.minimum((i + 1) * per, last), col(r))),
    ]


def _cat3(refs, sl):
    return jnp.concatenate([refs[0][:, sl], refs[1][:, sl], refs[2][:, sl]], axis=0)


def attn_fwd(qkv, d, DA):
    S = qkv.shape[0]
    L = S // d
    nh = DA // HEAD_DIM
    scale = HEAD_DIM ** -0.5
    TK = TQ + 2 * HALO

    def body(q_ref, kp_ref, kc_ref, kn_ref, vp_ref, vc_ref, vn_ref, o_ref, lse_ref):
        i = pl.program_id(1)
        row = lax.broadcasted_iota(jnp.int32, (TQ, TK), 0)
        col = lax.broadcasted_iota(jnp.int32, (TQ, TK), 1)
        kpos = i * TQ - HALO + col
        mask = (jnp.abs(col - HALO - row) <= N_SIDE) & (kpos >= 0) & (kpos < L)
        for h in range(nh):
            sl = slice(h * HEAD_DIM, (h + 1) * HEAD_DIM)
            k = _cat3((kp_ref, kc_ref, kn_ref), sl)
            v = _cat3((vp_ref, vc_ref, vn_ref), sl)
            s = _dot(q_ref[:, sl], k, NT) * scale
            s = jnp.where(mask, s, NEG)
            m = jnp.max(s, axis=-1, keepdims=True)
            p = jnp.exp(s - m)
            l = jnp.sum(p, axis=-1, keepdims=True)
            o = _dot(p.astype(v.dtype), v, NN) / l
            o_ref[:, sl] = o
            lse_ref[:, sl] = jnp.broadcast_to(m + jnp.log(l), (TQ, HEAD_DIM))

    out_spec = pl.BlockSpec((TQ, DA), lambda r, i: (i, r))
    o, lse = pl.pallas_call(
        body, name="attn_fwd_d%d" % d, grid=(d, L // TQ),
        in_specs=[pl.BlockSpec((TQ, DA), lambda r, i: (i, 3 * r))]
        + _halo_specs(L, DA, lambda r: 3 * r + 1) + _halo_specs(L, DA, lambda r: 3 * r + 2),
        out_specs=[out_spec, out_spec],
        out_shape=[_sds((L, d * DA), F32), _sds((L, d * DA), F32)],
        compiler_params=_params(("parallel", "parallel")),
    )(*([qkv.reshape(L, d * 3 * DA)] * 7))
    return o.reshape(S, DA), lse.reshape(S, DA)


def attn_merge(os_, lses, g):
    S, DA = os_[0].shape
    tr = _tile(S, 256)

    def body(o0, o1, o2, l0, l1, l2, g_ref, a_ref, lse_ref, mix_ref):
        a0, a1, a2 = l0[...], l1[...], l2[...]
        m = jnp.maximum(jnp.maximum(a0, a1), a2)
        e0, e1, e2 = jnp.exp(a0 - m), jnp.exp(a1 - m), jnp.exp(a2 - m)
        den = e0 + e1 + e2
        a = (e0 / den) * o0[...] + (e1 / den) * o1[...] + (e2 / den) * o2[...]
        a_ref[...] = a
        lse_ref[...] = m + jnp.log(den)
        r = lax.rsqrt(jnp.mean(a * a, axis=-1, keepdims=True) + EPS)
        mix_ref[...] = (a * r * g_ref[...]).astype(mix_ref.dtype)

    blk = pl.BlockSpec((tr, DA), lambda i: (i, 0))
    return pl.pallas_call(
        body, name="attn_merge", grid=(S // tr,),
        in_specs=[blk] * 6 + [pl.BlockSpec((1, DA), lambda i: (0, 0))],
        out_specs=[blk, blk, blk],
        out_shape=[_sds((S, DA), F32), _sds((S, DA), F32), _sds((S, DA), BF16)],
        compiler_params=_params(("parallel",)),
    )(*os_, *lses, g)


def attn_out_bwd(a, g, dmix):
    S, DA = a.shape
    tr = _tile(S, 256)
    nh = DA // HEAD_DIM

    def body(a_ref, g_ref, dy_ref, do_ref, dl_ref, dg_ref):
        av = a_ref[...]
        dx, dg = _rms_bwd_math(av, g_ref[...], dy_ref[...])
        do_ref[...] = dx.astype(do_ref.dtype)
        prod = dx * av
        for h in range(nh):
            sl = slice(h * HEAD_DIM, (h + 1) * HEAD_DIM)
            dl_ref[:, sl] = jnp.broadcast_to(jnp.sum(prod[:, sl], axis=-1, keepdims=True), (tr, HEAD_DIM))

        @pl.when(pl.program_id(0) == 0)
        def _():
            dg_ref[...] = dg

        @pl.when(pl.program_id(0) > 0)
        def _():
            dg_ref[...] += dg

    blk = pl.BlockSpec((tr, DA), lambda i: (i, 0))
    vec = pl.BlockSpec((1, DA), lambda i: (0, 0))
    return pl.pallas_call(
        body, name="attn_out_bwd", grid=(S // tr,),
        in_specs=[blk, vec, blk], out_specs=[blk, blk, vec],
        out_shape=[_sds((S, DA), BF16), _sds((S, DA), F32), _sds((1, DA), F32)],
        compiler_params=_params(("arbitrary",)),
    )(a, g, dmix)


def attn_bwd_dq(qkv, dob, lse, dl, d, DA):
    S = qkv.shape[0]
    L = S // d
    nh = DA // HEAD_DIM
    scale = HEAD_DIM ** -0.5
    TK = TQ + 2 * HALO

    def body(q_ref, kp_ref, kc_ref, kn_ref, vp_ref, vc_ref, vn_ref, do_ref, lse_ref, dl_ref, dq_ref):
        i = pl.program_id(1)
        row = lax.broadcasted_iota(jnp.int32, (TQ, TK), 0)
        col = lax.broadcasted_iota(jnp.int32, (TQ, TK), 1)
        kpos = i * TQ - HALO + col
        mask = (jnp.abs(col - HALO - row) <= N_SIDE) & (kpos >= 0) & (kpos < L)
        for h in range(nh):
            sl = slice(h * HEAD_DIM, (h + 1) * HEAD_DIM)
            k = _cat3((kp_ref, kc_ref, kn_ref), sl)
            v = _cat3((vp_ref, vc_ref, vn_ref), sl)
            s = _dot(q_ref[:, sl], k, NT) * scale
            s = jnp.where(mask, s, NEG)
            p = jnp.exp(s - lse_ref[:, sl][:, 0:1])
            dp = _dot(do_ref[:, sl], v, NT)
            ds = p * (dp - dl_ref[:, sl][:, 0:1])
            dq_ref[:, sl] = _dot(ds.astype(k.dtype), k, NN) * scale

    blk = pl.BlockSpec((TQ, DA), lambda r, i: (i, r))
    dq = pl.pallas_call(
        body, name="attn_bwd_dq_d%d" % d, grid=(d, L // TQ),
        in_specs=[pl.BlockSpec((TQ, DA), lambda r, i: (i, 3 * r))]
        + _halo_specs(L, DA, lambda r: 3 * r + 1) + _halo_specs(L, DA, lambda r: 3 * r + 2) + [blk, blk, blk],
        out_specs=blk, out_shape=_sds((L, d * DA), F32),
        compiler_params=_params(("parallel", "parallel")),
    )(*([qkv.reshape(L, d * 3 * DA)] * 7), dob.reshape(L, d * DA), lse.reshape(L, d * DA), dl.reshape(L, d * DA))
    return dq.reshape(S, DA)


def attn_bwd_dkv(qkv, dob, lse, dl, d, DA):
    S = qkv.shape[0]
    L = S // d
    nh = DA // HEAD_DIM
    scale = HEAD_DIM ** -0.5
    TR = TQ + 2 * HALO

    def body(k_ref, v_ref, qp, qc, qn, dop, doc, don, lp, lc, ln, dp_, dc_, dn_, dk_ref, dv_ref):
        j = pl.program_id(1)
        row = lax.broadcasted_iota(jnp.int32, (TR, TQ), 0)
        col = lax.broadcasted_iota(jnp.int32, (TR, TQ), 1)
        qpos = j * TQ - HALO + row
        mask = (jnp.abs(col - (row - HALO)) <= N_SIDE) & (qpos >= 0) & (qpos < L)
        for h in range(nh):
            sl = slice(h * HEAD_DIM, (h + 1) * HEAD_DIM)
            q = _cat3((qp, qc, qn), sl)
            do = _cat3((dop, doc, don), sl)
            lse_q = _cat3((lp, lc, ln), sl)
            dl_q = _cat3((dp_, dc_, dn_), sl)
            k = k_ref[:, sl]
            v = v_ref[:, sl]
            s = _dot(q, k, NT) * scale
            s = jnp.where(mask, s, NEG)
            p = jnp.exp(s - lse_q)
            dv_ref[:, sl] = _dot(p.astype(do.dtype), do, TN)
            dp = _dot(do, v, NT)
            ds = p * (dp - dl_q)
            dk_ref[:, sl] = _dot(ds.astype(q.dtype), q, TN) * scale

    blk = pl.BlockSpec((TQ, DA), lambda r, i: (i, r))
    ident = lambda r: r
    dk, dv = pl.pallas_call(
        body, name="attn_bwd_dkv_d%d" % d, grid=(d, L // TQ),
        in_specs=[pl.BlockSpec((TQ, DA), lambda r, i: (i, 3 * r + 1)), pl.BlockSpec((TQ, DA), lambda r, i: (i, 3 * r + 2))]
        + _halo_specs(L, DA, lambda r: 3 * r) + _halo_specs(L, DA, ident) + _halo_specs(L, DA, ident) + _halo_specs(L, DA, ident),
        out_specs=[blk, blk], out_shape=[_sds((L, d * DA), F32), _sds((L, d * DA), F32)],
        compiler_params=_params(("parallel", "parallel")),
    )(*([qkv.reshape(L, d * 3 * DA)] * 5), *([dob.reshape(L, d * DA)] * 3), *([lse.reshape(L, d * DA)] * 3),
      *([dl.reshape(L, d * DA)] * 3))
    return dk.reshape(S, DA), dv.reshape(S, DA)


def _gmlp_fwd_math(u_raw, v_raw, ln_g, ws_ref, bcol_ref, n_chunks, ng):
    ug = _gelu(u_raw)
    vg = _gelu(v_raw)
    mu = jnp.mean(vg, axis=-1, keepdims=True)
    xc = vg - mu
    rstd = lax.rsqrt(jnp.mean(xc * xc, axis=-1, keepdims=True) + EPS)
    xhat = xc * rstd
    vn = xhat * ln_g
    rows = []
    for n in range(n_chunks):
        cols = []
        for g in range(ng):
            blk = vn[n * CHUNK:(n + 1) * CHUNK, g * GROUP:(g + 1) * GROUP]
            cols.append(_dot(ws_ref[g].astype(BF16), blk.astype(BF16), NN) + bcol_ref[g])
        rows.append(jnp.concatenate(cols, axis=1))
    mixed = jnp.concatenate(rows, axis=0)
    return ug, xhat, rstd, vn, mixed


def gmlp_fwd(proj, ln_g, w_s, bcol, g_out, DA):
    S = proj.shape[0]
    ng = DA // GROUP
    tr = _tile(S, 2 * CHUNK)

    def body(u_ref, v_ref, ln_ref, ws_ref, bcol_ref, g_ref, G_ref, mix_ref):
        ug, _, _, _, mixed = _gmlp_fwd_math(u_ref[...], v_ref[...], ln_ref[...], ws_ref, bcol_ref, tr // CHUNK, ng)
        G = ug * mixed
        G_ref[...] = G
        r = lax.rsqrt(jnp.mean(G * G, axis=-1, keepdims=True) + EPS)
        mix_ref[...] = (G * r * g_ref[...]).astype(mix_ref.dtype)

    vec = pl.BlockSpec((1, DA), lambda i: (0, 0))
    par = pl.BlockSpec((ng, CHUNK, CHUNK), lambda i: (0, 0, 0))
    blk = pl.BlockSpec((tr, DA), lambda i: (i, 0))
    return pl.pallas_call(
        body, name="gmlp_fwd", grid=(S // tr,),
        in_specs=[pl.BlockSpec((tr, DA), lambda i: (i, 3)), pl.BlockSpec((tr, DA), lambda i: (i, 4)), vec, par, par, vec],
        out_specs=[blk, blk], out_shape=[_sds((S, DA), F32), _sds((S, DA), BF16)],
        compiler_params=_params(("parallel",)),
    )(proj, proj, ln_g, w_s, bcol, g_out)


def gmlp_bwd(proj, ln_g, w_s, w_st, bcol, g_out, dmix, DA):
    S = proj.shape[0]
    ng = DA // GROUP
    tr = _tile(S, 2 * CHUNK)
    nc = tr // CHUNK

    def body(u_ref, v_ref, ln_ref, ws_ref, wst_ref, bcol_ref, g_ref, dy_ref, duv_ref, dg_ref, dln_ref, dws_ref, db_ref):
        u_raw = u_ref[...]
        v_raw = v_ref[...]
        ln_g_v = ln_ref[...]
        ug, xhat, rstd, vn, mixed = _gmlp_fwd_math(u_raw, v_raw, ln_g_v, ws_ref, bcol_ref, nc, ng)
        G = ug * mixed
        dG, dg = _rms_bwd_math(G, g_ref[...], dy_ref[...])
        du_g = dG * mixed
        dmixed = dG * ug
        dws, dbs, rows = [], [], []
        for g in range(ng):
            dws.append(None)
            dbs.append(None)
        for n in range(nc):
            cols = []
            for g in range(ng):
                dm = dmixed[n * CHUNK:(n + 1) * CHUNK, g * GROUP:(g + 1) * GROUP]
                vb = vn[n * CHUNK:(n + 1) * CHUNK, g * GROUP:(g + 1) * GROUP]
                dmb = dm.astype(BF16)
                w = _dot(dmb, vb.astype(BF16), NT)
                b = jnp.broadcast_to(jnp.sum(dm, axis=-1, keepdims=True), (CHUNK, GROUP))
                dws[g] = w if dws[g] is None else dws[g] + w
                dbs[g] = b if dbs[g] is None else dbs[g] + b
                cols.append(_dot(wst_ref[g].astype(BF16), dmb, NN))
            rows.append(jnp.concatenate(cols, axis=1))
        dvn = jnp.concatenate(rows, axis=0)
        dln = jnp.sum(dvn * xhat, axis=0, keepdims=True)
        dxh = dvn * ln_g_v
        dvg = rstd * (dxh - jnp.mean(dxh, axis=-1, keepdims=True) - xhat * jnp.mean(dxh * xhat, axis=-1, keepdims=True))
        duv_ref[:, 0:DA] = (du_g * _gelu_grad(u_raw)).astype(duv_ref.dtype)
        duv_ref[:, DA:2 * DA] = (dvg * _gelu_grad(v_raw)).astype(duv_ref.dtype)

        first = pl.program_id(0) == 0

        @pl.when(first)
        def _():
            dg_ref[...] = dg
            dln_ref[...] = dln
            for g in range(ng):
                dws_ref[g] = dws[g]
                db_ref[g] = dbs[g]

        @pl.when(jnp.logical_not(first))
        def _():
            dg_ref[...] += dg
            dln_ref[...] += dln
            for g in range(ng):
                dws_ref[g] += dws[g]
                db_ref[g] += dbs[g]

    vec = pl.BlockSpec((1, DA), lambda i: (0, 0))
    par = pl.BlockSpec((ng, CHUNK, CHUNK), lambda i: (0, 0, 0))
    return pl.pallas_call(
        body, name="gmlp_bwd", grid=(S // tr,),
        in_specs=[pl.BlockSpec((tr, DA), lambda i: (i, 3)), pl.BlockSpec((tr, DA), lambda i: (i, 4)), vec, par, par, par, vec,
                  pl.BlockSpec((tr, DA), lambda i: (i, 1))],
        out_specs=[pl.BlockSpec((tr, 2 * DA), lambda i: (i, 0)), vec, vec, par, par],
        out_shape=[_sds((S, 2 * DA), BF16), _sds((1, DA), F32), _sds((1, DA), F32),
                   _sds((ng, CHUNK, CHUNK), F32), _sds((ng, CHUNK, CHUNK), F32)],
        compiler_params=_params(("arbitrary",)),
    )(proj, proj, ln_g, w_s, w_st, bcol, g_out, dmix)


def layer_fwd(x, sm, gw, tabs):
    D = x.shape[1]
    DA = D // 2
    cos2, sin2 = tabs
    h1 = rms_fwd(x, sm["norm1_g"])
    proj = mm_in(h1, gw["w_in"])
    qkv = rope_qkv(proj, cos2, sin2, DA)
    os_, lses = [], []
    for d in DILATIONS:
        o, l = attn_fwd(qkv, d, DA)
        os_.append(o)
        lses.append(l)
    a, lse, mix_a = attn_merge(os_, lses, sm["mix_norm_attn_g"])
    _, mix_g = gmlp_fwd(proj, sm["gmlp_ln_g"], sm["w_spatial"], sm["bcol"], sm["mix_norm_gmlp_g"], DA)
    x2 = mm_out(x, mix_a, mix_g, gw["w_out"])
    h2 = rms_fwd(x2, sm["norm2_g"])
    gate, up, ff = ff_fwd(h2, gw["w_gate"], gw["w_up"])
    x3 = mm_down(x2, ff, gw["w_down"])
    saved = dict(x=x, h1=h1, proj=proj, qkv=qkv, a=a, lse=lse, mix_a=mix_a, mix_g=mix_g, x2=x2, h2=h2, gate=gate, up=up, ff=ff)
    return x3, saved


def layer_bwd(dx, dxb, sv, sm, gw, tabs, dep):
    D = dx.shape[1]
    DA = D // 2
    cos2, sin2 = tabs
    NS = gw["w_in"].shape[0]
    dgate, dup = ff_bwd_act(dxb, gw["w_down"], sv["gate"], sv["up"], dep)
    g_down = dw_down(sv["ff"], dxb)
    g_gate, g_up = dw_gate_up(sv["h2"], dgate, dup)
    dh2 = dh_ff(dgate, dup, gw["w_gate"], gw["w_up"])
    dx2, dx2b, d_norm2 = rms_bwd_res(sv["x2"], sm["norm2_g"], dh2, dx)
    dmix = dmix_mm(dx2b, gw["w_out"].reshape(D, D))
    g_out_a, g_out_g = dw_out(sv["mix_a"], sv["mix_g"], dx2b)
    dob, dl, d_mix_a = attn_out_bwd(sv["a"], sm["mix_norm_attn_g"], dmix)
    duv, d_mix_g, d_ln, d_ws, d_bs = gmlp_bwd(sv["proj"], sm["gmlp_ln_g"], sm["w_spatial"], sm["w_spatial_t"], sm["bcol"],
                                              sm["mix_norm_gmlp_g"], dmix, DA)
    dqs, dks, dvs = [], [], []
    for d in DILATIONS:
        dqs.append(attn_bwd_dq(sv["qkv"], dob, sv["lse"], dl, d, DA))
        dk, dv = attn_bwd_dkv(sv["qkv"], dob, sv["lse"], dl, d, DA)
        dks.append(dk)
        dvs.append(dv)
    dproj = assemble_dproj(dqs, dks, dvs, duv, cos2, sin2)
    g_in = dw_in(sv["h1"], dproj, NS)
    dh1 = dh_in(dproj, gw["w_in"])
    dx0, dx0b, d_norm1 = rms_bwd_res(sv["x"], sm["norm1_g"], dh1, dx2)
    big = dict(w_in=g_in, w_out=jnp.concatenate([g_out_a, g_out_g], axis=0), w_gate=g_gate, w_up=g_up, w_down=g_down)
    small = dict(norm1_g=d_norm1, gmlp_ln_g=d_ln, w_spatial=d_ws, b_spatial=d_bs[:, :, 0], mix_norm_attn_g=d_mix_a,
                 mix_norm_gmlp_g=d_mix_g, norm2_g=d_norm2)
    return dx0, dx0b, big, small


def _place():
    x, y, c = lax.axis_index("x"), lax.axis_index("y"), lax.axis_index("c")
    return x, y, c, [(1 - x, y), (x, 1 - y), (1 - x, 1 - y)]


def _remote(src, dst, send_sems, recv_sems, k, to):
    return pltpu.make_async_remote_copy(src_ref=src, dst_ref=dst, send_sem=send_sems.at[k], recv_sem=recv_sems.at[k],
                                        device_id=to, device_id_type=MESH)


def gather_ici(shards):
    n = len(shards)

    def body(*refs):
        ins, outs = refs[:n], refs[n:2 * n]
        send_sems, recv_sems = refs[2 * n:]
        x, y, c, chips = _place()
        me = 4 * x + 2 * y + c
        sent = []
        for a in range(n):
            for j, (px, py) in enumerate(chips):
                cp = _remote(ins[a], outs[a].at[me], send_sems, recv_sems, 3 * a + j, (px, py, c))
                cp.start()
                sent.append(cp)
        for a in range(n):
            for j, (px, py) in enumerate(chips):
                slot = outs[a].at[4 * px + 2 * py + c]
                _remote(slot, slot, send_sems, recv_sems, 3 * a + j, (px, py, c)).wait_recv()
        for cp in sent:
            cp.wait_send()

    return pl.pallas_call(
        body, name="gather_ici", in_specs=[ANY] * n, out_specs=[ANY] * n,
        out_shape=[_sds((N_DEV,) + s.shape, s.dtype) for s in shards],
        scratch_shapes=[pltpu.SemaphoreType.DMA((3 * n,)), pltpu.SemaphoreType.DMA((3 * n,))],
    )(*shards)


HBM_SPEC = pl.BlockSpec(memory_space=pltpu.HBM)
SEM_SPEC = pl.BlockSpec(memory_space=pltpu.SEMAPHORE)
DATAFLOW = pltpu.SideEffectType.DATAFLOW_SIDE_EFFECTING
TOKEN = (8, 128)


def _in_hbm(a):
    return pltpu.with_memory_space_constraint(a, pltpu.HBM)


def gather_ici_start(shards):
    n = len(shards)
    lands = [lax.empty((N_DEV,) + s.shape, s.dtype) for s in shards]

    def body(*refs):
        ins, dsts = refs[:n], refs[n:2 * n]
        send_sems, recv_sems = refs[2 * n], refs[2 * n + 1]
        token = refs[-1]
        x, y, c, chips = _place()
        me = 4 * x + 2 * y + c
        for a in range(n):
            for j, (px, py) in enumerate(chips):
                _remote(ins[a], dsts[a].at[me], send_sems, recv_sems, 3 * a + j, (px, py, c)).start()
        token[...] = jnp.zeros_like(token)

    both = list(shards) + lands
    out = pl.pallas_call(
        body, name="gather_ici_start",
        out_shape=(pltpu.SemaphoreType.DMA((3 * n,)), pltpu.SemaphoreType.DMA((3 * n,)),
                   *[pltpu.HBM(b.shape, b.dtype) for b in both], _sds(TOKEN, F32)),
        in_specs=[HBM_SPEC] * (2 * n),
        out_specs=(SEM_SPEC, SEM_SPEC, *[HBM_SPEC] * (2 * n), pl.BlockSpec(memory_space=pltpu.VMEM)),
        input_output_aliases={i: 2 + i for i in range(2 * n)},
        compiler_params=pltpu.CompilerParams(has_side_effects=DATAFLOW),
    )(*[_in_hbm(b) for b in both])
    return out[0], out[1], list(out[2:2 + n]), list(out[2 + n:2 + 2 * n]), out[-1]


def gather_ici_wait(send_sems, recv_sems, shards, lands, after):
    n = len(shards)

    def body(*refs):
        ins, dsts = refs[:n], refs[n:2 * n]
        send_sems, recv_sems = refs[2 * n], refs[2 * n + 1]
        x, y, c, chips = _place()
        for a in range(n):
            for j, (px, py) in enumerate(chips):
                cp = _remote(ins[a], dsts[a].at[4 * px + 2 * py + c], send_sems, recv_sems, 3 * a + j, (px, py, c))
                cp.wait_send()
                cp.wait_recv()

    both = list(shards) + list(lands)
    out = pl.pallas_call(
        body, name="gather_ici_wait",
        out_shape=[pltpu.HBM(b.shape, b.dtype) for b in both],
        in_specs=[HBM_SPEC] * (2 * n) + [SEM_SPEC, SEM_SPEC, ANY], out_specs=[HBM_SPEC] * (2 * n),
        input_output_aliases={i: i for i in range(2 * n)},
        compiler_params=pltpu.CompilerParams(has_side_effects=DATAFLOW),
    )(*both, send_sems, recv_sems, after)
    return list(out[:n]), list(out[n:2 * n])


def gather_d2d(bufs, shards):
    n = len(bufs)

    def body(*refs):
        own, outs = refs[n:2 * n], refs[2 * n:3 * n]
        token = refs[3 * n]
        send_sems, recv_sems, local_sems = refs[3 * n + 1:]
        x, y, c, chips = _place()
        me = 4 * x + 2 * y + c
        sibling = (x, y, 1 - c)
        local = [pltpu.make_async_copy(own[a], outs[a].at[me], local_sems.at[a]) for a in range(n)]
        for cp in local:
            cp.start()
        sent = []
        for a in range(n):
            for j, (px, py) in enumerate(chips):
                slot = outs[a].at[4 * px + 2 * py + c]
                sent.append(_remote(slot, slot, send_sems, recv_sems, N_CHIP * a + j, sibling))
            sent.append(_remote(own[a], outs[a].at[me], send_sems, recv_sems, N_CHIP * a + 3, sibling))
        for cp in sent:
            cp.start()
        for a in range(n):
            for j, (px, py) in enumerate(chips):
                slot = outs[a].at[4 * px + 2 * py + 1 - c]
                _remote(slot, slot, send_sems, recv_sems, N_CHIP * a + j, sibling).wait_recv()
            slot = outs[a].at[4 * x + 2 * y + 1 - c]
            _remote(slot, slot, send_sems, recv_sems, N_CHIP * a + 3, sibling).wait_recv()
        for cp in sent:
            cp.wait_send()
        for cp in local:
            cp.wait()
        token[...] = jnp.zeros_like(token)

    out = pl.pallas_call(
        body, name="gather_d2d", in_specs=[ANY] * (2 * n),
        out_specs=[ANY] * n + [pl.BlockSpec(memory_space=pltpu.VMEM)],
        out_shape=[_sds(b.shape, b.dtype) for b in bufs] + [_sds(TOKEN, F32)],
        input_output_aliases={a: a for a in range(n)},
        scratch_shapes=[pltpu.SemaphoreType.DMA((N_CHIP * n,)), pltpu.SemaphoreType.DMA((N_CHIP * n,)),
                        pltpu.SemaphoreType.DMA((n,))],
    )(*bufs, *shards)
    return list(out[:n]), out[n]


def all_gather(shards):
    return gather_d2d(gather_ici(shards), shards)[0]


def scatter_d2d(parts):
    n = len(parts)

    def body(*refs):
        ins, outs = refs[:n], refs[n:2 * n]
        send_sems, recv_sems = refs[2 * n:]
        x, y, c, _ = _place()
        sent = []
        for a in range(n):
            for j in range(N_CHIP):
                cp = _remote(ins[a].at[2 * j + 1 - c], outs[a].at[j], send_sems, recv_sems, N_CHIP * a + j, (x, y, 1 - c))
                cp.start()
                sent.append(cp)
        for a in range(n):
            for j in range(N_CHIP):
                slot = outs[a].at[j]
                _remote(slot, slot, send_sems, recv_sems, N_CHIP * a + j, (x, y, 1 - c)).wait_recv()
        for cp in sent:
            cp.wait_send()

    return pl.pallas_call(
        body, name="scatter_d2d", in_specs=[ANY] * n, out_specs=[ANY] * n,
        out_shape=[_sds((N_CHIP,) + p.shape[1:], p.dtype) for p in parts],
        scratch_shapes=[pltpu.SemaphoreType.DMA((N_CHIP * n,)), pltpu.SemaphoreType.DMA((N_CHIP * n,))],
    )(*parts)


def pair_sum(part, got, c_arr):
    _, R, C = part.shape
    tr = _tile(R, 512)
    p4 = part.reshape(N_CHIP, 2, R, C)

    def body(c_ref, p_ref, g_ref, o_ref):
        o_ref[...] = (p_ref[...].astype(F32) + g_ref[...].astype(F32)).astype(o_ref.dtype)

    return pl.pallas_call(
        body, name="pair_sum",
        grid_spec=pltpu.PrefetchScalarGridSpec(
            num_scalar_prefetch=1, grid=(N_CHIP, R // tr),
            in_specs=[pl.BlockSpec((None, None, tr, C), lambda j, i, c_ref: (j, c_ref[0], i, 0)),
                      pl.BlockSpec((None, tr, C), lambda j, i, c_ref: (j, i, 0))],
            out_specs=pl.BlockSpec((None, tr, C), lambda j, i, c_ref: (j, i, 0))),
        out_shape=_sds((N_CHIP, R, C), part.dtype), compiler_params=_params(("parallel", "parallel")),
    )(c_arr, p4, got)


def scatter_ici_start(sums):
    n = len(sums)
    lands = [lax.empty(s.shape, s.dtype) for s in sums]

    def body(*refs):
        ins, dsts = refs[:n], refs[n:2 * n]
        send_sems, recv_sems = refs[2 * n], refs[2 * n + 1]
        token = refs[-1]
        x, y, c, chips = _place()
        mine = 2 * x + y
        for a in range(n):
            for j, (px, py) in enumerate(chips):
                _remote(ins[a].at[2 * px + py], dsts[a].at[mine], send_sems, recv_sems, 3 * a + j, (px, py, c)).start()
        token[...] = jnp.zeros_like(token)

    both = list(sums) + lands
    out = pl.pallas_call(
        body, name="scatter_ici_start",
        out_shape=(pltpu.SemaphoreType.DMA((3 * n,)), pltpu.SemaphoreType.DMA((3 * n,)),
                   *[pltpu.HBM(b.shape, b.dtype) for b in both], _sds(TOKEN, F32)),
        in_specs=[HBM_SPEC] * (2 * n),
        out_specs=(SEM_SPEC, SEM_SPEC, *[HBM_SPEC] * (2 * n), pl.BlockSpec(memory_space=pltpu.VMEM)),
        input_output_aliases={i: 2 + i for i in range(2 * n)},
        compiler_params=pltpu.CompilerParams(has_side_effects=DATAFLOW),
    )(*[_in_hbm(b) for b in both])
    return out[0], out[1], list(out[2:2 + n]), list(out[2 + n:2 + 2 * n]), out[-1]


def scatter_ici_wait(send_sems, recv_sems, sums, lands, after):
    n = len(sums)

    def body(*refs):
        ins, dsts = refs[:n], refs[n:2 * n]
        send_sems, recv_sems = refs[2 * n], refs[2 * n + 1]
        x, y, c, chips = _place()
        for a in range(n):
            for j, (px, py) in enumerate(chips):
                cp = _remote(ins[a].at[2 * px + py], dsts[a].at[2 * px + py], send_sems, recv_sems, 3 * a + j, (px, py, c))
                cp.wait_send()
                cp.wait_recv()

    both = list(sums) + list(lands)
    out = pl.pallas_call(
        body, name="scatter_ici_wait",
        out_shape=[pltpu.HBM(b.shape, b.dtype) for b in both],
        in_specs=[HBM_SPEC] * (2 * n) + [SEM_SPEC, SEM_SPEC, ANY], out_specs=[HBM_SPEC] * (2 * n),
        input_output_aliases={i: i for i in range(2 * n)},
        compiler_params=pltpu.CompilerParams(has_side_effects=DATAFLOW),
    )(*both, send_sems, recv_sems, after)
    return list(out[:n]), list(out[n:2 * n])


def own_slot_copy(sums, lands):
    n = len(sums)

    def body(*refs):
        ins, outs = refs[:n], refs[2 * n:3 * n]
        sems = refs[3 * n]
        mine = 2 * lax.axis_index("x") + lax.axis_index("y")
        local = [pltpu.make_async_copy(ins[a].at[mine], outs[a].at[mine], sems.at[a]) for a in range(n)]
        for cp in local:
            cp.start()
        for cp in local:
            cp.wait()

    return pl.pallas_call(
        body, name="own_slot_copy", in_specs=[ANY] * (2 * n), out_specs=[ANY] * n,
        out_shape=[_sds(b.shape, b.dtype) for b in lands], input_output_aliases={n + a: a for a in range(n)},
        scratch_shapes=[pltpu.SemaphoreType.DMA((n,))],
    )(*sums, *lands)


def adamw(w, m, v, parts_per_layer):
    NL, R, C = w.shape
    P = parts_per_layer[0].shape[0]
    tr = _tile(R, max(16, ADAMW_BLOCK_ELEMS // C))
    nb = R // tr

    def body(w_ref, m_ref, v_ref, *rest):
        part_refs = rest[:NL]
        g_ref, d_ref, nm_ref, nv_ref = rest[NL:]
        layer = pl.program_id(0)
        g = None
        for q in range(NL):
            s = part_refs[q][0].astype(F32)
            for t in range(1, P):
                s = s + part_refs[q][t].astype(F32)
            s = jnp.where(layer == q, s, 0.0)
            g = s if g is None else g + s
        wv = w_ref[...]
        nm = ADAM_B1 * m_ref[...] + (1.0 - ADAM_B1) * g
        nv = ADAM_B2 * v_ref[...] + (1.0 - ADAM_B2) * (g * g)
        m_hat = nm / (1.0 - ADAM_B1 ** ADAM_STEP)
        v_hat = nv / (1.0 - ADAM_B2 ** ADAM_STEP)
        g_ref[...] = g
        d_ref[...] = -ADAM_LR * (m_hat / (jnp.sqrt(v_hat) + ADAM_EPS) + ADAM_WD * wv)
        nm_ref[...] = nm
        nv_ref[...] = nv

    def part_spec(q):
        return pl.BlockSpec((P, tr, C), lambda l, i: (0, jnp.where(l == q, i, jnp.where(l < q, 0, nb - 1)), 0))

    blk = pl.BlockSpec((None, tr, C), lambda l, i: (l, i, 0))
    return pl.pallas_call(
        body, name="adamw", grid=(NL, nb),
        in_specs=[blk, blk, blk] + [part_spec(q) for q in range(NL)],
        out_specs=[blk, blk, blk, blk], out_shape=[_sds((NL, R, C), F32)] * 4,
        compiler_params=_params(("arbitrary", "arbitrary")),
    )(w, m, v, *parts_per_layer)


SMALL = ("norm1_g", "gmlp_ln_g", "w_spatial", "b_spatial", "mix_norm_attn_g", "mix_norm_gmlp_g", "norm2_g")
BIG = ("w_in", "w_out", "w_gate", "w_up", "w_down")
TRANSPOSED = ("w_gate", "w_up")
LANES = 128


def _pack(layers, final):
    flat = [layer[n].reshape(-1) for layer in layers for n in SMALL] + [final.reshape(-1)]
    return jnp.concatenate(flat).reshape(-1, LANES)


def _unpack(packed, like_layers, like_final):
    flat = packed.reshape(-1)
    out, off = [], 0
    for layer in like_layers:
        d = {}
        for n in SMALL:
            size = layer[n].size
            d[n] = flat[off:off + size].reshape(layer[n].shape)
            off += size
        out.append(d)
    return out, flat[off:off + like_final.size].reshape(like_final.shape)


def kernel(x, norm1_g, w_in, gmlp_ln_g, w_spatial, b_spatial, mix_norm_attn_g, mix_norm_gmlp_g, w_out, norm2_g, w_gate, w_up, w_down, final_g, loss_target, m_norm1_g, m_w_in, m_gmlp_ln_g, m_w_spatial, m_b_spatial, m_mix_norm_attn_g, m_mix_norm_gmlp_g, m_w_out, m_norm2_g, m_w_gate, m_w_up, m_w_down, m_final_g, v_norm1_g, v_w_in, v_gmlp_ln_g, v_w_spatial, v_b_spatial, v_mix_norm_attn_g, v_mix_norm_gmlp_g, v_w_out, v_norm2_g, v_w_gate, v_w_up, v_w_down, v_final_g):
    S, D = x.shape[1], x.shape[2]
    NL = norm1_g.shape[0]
    DA = D // 2
    xs = x.reshape(S, D)
    tabs = rope_tables(S)
    c_arr = lax.axis_index("c").astype(jnp.int32).reshape(1)
    small_w = dict(norm1_g=norm1_g, gmlp_ln_g=gmlp_ln_g, w_spatial=w_spatial, b_spatial=b_spatial,
                   mix_norm_attn_g=mix_norm_attn_g, mix_norm_gmlp_g=mix_norm_gmlp_g, norm2_g=norm2_g)
    small_m = dict(norm1_g=m_norm1_g, gmlp_ln_g=m_gmlp_ln_g, w_spatial=m_w_spatial, b_spatial=m_b_spatial,
                   mix_norm_attn_g=m_mix_norm_attn_g, mix_norm_gmlp_g=m_mix_norm_gmlp_g, norm2_g=m_norm2_g)
    small_v = dict(norm1_g=v_norm1_g, gmlp_ln_g=v_gmlp_ln_g, w_spatial=v_w_spatial, b_spatial=v_b_spatial,
                   mix_norm_attn_g=v_mix_norm_attn_g, mix_norm_gmlp_g=v_mix_norm_gmlp_g, norm2_g=v_norm2_g)
    def view(n, a):
        return jnp.swapaxes(a, 1, 2) if n in TRANSPOSED else a

    big_w = {n: view(n, a) for n, a in dict(w_in=w_in, w_out=w_out, w_gate=w_gate, w_up=w_up, w_down=w_down).items()}
    big_m = {n: view(n, a) for n, a in dict(w_in=m_w_in, w_out=m_w_out, w_gate=m_w_gate, w_up=m_w_up, w_down=m_w_down).items()}
    big_v = {n: view(n, a) for n, a in dict(w_in=v_w_in, w_out=v_w_out, w_gate=v_w_gate, w_up=v_w_up, w_down=v_w_down).items()}

    def layer_small(l):
        ws = w_spatial[l]
        return dict(norm1_g=norm1_g[l][None], gmlp_ln_g=gmlp_ln_g[l][None], mix_norm_attn_g=mix_norm_attn_g[l][None],
                    mix_norm_gmlp_g=mix_norm_gmlp_g[l][None], norm2_g=norm2_g[l][None], w_spatial=ws,
                    w_spatial_t=jnp.swapaxes(ws, 1, 2), bcol=jnp.broadcast_to(b_spatial[l][:, :, None], ws.shape))

    def cast_shards(l, dep):
        return [(big_w[n][l] + dep[0, 0]).astype(BF16) for n in BIG]

    pending = gather_ici_start(cast_shards(0, jnp.zeros(TOKEN, F32)))
    h = xs
    saved, weights = [], []
    for l in range(NL):
        send_sems, recv_sems, shards, lands, _ = pending
        shards, lands = gather_ici_wait(send_sems, recv_sems, shards, lands, h)
        got, dep = gather_d2d(lands, shards)
        gw = dict(zip(BIG, got))
        gw["w_out"] = gw["w_out"].reshape(2, DA, D)
        if l + 1 < NL:
            pending = gather_ici_start(cast_shards(l + 1, dep))
            dep = pending[4]
        sm = layer_small(l)
        sm["norm1_g"] = sm["norm1_g"] + dep[0, 0]
        h, sv = layer_fwd(h, sm, gw, tabs)
        saved.append(sv)
        weights.append(gw)
    loss_part, dx, dxb, d_final = loss_and_grad(h, final_g[None], loss_target.reshape(S, D))
    loss = lax.psum(loss_part[0, 0], ("x", "y", "c"))

    def finish_scatter(pend, after):
        send_sems, recv_sems, sums, lands, _ = pend
        sums, lands = scatter_ici_wait(send_sems, recv_sems, sums, lands, after)
        return dict(zip(BIG, own_slot_copy(sums, lands)))

    big_sums = [None] * NL
    small_grads = [None] * NL
    pending = None
    for l in reversed(range(NL)):
        dx, dxb, big, small = layer_bwd(dx, dxb, saved[l], layer_small(l), weights[l], tabs, dep)
        small_grads[l] = small
        if pending is not None:
            big_sums[l + 1] = finish_scatter(pending, dx)
        parts = [big[n].reshape((N_DEV, -1, big[n].shape[-1])) for n in BIG]
        got = scatter_d2d(parts)
        pending = scatter_ici_start([pair_sum(p, g, c_arr) for p, g in zip(parts, got)])
        dep = pending[4]

    packed = _pack(small_grads, d_final) + dep[0, 0]
    gathered_small = all_gather([packed])[0]
    pw = _pack([{n: small_w[n][l] for n in SMALL} for l in range(NL)], final_g)[None]
    pm = _pack([{n: small_m[n][l] for n in SMALL} for l in range(NL)], m_final_g)[None]
    pv = _pack([{n: small_v[n][l] for n in SMALL} for l in range(NL)], v_final_g)[None]
    like_layers = [{n: small_w[n][l] for n in SMALL} for l in range(NL)]
    small_res = adamw(pw, pm, pv, [gathered_small])
    small_out = [_unpack(t[0], like_layers, final_g) for t in small_res]
    big_sums[0] = finish_scatter(pending, small_res[0])

    def small_stack(k, n):
        return jnp.stack([small_out[k][0][l][n] for l in range(NL)])

    big_out = {}
    for n in BIG:
        R, C = big_w[n].shape[1], big_w[n].shape[2]
        res = adamw(big_w[n], big_m[n], big_v[n], [big_sums[l][n].reshape(N_CHIP, R, C) for l in range(NL)])
        big_out[n] = [view(n, t) for t in res]

    order = ("norm1_g", "w_in", "gmlp_ln_g", "w_spatial", "b_spatial", "mix_norm_attn_g", "mix_norm_gmlp_g", "w_out",
             "norm2_g", "w_gate", "w_up", "w_down")
    outs = [loss, dx.reshape(x.shape)]
    for k in range(4):
        for n in order:
            outs.append(big_out[n][k] if n in BIG else small_stack(k, n))
        outs.append(small_out[k][1])
    return tuple(outs)
```

```python
import functools

import jax
import jax.numpy as jnp
from jax import lax
from jax.experimental import pallas as pl
from jax.experimental.pallas import tpu as pltpu

F32 = jnp.float32
BF16 = jnp.bfloat16

HEAD_DIM = 128
CHUNK = 128
GROUP = 128
N_SIDE = 64
DILATIONS = (1, 4, 16)
TQ = 128
HALO = 64
EPS = 1e-6
NEG = -1e30
ROPE_THETA = 10000.0
ADAM_LR = 0.001
ADAM_B1 = 0.9
ADAM_B2 = 0.999
ADAM_EPS = 1e-08
ADAM_WD = 0.01
ADAM_STEP = 10
N_DEV = 8
N_CHIP = 4
VMEM_LIMIT_V7X = 56 * 1024 * 1024
ADAMW_BLOCK_ELEMS = 128 * 1024
MESH = pl.DeviceIdType.MESH
ANY = pl.BlockSpec(memory_space=pl.ANY)


def _params(sem=None):
    return pltpu.CompilerParams(dimension_semantics=sem, vmem_limit_bytes=VMEM_LIMIT_V7X)


def _tile(n, want):
    for t in range(min(want, n), 15, -1):
        if n % t == 0 and t % 16 == 0:
            return t
    return n


def _sds(shape, dtype):
    return jax.ShapeDtypeStruct(shape, dtype)


def _dot(a, b, dims):
    return lax.dot_general(a, b, (dims, ((), ())), preferred_element_type=F32)


NN = ((1,), (0,))
NT = ((1,), (1,))
TN = ((0,), (0,))


def _sigmoid(x):
    return 1.0 / (1.0 + jnp.exp(-x))


def _gelu(x):
    return 0.5 * x * (1.0 + lax.erf(x * 0.7071067811865476))


def _gelu_grad(x):
    cdf = 0.5 * (1.0 + lax.erf(x * 0.7071067811865476))
    pdf = 0.3989422804014327 * jnp.exp(-0.5 * x * x)
    return cdf + x * pdf


def rms_fwd(x, g):
    S, D = x.shape
    tr = _tile(S, 512)

    def body(x_ref, g_ref, o_ref):
        xv = x_ref[...]
        r = lax.rsqrt(jnp.mean(xv * xv, axis=-1, keepdims=True) + EPS)
        o_ref[...] = (xv * r * g_ref[...]).astype(o_ref.dtype)

    return pl.pallas_call(
        body, name="rms_fwd", grid=(S // tr,),
        in_specs=[pl.BlockSpec((tr, D), lambda i: (i, 0)), pl.BlockSpec((1, D), lambda i: (0, 0))],
        out_specs=pl.BlockSpec((tr, D), lambda i: (i, 0)),
        out_shape=_sds((S, D), BF16), compiler_params=_params(("parallel",)),
    )(x, g)


def _rms_bwd_math(xv, gv, dy):
    r = lax.rsqrt(jnp.mean(xv * xv, axis=-1, keepdims=True) + EPS)
    dyg = dy * gv
    dx = r * dyg - xv * (r * r * r) * jnp.mean(dyg * xv, axis=-1, keepdims=True)
    dg = jnp.sum(dy * xv * r, axis=0, keepdims=True)
    return dx, dg


def rms_bwd_res(x, g, dy, dres):
    S, D = x.shape
    tr = _tile(S, 512)

    def body(x_ref, g_ref, dy_ref, dres_ref, dx_ref, dxb_ref, dg_ref):
        dx, dg = _rms_bwd_math(x_ref[...], g_ref[...], dy_ref[...])
        tot = dres_ref[...] + dx
        dx_ref[...] = tot
        dxb_ref[...] = tot.astype(dxb_ref.dtype)

        @pl.when(pl.program_id(0) == 0)
        def _():
            dg_ref[...] = dg

        @pl.when(pl.program_id(0) > 0)
        def _():
            dg_ref[...] += dg

    row = pl.BlockSpec((tr, D), lambda i: (i, 0))
    vec = pl.BlockSpec((1, D), lambda i: (0, 0))
    return pl.pallas_call(
        body, name="rms_bwd_res", grid=(S // tr,),
        in_specs=[row, vec, row, row], out_specs=[row, row, vec],
        out_shape=[_sds((S, D), F32), _sds((S, D), BF16), _sds((1, D), F32)],
        compiler_params=_params(("arbitrary",)),
    )(x, g, dy, dres)


def loss_and_grad(x, g, target):
    S, D = x.shape
    tr = _tile(S, 512)

    def body(x_ref, g_ref, t_ref, loss_ref, dx_ref, dxb_ref, dg_ref):
        xv = x_ref[...]
        gv = g_ref[...]
        r = lax.rsqrt(jnp.mean(xv * xv, axis=-1, keepdims=True) + EPS)
        y = xv * r * gv
        diff = y - t_ref[...]
        part = 0.5 * jnp.sum(jnp.mean(diff * diff, axis=-1, keepdims=True), axis=0, keepdims=True)
        dy = diff * (1.0 / D)
        dx, dg = _rms_bwd_math(xv, gv, dy)
        dx_ref[...] = dx
        dxb_ref[...] = dx.astype(dxb_ref.dtype)

        @pl.when(pl.program_id(0) == 0)
        def _():
            dg_ref[...] = dg
            loss_ref[...] = jnp.broadcast_to(part, loss_ref.shape)

        @pl.when(pl.program_id(0) > 0)
        def _():
            dg_ref[...] += dg
            loss_ref[...] += jnp.broadcast_to(part, loss_ref.shape)

    row = pl.BlockSpec((tr, D), lambda i: (i, 0))
    vec = pl.BlockSpec((1, D), lambda i: (0, 0))
    return pl.pallas_call(
        body, name="loss_and_grad", grid=(S // tr,),
        in_specs=[row, vec, row],
        out_specs=[pl.BlockSpec((8, 128), lambda i: (0, 0)), row, row, vec],
        out_shape=[_sds((8, 128), F32), _sds((S, D), F32), _sds((S, D), BF16), _sds((1, D), F32)],
        compiler_params=_params(("arbitrary",)),
    )(x, g, target)


def _mm_body(n_pairs, dims, red_axis, n_red, has_res):
    def body(*refs):
        ins = refs[:2 * n_pairs]
        res_ref = refs[2 * n_pairs] if has_res else None
        o_ref = refs[2 * n_pairs + has_res]
        p = None
        for t in range(n_pairs):
            d = _dot(ins[2 * t][...], ins[2 * t + 1][...], dims)
            p = d if p is None else p + d
        if red_axis is None:
            if has_res:
                p = res_ref[...] + p
            o_ref[...] = p.astype(o_ref.dtype)
            return
        acc_ref = o_ref if o_ref.dtype == F32 else refs[2 * n_pairs + has_res + 1]
        r = pl.program_id(red_axis)

        @pl.when(r == 0)
        def _():
            acc_ref[...] = res_ref[...] + p if has_res else p

        @pl.when(r > 0)
        def _():
            acc_ref[...] += p

        if acc_ref is not o_ref:
            @pl.when(r == n_red - 1)
            def _():
                o_ref[...] = acc_ref[...].astype(o_ref.dtype)

    return body


def mm_in(h, w):
    S, D = h.shape
    NS, _, n = w.shape
    tm = _tile(S, 1024)
    return pl.pallas_call(
        _mm_body(1, NN, None, 1, False), name="mm_in", grid=(S // tm, NS),
        in_specs=[pl.BlockSpec((tm, D), lambda i, k: (i, 0)), pl.BlockSpec((None, D, n), lambda i, k: (k, 0, 0))],
        out_specs=pl.BlockSpec((tm, n), lambda i, k: (i, k)),
        out_shape=_sds((S, NS * n), F32), compiler_params=_params(("parallel", "parallel")),
    )(h, w)


def mm_out(x, mix_a, mix_g, w):
    S, D = x.shape
    DA = mix_a.shape[1]
    tm = _tile(S, 1024)
    tn = _tile(D, 1024)
    act = pl.BlockSpec((tm, DA), lambda i, j: (i, 0))
    return pl.pallas_call(
        _mm_body(2, NN, None, 1, True), name="mm_out", grid=(S // tm, D // tn),
        in_specs=[act, pl.BlockSpec((None, DA, tn), lambda i, j: (0, 0, j)),
                  act, pl.BlockSpec((None, DA, tn), lambda i, j: (1, 0, j)),
                  pl.BlockSpec((tm, tn), lambda i, j: (i, j))],
        out_specs=pl.BlockSpec((tm, tn), lambda i, j: (i, j)),
        out_shape=_sds((S, D), F32), compiler_params=_params(("parallel", "parallel")),
    )(mix_a, w, mix_g, w, x)


def ff_fwd(h, wg, wu):
    S, D = h.shape
    NS, f, _ = wg.shape
    tm = _tile(S, 512)

    def body(h_ref, wg_ref, wu_ref, gate_ref, up_ref, ff_ref):
        hv = h_ref[...]
        g = _dot(hv, wg_ref[...], NT)
        u = _dot(hv, wu_ref[...], NT)
        gate_ref[...] = g
        up_ref[...] = u
        ff_ref[...] = (g * _sigmoid(g) * u).astype(ff_ref.dtype)

    wspec = pl.BlockSpec((None, f, D), lambda i, k: (k, 0, 0))
    ospec = pl.BlockSpec((None, tm, f), lambda i, k: (k, i, 0))
    return pl.pallas_call(
        body, name="ff_fwd", grid=(S // tm, NS),
        in_specs=[pl.BlockSpec((tm, D), lambda i, k: (i, 0)), wspec, wspec],
        out_specs=[ospec, ospec, ospec],
        out_shape=[_sds((NS, S, f), F32), _sds((NS, S, f), F32), _sds((NS, S, f), BF16)],
        compiler_params=_params(("parallel", "parallel")),
    )(h, wg, wu)


def mm_down(x, ff, wd):
    S, D = x.shape
    NS, _, f = ff.shape
    tm = _tile(S, 512)
    return pl.pallas_call(
        _mm_body(1, NN, 1, NS, True), name="mm_down", grid=(S // tm, NS),
        in_specs=[pl.BlockSpec((None, tm, f), lambda i, k: (k, i, 0)), pl.BlockSpec((None, f, D), lambda i, k: (k, 0, 0)),
                  pl.BlockSpec((tm, D), lambda i, k: (i, 0))],
        out_specs=pl.BlockSpec((tm, D), lambda i, k: (i, 0)),
        out_shape=_sds((S, D), F32), compiler_params=_params(("parallel", "arbitrary")),
    )(ff, wd, x)


def ff_bwd_act(dxb, wd, gate, up, dep):
    S, D = dxb.shape
    NS, f, _ = wd.shape
    tm = _tile(S, 512)

    def body(dx_ref, wd_ref, gate_ref, up_ref, dep_ref, dgate_ref, dup_ref):
        dff = _dot(dx_ref[...], wd_ref[...], NT)
        g = gate_ref[...]
        sg = _sigmoid(g)
        dup_ref[...] = (dff * (g * sg)).astype(dup_ref.dtype)
        dgate_ref[...] = (dff * up_ref[...] * (sg * (1.0 + g * (1.0 - sg)))).astype(dgate_ref.dtype)

    aspec = pl.BlockSpec((None, tm, f), lambda i, k: (k, i, 0))
    return pl.pallas_call(
        body, name="ff_bwd_act", grid=(S // tm, NS),
        in_specs=[pl.BlockSpec((tm, D), lambda i, k: (i, 0)), pl.BlockSpec((None, f, D), lambda i, k: (k, 0, 0)), aspec, aspec,
                  pl.BlockSpec((8, 128), lambda i, k: (0, 0))],
        out_specs=[aspec, aspec],
        out_shape=[_sds((NS, S, f), BF16), _sds((NS, S, f), BF16)],
        compiler_params=_params(("parallel", "parallel")),
    )(dxb, wd, gate, up, dep)


def dw_down(ff, dxb):
    NS, S, f = ff.shape
    D = dxb.shape[1]
    ts = _tile(S, 1024)
    return pl.pallas_call(
        _mm_body(1, TN, 1, S // ts, False), name="dw_down", grid=(NS, S // ts),
        in_specs=[pl.BlockSpec((None, ts, f), lambda k, s: (k, s, 0)), pl.BlockSpec((ts, D), lambda k, s: (s, 0))],
        out_specs=pl.BlockSpec((None, f, D), lambda k, s: (k, 0, 0)),
        out_shape=_sds((NS, f, D), BF16), scratch_shapes=[pltpu.VMEM((f, D), F32)],
        compiler_params=_params(("parallel", "arbitrary")),
    )(ff, dxb)


def dw_gate_up(h, dgate, dup):
    S, D = h.shape
    NS, _, f = dgate.shape
    ts = _tile(S, 512)
    n_red = S // ts

    def body(h_ref, dg_ref, du_ref, og_ref, ou_ref, accg_ref, accu_ref):
        hv = h_ref[...]
        pg = _dot(dg_ref[...], hv, TN)
        pu = _dot(du_ref[...], hv, TN)
        r = pl.program_id(1)

        @pl.when(r == 0)
        def _():
            accg_ref[...] = pg
            accu_ref[...] = pu

        @pl.when(r > 0)
        def _():
            accg_ref[...] += pg
            accu_ref[...] += pu

        @pl.when(r == n_red - 1)
        def _():
            og_ref[...] = accg_ref[...].astype(og_ref.dtype)
            ou_ref[...] = accu_ref[...].astype(ou_ref.dtype)

    aspec = pl.BlockSpec((None, ts, f), lambda k, s: (k, s, 0))
    ospec = pl.BlockSpec((None, f, D), lambda k, s: (k, 0, 0))
    return pl.pallas_call(
        body, name="dw_gate_up", grid=(NS, n_red),
        in_specs=[pl.BlockSpec((ts, D), lambda k, s: (s, 0)), aspec, aspec],
        out_specs=[ospec, ospec],
        out_shape=[_sds((NS, f, D), BF16), _sds((NS, f, D), BF16)],
        scratch_shapes=[pltpu.VMEM((f, D), F32), pltpu.VMEM((f, D), F32)],
        compiler_params=_params(("parallel", "arbitrary")),
    )(h, dgate, dup)


def dh_ff(dgate, dup, wg, wu):
    NS, S, f = dgate.shape
    D = wg.shape[2]
    tm = _tile(S, 512)
    aspec = pl.BlockSpec((None, tm, f), lambda i, k: (k, i, 0))
    wspec = pl.BlockSpec((None, f, D), lambda i, k: (k, 0, 0))
    return pl.pallas_call(
        _mm_body(2, NN, 1, NS, False), name="dh_ff", grid=(S // tm, NS),
        in_specs=[aspec, wspec, aspec, wspec],
        out_specs=pl.BlockSpec((tm, D), lambda i, k: (i, 0)),
        out_shape=_sds((S, D), F32), compiler_params=_params(("parallel", "arbitrary")),
    )(dgate, wg, dup, wu)


def dmix_mm(dxb, w):
    S, D = dxb.shape
    tm = _tile(S, 1024)
    tn = _tile(D, 1024)
    return pl.pallas_call(
        _mm_body(1, NT, None, 1, False), name="dmix_mm", grid=(S // tm, D // tn),
        in_specs=[pl.BlockSpec((tm, D), lambda i, j: (i, 0)), pl.BlockSpec((tn, D), lambda i, j: (j, 0))],
        out_specs=pl.BlockSpec((tm, tn), lambda i, j: (i, j)),
        out_shape=_sds((S, D), F32), compiler_params=_params(("parallel", "parallel")),
    )(dxb, w)


def dw_out(mix_a, mix_g, dxb):
    S, DA = mix_a.shape
    D = dxb.shape[1]
    ts = _tile(S, 1024)

    def half(name, m):
        return pl.pallas_call(
            _mm_body(1, TN, 0, S // ts, False), name=name, grid=(S // ts,),
            in_specs=[pl.BlockSpec((ts, DA), lambda s: (s, 0)), pl.BlockSpec((ts, D), lambda s: (s, 0))],
            out_specs=pl.BlockSpec((DA, D), lambda s: (0, 0)),
            out_shape=_sds((DA, D), BF16), scratch_shapes=[pltpu.VMEM((DA, D), F32)],
            compiler_params=_params(("arbitrary",)),
        )(m, dxb)

    return half("dw_out_a", mix_a), half("dw_out_g", mix_g)


def dw_in(h, dproj, NS):
    S, D = h.shape
    n = dproj.shape[1] // NS
    ts = _tile(S, 1024)
    return pl.pallas_call(
        _mm_body(1, TN, 1, S // ts, False), name="dw_in", grid=(NS, S // ts),
        in_specs=[pl.BlockSpec((ts, D), lambda k, s: (s, 0)), pl.BlockSpec((ts, n), lambda k, s: (s, k))],
        out_specs=pl.BlockSpec((None, D, n), lambda k, s: (k, 0, 0)),
        out_shape=_sds((NS, D, n), BF16), scratch_shapes=[pltpu.VMEM((D, n), F32)],
        compiler_params=_params(("parallel", "arbitrary")),
    )(h, dproj)


def dh_in(dproj, w):
    S = dproj.shape[0]
    NS, D, n = w.shape
    tm = _tile(S, 512)
    return pl.pallas_call(
        _mm_body(1, NT, 1, NS, False), name="dh_in", grid=(S // tm, NS),
        in_specs=[pl.BlockSpec((tm, n), lambda i, k: (i, k)), pl.BlockSpec((None, D, n), lambda i, k: (k, 0, 0))],
        out_specs=pl.BlockSpec((tm, D), lambda i, k: (i, 0)),
        out_shape=_sds((S, D), F32), compiler_params=_params(("parallel", "arbitrary")),
    )(dproj, w)


def rope_tables(S):
    pos = jnp.arange(S, dtype=F32)
    inv = ROPE_THETA ** (-jnp.arange(0, HEAD_DIM, 2, dtype=F32) / HEAD_DIM)
    ang = pos[:, None] * inv[None, :]
    cos, sin = jnp.cos(ang), jnp.sin(ang)
    return jnp.concatenate([cos, cos], axis=-1), jnp.concatenate([-sin, sin], axis=-1)


def _rot_half(t):
    return pltpu.roll(t, HEAD_DIM // 2, 1)


def rope_qkv(proj, cos2, sin2, DA):
    S = proj.shape[0]
    tr = _tile(S, 512)
    nh = DA // HEAD_DIM

    def body(q_ref, k_ref, v_ref, cos_ref, sin_ref, o_ref):
        c = cos_ref[...]
        s = sin_ref[...]
        for h in range(nh):
            sl = slice(h * HEAD_DIM, (h + 1) * HEAD_DIM)
            for j, ref in enumerate((q_ref, k_ref)):
                t = ref[:, sl]
                o_ref[:, j * DA + h * HEAD_DIM:j * DA + (h + 1) * HEAD_DIM] = (t * c + _rot_half(t) * s).astype(o_ref.dtype)
        o_ref[:, 2 * DA:3 * DA] = v_ref[...].astype(o_ref.dtype)

    tab = pl.BlockSpec((tr, HEAD_DIM), lambda i: (i, 0))
    return pl.pallas_call(
        body, name="rope_qkv", grid=(S // tr,),
        in_specs=[pl.BlockSpec((tr, DA), lambda i: (i, 0)), pl.BlockSpec((tr, DA), lambda i: (i, 1)),
                  pl.BlockSpec((tr, DA), lambda i: (i, 2)), tab, tab],
        out_specs=pl.BlockSpec((tr, 3 * DA), lambda i: (i, 0)),
        out_shape=_sds((S, 3 * DA), BF16), compiler_params=_params(("parallel",)),
    )(proj, proj, proj, cos2, sin2)


def assemble_dproj(dqs, dks, dvs, duv, cos2, sin2):
    S, DA = dqs[0].shape
    tr = _tile(S, 256)
    nh = DA // HEAD_DIM

    def body(*refs):
        dq_refs, dk_refs, dv_refs = refs[0:3], refs[3:6], refs[6:9]
        duv_ref, cos_ref, sin_ref, o_ref = refs[9:13]
        c = cos_ref[...]
        s = sin_ref[...]
        for j, trio in enumerate((dq_refs, dk_refs)):
            for h in range(nh):
                sl = slice(h * HEAD_DIM, (h + 1) * HEAD_DIM)
                t = trio[0][:, sl] + trio[1][:, sl] + trio[2][:, sl]
                o_ref[:, j * DA + h * HEAD_DIM:j * DA + (h + 1) * HEAD_DIM] = (t * c - _rot_half(t) * s).astype(o_ref.dtype)
        o_ref[:, 2 * DA:3 * DA] = (dv_refs[0][...] + dv_refs[1][...] + dv_refs[2][...]).astype(o_ref.dtype)
        o_ref[:, 3 * DA:5 * DA] = duv_ref[...]

    blk = pl.BlockSpec((tr, DA), lambda i: (i, 0))
    tab = pl.BlockSpec((tr, HEAD_DIM), lambda i: (i, 0))
    return pl.pallas_call(
        body, name="assemble_dproj", grid=(S // tr,),
        in_specs=[blk] * 9 + [pl.BlockSpec((tr, 2 * DA), lambda i: (i, 0)), tab, tab],
        out_specs=pl.BlockSpec((tr, 5 * DA), lambda i: (i, 0)),
        out_shape=_sds((S, 5 * DA), BF16), compiler_params=_params(("parallel",)),
    )(*dqs, *dks, *dvs, duv, cos2, sin2)


def _halo_specs(L, width_blocks, col):
    per = TQ // HALO
    last = L // HALO - 1
    W = width_blocks
    return [
        pl.BlockSpec((HALO, W), lambda r, i: (jnp.maximum(i * per - 1, 0), col(r))),
        pl.BlockSpec((TQ, W), lambda r, i: (i, col(r))),
        pl.BlockSpec((HALO, W), lambda r, i: (jnp.minimum((i + 1) * per, last), col(r))),
    ]


def _cat3(refs, sl):
    return jnp.concatenate([refs[0][:, sl], refs[1][:, sl], refs[2][:, sl]], axis=0)


def attn_fwd(qkv, d, DA):
    S = qkv.shape[0]
    L = S // d
    nh = DA // HEAD_DIM
    scale = HEAD_DIM ** -0.5
    TK = TQ + 2 * HALO

    def body(q_ref, kp_ref, kc_ref, kn_ref, vp_ref, vc_ref, vn_ref, o_ref, lse_ref):
        i = pl.program_id(1)
        row = lax.broadcasted_iota(jnp.int32, (TQ, TK), 0)
        col = lax.broadcasted_iota(jnp.int32, (TQ, TK), 1)
        kpos = i * TQ - HALO + col
        mask = (jnp.abs(col - HALO - row) <= N_SIDE) & (kpos >= 0) & (kpos < L)
        for h in range(nh):
            sl = slice(h * HEAD_DIM, (h + 1) * HEAD_DIM)
            k = _cat3((kp_ref, kc_ref, kn_ref), sl)
            v = _cat3((vp_ref, vc_ref, vn_ref), sl)
            s = _dot(q_ref[:, sl], k, NT) * scale
            s = jnp.where(mask, s, NEG)
            m = jnp.max(s, axis=-1, keepdims=True)
            p = jnp.exp(s - m)
            l = jnp.sum(p, axis=-1, keepdims=True)
            o = _dot(p.astype(v.dtype), v, NN) / l
            o_ref[:, sl] = o
            lse_ref[:, sl] = jnp.broadcast_to(m + jnp.log(l), (TQ, HEAD_DIM))

    out_spec = pl.BlockSpec((TQ, DA), lambda r, i: (i, r))
    o, lse = pl.pallas_call(
        body, name="attn_fwd_d%d" % d, grid=(d, L // TQ),
        in_specs=[pl.BlockSpec((TQ, DA), lambda r, i: (i, 3 * r))]
        + _halo_specs(L, DA, lambda r: 3 * r + 1) + _halo_specs(L, DA, lambda r: 3 * r + 2),
        out_specs=[out_spec, out_spec],
        out_shape=[_sds((L, d * DA), F32), _sds((L, d * DA), F32)],
        compiler_params=_params(("parallel", "parallel")),
    )(*([qkv.reshape(L, d * 3 * DA)] * 7))
    return o.reshape(S, DA), lse.reshape(S, DA)


def attn_merge(os_, lses, g):
    S, DA = os_[0].shape
    tr = _tile(S, 256)

    def body(o0, o1, o2, l0, l1, l2, g_ref, a_ref, lse_ref, mix_ref):
        a0, a1, a2 = l0[...], l1[...], l2[...]
        m = jnp.maximum(jnp.maximum(a0, a1), a2)
        e0, e1, e2 = jnp.exp(a0 - m), jnp.exp(a1 - m), jnp.exp(a2 - m)
        den = e0 + e1 + e2
        a = (e0 / den) * o0[...] + (e1 / den) * o1[...] + (e2 / den) * o2[...]
        a_ref[...] = a
        lse_ref[...] = m + jnp.log(den)
        r = lax.rsqrt(jnp.mean(a * a, axis=-1, keepdims=True) + EPS)
        mix_ref[...] = (a * r * g_ref[...]).astype(mix_ref.dtype)

    blk = pl.BlockSpec((tr, DA), lambda i: (i, 0))
    return pl.pallas_call(
        body, name="attn_merge", grid=(S // tr,),
        in_specs=[blk] * 6 + [pl.BlockSpec((1, DA), lambda i: (0, 0))],
        out_specs=[blk, blk, blk],
        out_shape=[_sds((S, DA), F32), _sds((S, DA), F32), _sds((S, DA), BF16)],
        compiler_params=_params(("parallel",)),
    )(*os_, *lses, g)


def attn_out_bwd(a, g, dmix):
    S, DA = a.shape
    tr = _tile(S, 256)
    nh = DA // HEAD_DIM

    def body(a_ref, g_ref, dy_ref, do_ref, dl_ref, dg_ref):
        av = a_ref[...]
        dx, dg = _rms_bwd_math(av, g_ref[...], dy_ref[...])
        do_ref[...] = dx.astype(do_ref.dtype)
        prod = dx * av
        for h in range(nh):
            sl = slice(h * HEAD_DIM, (h + 1) * HEAD_DIM)
            dl_ref[:, sl] = jnp.broadcast_to(jnp.sum(prod[:, sl], axis=-1, keepdims=True), (tr, HEAD_DIM))

        @pl.when(pl.program_id(0) == 0)
        def _():
            dg_ref[...] = dg

        @pl.when(pl.program_id(0) > 0)
        def _():
            dg_ref[...] += dg

    blk = pl.BlockSpec((tr, DA), lambda i: (i, 0))
    vec = pl.BlockSpec((1, DA), lambda i: (0, 0))
    return pl.pallas_call(
        body, name="attn_out_bwd", grid=(S // tr,),
        in_specs=[blk, vec, blk], out_specs=[blk, blk, vec],
        out_shape=[_sds((S, DA), BF16), _sds((S, DA), F32), _sds((1, DA), F32)],
        compiler_params=_params(("arbitrary",)),
    )(a, g, dmix)


def attn_bwd_dq(qkv, dob, lse, dl, d, DA):
    S = qkv.shape[0]
    L = S // d
    nh = DA // HEAD_DIM
    scale = HEAD_DIM ** -0.5
    TK = TQ + 2 * HALO

    def body(q_ref, kp_ref, kc_ref, kn_ref, vp_ref, vc_ref, vn_ref, do_ref, lse_ref, dl_ref, dq_ref):
        i = pl.program_id(1)
        row = lax.broadcasted_iota(jnp.int32, (TQ, TK), 0)
        col = lax.broadcasted_iota(jnp.int32, (TQ, TK), 1)
        kpos = i * TQ - HALO + col
        mask = (jnp.abs(col - HALO - row) <= N_SIDE) & (kpos >= 0) & (kpos < L)
        for h in range(nh):
            sl = slice(h * HEAD_DIM, (h + 1) * HEAD_DIM)
            k = _cat3((kp_ref, kc_ref, kn_ref), sl)
            v = _cat3((vp_ref, vc_ref, vn_ref), sl)
            s = _dot(q_ref[:, sl], k, NT) * scale
            s = jnp.where(mask, s, NEG)
            p = jnp.exp(s - lse_ref[:, sl][:, 0:1])
            dp = _dot(do_ref[:, sl], v, NT)
            ds = p * (dp - dl_ref[:, sl][:, 0:1])
            dq_ref[:, sl] = _dot(ds.astype(k.dtype), k, NN) * scale

    blk = pl.BlockSpec((TQ, DA), lambda r, i: (i, r))
    dq = pl.pallas_call(
        body, name="attn_bwd_dq_d%d" % d, grid=(d, L // TQ),
        in_specs=[pl.BlockSpec((TQ, DA), lambda r, i: (i, 3 * r))]
        + _halo_specs(L, DA, lambda r: 3 * r + 1) + _halo_specs(L, DA, lambda r: 3 * r + 2) + [blk, blk, blk],
        out_specs=blk, out_shape=_sds((L, d * DA), F32),
        compiler_params=_params(("parallel", "parallel")),
    )(*([qkv.reshape(L, d * 3 * DA)] * 7), dob.reshape(L, d * DA), lse.reshape(L, d * DA), dl.reshape(L, d * DA))
    return dq.reshape(S, DA)


def attn_bwd_dkv(qkv, dob, lse, dl, d, DA):
    S = qkv.shape[0]
    L = S // d
    nh = DA // HEAD_DIM
    scale = HEAD_DIM ** -0.5
    TR = TQ + 2 * HALO

    def body(k_ref, v_ref, qp, qc, qn, dop, doc, don, lp, lc, ln, dp_, dc_, dn_, dk_ref, dv_ref):
        j = pl.program_id(1)
        row = lax.broadcasted_iota(jnp.int32, (TR, TQ), 0)
        col = lax.broadcasted_iota(jnp.int32, (TR, TQ), 1)
        qpos = j * TQ - HALO + row
        mask = (jnp.abs(col - (row - HALO)) <= N_SIDE) & (qpos >= 0) & (qpos < L)
        for h in range(nh):
            sl = slice(h * HEAD_DIM, (h + 1) * HEAD_DIM)
            q = _cat3((qp, qc, qn), sl)
            do = _cat3((dop, doc, don), sl)
            lse_q = _cat3((lp, lc, ln), sl)
            dl_q = _cat3((dp_, dc_, dn_), sl)
            k = k_ref[:, sl]
            v = v_ref[:, sl]
            s = _dot(q, k, NT) * scale
            s = jnp.where(mask, s, NEG)
            p = jnp.exp(s - lse_q)
            dv_ref[:, sl] = _dot(p.astype(do.dtype), do, TN)
            dp = _dot(do, v, NT)
            ds = p * (dp - dl_q)
            dk_ref[:, sl] = _dot(ds.astype(q.dtype), q, TN) * scale

    blk = pl.BlockSpec((TQ, DA), lambda r, i: (i, r))
    ident = lambda r: r
    dk, dv = pl.pallas_call(
        body, name="attn_bwd_dkv_d%d" % d, grid=(d, L // TQ),
        in_specs=[pl.BlockSpec((TQ, DA), lambda r, i: (i, 3 * r + 1)), pl.BlockSpec((TQ, DA), lambda r, i: (i, 3 * r + 2))]
        + _halo_specs(L, DA, lambda r: 3 * r) + _halo_specs(L, DA, ident) + _halo_specs(L, DA, ident) + _halo_specs(L, DA, ident),
        out_specs=[blk, blk], out_shape=[_sds((L, d * DA), F32), _sds((L, d * DA), F32)],
        compiler_params=_params(("parallel", "parallel")),
    )(*([qkv.reshape(L, d * 3 * DA)] * 5), *([dob.reshape(L, d * DA)] * 3), *([lse.reshape(L, d * DA)] * 3),
      *([dl.reshape(L, d * DA)] * 3))
    return dk.reshape(S, DA), dv.reshape(S, DA)


def _gmlp_fwd_math(u_raw, v_raw, ln_g, ws_ref, bcol_ref, n_chunks, ng):
    ug = _gelu(u_raw)
    vg = _gelu(v_raw)
    mu = jnp.mean(vg, axis=-1, keepdims=True)
    xc = vg - mu
    rstd = lax.rsqrt(jnp.mean(xc * xc, axis=-1, keepdims=True) + EPS)
    xhat = xc * rstd
    vn = xhat * ln_g
    rows = []
    for n in range(n_chunks):
        cols = []
        for g in range(ng):
            blk = vn[n * CHUNK:(n + 1) * CHUNK, g * GROUP:(g + 1) * GROUP]
            cols.append(_dot(ws_ref[g].astype(BF16), blk.astype(BF16), NN) + bcol_ref[g])
        rows.append(jnp.concatenate(cols, axis=1))
    mixed = jnp.concatenate(rows, axis=0)
    return ug, xhat, rstd, vn, mixed


def gmlp_fwd(proj, ln_g, w_s, bcol, g_out, DA):
    S = proj.shape[0]
    ng = DA // GROUP
    tr = _tile(S, 2 * CHUNK)

    def body(u_ref, v_ref, ln_ref, ws_ref, bcol_ref, g_ref, G_ref, mix_ref):
        ug, _, _, _, mixed = _gmlp_fwd_math(u_ref[...], v_ref[...], ln_ref[...], ws_ref, bcol_ref, tr // CHUNK, ng)
        G = ug * mixed
        G_ref[...] = G
        r = lax.rsqrt(jnp.mean(G * G, axis=-1, keepdims=True) + EPS)
        mix_ref[...] = (G * r * g_ref[...]).astype(mix_ref.dtype)

    vec = pl.BlockSpec((1, DA), lambda i: (0, 0))
    par = pl.BlockSpec((ng, CHUNK, CHUNK), lambda i: (0, 0, 0))
    blk = pl.BlockSpec((tr, DA), lambda i: (i, 0))
    return pl.pallas_call(
        body, name="gmlp_fwd", grid=(S // tr,),
        in_specs=[pl.BlockSpec((tr, DA), lambda i: (i, 3)), pl.BlockSpec((tr, DA), lambda i: (i, 4)), vec, par, par, vec],
        out_specs=[blk, blk], out_shape=[_sds((S, DA), F32), _sds((S, DA), BF16)],
        compiler_params=_params(("parallel",)),
    )(proj, proj, ln_g, w_s, bcol, g_out)


def gmlp_bwd(proj, ln_g, w_s, w_st, bcol, g_out, dmix, DA):
    S = proj.shape[0]
    ng = DA // GROUP
    tr = _tile(S, 2 * CHUNK)
    nc = tr // CHUNK

    def body(u_ref, v_ref, ln_ref, ws_ref, wst_ref, bcol_ref, g_ref, dy_ref, duv_ref, dg_ref, dln_ref, dws_ref, db_ref):
        u_raw = u_ref[...]
        v_raw = v_ref[...]
        ln_g_v = ln_ref[...]
        ug, xhat, rstd, vn, mixed = _gmlp_fwd_math(u_raw, v_raw, ln_g_v, ws_ref, bcol_ref, nc, ng)
        G = ug * mixed
        dG, dg = _rms_bwd_math(G, g_ref[...], dy_ref[...])
        du_g = dG * mixed
        dmixed = dG * ug
        dws, dbs, rows = [], [], []
        for g in range(ng):
            dws.append(None)
            dbs.append(None)
        for n in range(nc):
            cols = []
            for g in range(ng):
                dm = dmixed[n * CHUNK:(n + 1) * CHUNK, g * GROUP:(g + 1) * GROUP]
                vb = vn[n * CHUNK:(n + 1) * CHUNK, g * GROUP:(g + 1) * GROUP]
                dmb = dm.astype(BF16)
                w = _dot(dmb, vb.astype(BF16), NT)
                b = jnp.broadcast_to(jnp.sum(dm, axis=-1, keepdims=True), (CHUNK, GROUP))
                dws[g] = w if dws[g] is None else dws[g] + w
                dbs[g] = b if dbs[g] is None else dbs[g] + b
                cols.append(_dot(wst_ref[g].astype(BF16), dmb, NN))
            rows.append(jnp.concatenate(cols, axis=1))
        dvn = jnp.concatenate(rows, axis=0)
        dln = jnp.sum(dvn * xhat, axis=0, keepdims=True)
        dxh = dvn * ln_g_v
        dvg = rstd * (dxh - jnp.mean(dxh, axis=-1, keepdims=True) - xhat * jnp.mean(dxh * xhat, axis=-1, keepdims=True))
        duv_ref[:, 0:DA] = (du_g * _gelu_grad(u_raw)).astype(duv_ref.dtype)
        duv_ref[:, DA:2 * DA] = (dvg * _gelu_grad(v_raw)).astype(duv_ref.dtype)

        first = pl.program_id(0) == 0

        @pl.when(first)
        def _():
            dg_ref[...] = dg
            dln_ref[...] = dln
            for g in range(ng):
                dws_ref[g] = dws[g]
                db_ref[g] = dbs[g]

        @pl.when(jnp.logical_not(first))
        def _():
            dg_ref[...] += dg
            dln_ref[...] += dln
            for g in range(ng):
                dws_ref[g] += dws[g]
                db_ref[g] += dbs[g]

    vec = pl.BlockSpec((1, DA), lambda i: (0, 0))
    par = pl.BlockSpec((ng, CHUNK, CHUNK), lambda i: (0, 0, 0))
    return pl.pallas_call(
        body, name="gmlp_bwd", grid=(S // tr,),
        in_specs=[pl.BlockSpec((tr, DA), lambda i: (i, 3)), pl.BlockSpec((tr, DA), lambda i: (i, 4)), vec, par, par, par, vec,
                  pl.BlockSpec((tr, DA), lambda i: (i, 1))],
        out_specs=[pl.BlockSpec((tr, 2 * DA), lambda i: (i, 0)), vec, vec, par, par],
        out_shape=[_sds((S, 2 * DA), BF16), _sds((1, DA), F32), _sds((1, DA), F32),
                   _sds((ng, CHUNK, CHUNK), F32), _sds((ng, CHUNK, CHUNK), F32)],
        compiler_params=_params(("arbitrary",)),
    )(proj, proj, ln_g, w_s, w_st, bcol, g_out, dmix)


def layer_fwd(x, sm, gw, tabs):
    D = x.shape[1]
    DA = D // 2
    cos2, sin2 = tabs
    h1 = rms_fwd(x, sm["norm1_g"])
    proj = mm_in(h1, gw["w_in"])
    qkv = rope_qkv(proj, cos2, sin2, DA)
    os_, lses = [], []
    for d in DILATIONS:
        o, l = attn_fwd(qkv, d, DA)
        os_.append(o)
        lses.append(l)
    a, lse, mix_a = attn_merge(os_, lses, sm["mix_norm_attn_g"])
    _, mix_g = gmlp_fwd(proj, sm["gmlp_ln_g"], sm["w_spatial"], sm["bcol"], sm["mix_norm_gmlp_g"], DA)
    x2 = mm_out(x, mix_a, mix_g, gw["w_out"])
    h2 = rms_fwd(x2, sm["norm2_g"])
    gate, up, ff = ff_fwd(h2, gw["w_gate"], gw["w_up"])
    x3 = mm_down(x2, ff, gw["w_down"])
    saved = dict(x=x, h1=h1, proj=proj, qkv=qkv, a=a, lse=lse, mix_a=mix_a, mix_g=mix_g, x2=x2, h2=h2, gate=gate, up=up, ff=ff)
    return x3, saved


def layer_bwd(dx, dxb, sv, sm, gw, tabs, dep):
    D = dx.shape[1]
    DA = D // 2
    cos2, sin2 = tabs
    NS = gw["w_in"].shape[0]
    dgate, dup = ff_bwd_act(dxb, gw["w_down"], sv["gate"], sv["up"], dep)
    g_down = dw_down(sv["ff"], dxb)
    g_gate, g_up = dw_gate_up(sv["h2"], dgate, dup)
    dh2 = dh_ff(dgate, dup, gw["w_gate"], gw["w_up"])
    dx2, dx2b, d_norm2 = rms_bwd_res(sv["x2"], sm["norm2_g"], dh2, dx)
    dmix = dmix_mm(dx2b, gw["w_out"].reshape(D, D))
    g_out_a, g_out_g = dw_out(sv["mix_a"], sv["mix_g"], dx2b)
    dob, dl, d_mix_a = attn_out_bwd(sv["a"], sm["mix_norm_attn_g"], dmix)
    duv, d_mix_g, d_ln, d_ws, d_bs = gmlp_bwd(sv["proj"], sm["gmlp_ln_g"], sm["w_spatial"], sm["w_spatial_t"], sm["bcol"],
                                              sm["mix_norm_gmlp_g"], dmix, DA)
    dqs, dks, dvs = [], [], []
    for d in DILATIONS:
        dqs.append(attn_bwd_dq(sv["qkv"], dob, sv["lse"], dl, d, DA))
        dk, dv = attn_bwd_dkv(sv["qkv"], dob, sv["lse"], dl, d, DA)
        dks.append(dk)
        dvs.append(dv)
    dproj = assemble_dproj(dqs, dks, dvs, duv, cos2, sin2)
    g_in = dw_in(sv["h1"], dproj, NS)
    dh1 = dh_in(dproj, gw["w_in"])
    dx0, dx0b, d_norm1 = rms_bwd_res(sv["x"], sm["norm1_g"], dh1, dx2)
    big = dict(w_in=g_in, w_out=jnp.concatenate([g_out_a, g_out_g], axis=0), w_gate=g_gate, w_up=g_up, w_down=g_down)
    small = dict(norm1_g=d_norm1, gmlp_ln_g=d_ln, w_spatial=d_ws, b_spatial=d_bs[:, :, 0], mix_norm_attn_g=d_mix_a,
                 mix_norm_gmlp_g=d_mix_g, norm2_g=d_norm2)
    return dx0, dx0b, big, small


def _place():
    x, y, c = lax.axis_index("x"), lax.axis_index("y"), lax.axis_index("c")
    return x, y, c, [(1 - x, y), (x, 1 - y), (1 - x, 1 - y)]


def _remote(src, dst, send_sems, recv_sems, k, to):
    return pltpu.make_async_remote_copy(src_ref=src, dst_ref=dst, send_sem=send_sems.at[k], recv_sem=recv_sems.at[k],
                                        device_id=to, device_id_type=MESH)


def to_slot(src, layer, me_arr, dep, dtype):
    R, C = src.shape[-2:]
    tr = _tile(R, max(16, 4 * ADAMW_BLOCK_ELEMS // C))

    def body(me_ref, src_ref, dep_ref, o_ref):
        o_ref[...] = src_ref[...].astype(o_ref.dtype)

    if layer is None:
        src_spec = pl.BlockSpec((tr, C), lambda i, me_ref: (i, 0))
    else:
        src_spec = pl.BlockSpec((None, tr, C), lambda i, me_ref: (layer, i, 0))
    return pl.pallas_call(
        body, name="to_slot",
        grid_spec=pltpu.PrefetchScalarGridSpec(
            num_scalar_prefetch=1, grid=(R // tr,),
            in_specs=[src_spec, pl.BlockSpec(TOKEN, lambda i, me_ref: (0, 0))],
            out_specs=pl.BlockSpec((None, tr, C), lambda i, me_ref: (me_ref[0], i, 0))),
        out_shape=_sds((N_DEV, R, C), dtype), compiler_params=_params(("parallel",)),
    )(me_arr, src, dep)


def gather_ici(bufs):
    n = len(bufs)

    def body(*refs):
        outs = refs[n:2 * n]
        send_sems, recv_sems = refs[2 * n:]
        x, y, c, chips = _place()
        mine = [o.at[4 * x + 2 * y + c] for o in outs]
        sent = []
        for a in range(n):
            for j, (px, py) in enumerate(chips):
                cp = _remote(mine[a], mine[a], send_sems, recv_sems, 3 * a + j, (px, py, c))
                cp.start()
                sent.append(cp)
        for a in range(n):
            for j, (px, py) in enumerate(chips):
                slot = outs[a].at[4 * px + 2 * py + c]
                _remote(slot, slot, send_sems, recv_sems, 3 * a + j, (px, py, c)).wait_recv()
        for cp in sent:
            cp.wait_send()

    return pl.pallas_call(
        body, name="gather_ici", in_specs=[ANY] * n, out_specs=[ANY] * n,
        out_shape=[_sds(b.shape, b.dtype) for b in bufs], input_output_aliases={a: a for a in range(n)},
        scratch_shapes=[pltpu.SemaphoreType.DMA((3 * n,)), pltpu.SemaphoreType.DMA((3 * n,))],
    )(*bufs)


HBM_SPEC = pl.BlockSpec(memory_space=pltpu.HBM)
SEM_SPEC = pl.BlockSpec(memory_space=pltpu.SEMAPHORE)
DATAFLOW = pltpu.SideEffectType.DATAFLOW_SIDE_EFFECTING
TOKEN = (8, 128)


def _in_hbm(a):
    return pltpu.with_memory_space_constraint(a, pltpu.HBM)


def gather_ici_start(bufs):
    n = len(bufs)

    def body(*refs):
        ins = refs[:n]
        send_sems, recv_sems = refs[n], refs[n + 1]
        token = refs[-1]
        x, y, c, chips = _place()
        for a in range(n):
            mine = ins[a].at[4 * x + 2 * y + c]
            for j, (px, py) in enumerate(chips):
                _remote(mine, mine, send_sems, recv_sems, 3 * a + j, (px, py, c)).start()
        token[...] = jnp.zeros_like(token)

    out = pl.pallas_call(
        body, name="gather_ici_start",
        out_shape=(pltpu.SemaphoreType.DMA((3 * n,)), pltpu.SemaphoreType.DMA((3 * n,)),
                   *[pltpu.HBM(b.shape, b.dtype) for b in bufs], _sds(TOKEN, F32)),
        in_specs=[HBM_SPEC] * n,
        out_specs=(SEM_SPEC, SEM_SPEC, *[HBM_SPEC] * n, pl.BlockSpec(memory_space=pltpu.VMEM)),
        input_output_aliases={i: 2 + i for i in range(n)},
        compiler_params=pltpu.CompilerParams(has_side_effects=DATAFLOW),
    )(*[_in_hbm(b) for b in bufs])
    return out[0], out[1], list(out[2:2 + n]), out[-1]


def gather_ici_wait(send_sems, recv_sems, bufs, after):
    n = len(bufs)

    def body(*refs):
        ins = refs[:n]
        send_sems, recv_sems = refs[n], refs[n + 1]
        x, y, c, chips = _place()
        for a in range(n):
            mine = ins[a].at[4 * x + 2 * y + c]
            for j, (px, py) in enumerate(chips):
                cp = _remote(mine, ins[a].at[4 * px + 2 * py + c], send_sems, recv_sems, 3 * a + j, (px, py, c))
                cp.wait_send()
                cp.wait_recv()

    out = pl.pallas_call(
        body, name="gather_ici_wait",
        out_shape=[pltpu.HBM(b.shape, b.dtype) for b in bufs],
        in_specs=[HBM_SPEC] * n + [SEM_SPEC, SEM_SPEC, ANY], out_specs=[HBM_SPEC] * n,
        input_output_aliases={i: i for i in range(n)},
        compiler_params=pltpu.CompilerParams(has_side_effects=DATAFLOW),
    )(*bufs, send_sems, recv_sems, after)
    return list(out)


def gather_d2d(bufs):
    n = len(bufs)

    def body(*refs):
        outs = refs[n:2 * n]
        token = refs[2 * n]
        send_sems, recv_sems = refs[2 * n + 1:]
        x, y, c, _ = _place()
        sent = []
        for a in range(n):
            for j in range(N_CHIP):
                slot = outs[a].at[2 * j + c]
                cp = _remote(slot, slot, send_sems, recv_sems, N_CHIP * a + j, (x, y, 1 - c))
                cp.start()
                sent.append(cp)
        for a in range(n):
            for j in range(N_CHIP):
                slot = outs[a].at[2 * j + 1 - c]
                _remote(slot, slot, send_sems, recv_sems, N_CHIP * a + j, (x, y, 1 - c)).wait_recv()
        for cp in sent:
            cp.wait_send()
        token[...] = jnp.zeros_like(token)

    out = pl.pallas_call(
        body, name="gather_d2d", in_specs=[ANY] * n,
        out_specs=[ANY] * n + [pl.BlockSpec(memory_space=pltpu.VMEM)],
        out_shape=[_sds(b.shape, b.dtype) for b in bufs] + [_sds(TOKEN, F32)],
        input_output_aliases={a: a for a in range(n)},
        scratch_shapes=[pltpu.SemaphoreType.DMA((N_CHIP * n,)), pltpu.SemaphoreType.DMA((N_CHIP * n,))],
    )(*bufs)
    return list(out[:n]), out[n]


def scatter_d2d(parts):
    n = len(parts)

    def body(*refs):
        ins, outs = refs[:n], refs[n:2 * n]
        send_sems, recv_sems = refs[2 * n:]
        x, y, c, _ = _place()
        sent = []
        for a in range(n):
            for j in range(N_CHIP):
                cp = _remote(ins[a].at[2 * j + 1 - c], outs[a].at[j], send_sems, recv_sems, N_CHIP * a + j, (x, y, 1 - c))
                cp.start()
                sent.append(cp)
        for a in range(n):
            for j in range(N_CHIP):
                slot = outs[a].at[j]
                _remote(slot, slot, send_sems, recv_sems, N_CHIP * a + j, (x, y, 1 - c)).wait_recv()
        for cp in sent:
            cp.wait_send()

    return pl.pallas_call(
        body, name="scatter_d2d", in_specs=[ANY] * n, out_specs=[ANY] * n,
        out_shape=[_sds((N_CHIP,) + p.shape[1:], p.dtype) for p in parts],
        scratch_shapes=[pltpu.SemaphoreType.DMA((N_CHIP * n,)), pltpu.SemaphoreType.DMA((N_CHIP * n,))],
    )(*parts)


def pair_sum(part, got, c_arr, chip_arr):
    _, R, C = part.shape
    tr = _tile(R, 512)
    p4 = part.reshape(N_CHIP, 2, R, C)

    def body(c_ref, chip_ref, p_ref, g_ref, o_ref, own_ref):
        s = (p_ref[...].astype(F32) + g_ref[...].astype(F32)).astype(o_ref.dtype)
        o_ref[...] = s

        @pl.when(pl.program_id(1) == chip_ref[0])
        def _():
            own_ref[...] = s

    return pl.pallas_call(
        body, name="pair_sum",
        grid_spec=pltpu.PrefetchScalarGridSpec(
            num_scalar_prefetch=2, grid=(R // tr, N_CHIP),
            in_specs=[pl.BlockSpec((None, None, tr, C), lambda i, j, c_ref, chip_ref: (j, c_ref[0], i, 0)),
                      pl.BlockSpec((None, tr, C), lambda i, j, c_ref, chip_ref: (j, i, 0))],
            out_specs=[pl.BlockSpec((None, tr, C), lambda i, j, c_ref, chip_ref: (j, i, 0)),
                       pl.BlockSpec((None, tr, C), lambda i, j, c_ref, chip_ref: (chip_ref[0], i, 0))]),
        out_shape=[_sds((N_CHIP, R, C), part.dtype), _sds((N_CHIP, R, C), part.dtype)],
        compiler_params=_params(("parallel", "arbitrary")),
    )(c_arr, chip_arr, p4, got)


def scatter_ici_start(sums, lands):
    n = len(sums)

    def body(*refs):
        ins, dsts = refs[:n], refs[n:2 * n]
        send_sems, recv_sems = refs[2 * n], refs[2 * n + 1]
        token = refs[-1]
        x, y, c, chips = _place()
        mine = 2 * x + y
        for a in range(n):
            for j, (px, py) in enumerate(chips):
                _remote(ins[a].at[2 * px + py], dsts[a].at[mine], send_sems, recv_sems, 3 * a + j, (px, py, c)).start()
        token[...] = jnp.zeros_like(token)

    both = list(sums) + list(lands)
    out = pl.pallas_call(
        body, name="scatter_ici_start",
        out_shape=(pltpu.SemaphoreType.DMA((3 * n,)), pltpu.SemaphoreType.DMA((3 * n,)),
                   *[pltpu.HBM(b.shape, b.dtype) for b in both], _sds(TOKEN, F32)),
        in_specs=[HBM_SPEC] * (2 * n),
        out_specs=(SEM_SPEC, SEM_SPEC, *[HBM_SPEC] * (2 * n), pl.BlockSpec(memory_space=pltpu.VMEM)),
        input_output_aliases={i: 2 + i for i in range(2 * n)},
        compiler_params=pltpu.CompilerParams(has_side_effects=DATAFLOW),
    )(*[_in_hbm(b) for b in both])
    return out[0], out[1], list(out[2:2 + n]), list(out[2 + n:2 + 2 * n]), out[-1]


def scatter_ici_wait(send_sems, recv_sems, sums, lands, after):
    n = len(sums)

    def body(*refs):
        ins, dsts = refs[:n], refs[n:2 * n]
        send_sems, recv_sems = refs[2 * n], refs[2 * n + 1]
        x, y, c, chips = _place()
        for a in range(n):
            for j, (px, py) in enumerate(chips):
                cp = _remote(ins[a].at[2 * px + py], dsts[a].at[2 * px + py], send_sems, recv_sems, 3 * a + j, (px, py, c))
                cp.wait_send()
                cp.wait_recv()

    both = list(sums) + list(lands)
    out = pl.pallas_call(
        body, name="scatter_ici_wait",
        out_shape=[pltpu.HBM(b.shape, b.dtype) for b in both],
        in_specs=[HBM_SPEC] * (2 * n) + [SEM_SPEC, SEM_SPEC, ANY], out_specs=[HBM_SPEC] * (2 * n),
        input_output_aliases={i: i for i in range(2 * n)},
        compiler_params=pltpu.CompilerParams(has_side_effects=DATAFLOW),
    )(*both, send_sems, recv_sems, after)
    return list(out[:n]), list(out[n:2 * n])


def adamw(w, m, v, parts_per_layer):
    NL, R, C = w.shape
    P = parts_per_layer[0].shape[0]
    tr = _tile(R, max(16, ADAMW_BLOCK_ELEMS // C))
    nb = R // tr

    def body(w_ref, m_ref, v_ref, *rest):
        part_refs = rest[:NL]
        g_ref, d_ref, nm_ref, nv_ref = rest[NL:]
        layer = pl.program_id(0)
        g = None
        for q in range(NL):
            s = part_refs[q][0].astype(F32)
            for t in range(1, P):
                s = s + part_refs[q][t].astype(F32)
            s = jnp.where(layer == q, s, 0.0)
            g = s if g is None else g + s
        wv = w_ref[...]
        nm = ADAM_B1 * m_ref[...] + (1.0 - ADAM_B1) * g
        nv = ADAM_B2 * v_ref[...] + (1.0 - ADAM_B2) * (g * g)
        m_hat = nm / (1.0 - ADAM_B1 ** ADAM_STEP)
        v_hat = nv / (1.0 - ADAM_B2 ** ADAM_STEP)
        g_ref[...] = g
        d_ref[...] = -ADAM_LR * (m_hat / (jnp.sqrt(v_hat) + ADAM_EPS) + ADAM_WD * wv)
        nm_ref[...] = nm
        nv_ref[...] = nv

    def part_spec(q):
        return pl.BlockSpec((P, tr, C), lambda l, i: (0, jnp.where(l == q, i, jnp.where(l < q, 0, nb - 1)), 0))

    blk = pl.BlockSpec((None, tr, C), lambda l, i: (l, i, 0))
    return pl.pallas_call(
        body, name="adamw", grid=(NL, nb),
        in_specs=[blk, blk, blk] + [part_spec(q) for q in range(NL)],
        out_specs=[blk, blk, blk, blk], out_shape=[_sds((NL, R, C), F32)] * 4,
        compiler_params=_params(("arbitrary", "arbitrary")),
    )(w, m, v, *parts_per_layer)


SMALL = ("norm1_g", "gmlp_ln_g", "w_spatial", "b_spatial", "mix_norm_attn_g", "mix_norm_gmlp_g", "norm2_g")
BIG = ("w_in", "w_out", "w_gate", "w_up", "w_down")
TRANSPOSED = ("w_gate", "w_up")
LANES = 128


def _pack(layers, final):
    flat = [layer[n].reshape(-1) for layer in layers for n in SMALL] + [final.reshape(-1)]
    return jnp.concatenate(flat).reshape(-1, LANES)


def _unpack(packed, like_layers, like_final):
    flat = packed.reshape(-1)
    out, off = [], 0
    for layer in like_layers:
        d = {}
        for n in SMALL:
            size = layer[n].size
            d[n] = flat[off:off + size].reshape(layer[n].shape)
            off += size
        out.append(d)
    return out, flat[off:off + like_final.size].reshape(like_final.shape)


def kernel(x, norm1_g, w_in, gmlp_ln_g, w_spatial, b_spatial, mix_norm_attn_g, mix_norm_gmlp_g, w_out, norm2_g, w_gate, w_up, w_down, final_g, loss_target, m_norm1_g, m_w_in, m_gmlp_ln_g, m_w_spatial, m_b_spatial, m_mix_norm_attn_g, m_mix_norm_gmlp_g, m_w_out, m_norm2_g, m_w_gate, m_w_up, m_w_down, m_final_g, v_norm1_g, v_w_in, v_gmlp_ln_g, v_w_spatial, v_b_spatial, v_mix_norm_attn_g, v_mix_norm_gmlp_g, v_w_out, v_norm2_g, v_w_gate, v_w_up, v_w_down, v_final_g):
    S, D = x.shape[1], x.shape[2]
    NL = norm1_g.shape[0]
    DA = D // 2
    xs = x.reshape(S, D)
    tabs = rope_tables(S)
    ax, ay, ac = lax.axis_index("x"), lax.axis_index("y"), lax.axis_index("c")
    me_arr = (4 * ax + 2 * ay + ac).astype(jnp.int32).reshape(1)
    c_arr = ac.astype(jnp.int32).reshape(1)
    chip_arr = (2 * ax + ay).astype(jnp.int32).reshape(1)
    small_w = dict(norm1_g=norm1_g, gmlp_ln_g=gmlp_ln_g, w_spatial=w_spatial, b_spatial=b_spatial,
                   mix_norm_attn_g=mix_norm_attn_g, mix_norm_gmlp_g=mix_norm_gmlp_g, norm2_g=norm2_g)
    small_m = dict(norm1_g=m_norm1_g, gmlp_ln_g=m_gmlp_ln_g, w_spatial=m_w_spatial, b_spatial=m_b_spatial,
                   mix_norm_attn_g=m_mix_norm_attn_g, mix_norm_gmlp_g=m_mix_norm_gmlp_g, norm2_g=m_norm2_g)
    small_v = dict(norm1_g=v_norm1_g, gmlp_ln_g=v_gmlp_ln_g, w_spatial=v_w_spatial, b_spatial=v_b_spatial,
                   mix_norm_attn_g=v_mix_norm_attn_g, mix_norm_gmlp_g=v_mix_norm_gmlp_g, norm2_g=v_norm2_g)
    def view(n, a):
        return jnp.swapaxes(a, 1, 2) if n in TRANSPOSED else a

    big_w = {n: view(n, a) for n, a in dict(w_in=w_in, w_out=w_out, w_gate=w_gate, w_up=w_up, w_down=w_down).items()}
    big_m = {n: view(n, a) for n, a in dict(w_in=m_w_in, w_out=m_w_out, w_gate=m_w_gate, w_up=m_w_up, w_down=m_w_down).items()}
    big_v = {n: view(n, a) for n, a in dict(w_in=v_w_in, w_out=v_w_out, w_gate=v_w_gate, w_up=v_w_up, w_down=v_w_down).items()}

    def layer_small(l):
        ws = w_spatial[l]
        return dict(norm1_g=norm1_g[l][None], gmlp_ln_g=gmlp_ln_g[l][None], mix_norm_attn_g=mix_norm_attn_g[l][None],
                    mix_norm_gmlp_g=mix_norm_gmlp_g[l][None], norm2_g=norm2_g[l][None], w_spatial=ws,
                    w_spatial_t=jnp.swapaxes(ws, 1, 2), bcol=jnp.broadcast_to(b_spatial[l][:, :, None], ws.shape))

    def cast_shards(l, dep):
        return [to_slot(big_w[n], l, me_arr, dep, BF16) for n in BIG]

    pending = gather_ici_start(cast_shards(0, jnp.zeros(TOKEN, F32)))
    h = xs
    saved, weights = [], []
    for l in range(NL):
        send_sems, recv_sems, bufs, _ = pending
        got, dep = gather_d2d(gather_ici_wait(send_sems, recv_sems, bufs, h))
        gw = dict(zip(BIG, got))
        gw["w_out"] = gw["w_out"].reshape(2, DA, D)
        if l + 1 < NL:
            pending = gather_ici_start(cast_shards(l + 1, dep))
            dep = pending[3]
        sm = layer_small(l)
        sm["norm1_g"] = sm["norm1_g"] + dep[0, 0]
        h, sv = layer_fwd(h, sm, gw, tabs)
        saved.append(sv)
        weights.append(gw)
    loss_part, dx, dxb, d_final = loss_and_grad(h, final_g[None], loss_target.reshape(S, D))
    loss = lax.psum(loss_part[0, 0], ("x", "y", "c"))

    def finish_scatter(pend, after):
        send_sems, recv_sems, sums, lands, _ = pend
        _, lands = scatter_ici_wait(send_sems, recv_sems, sums, lands, after)
        return dict(zip(BIG, lands))

    big_sums = [None] * NL
    small_grads = [None] * NL
    pending = None
    for l in reversed(range(NL)):
        dx, dxb, big, small = layer_bwd(dx, dxb, saved[l], layer_small(l), weights[l], tabs, dep)
        small_grads[l] = small
        if pending is not None:
            big_sums[l + 1] = finish_scatter(pending, dx)
        parts = [big[n].reshape((N_DEV, -1, big[n].shape[-1])) for n in BIG]
        got = scatter_d2d(parts)
        pairs = [pair_sum(p, g, c_arr, chip_arr) for p, g in zip(parts, got)]
        pending = scatter_ici_start([s for s, _ in pairs], [o for _, o in pairs])
        dep = pending[4]

    packed = to_slot(_pack(small_grads, d_final), None, me_arr, dep, F32)
    gathered_small = gather_d2d(gather_ici([packed]))[0][0]
    pw = _pack([{n: small_w[n][l] for n in SMALL} for l in range(NL)], final_g)[None]
    pm = _pack([{n: small_m[n][l] for n in SMALL} for l in range(NL)], m_final_g)[None]
    pv = _pack([{n: small_v[n][l] for n in SMALL} for l in range(NL)], v_final_g)[None]
    like_layers = [{n: small_w[n][l] for n in SMALL} for l in range(NL)]
    small_res = adamw(pw, pm, pv, [gathered_small])
    small_out = [_unpack(t[0], like_layers, final_g) for t in small_res]
    big_sums[0] = finish_scatter(pending, small_res[0])

    def small_stack(k, n):
        return jnp.stack([small_out[k][0][l][n] for l in range(NL)])

    big_out = {}
    for n in BIG:
        R, C = big_w[n].shape[1], big_w[n].shape[2]
        res = adamw(big_w[n], big_m[n], big_v[n], [big_sums[l][n].reshape(N_CHIP, R, C) for l in range(NL)])
        big_out[n] = [view(n, t) for t in res]

    order = ("norm1_g", "w_in", "gmlp_ln_g", "w_spatial", "b_spatial", "mix_norm_attn_g", "mix_norm_gmlp_g", "w_out",
             "norm2_g", "w_gate", "w_up", "w_down")
    outs = [loss, dx.reshape(x.shape)]
    for k in range(4):
        for n in order:
            outs.append(big_out[n][k] if n in BIG else small_stack(k, n))
        outs.append(small_out[k][1])
    return tuple(outs)
```

```python
import functools

import jax
import jax.numpy as jnp
from jax import lax
from jax.experimental import pallas as pl
from jax.experimental.pallas import tpu as pltpu

F32 = jnp.float32
BF16 = jnp.bfloat16

HEAD_DIM = 128
CHUNK = 128
GROUP = 128
N_SIDE = 64
DILATIONS = (1, 4, 16)
TQ = 128
HALO = 64
EPS = 1e-6
NEG = -1e30
ROPE_THETA = 10000.0
ADAM_LR = 0.001
ADAM_B1 = 0.9
ADAM_B2 = 0.999
ADAM_EPS = 1e-08
ADAM_WD = 0.01
ADAM_STEP = 10
N_DEV = 8
N_CHIP = 4
VMEM_LIMIT_V7X = 56 * 1024 * 1024
ADAMW_BLOCK_ELEMS = 128 * 1024
MESH = pl.DeviceIdType.MESH
ANY = pl.BlockSpec(memory_space=pl.ANY)


def _params(sem=None):
    return pltpu.CompilerParams(dimension_semantics=sem, vmem_limit_bytes=VMEM_LIMIT_V7X)


def _tile(n, want):
    for t in range(min(want, n), 15, -1):
        if n % t == 0 and t % 16 == 0:
            return t
    return n


def _sds(shape, dtype):
    return jax.ShapeDtypeStruct(shape, dtype)


def _dot(a, b, dims):
    return lax.dot_general(a, b, (dims, ((), ())), preferred_element_type=F32)


NN = ((1,), (0,))
NT = ((1,), (1,))
TN = ((0,), (0,))


def _sigmoid(x):
    return 1.0 / (1.0 + jnp.exp(-x))


def _gelu(x):
    return 0.5 * x * (1.0 + lax.erf(x * 0.7071067811865476))


def _gelu_grad(x):
    cdf = 0.5 * (1.0 + lax.erf(x * 0.7071067811865476))
    pdf = 0.3989422804014327 * jnp.exp(-0.5 * x * x)
    return cdf + x * pdf


def rms_fwd(x, g):
    S, D = x.shape
    tr = _tile(S, 512)

    def body(x_ref, g_ref, o_ref):
        xv = x_ref[...]
        r = lax.rsqrt(jnp.mean(xv * xv, axis=-1, keepdims=True) + EPS)
        o_ref[...] = (xv * r * g_ref[...]).astype(o_ref.dtype)

    return pl.pallas_call(
        body, name="rms_fwd", grid=(S // tr,),
        in_specs=[pl.BlockSpec((tr, D), lambda i: (i, 0)), pl.BlockSpec((1, D), lambda i: (0, 0))],
        out_specs=pl.BlockSpec((tr, D), lambda i: (i, 0)),
        out_shape=_sds((S, D), BF16), compiler_params=_params(("parallel",)),
    )(x, g)


def _rms_bwd_math(xv, gv, dy):
    r = lax.rsqrt(jnp.mean(xv * xv, axis=-1, keepdims=True) + EPS)
    dyg = dy * gv
    dx = r * dyg - xv * (r * r * r) * jnp.mean(dyg * xv, axis=-1, keepdims=True)
    dg = jnp.sum(dy * xv * r, axis=0, keepdims=True)
    return dx, dg


def rms_bwd_res(x, g, dy, dres):
    S, D = x.shape
    tr = _tile(S, 512)

    def body(x_ref, g_ref, dy_ref, dres_ref, dx_ref, dxb_ref, dg_ref):
        dx, dg = _rms_bwd_math(x_ref[...], g_ref[...], dy_ref[...])
        tot = dres_ref[...] + dx
        dx_ref[...] = tot
        dxb_ref[...] = tot.astype(dxb_ref.dtype)

        @pl.when(pl.program_id(0) == 0)
        def _():
            dg_ref[...] = dg

        @pl.when(pl.program_id(0) > 0)
        def _():
            dg_ref[...] += dg

    row = pl.BlockSpec((tr, D), lambda i: (i, 0))
    vec = pl.BlockSpec((1, D), lambda i: (0, 0))
    return pl.pallas_call(
        body, name="rms_bwd_res", grid=(S // tr,),
        in_specs=[row, vec, row, row], out_specs=[row, row, vec],
        out_shape=[_sds((S, D), F32), _sds((S, D), BF16), _sds((1, D), F32)],
        compiler_params=_params(("arbitrary",)),
    )(x, g, dy, dres)


def loss_and_grad(x, g, target):
    S, D = x.shape
    tr = _tile(S, 512)

    def body(x_ref, g_ref, t_ref, loss_ref, dx_ref, dxb_ref, dg_ref):
        xv = x_ref[...]
        gv = g_ref[...]
        r = lax.rsqrt(jnp.mean(xv * xv, axis=-1, keepdims=True) + EPS)
        y = xv * r * gv
        diff = y - t_ref[...]
        part = 0.5 * jnp.sum(jnp.mean(diff * diff, axis=-1, keepdims=True), axis=0, keepdims=True)
        dy = diff * (1.0 / D)
        dx, dg = _rms_bwd_math(xv, gv, dy)
        dx_ref[...] = dx
        dxb_ref[...] = dx.astype(dxb_ref.dtype)

        @pl.when(pl.program_id(0) == 0)
        def _():
            dg_ref[...] = dg
            loss_ref[...] = jnp.broadcast_to(part, loss_ref.shape)

        @pl.when(pl.program_id(0) > 0)
        def _():
            dg_ref[...] += dg
            loss_ref[...] += jnp.broadcast_to(part, loss_ref.shape)

    row = pl.BlockSpec((tr, D), lambda i: (i, 0))
    vec = pl.BlockSpec((1, D), lambda i: (0, 0))
    return pl.pallas_call(
        body, name="loss_and_grad", grid=(S // tr,),
        in_specs=[row, vec, row],
        out_specs=[pl.BlockSpec((8, 128), lambda i: (0, 0)), row, row, vec],
        out_shape=[_sds((8, 128), F32), _sds((S, D), F32), _sds((S, D), BF16), _sds((1, D), F32)],
        compiler_params=_params(("arbitrary",)),
    )(x, g, target)


def _mm_body(n_pairs, dims, red_axis, n_red, has_res):
    def body(*refs):
        ins = refs[:2 * n_pairs]
        res_ref = refs[2 * n_pairs] if has_res else None
        o_ref = refs[2 * n_pairs + has_res]
        p = None
        for t in range(n_pairs):
            d = _dot(ins[2 * t][...], ins[2 * t + 1][...], dims)
            p = d if p is None else p + d
        if red_axis is None:
            if has_res:
                p = res_ref[...] + p
            o_ref[...] = p.astype(o_ref.dtype)
            return
        acc_ref = o_ref if o_ref.dtype == F32 else refs[2 * n_pairs + has_res + 1]
        r = pl.program_id(red_axis)

        @pl.when(r == 0)
        def _():
            acc_ref[...] = res_ref[...] + p if has_res else p

        @pl.when(r > 0)
        def _():
            acc_ref[...] += p

        if acc_ref is not o_ref:
            @pl.when(r == n_red - 1)
            def _():
                o_ref[...] = acc_ref[...].astype(o_ref.dtype)

    return body


def mm_in(h, w):
    S, D = h.shape
    NS, _, n = w.shape
    tm = _tile(S, 1024)
    return pl.pallas_call(
        _mm_body(1, NN, None, 1, False), name="mm_in", grid=(S // tm, NS),
        in_specs=[pl.BlockSpec((tm, D), lambda i, k: (i, 0)), pl.BlockSpec((None, D, n), lambda i, k: (k, 0, 0))],
        out_specs=pl.BlockSpec((tm, n), lambda i, k: (i, k)),
        out_shape=_sds((S, NS * n), F32), compiler_params=_params(("parallel", "parallel")),
    )(h, w)


def mm_out(x, mix_a, mix_g, w):
    S, D = x.shape
    DA = mix_a.shape[1]
    tm = _tile(S, 1024)
    tn = _tile(D, 1024)
    act = pl.BlockSpec((tm, DA), lambda i, j: (i, 0))
    return pl.pallas_call(
        _mm_body(2, NN, None, 1, True), name="mm_out", grid=(S // tm, D // tn),
        in_specs=[act, pl.BlockSpec((None, DA, tn), lambda i, j: (0, 0, j)),
                  act, pl.BlockSpec((None, DA, tn), lambda i, j: (1, 0, j)),
                  pl.BlockSpec((tm, tn), lambda i, j: (i, j))],
        out_specs=pl.BlockSpec((tm, tn), lambda i, j: (i, j)),
        out_shape=_sds((S, D), F32), compiler_params=_params(("parallel", "parallel")),
    )(mix_a, w, mix_g, w, x)


def _lane_tile(n, want):
    for t in range(min(want, n) // 128 * 128, 127, -128):
        if n % t == 0:
            return t
    return n


def ff_fwd(h, wg, wu):
    S, D = h.shape
    F = wg.shape[0]
    tm = _tile(S, 1024)
    tn = _lane_tile(F, 512)

    def body(h_ref, wg_ref, wu_ref, gate_ref, up_ref, ff_ref):
        hv = h_ref[...]
        g = _dot(hv, wg_ref[...], NT)
        u = _dot(hv, wu_ref[...], NT)
        gate_ref[...] = g
        up_ref[...] = u
        ff_ref[...] = (g * _sigmoid(g) * u).astype(ff_ref.dtype)

    wspec = pl.BlockSpec((tn, D), lambda i, j: (j, 0))
    ospec = pl.BlockSpec((tm, tn), lambda i, j: (i, j))
    return pl.pallas_call(
        body, name="ff_fwd", grid=(S // tm, F // tn),
        in_specs=[pl.BlockSpec((tm, D), lambda i, j: (i, 0)), wspec, wspec],
        out_specs=[ospec, ospec, ospec],
        out_shape=[_sds((S, F), F32), _sds((S, F), F32), _sds((S, F), BF16)],
        compiler_params=_params(("parallel", "parallel")),
    )(h, wg, wu)


def mm_down(x, ff, wd):
    S, D = x.shape
    F = ff.shape[1]
    tm = _tile(S, 512)
    tn = _lane_tile(D, 512)
    blk = pl.BlockSpec((tm, tn), lambda i, j: (i, j))
    return pl.pallas_call(
        _mm_body(1, NN, None, 1, True), name="mm_down", grid=(S // tm, D // tn),
        in_specs=[pl.BlockSpec((tm, F), lambda i, j: (i, 0)), pl.BlockSpec((F, tn), lambda i, j: (0, j)), blk],
        out_specs=blk, out_shape=_sds((S, D), F32), compiler_params=_params(("parallel", "parallel")),
    )(ff, wd, x)


def ff_bwd_act(dxb, wd, gate, up, dep):
    S, D = dxb.shape
    F = wd.shape[0]
    tm = _tile(S, 1024)
    tn = _lane_tile(F, 512)

    def body(dx_ref, wd_ref, gate_ref, up_ref, dep_ref, dgate_ref, dup_ref):
        dff = _dot(dx_ref[...], wd_ref[...], NT)
        g = gate_ref[...]
        sg = _sigmoid(g)
        dup_ref[...] = (dff * (g * sg)).astype(dup_ref.dtype)
        dgate_ref[...] = (dff * up_ref[...] * (sg * (1.0 + g * (1.0 - sg)))).astype(dgate_ref.dtype)

    aspec = pl.BlockSpec((tm, tn), lambda i, j: (i, j))
    return pl.pallas_call(
        body, name="ff_bwd_act", grid=(S // tm, F // tn),
        in_specs=[pl.BlockSpec((tm, D), lambda i, j: (i, 0)), pl.BlockSpec((tn, D), lambda i, j: (j, 0)), aspec, aspec,
                  pl.BlockSpec(TOKEN, lambda i, j: (0, 0))],
        out_specs=[aspec, aspec],
        out_shape=[_sds((S, F), BF16), _sds((S, F), BF16)],
        compiler_params=_params(("parallel", "parallel")),
    )(dxb, wd, gate, up, dep)


def dw_down(ff, dxb):
    S, F = ff.shape
    D = dxb.shape[1]
    tf = _lane_tile(F, 256)
    return pl.pallas_call(
        _mm_body(1, TN, None, 1, False), name="dw_down", grid=(F // tf,),
        in_specs=[pl.BlockSpec((S, tf), lambda k: (0, k)), pl.BlockSpec((S, D), lambda k: (0, 0))],
        out_specs=pl.BlockSpec((tf, D), lambda k: (k, 0)),
        out_shape=_sds((F, D), BF16), compiler_params=_params(("parallel",)),
    )(ff, dxb)


def dw_gate_up(h, dgate, dup):
    S, D = h.shape
    F = dgate.shape[1]
    f = _lane_tile(F, 256)

    def body(h_ref, dg_ref, du_ref, og_ref, ou_ref):
        hv = h_ref[...]
        og_ref[...] = _dot(dg_ref[...], hv, TN).astype(og_ref.dtype)
        ou_ref[...] = _dot(du_ref[...], hv, TN).astype(ou_ref.dtype)

    aspec = pl.BlockSpec((S, f), lambda k: (0, k))
    ospec = pl.BlockSpec((f, D), lambda k: (k, 0))
    return pl.pallas_call(
        body, name="dw_gate_up", grid=(F // f,),
        in_specs=[pl.BlockSpec((S, D), lambda k: (0, 0)), aspec, aspec],
        out_specs=[ospec, ospec],
        out_shape=[_sds((F, D), BF16), _sds((F, D), BF16)],
        compiler_params=_params(("parallel",)),
    )(h, dgate, dup)


def dh_ff(dgate, dup, wg, wu):
    S, F = dgate.shape
    D = wg.shape[1]
    tm = _tile(S, 512)
    tn = _lane_tile(D, 256)
    aspec = pl.BlockSpec((tm, F), lambda i, j: (i, 0))
    wspec = pl.BlockSpec((F, tn), lambda i, j: (0, j))
    return pl.pallas_call(
        _mm_body(2, NN, None, 1, False), name="dh_ff", grid=(S // tm, D // tn),
        in_specs=[aspec, wspec, aspec, wspec],
        out_specs=pl.BlockSpec((tm, tn), lambda i, j: (i, j)),
        out_shape=_sds((S, D), F32), compiler_params=_params(("parallel", "parallel")),
    )(dgate, wg, dup, wu)


def dmix_mm(dxb, w):
    S, D = dxb.shape
    tm = _tile(S, 1024)
    tn = _tile(D, 1024)
    return pl.pallas_call(
        _mm_body(1, NT, None, 1, False), name="dmix_mm", grid=(S // tm, D // tn),
        in_specs=[pl.BlockSpec((tm, D), lambda i, j: (i, 0)), pl.BlockSpec((tn, D), lambda i, j: (j, 0))],
        out_specs=pl.BlockSpec((tm, tn), lambda i, j: (i, j)),
        out_shape=_sds((S, D), F32), compiler_params=_params(("parallel", "parallel")),
    )(dxb, w)


def dw_out(mix_a, mix_g, dxb):
    S, DA = mix_a.shape
    D = dxb.shape[1]
    tn = _lane_tile(D, 512)

    def half(name, m):
        return pl.pallas_call(
            _mm_body(1, TN, None, 1, False), name=name, grid=(D // tn,),
            in_specs=[pl.BlockSpec((S, DA), lambda j: (0, 0)), pl.BlockSpec((S, tn), lambda j: (0, j))],
            out_specs=pl.BlockSpec((DA, tn), lambda j: (0, j)),
            out_shape=_sds((DA, D), BF16), compiler_params=_params(("parallel",)),
        )(m, dxb)

    return half("dw_out_a", mix_a), half("dw_out_g", mix_g)


def dw_in(h, dproj, NS):
    S, D = h.shape
    n = dproj.shape[1] // NS
    tm = _lane_tile(D, 1024)
    return pl.pallas_call(
        _mm_body(1, TN, None, 1, False), name="dw_in", grid=(NS, D // tm),
        in_specs=[pl.BlockSpec((S, tm), lambda k, j: (0, j)), pl.BlockSpec((S, n), lambda k, j: (0, k))],
        out_specs=pl.BlockSpec((None, tm, n), lambda k, j: (k, j, 0)),
        out_shape=_sds((NS, D, n), BF16), compiler_params=_params(("parallel", "parallel")),
    )(h, dproj)


def dh_in(dproj, w):
    S = dproj.shape[0]
    NS, D, n = w.shape
    tm = _tile(S, 512)
    tn = _tile(D, 512)

    def body(dp_ref, w_ref, o_ref):
        acc = None
        for k in range(NS):
            p = _dot(dp_ref[:, k * n:(k + 1) * n], w_ref[k], NT)
            acc = p if acc is None else acc + p
        o_ref[...] = acc

    return pl.pallas_call(
        body, name="dh_in", grid=(S // tm, D // tn),
        in_specs=[pl.BlockSpec((tm, NS * n), lambda i, j: (i, 0)), pl.BlockSpec((NS, tn, n), lambda i, j: (0, j, 0))],
        out_specs=pl.BlockSpec((tm, tn), lambda i, j: (i, j)),
        out_shape=_sds((S, D), F32), compiler_params=_params(("parallel", "parallel")),
    )(dproj, w)


def rope_tables(S):
    pos = jnp.arange(S, dtype=F32)
    inv = ROPE_THETA ** (-jnp.arange(0, HEAD_DIM, 2, dtype=F32) / HEAD_DIM)
    ang = pos[:, None] * inv[None, :]
    cos, sin = jnp.cos(ang), jnp.sin(ang)
    return jnp.concatenate([cos, cos], axis=-1), jnp.concatenate([-sin, sin], axis=-1)


def _rot_half(t):
    return pltpu.roll(t, HEAD_DIM // 2, 1)


def rope_qkv(proj, cos2, sin2, DA):
    S = proj.shape[0]
    tr = _tile(S, 512)
    nh = DA // HEAD_DIM

    def body(q_ref, k_ref, v_ref, cos_ref, sin_ref, o_ref):
        c = cos_ref[...]
        s = sin_ref[...]
        for h in range(nh):
            sl = slice(h * HEAD_DIM, (h + 1) * HEAD_DIM)
            for j, ref in enumerate((q_ref, k_ref)):
                t = ref[:, sl]
                o_ref[:, j * DA + h * HEAD_DIM:j * DA + (h + 1) * HEAD_DIM] = (t * c + _rot_half(t) * s).astype(o_ref.dtype)
        o_ref[:, 2 * DA:3 * DA] = v_ref[...].astype(o_ref.dtype)

    tab = pl.BlockSpec((tr, HEAD_DIM), lambda i: (i, 0))
    return pl.pallas_call(
        body, name="rope_qkv", grid=(S // tr,),
        in_specs=[pl.BlockSpec((tr, DA), lambda i: (i, 0)), pl.BlockSpec((tr, DA), lambda i: (i, 1)),
                  pl.BlockSpec((tr, DA), lambda i: (i, 2)), tab, tab],
        out_specs=pl.BlockSpec((tr, 3 * DA), lambda i: (i, 0)),
        out_shape=_sds((S, 3 * DA), BF16), compiler_params=_params(("parallel",)),
    )(proj, proj, proj, cos2, sin2)


def assemble_dproj(dqs, dks, dvs, duv, cos2, sin2):
    S, DA = dqs[0].shape
    tr = _tile(S, 256)
    nh = DA // HEAD_DIM

    def body(*refs):
        dq_refs, dk_refs, dv_refs = refs[0:3], refs[3:6], refs[6:9]
        duv_ref, cos_ref, sin_ref, o_ref = refs[9:13]
        c = cos_ref[...]
        s = sin_ref[...]
        for j, trio in enumerate((dq_refs, dk_refs)):
            for h in range(nh):
                sl = slice(h * HEAD_DIM, (h + 1) * HEAD_DIM)
                t = trio[0][:, sl] + trio[1][:, sl] + trio[2][:, sl]
                o_ref[:, j * DA + h * HEAD_DIM:j * DA + (h + 1) * HEAD_DIM] = (t * c - _rot_half(t) * s).astype(o_ref.dtype)
        o_ref[:, 2 * DA:3 * DA] = (dv_refs[0][...] + dv_refs[1][...] + dv_refs[2][...]).astype(o_ref.dtype)
        o_ref[:, 3 * DA:5 * DA] = duv_ref[...]

    blk = pl.BlockSpec((tr, DA), lambda i: (i, 0))
    tab = pl.BlockSpec((tr, HEAD_DIM), lambda i: (i, 0))
    return pl.pallas_call(
        body, name="assemble_dproj", grid=(S // tr,),
        in_specs=[blk] * 9 + [pl.BlockSpec((tr, 2 * DA), lambda i: (i, 0)), tab, tab],
        out_specs=pl.BlockSpec((tr, 5 * DA), lambda i: (i, 0)),
        out_shape=_sds((S, 5 * DA), BF16), compiler_params=_params(("parallel",)),
    )(*dqs, *dks, *dvs, duv, cos2, sin2)


def _halo_specs(L, width_blocks, col):
    per = TQ // HALO
    last = L // HALO - 1
    W = width_blocks
    return [
        pl.BlockSpec((HALO, W), lambda r, i: (jnp.maximum(i * per - 1, 0), col(r))),
        pl.BlockSpec((TQ, W), lambda r, i: (i, col(r))),
        pl.BlockSpec((HALO, W), lambda r, i: (jnp.minimum((i + 1) * per, last), col(r))),
    ]


def _cat3(refs, sl):
    return jnp.concatenate([refs[0][:, sl], refs[1][:, sl], refs[2][:, sl]], axis=0)


def attn_fwd(qkv, d, DA):
    S = qkv.shape[0]
    L = S // d
    nh = DA // HEAD_DIM
    scale = HEAD_DIM ** -0.5
    TK = TQ + 2 * HALO

    def body(q_ref, kp_ref, kc_ref, kn_ref, vp_ref, vc_ref, vn_ref, o_ref, lse_ref):
        i = pl.program_id(1)
        row = lax.broadcasted_iota(jnp.int32, (TQ, TK), 0)
        col = lax.broadcasted_iota(jnp.int32, (TQ, TK), 1)
        kpos = i * TQ - HALO + col
        mask = (jnp.abs(col - HALO - row) <= N_SIDE) & (kpos >= 0) & (kpos < L)
        for h in range(nh):
            sl = slice(h * HEAD_DIM, (h + 1) * HEAD_DIM)
            k = _cat3((kp_ref, kc_ref, kn_ref), sl)
            v = _cat3((vp_ref, vc_ref, vn_ref), sl)
            s = _dot(q_ref[:, sl], k, NT) * scale
            s = jnp.where(mask, s, NEG)
            m = jnp.max(s, axis=-1, keepdims=True)
            p = jnp.exp(s - m)
            l = jnp.sum(p, axis=-1, keepdims=True)
            o = _dot(p.astype(v.dtype), v, NN) / l
            o_ref[:, sl] = o
            lse_ref[:, sl] = jnp.broadcast_to(m + jnp.log(l), (TQ, HEAD_DIM))

    out_spec = pl.BlockSpec((TQ, DA), lambda r, i: (i, r))
    o, lse = pl.pallas_call(
        body, name="attn_fwd_d%d" % d, grid=(d, L // TQ),
        in_specs=[pl.BlockSpec((TQ, DA), lambda r, i: (i, 3 * r))]
        + _halo_specs(L, DA, lambda r: 3 * r + 1) + _halo_specs(L, DA, lambda r: 3 * r + 2),
        out_specs=[out_spec, out_spec],
        out_shape=[_sds((L, d * DA), F32), _sds((L, d * DA), F32)],
        compiler_params=_params(("parallel", "parallel")),
    )(*([qkv.reshape(L, d * 3 * DA)] * 7))
    return o.reshape(S, DA), lse.reshape(S, DA)


def attn_merge(os_, lses, g):
    S, DA = os_[0].shape
    tr = _tile(S, 256)

    def body(o0, o1, o2, l0, l1, l2, g_ref, a_ref, lse_ref, mix_ref):
        a0, a1, a2 = l0[...], l1[...], l2[...]
        m = jnp.maximum(jnp.maximum(a0, a1), a2)
        e0, e1, e2 = jnp.exp(a0 - m), jnp.exp(a1 - m), jnp.exp(a2 - m)
        den = e0 + e1 + e2
        a = (e0 / den) * o0[...] + (e1 / den) * o1[...] + (e2 / den) * o2[...]
        a_ref[...] = a
        lse_ref[...] = m + jnp.log(den)
        r = lax.rsqrt(jnp.mean(a * a, axis=-1, keepdims=True) + EPS)
        mix_ref[...] = (a * r * g_ref[...]).astype(mix_ref.dtype)

    blk = pl.BlockSpec((tr, DA), lambda i: (i, 0))
    return pl.pallas_call(
        body, name="attn_merge", grid=(S // tr,),
        in_specs=[blk] * 6 + [pl.BlockSpec((1, DA), lambda i: (0, 0))],
        out_specs=[blk, blk, blk],
        out_shape=[_sds((S, DA), F32), _sds((S, DA), F32), _sds((S, DA), BF16)],
        compiler_params=_params(("parallel",)),
    )(*os_, *lses, g)


def attn_out_bwd(a, g, dmix):
    S, DA = a.shape
    tr = _tile(S, 256)
    nh = DA // HEAD_DIM

    def body(a_ref, g_ref, dy_ref, do_ref, dl_ref, dg_ref):
        av = a_ref[...]
        dx, dg = _rms_bwd_math(av, g_ref[...], dy_ref[...])
        do_ref[...] = dx.astype(do_ref.dtype)
        prod = dx * av
        for h in range(nh):
            sl = slice(h * HEAD_DIM, (h + 1) * HEAD_DIM)
            dl_ref[:, sl] = jnp.broadcast_to(jnp.sum(prod[:, sl], axis=-1, keepdims=True), (tr, HEAD_DIM))

        @pl.when(pl.program_id(0) == 0)
        def _():
            dg_ref[...] = dg

        @pl.when(pl.program_id(0) > 0)
        def _():
            dg_ref[...] += dg

    blk = pl.BlockSpec((tr, DA), lambda i: (i, 0))
    vec = pl.BlockSpec((1, DA), lambda i: (0, 0))
    return pl.pallas_call(
        body, name="attn_out_bwd", grid=(S // tr,),
        in_specs=[blk, vec, blk], out_specs=[blk, blk, vec],
        out_shape=[_sds((S, DA), BF16), _sds((S, DA), F32), _sds((1, DA), F32)],
        compiler_params=_params(("arbitrary",)),
    )(a, g, dmix)


def attn_bwd_dq(qkv, dob, lse, dl, d, DA):
    S = qkv.shape[0]
    L = S // d
    nh = DA // HEAD_DIM
    scale = HEAD_DIM ** -0.5
    TK = TQ + 2 * HALO

    def body(q_ref, kp_ref, kc_ref, kn_ref, vp_ref, vc_ref, vn_ref, do_ref, lse_ref, dl_ref, dq_ref):
        i = pl.program_id(1)
        row = lax.broadcasted_iota(jnp.int32, (TQ, TK), 0)
        col = lax.broadcasted_iota(jnp.int32, (TQ, TK), 1)
        kpos = i * TQ - HALO + col
        mask = (jnp.abs(col - HALO - row) <= N_SIDE) & (kpos >= 0) & (kpos < L)
        for h in range(nh):
            sl = slice(h * HEAD_DIM, (h + 1) * HEAD_DIM)
            k = _cat3((kp_ref, kc_ref, kn_ref), sl)
            v = _cat3((vp_ref, vc_ref, vn_ref), sl)
            s = _dot(q_ref[:, sl], k, NT) * scale
            s = jnp.where(mask, s, NEG)
            p = jnp.exp(s - lse_ref[:, sl][:, 0:1])
            dp = _dot(do_ref[:, sl], v, NT)
            ds = p * (dp - dl_ref[:, sl][:, 0:1])
            dq_ref[:, sl] = _dot(ds.astype(k.dtype), k, NN) * scale

    blk = pl.BlockSpec((TQ, DA), lambda r, i: (i, r))
    dq = pl.pallas_call(
        body, name="attn_bwd_dq_d%d" % d, grid=(d, L // TQ),
        in_specs=[pl.BlockSpec((TQ, DA), lambda r, i: (i, 3 * r))]
        + _halo_specs(L, DA, lambda r: 3 * r + 1) + _halo_specs(L, DA, lambda r: 3 * r + 2) + [blk, blk, blk],
        out_specs=blk, out_shape=_sds((L, d * DA), F32),
        compiler_params=_params(("parallel", "parallel")),
    )(*([qkv.reshape(L, d * 3 * DA)] * 7), dob.reshape(L, d * DA), lse.reshape(L, d * DA), dl.reshape(L, d * DA))
    return dq.reshape(S, DA)


def attn_bwd_dkv(qkv, dob, lse, dl, d, DA):
    S = qkv.shape[0]
    L = S // d
    nh = DA // HEAD_DIM
    scale = HEAD_DIM ** -0.5
    TR = TQ + 2 * HALO

    def body(k_ref, v_ref, qp, qc, qn, dop, doc, don, lp, lc, ln, dp_, dc_, dn_, dk_ref, dv_ref):
        j = pl.program_id(1)
        row = lax.broadcasted_iota(jnp.int32, (TR, TQ), 0)
        col = lax.broadcasted_iota(jnp.int32, (TR, TQ), 1)
        qpos = j * TQ - HALO + row
        mask = (jnp.abs(col - (row - HALO)) <= N_SIDE) & (qpos >= 0) & (qpos < L)
        for h in range(nh):
            sl = slice(h * HEAD_DIM, (h + 1) * HEAD_DIM)
            q = _cat3((qp, qc, qn), sl)
            do = _cat3((dop, doc, don), sl)
            lse_q = _cat3((lp, lc, ln), sl)
            dl_q = _cat3((dp_, dc_, dn_), sl)
            k = k_ref[:, sl]
            v = v_ref[:, sl]
            s = _dot(q, k, NT) * scale
            s = jnp.where(mask, s, NEG)
            p = jnp.exp(s - lse_q)
            dv_ref[:, sl] = _dot(p.astype(do.dtype), do, TN)
            dp = _dot(do, v, NT)
            ds = p * (dp - dl_q)
            dk_ref[:, sl] = _dot(ds.astype(q.dtype), q, TN) * scale

    blk = pl.BlockSpec((TQ, DA), lambda r, i: (i, r))
    ident = lambda r: r
    dk, dv = pl.pallas_call(
        body, name="attn_bwd_dkv_d%d" % d, grid=(d, L // TQ),
        in_specs=[pl.BlockSpec((TQ, DA), lambda r, i: (i, 3 * r + 1)), pl.BlockSpec((TQ, DA), lambda r, i: (i, 3 * r + 2))]
        + _halo_specs(L, DA, lambda r: 3 * r) + _halo_specs(L, DA, ident) + _halo_specs(L, DA, ident) + _halo_specs(L, DA, ident),
        out_specs=[blk, blk], out_shape=[_sds((L, d * DA), F32), _sds((L, d * DA), F32)],
        compiler_params=_params(("parallel", "parallel")),
    )(*([qkv.reshape(L, d * 3 * DA)] * 5), *([dob.reshape(L, d * DA)] * 3), *([lse.reshape(L, d * DA)] * 3),
      *([dl.reshape(L, d * DA)] * 3))
    return dk.reshape(S, DA), dv.reshape(S, DA)


def _gmlp_fwd_math(u_raw, v_raw, ln_g, ws_ref, bcol_ref, n_chunks, ng):
    ug = _gelu(u_raw)
    vg = _gelu(v_raw)
    mu = jnp.mean(vg, axis=-1, keepdims=True)
    xc = vg - mu
    rstd = lax.rsqrt(jnp.mean(xc * xc, axis=-1, keepdims=True) + EPS)
    xhat = xc * rstd
    vn = xhat * ln_g
    rows = []
    for n in range(n_chunks):
        cols = []
        for g in range(ng):
            blk = vn[n * CHUNK:(n + 1) * CHUNK, g * GROUP:(g + 1) * GROUP]
            cols.append(_dot(ws_ref[g].astype(BF16), blk.astype(BF16), NN) + bcol_ref[g])
        rows.append(jnp.concatenate(cols, axis=1))
    mixed = jnp.concatenate(rows, axis=0)
    return ug, xhat, rstd, vn, mixed


def gmlp_fwd(proj, ln_g, w_s, bcol, g_out, DA):
    S = proj.shape[0]
    ng = DA // GROUP
    tr = _tile(S, 2 * CHUNK)

    def body(u_ref, v_ref, ln_ref, ws_ref, bcol_ref, g_ref, G_ref, mix_ref):
        ug, _, _, _, mixed = _gmlp_fwd_math(u_ref[...], v_ref[...], ln_ref[...], ws_ref, bcol_ref, tr // CHUNK, ng)
        G = ug * mixed
        G_ref[...] = G
        r = lax.rsqrt(jnp.mean(G * G, axis=-1, keepdims=True) + EPS)
        mix_ref[...] = (G * r * g_ref[...]).astype(mix_ref.dtype)

    vec = pl.BlockSpec((1, DA), lambda i: (0, 0))
    par = pl.BlockSpec((ng, CHUNK, CHUNK), lambda i: (0, 0, 0))
    blk = pl.BlockSpec((tr, DA), lambda i: (i, 0))
    return pl.pallas_call(
        body, name="gmlp_fwd", grid=(S // tr,),
        in_specs=[pl.BlockSpec((tr, DA), lambda i: (i, 3)), pl.BlockSpec((tr, DA), lambda i: (i, 4)), vec, par, par, vec],
        out_specs=[blk, blk], out_shape=[_sds((S, DA), F32), _sds((S, DA), BF16)],
        compiler_params=_params(("parallel",)),
    )(proj, proj, ln_g, w_s, bcol, g_out)


def gmlp_bwd(proj, ln_g, w_s, w_st, bcol, g_out, dmix, DA):
    S = proj.shape[0]
    ng = DA // GROUP
    tr = _tile(S, 2 * CHUNK)
    nc = tr // CHUNK

    def body(u_ref, v_ref, ln_ref, ws_ref, wst_ref, bcol_ref, g_ref, dy_ref, duv_ref, dg_ref, dln_ref, dws_ref, db_ref):
        u_raw = u_ref[...]
        v_raw = v_ref[...]
        ln_g_v = ln_ref[...]
        ug, xhat, rstd, vn, mixed = _gmlp_fwd_math(u_raw, v_raw, ln_g_v, ws_ref, bcol_ref, nc, ng)
        G = ug * mixed
        dG, dg = _rms_bwd_math(G, g_ref[...], dy_ref[...])
        du_g = dG * mixed
        dmixed = dG * ug
        dws, dbs, rows = [], [], []
        for g in range(ng):
            dws.append(None)
            dbs.append(None)
        for n in range(nc):
            cols = []
            for g in range(ng):
                dm = dmixed[n * CHUNK:(n + 1) * CHUNK, g * GROUP:(g + 1) * GROUP]
                vb = vn[n * CHUNK:(n + 1) * CHUNK, g * GROUP:(g + 1) * GROUP]
                dmb = dm.astype(BF16)
                w = _dot(dmb, vb.astype(BF16), NT)
                b = jnp.broadcast_to(jnp.sum(dm, axis=-1, keepdims=True), (CHUNK, GROUP))
                dws[g] = w if dws[g] is None else dws[g] + w
                dbs[g] = b if dbs[g] is None else dbs[g] + b
                cols.append(_dot(wst_ref[g].astype(BF16), dmb, NN))
            rows.append(jnp.concatenate(cols, axis=1))
        dvn = jnp.concatenate(rows, axis=0)
        dln = jnp.sum(dvn * xhat, axis=0, keepdims=True)
        dxh = dvn * ln_g_v
        dvg = rstd * (dxh - jnp.mean(dxh, axis=-1, keepdims=True) - xhat * jnp.mean(dxh * xhat, axis=-1, keepdims=True))
        duv_ref[:, 0:DA] = (du_g * _gelu_grad(u_raw)).astype(duv_ref.dtype)
        duv_ref[:, DA:2 * DA] = (dvg * _gelu_grad(v_raw)).astype(duv_ref.dtype)

        first = pl.program_id(0) == 0

        @pl.when(first)
        def _():
            dg_ref[...] = dg
            dln_ref[...] = dln
            for g in range(ng):
                dws_ref[g] = dws[g]
                db_ref[g] = dbs[g]

        @pl.when(jnp.logical_not(first))
        def _():
            dg_ref[...] += dg
            dln_ref[...] += dln
            for g in range(ng):
                dws_ref[g] += dws[g]
                db_ref[g] += dbs[g]

    vec = pl.BlockSpec((1, DA), lambda i: (0, 0))
    par = pl.BlockSpec((ng, CHUNK, CHUNK), lambda i: (0, 0, 0))
    return pl.pallas_call(
        body, name="gmlp_bwd", grid=(S // tr,),
        in_specs=[pl.BlockSpec((tr, DA), lambda i: (i, 3)), pl.BlockSpec((tr, DA), lambda i: (i, 4)), vec, par, par, par, vec,
                  pl.BlockSpec((tr, DA), lambda i: (i, 1))],
        out_specs=[pl.BlockSpec((tr, 2 * DA), lambda i: (i, 0)), vec, vec, par, par],
        out_shape=[_sds((S, 2 * DA), BF16), _sds((1, DA), F32), _sds((1, DA), F32),
                   _sds((ng, CHUNK, CHUNK), F32), _sds((ng, CHUNK, CHUNK), F32)],
        compiler_params=_params(("arbitrary",)),
    )(proj, proj, ln_g, w_s, w_st, bcol, g_out, dmix)


def layer_fwd(x, sm, gw, tabs):
    D = x.shape[1]
    DA = D // 2
    cos2, sin2 = tabs
    h1 = rms_fwd(x, sm["norm1_g"])
    proj = mm_in(h1, gw["w_in"])
    qkv = rope_qkv(proj, cos2, sin2, DA)
    os_, lses = [], []
    for d in DILATIONS:
        o, l = attn_fwd(qkv, d, DA)
        os_.append(o)
        lses.append(l)
    a, lse, mix_a = attn_merge(os_, lses, sm["mix_norm_attn_g"])
    _, mix_g = gmlp_fwd(proj, sm["gmlp_ln_g"], sm["w_spatial"], sm["bcol"], sm["mix_norm_gmlp_g"], DA)
    x2 = mm_out(x, mix_a, mix_g, gw["w_out"])
    h2 = rms_fwd(x2, sm["norm2_g"])
    gate, up, ff = ff_fwd(h2, gw["w_gate"], gw["w_up"])
    x3 = mm_down(x2, ff, gw["w_down"])
    saved = dict(x=x, h1=h1, proj=proj, qkv=qkv, a=a, lse=lse, mix_a=mix_a, mix_g=mix_g, x2=x2, h2=h2, gate=gate, up=up, ff=ff)
    return x3, saved


def layer_bwd(dx, dxb, sv, sm, gw, tabs, dep):
    D = dx.shape[1]
    DA = D // 2
    cos2, sin2 = tabs
    NS = gw["w_in"].shape[0]
    dgate, dup = ff_bwd_act(dxb, gw["w_down"], sv["gate"], sv["up"], dep)
    g_down = dw_down(sv["ff"], dxb)
    g_gate, g_up = dw_gate_up(sv["h2"], dgate, dup)
    dh2 = dh_ff(dgate, dup, gw["w_gate"], gw["w_up"])
    dx2, dx2b, d_norm2 = rms_bwd_res(sv["x2"], sm["norm2_g"], dh2, dx)
    dmix = dmix_mm(dx2b, gw["w_out"].reshape(D, D))
    g_out_a, g_out_g = dw_out(sv["mix_a"], sv["mix_g"], dx2b)
    dob, dl, d_mix_a = attn_out_bwd(sv["a"], sm["mix_norm_attn_g"], dmix)
    duv, d_mix_g, d_ln, d_ws, d_bs = gmlp_bwd(sv["proj"], sm["gmlp_ln_g"], sm["w_spatial"], sm["w_spatial_t"], sm["bcol"],
                                              sm["mix_norm_gmlp_g"], dmix, DA)
    dqs, dks, dvs = [], [], []
    for d in DILATIONS:
        dqs.append(attn_bwd_dq(sv["qkv"], dob, sv["lse"], dl, d, DA))
        dk, dv = attn_bwd_dkv(sv["qkv"], dob, sv["lse"], dl, d, DA)
        dks.append(dk)
        dvs.append(dv)
    dproj = assemble_dproj(dqs, dks, dvs, duv, cos2, sin2)
    g_in = dw_in(sv["h1"], dproj, NS)
    dh1 = dh_in(dproj, gw["w_in"])
    dx0, dx0b, d_norm1 = rms_bwd_res(sv["x"], sm["norm1_g"], dh1, dx2)
    big = dict(w_in=g_in, w_out=jnp.concatenate([g_out_a, g_out_g], axis=0), w_gate=g_gate, w_up=g_up, w_down=g_down)
    small = dict(norm1_g=d_norm1, gmlp_ln_g=d_ln, w_spatial=d_ws, b_spatial=d_bs[:, :, 0], mix_norm_attn_g=d_mix_a,
                 mix_norm_gmlp_g=d_mix_g, norm2_g=d_norm2)
    return dx0, dx0b, big, small


def _place():
    x, y, c = lax.axis_index("x"), lax.axis_index("y"), lax.axis_index("c")
    return x, y, c, [(1 - x, y), (x, 1 - y), (1 - x, 1 - y)]


def _remote(src, dst, send_sems, recv_sems, k, to):
    return pltpu.make_async_remote_copy(src_ref=src, dst_ref=dst, send_sem=send_sems.at[k], recv_sem=recv_sems.at[k],
                                        device_id=to, device_id_type=MESH)


def to_slot(src, layer, me_arr, dep, dtype):
    R, C = src.shape[-2:]
    tr = _tile(R, max(16, 4 * ADAMW_BLOCK_ELEMS // C))

    def body(me_ref, src_ref, dep_ref, o_ref):
        o_ref[...] = src_ref[...].astype(o_ref.dtype)

    if layer is None:
        src_spec = pl.BlockSpec((tr, C), lambda i, me_ref: (i, 0))
    else:
        src_spec = pl.BlockSpec((None, tr, C), lambda i, me_ref: (layer, i, 0))
    return pl.pallas_call(
        body, name="to_slot",
        grid_spec=pltpu.PrefetchScalarGridSpec(
            num_scalar_prefetch=1, grid=(R // tr,),
            in_specs=[src_spec, pl.BlockSpec(TOKEN, lambda i, me_ref: (0, 0))],
            out_specs=pl.BlockSpec((None, tr, C), lambda i, me_ref: (me_ref[0], i, 0))),
        out_shape=_sds((N_DEV, R, C), dtype), compiler_params=_params(("parallel",)),
    )(me_arr, src, dep)


def gather_ici(bufs):
    n = len(bufs)

    def body(*refs):
        outs = refs[n:2 * n]
        send_sems, recv_sems = refs[2 * n:]
        x, y, c, chips = _place()
        mine = [o.at[4 * x + 2 * y + c] for o in outs]
        sent = []
        for a in range(n):
            for j, (px, py) in enumerate(chips):
                cp = _remote(mine[a], mine[a], send_sems, recv_sems, 3 * a + j, (px, py, c))
                cp.start()
                sent.append(cp)
        for a in range(n):
            for j, (px, py) in enumerate(chips):
                slot = outs[a].at[4 * px + 2 * py + c]
                _remote(slot, slot, send_sems, recv_sems, 3 * a + j, (px, py, c)).wait_recv()
        for cp in sent:
            cp.wait_send()

    return pl.pallas_call(
        body, name="gather_ici", in_specs=[ANY] * n, out_specs=[ANY] * n,
        out_shape=[_sds(b.shape, b.dtype) for b in bufs], input_output_aliases={a: a for a in range(n)},
        scratch_shapes=[pltpu.SemaphoreType.DMA((3 * n,)), pltpu.SemaphoreType.DMA((3 * n,))],
    )(*bufs)


HBM_SPEC = pl.BlockSpec(memory_space=pltpu.HBM)
SEM_SPEC = pl.BlockSpec(memory_space=pltpu.SEMAPHORE)
DATAFLOW = pltpu.SideEffectType.DATAFLOW_SIDE_EFFECTING
TOKEN = (8, 128)


def _in_hbm(a):
    return pltpu.with_memory_space_constraint(a, pltpu.HBM)


def gather_ici_start(bufs):
    n = len(bufs)

    def body(*refs):
        ins = refs[:n]
        send_sems, recv_sems = refs[n], refs[n + 1]
        token = refs[-1]
        x, y, c, chips = _place()
        for a in range(n):
            mine = ins[a].at[4 * x + 2 * y + c]
            for j, (px, py) in enumerate(chips):
                _remote(mine, mine, send_sems, recv_sems, 3 * a + j, (px, py, c)).start()
        token[...] = jnp.zeros_like(token)

    out = pl.pallas_call(
        body, name="gather_ici_start",
        out_shape=(pltpu.SemaphoreType.DMA((3 * n,)), pltpu.SemaphoreType.DMA((3 * n,)),
                   *[pltpu.HBM(b.shape, b.dtype) for b in bufs], _sds(TOKEN, F32)),
        in_specs=[HBM_SPEC] * n,
        out_specs=(SEM_SPEC, SEM_SPEC, *[HBM_SPEC] * n, pl.BlockSpec(memory_space=pltpu.VMEM)),
        input_output_aliases={i: 2 + i for i in range(n)},
        compiler_params=pltpu.CompilerParams(has_side_effects=DATAFLOW),
    )(*[_in_hbm(b) for b in bufs])
    return out[0], out[1], list(out[2:2 + n]), out[-1]


def gather_ici_wait(send_sems, recv_sems, bufs, after):
    n = len(bufs)

    def body(*refs):
        ins = refs[:n]
        send_sems, recv_sems = refs[n], refs[n + 1]
        x, y, c, chips = _place()
        for a in range(n):
            mine = ins[a].at[4 * x + 2 * y + c]
            for j, (px, py) in enumerate(chips):
                cp = _remote(mine, ins[a].at[4 * px + 2 * py + c], send_sems, recv_sems, 3 * a + j, (px, py, c))
                cp.wait_send()
                cp.wait_recv()

    out = pl.pallas_call(
        body, name="gather_ici_wait",
        out_shape=[pltpu.HBM(b.shape, b.dtype) for b in bufs],
        in_specs=[HBM_SPEC] * n + [SEM_SPEC, SEM_SPEC, ANY], out_specs=[HBM_SPEC] * n,
        input_output_aliases={i: i for i in range(n)},
        compiler_params=pltpu.CompilerParams(has_side_effects=DATAFLOW),
    )(*bufs, send_sems, recv_sems, after)
    return list(out)


def gather_d2d(bufs):
    n = len(bufs)

    def body(*refs):
        outs = refs[n:2 * n]
        token = refs[2 * n]
        send_sems, recv_sems = refs[2 * n + 1:]
        x, y, c, _ = _place()
        sent = []
        for a in range(n):
            for j in range(N_CHIP):
                slot = outs[a].at[2 * j + c]
                cp = _remote(slot, slot, send_sems, recv_sems, N_CHIP * a + j, (x, y, 1 - c))
                cp.start()
                sent.append(cp)
        for a in range(n):
            for j in range(N_CHIP):
                slot = outs[a].at[2 * j + 1 - c]
                _remote(slot, slot, send_sems, recv_sems, N_CHIP * a + j, (x, y, 1 - c)).wait_recv()
        for cp in sent:
            cp.wait_send()
        token[...] = jnp.zeros_like(token)

    out = pl.pallas_call(
        body, name="gather_d2d", in_specs=[ANY] * n,
        out_specs=[ANY] * n + [pl.BlockSpec(memory_space=pltpu.VMEM)],
        out_shape=[_sds(b.shape, b.dtype) for b in bufs] + [_sds(TOKEN, F32)],
        input_output_aliases={a: a for a in range(n)},
        scratch_shapes=[pltpu.SemaphoreType.DMA((N_CHIP * n,)), pltpu.SemaphoreType.DMA((N_CHIP * n,))],
    )(*bufs)
    return list(out[:n]), out[n]


def scatter_d2d(parts):
    n = len(parts)

    def body(*refs):
        ins, outs = refs[:n], refs[n:2 * n]
        send_sems, recv_sems = refs[2 * n:]
        x, y, c, _ = _place()
        sent = []
        for a in range(n):
            for j in range(N_CHIP):
                cp = _remote(ins[a].at[2 * j + 1 - c], outs[a].at[j], send_sems, recv_sems, N_CHIP * a + j, (x, y, 1 - c))
                cp.start()
                sent.append(cp)
        for a in range(n):
            for j in range(N_CHIP):
                slot = outs[a].at[j]
                _remote(slot, slot, send_sems, recv_sems, N_CHIP * a + j, (x, y, 1 - c)).wait_recv()
        for cp in sent:
            cp.wait_send()

    return pl.pallas_call(
        body, name="scatter_d2d", in_specs=[ANY] * n, out_specs=[ANY] * n,
        out_shape=[_sds((N_CHIP,) + p.shape[1:], p.dtype) for p in parts],
        scratch_shapes=[pltpu.SemaphoreType.DMA((N_CHIP * n,)), pltpu.SemaphoreType.DMA((N_CHIP * n,))],
    )(*parts)


def pair_sum(part, got, c_arr, chip_arr):
    _, R, C = part.shape
    tr = _tile(R, 512)
    p4 = part.reshape(N_CHIP, 2, R, C)

    def body(c_ref, chip_ref, p_ref, g_ref, o_ref, own_ref):
        s = (p_ref[...].astype(F32) + g_ref[...].astype(F32)).astype(o_ref.dtype)
        o_ref[...] = s

        @pl.when(pl.program_id(1) == chip_ref[0])
        def _():
            own_ref[...] = s

    return pl.pallas_call(
        body, name="pair_sum",
        grid_spec=pltpu.PrefetchScalarGridSpec(
            num_scalar_prefetch=2, grid=(R // tr, N_CHIP),
            in_specs=[pl.BlockSpec((None, None, tr, C), lambda i, j, c_ref, chip_ref: (j, c_ref[0], i, 0)),
                      pl.BlockSpec((None, tr, C), lambda i, j, c_ref, chip_ref: (j, i, 0))],
            out_specs=[pl.BlockSpec((None, tr, C), lambda i, j, c_ref, chip_ref: (j, i, 0)),
                       pl.BlockSpec((None, tr, C), lambda i, j, c_ref, chip_ref: (chip_ref[0], i, 0))]),
        out_shape=[_sds((N_CHIP, R, C), part.dtype), _sds((N_CHIP, R, C), part.dtype)],
        compiler_params=_params(("parallel", "arbitrary")),
    )(c_arr, chip_arr, p4, got)


def scatter_ici_start(sums, lands):
    n = len(sums)

    def body(*refs):
        ins, dsts = refs[:n], refs[n:2 * n]
        send_sems, recv_sems = refs[2 * n], refs[2 * n + 1]
        token = refs[-1]
        x, y, c, chips = _place()
        mine = 2 * x + y
        for a in range(n):
            for j, (px, py) in enumerate(chips):
                _remote(ins[a].at[2 * px + py], dsts[a].at[mine], send_sems, recv_sems, 3 * a + j, (px, py, c)).start()
        token[...] = jnp.zeros_like(token)

    both = list(sums) + list(lands)
    out = pl.pallas_call(
        body, name="scatter_ici_start",
        out_shape=(pltpu.SemaphoreType.DMA((3 * n,)), pltpu.SemaphoreType.DMA((3 * n,)),
                   *[pltpu.HBM(b.shape, b.dtype) for b in both], _sds(TOKEN, F32)),
        in_specs=[HBM_SPEC] * (2 * n),
        out_specs=(SEM_SPEC, SEM_SPEC, *[HBM_SPEC] * (2 * n), pl.BlockSpec(memory_space=pltpu.VMEM)),
        input_output_aliases={i: 2 + i for i in range(2 * n)},
        compiler_params=pltpu.CompilerParams(has_side_effects=DATAFLOW),
    )(*[_in_hbm(b) for b in both])
    return out[0], out[1], list(out[2:2 + n]), list(out[2 + n:2 + 2 * n]), out[-1]


def scatter_ici_wait(send_sems, recv_sems, sums, lands, after):
    n = len(sums)

    def body(*refs):
        ins, dsts = refs[:n], refs[n:2 * n]
        send_sems, recv_sems = refs[2 * n], refs[2 * n + 1]
        x, y, c, chips = _place()
        for a in range(n):
            for j, (px, py) in enumerate(chips):
                cp = _remote(ins[a].at[2 * px + py], dsts[a].at[2 * px + py], send_sems, recv_sems, 3 * a + j, (px, py, c))
                cp.wait_send()
                cp.wait_recv()

    both = list(sums) + list(lands)
    out = pl.pallas_call(
        body, name="scatter_ici_wait",
        out_shape=[pltpu.HBM(b.shape, b.dtype) for b in both],
        in_specs=[HBM_SPEC] * (2 * n) + [SEM_SPEC, SEM_SPEC, ANY], out_specs=[HBM_SPEC] * (2 * n),
        input_output_aliases={i: i for i in range(2 * n)},
        compiler_params=pltpu.CompilerParams(has_side_effects=DATAFLOW),
    )(*both, send_sems, recv_sems, after)
    return list(out[:n]), list(out[n:2 * n])


def adamw(w, m, v, parts_per_layer):
    NL, R, C = w.shape
    P = parts_per_layer[0].shape[0]
    tr = _tile(R, max(16, ADAMW_BLOCK_ELEMS // C))
    nb = R // tr

    def body(w_ref, m_ref, v_ref, *rest):
        part_refs = rest[:NL]
        g_ref, d_ref, nm_ref, nv_ref = rest[NL:]
        layer = pl.program_id(0)
        g = None
        for q in range(NL):
            s = part_refs[q][0].astype(F32)
            for t in range(1, P):
                s = s + part_refs[q][t].astype(F32)
            s = jnp.where(layer == q, s, 0.0)
            g = s if g is None else g + s
        wv = w_ref[...]
        nm = ADAM_B1 * m_ref[...] + (1.0 - ADAM_B1) * g
        nv = ADAM_B2 * v_ref[...] + (1.0 - ADAM_B2) * (g * g)
        m_hat = nm / (1.0 - ADAM_B1 ** ADAM_STEP)
        v_hat = nv / (1.0 - ADAM_B2 ** ADAM_STEP)
        g_ref[...] = g
        d_ref[...] = -ADAM_LR * (m_hat / (jnp.sqrt(v_hat) + ADAM_EPS) + ADAM_WD * wv)
        nm_ref[...] = nm
        nv_ref[...] = nv

    def part_spec(q):
        return pl.BlockSpec((P, tr, C), lambda l, i: (0, jnp.where(l == q, i, jnp.where(l < q, 0, nb - 1)), 0))

    blk = pl.BlockSpec((None, tr, C), lambda l, i: (l, i, 0))
    return pl.pallas_call(
        body, name="adamw", grid=(NL, nb),
        in_specs=[blk, blk, blk] + [part_spec(q) for q in range(NL)],
        out_specs=[blk, blk, blk, blk], out_shape=[_sds((NL, R, C), F32)] * 4,
        compiler_params=_params(("arbitrary", "arbitrary")),
    )(w, m, v, *parts_per_layer)


SMALL = ("norm1_g", "gmlp_ln_g", "w_spatial", "b_spatial", "mix_norm_attn_g", "mix_norm_gmlp_g", "norm2_g")
BIG = ("w_in", "w_out", "w_gate", "w_up", "w_down")
TRANSPOSED = ("w_gate", "w_up")
LANES = 128


def _pack(layers, final):
    flat = [layer[n].reshape(-1) for layer in layers for n in SMALL] + [final.reshape(-1)]
    return jnp.concatenate(flat).reshape(-1, LANES)


def _unpack(packed, like_layers, like_final):
    flat = packed.reshape(-1)
    out, off = [], 0
    for layer in like_layers:
        d = {}
        for n in SMALL:
            size = layer[n].size
            d[n] = flat[off:off + size].reshape(layer[n].shape)
            off += size
        out.append(d)
    return out, flat[off:off + like_final.size].reshape(like_final.shape)


def kernel(x, norm1_g, w_in, gmlp_ln_g, w_spatial, b_spatial, mix_norm_attn_g, mix_norm_gmlp_g, w_out, norm2_g, w_gate, w_up, w_down, final_g, loss_target, m_norm1_g, m_w_in, m_gmlp_ln_g, m_w_spatial, m_b_spatial, m_mix_norm_attn_g, m_mix_norm_gmlp_g, m_w_out, m_norm2_g, m_w_gate, m_w_up, m_w_down, m_final_g, v_norm1_g, v_w_in, v_gmlp_ln_g, v_w_spatial, v_b_spatial, v_mix_norm_attn_g, v_mix_norm_gmlp_g, v_w_out, v_norm2_g, v_w_gate, v_w_up, v_w_down, v_final_g):
    S, D = x.shape[1], x.shape[2]
    NL = norm1_g.shape[0]
    DA = D // 2
    xs = x.reshape(S, D)
    tabs = rope_tables(S)
    ax, ay, ac = lax.axis_index("x"), lax.axis_index("y"), lax.axis_index("c")
    me_arr = (4 * ax + 2 * ay + ac).astype(jnp.int32).reshape(1)
    c_arr = ac.astype(jnp.int32).reshape(1)
    chip_arr = (2 * ax + ay).astype(jnp.int32).reshape(1)
    small_w = dict(norm1_g=norm1_g, gmlp_ln_g=gmlp_ln_g, w_spatial=w_spatial, b_spatial=b_spatial,
                   mix_norm_attn_g=mix_norm_attn_g, mix_norm_gmlp_g=mix_norm_gmlp_g, norm2_g=norm2_g)
    small_m = dict(norm1_g=m_norm1_g, gmlp_ln_g=m_gmlp_ln_g, w_spatial=m_w_spatial, b_spatial=m_b_spatial,
                   mix_norm_attn_g=m_mix_norm_attn_g, mix_norm_gmlp_g=m_mix_norm_gmlp_g, norm2_g=m_norm2_g)
    small_v = dict(norm1_g=v_norm1_g, gmlp_ln_g=v_gmlp_ln_g, w_spatial=v_w_spatial, b_spatial=v_b_spatial,
                   mix_norm_attn_g=v_mix_norm_attn_g, mix_norm_gmlp_g=v_mix_norm_gmlp_g, norm2_g=v_norm2_g)
    def view(n, a):
        return jnp.swapaxes(a, 1, 2) if n in TRANSPOSED else a

    big_w = {n: view(n, a) for n, a in dict(w_in=w_in, w_out=w_out, w_gate=w_gate, w_up=w_up, w_down=w_down).items()}
    big_m = {n: view(n, a) for n, a in dict(w_in=m_w_in, w_out=m_w_out, w_gate=m_w_gate, w_up=m_w_up, w_down=m_w_down).items()}
    big_v = {n: view(n, a) for n, a in dict(w_in=v_w_in, w_out=v_w_out, w_gate=v_w_gate, w_up=v_w_up, w_down=v_w_down).items()}

    def layer_small(l):
        ws = w_spatial[l]
        return dict(norm1_g=norm1_g[l][None], gmlp_ln_g=gmlp_ln_g[l][None], mix_norm_attn_g=mix_norm_attn_g[l][None],
                    mix_norm_gmlp_g=mix_norm_gmlp_g[l][None], norm2_g=norm2_g[l][None], w_spatial=ws,
                    w_spatial_t=jnp.swapaxes(ws, 1, 2), bcol=jnp.broadcast_to(b_spatial[l][:, :, None], ws.shape))

    def cast_shards(l, dep):
        return [to_slot(big_w[n], l, me_arr, dep, BF16) for n in BIG]

    pending = gather_ici_start(cast_shards(0, jnp.zeros(TOKEN, F32)))
    h = xs
    saved, weights = [], []
    for l in range(NL):
        send_sems, recv_sems, bufs, _ = pending
        got, dep = gather_d2d(gather_ici_wait(send_sems, recv_sems, bufs, h))
        gw = dict(zip(BIG, got))
        gw["w_out"] = gw["w_out"].reshape(2, DA, D)
        for n in ("w_gate", "w_up", "w_down"):
            gw[n] = gw[n].reshape(-1, D)
        if l + 1 < NL:
            pending = gather_ici_start(cast_shards(l + 1, dep))
            dep = pending[3]
        sm = layer_small(l)
        sm["norm1_g"] = sm["norm1_g"] + dep[0, 0]
        h, sv = layer_fwd(h, sm, gw, tabs)
        saved.append(sv)
        weights.append(gw)
    loss_part, dx, dxb, d_final = loss_and_grad(h, final_g[None], loss_target.reshape(S, D))
    loss = lax.psum(loss_part[0, 0], ("x", "y", "c"))

    def finish_scatter(pend, after):
        send_sems, recv_sems, sums, lands, _ = pend
        _, lands = scatter_ici_wait(send_sems, recv_sems, sums, lands, after)
        return dict(zip(BIG, lands))

    big_sums = [None] * NL
    small_grads = [None] * NL
    pending = None
    for l in reversed(range(NL)):
        dx, dxb, big, small = layer_bwd(dx, dxb, saved[l], layer_small(l), weights[l], tabs, dep)
        small_grads[l] = small
        if pending is not None:
            big_sums[l + 1] = finish_scatter(pending, dx)
        parts = [big[n].reshape((N_DEV, -1, big[n].shape[-1])) for n in BIG]
        got = scatter_d2d(parts)
        pairs = [pair_sum(p, g, c_arr, chip_arr) for p, g in zip(parts, got)]
        pending = scatter_ici_start([s for s, _ in pairs], [o for _, o in pairs])
        dep = pending[4]

    packed = to_slot(_pack(small_grads, d_final), None, me_arr, dep, F32)
    gathered_small = gather_d2d(gather_ici([packed]))[0][0]
    pw = _pack([{n: small_w[n][l] for n in SMALL} for l in range(NL)], final_g)[None]
    pm = _pack([{n: small_m[n][l] for n in SMALL} for l in range(NL)], m_final_g)[None]
    pv = _pack([{n: small_v[n][l] for n in SMALL} for l in range(NL)], v_final_g)[None]
    like_layers = [{n: small_w[n][l] for n in SMALL} for l in range(NL)]
    small_res = adamw(pw, pm, pv, [gathered_small])
    small_out = [_unpack(t[0], like_layers, final_g) for t in small_res]
    big_sums[0] = finish_scatter(pending, small_res[0])

    def small_stack(k, n):
        return jnp.stack([small_out[k][0][l][n] for l in range(NL)])

    big_out = {}
    for n in BIG:
        R, C = big_w[n].shape[1], big_w[n].shape[2]
        res = adamw(big_w[n], big_m[n], big_v[n], [big_sums[l][n].reshape(N_CHIP, R, C) for l in range(NL)])
        big_out[n] = [view(n, t) for t in res]

    order = ("norm1_g", "w_in", "gmlp_ln_g", "w_spatial", "b_spatial", "mix_norm_attn_g", "mix_norm_gmlp_g", "w_out",
             "norm2_g", "w_gate", "w_up", "w_down")
    outs = [loss, dx.reshape(x.shape)]
    for k in range(4):
        for n in order:
            outs.append(big_out[n][k] if n in BIG else small_stack(k, n))
        outs.append(small_out[k][1])
    return tuple(outs)
```

```python
import functools

import jax
import jax.numpy as jnp
from jax import lax
from jax.experimental import pallas as pl
from jax.experimental.pallas import tpu as pltpu

F32 = jnp.float32
BF16 = jnp.bfloat16

HEAD_DIM = 128
CHUNK = 128
GROUP = 128
N_SIDE = 64
DILATIONS = (1, 4, 16)
TQ = 128
HALO = 64
EPS = 1e-6
NEG = -1e30
ROPE_THETA = 10000.0
ADAM_LR = 0.001
ADAM_B1 = 0.9
ADAM_B2 = 0.999
ADAM_EPS = 1e-08
ADAM_WD = 0.01
ADAM_STEP = 10
N_DEV = 8
N_CHIP = 4
VMEM_LIMIT_V7X = 56 * 1024 * 1024
ADAMW_BLOCK_ELEMS = 128 * 1024
MESH = pl.DeviceIdType.MESH
ANY = pl.BlockSpec(memory_space=pl.ANY)


def _params(sem=None):
    return pltpu.CompilerParams(dimension_semantics=sem, vmem_limit_bytes=VMEM_LIMIT_V7X)


def _tile(n, want):
    for t in range(min(want, n), 15, -1):
        if n % t == 0 and t % 16 == 0:
            return t
    return n


def _sds(shape, dtype):
    return jax.ShapeDtypeStruct(shape, dtype)


def _dot(a, b, dims):
    return lax.dot_general(a, b, (dims, ((), ())), preferred_element_type=F32)


NN = ((1,), (0,))
NT = ((1,), (1,))
TN = ((0,), (0,))


def _sigmoid(x):
    return 1.0 / (1.0 + jnp.exp(-x))


def _gelu(x):
    return 0.5 * x * (1.0 + lax.erf(x * 0.7071067811865476))


def _gelu_grad(x):
    cdf = 0.5 * (1.0 + lax.erf(x * 0.7071067811865476))
    pdf = 0.3989422804014327 * jnp.exp(-0.5 * x * x)
    return cdf + x * pdf


def rms_fwd(x, g):
    S, D = x.shape
    tr = _tile(S, 512)

    def body(x_ref, g_ref, o_ref):
        xv = x_ref[...]
        r = lax.rsqrt(jnp.mean(xv * xv, axis=-1, keepdims=True) + EPS)
        o_ref[...] = (xv * r * g_ref[...]).astype(o_ref.dtype)

    return pl.pallas_call(
        body, name="rms_fwd", grid=(S // tr,),
        in_specs=[pl.BlockSpec((tr, D), lambda i: (i, 0)), pl.BlockSpec((1, D), lambda i: (0, 0))],
        out_specs=pl.BlockSpec((tr, D), lambda i: (i, 0)),
        out_shape=_sds((S, D), BF16), compiler_params=_params(("parallel",)),
    )(x, g)


def _rms_bwd_math(xv, gv, dy):
    r = lax.rsqrt(jnp.mean(xv * xv, axis=-1, keepdims=True) + EPS)
    dyg = dy * gv
    dx = r * dyg - xv * (r * r * r) * jnp.mean(dyg * xv, axis=-1, keepdims=True)
    dg = jnp.sum(dy * xv * r, axis=0, keepdims=True)
    return dx, dg


def rms_bwd_res(x, g, dy, dres):
    S, D = x.shape
    tr = _tile(S, 512)

    def body(x_ref, g_ref, dy_ref, dres_ref, dx_ref, dxb_ref, dg_ref):
        dx, dg = _rms_bwd_math(x_ref[...], g_ref[...], dy_ref[...])
        tot = dres_ref[...] + dx
        dx_ref[...] = tot
        dxb_ref[...] = tot.astype(dxb_ref.dtype)

        @pl.when(pl.program_id(0) == 0)
        def _():
            dg_ref[...] = dg

        @pl.when(pl.program_id(0) > 0)
        def _():
            dg_ref[...] += dg

    row = pl.BlockSpec((tr, D), lambda i: (i, 0))
    vec = pl.BlockSpec((1, D), lambda i: (0, 0))
    return pl.pallas_call(
        body, name="rms_bwd_res", grid=(S // tr,),
        in_specs=[row, vec, row, row], out_specs=[row, row, vec],
        out_shape=[_sds((S, D), F32), _sds((S, D), BF16), _sds((1, D), F32)],
        compiler_params=_params(("arbitrary",)),
    )(x, g, dy, dres)


def loss_and_grad(x, g, target):
    S, D = x.shape
    tr = _tile(S, 512)

    def body(x_ref, g_ref, t_ref, loss_ref, dx_ref, dxb_ref, dg_ref):
        xv = x_ref[...]
        gv = g_ref[...]
        r = lax.rsqrt(jnp.mean(xv * xv, axis=-1, keepdims=True) + EPS)
        y = xv * r * gv
        diff = y - t_ref[...]
        part = 0.5 * jnp.sum(jnp.mean(diff * diff, axis=-1, keepdims=True), axis=0, keepdims=True)
        dy = diff * (1.0 / D)
        dx, dg = _rms_bwd_math(xv, gv, dy)
        dx_ref[...] = dx
        dxb_ref[...] = dx.astype(dxb_ref.dtype)

        @pl.when(pl.program_id(0) == 0)
        def _():
            dg_ref[...] = dg
            loss_ref[...] = jnp.broadcast_to(part, loss_ref.shape)

        @pl.when(pl.program_id(0) > 0)
        def _():
            dg_ref[...] += dg
            loss_ref[...] += jnp.broadcast_to(part, loss_ref.shape)

    row = pl.BlockSpec((tr, D), lambda i: (i, 0))
    vec = pl.BlockSpec((1, D), lambda i: (0, 0))
    return pl.pallas_call(
        body, name="loss_and_grad", grid=(S // tr,),
        in_specs=[row, vec, row],
        out_specs=[pl.BlockSpec((8, 128), lambda i: (0, 0)), row, row, vec],
        out_shape=[_sds((8, 128), F32), _sds((S, D), F32), _sds((S, D), BF16), _sds((1, D), F32)],
        compiler_params=_params(("arbitrary",)),
    )(x, g, target)


def _mm_body(n_pairs, dims, red_axis, n_red, has_res):
    def body(*refs):
        ins = refs[:2 * n_pairs]
        res_ref = refs[2 * n_pairs] if has_res else None
        o_ref = refs[2 * n_pairs + has_res]
        p = None
        for t in range(n_pairs):
            d = _dot(ins[2 * t][...], ins[2 * t + 1][...], dims)
            p = d if p is None else p + d
        if red_axis is None:
            if has_res:
                p = res_ref[...] + p
            o_ref[...] = p.astype(o_ref.dtype)
            return
        acc_ref = o_ref if o_ref.dtype == F32 else refs[2 * n_pairs + has_res + 1]
        r = pl.program_id(red_axis)

        @pl.when(r == 0)
        def _():
            acc_ref[...] = res_ref[...] + p if has_res else p

        @pl.when(r > 0)
        def _():
            acc_ref[...] += p

        if acc_ref is not o_ref:
            @pl.when(r == n_red - 1)
            def _():
                o_ref[...] = acc_ref[...].astype(o_ref.dtype)

    return body


def mm_in(h, w):
    S, D = h.shape
    NS, _, n = w.shape
    tm = _tile(S, 1024)
    return pl.pallas_call(
        _mm_body(1, NN, None, 1, False), name="mm_in", grid=(S // tm, NS),
        in_specs=[pl.BlockSpec((tm, D), lambda i, k: (i, 0)), pl.BlockSpec((None, D, n), lambda i, k: (k, 0, 0))],
        out_specs=pl.BlockSpec((tm, n), lambda i, k: (i, k)),
        out_shape=_sds((S, NS * n), F32), compiler_params=_params(("parallel", "parallel")),
    )(h, w)


def mm_out(x, mix_a, mix_g, w):
    S, D = x.shape
    DA = mix_a.shape[1]
    tm = _tile(S, 1024)
    tn = _tile(D, 1024)
    act = pl.BlockSpec((tm, DA), lambda i, j: (i, 0))
    return pl.pallas_call(
        _mm_body(2, NN, None, 1, True), name="mm_out", grid=(S // tm, D // tn),
        in_specs=[act, pl.BlockSpec((None, DA, tn), lambda i, j: (0, 0, j)),
                  act, pl.BlockSpec((None, DA, tn), lambda i, j: (1, 0, j)),
                  pl.BlockSpec((tm, tn), lambda i, j: (i, j))],
        out_specs=pl.BlockSpec((tm, tn), lambda i, j: (i, j)),
        out_shape=_sds((S, D), F32), compiler_params=_params(("parallel", "parallel")),
    )(mix_a, w, mix_g, w, x)


def _lane_tile(n, want):
    for t in range(min(want, n) // 128 * 128, 127, -128):
        if n % t == 0:
            return t
    return n


def ff_fwd(h, wg, wu):
    S, D = h.shape
    F = wg.shape[0]
    tm = _tile(S, 1024)
    tn = _lane_tile(F, 512)

    def body(h_ref, wg_ref, wu_ref, gate_ref, up_ref, ff_ref):
        hv = h_ref[...]
        g = _dot(hv, wg_ref[...], NT)
        u = _dot(hv, wu_ref[...], NT)
        gate_ref[...] = g
        up_ref[...] = u
        ff_ref[...] = (g * _sigmoid(g) * u).astype(ff_ref.dtype)

    wspec = pl.BlockSpec((tn, D), lambda i, j: (j, 0))
    ospec = pl.BlockSpec((tm, tn), lambda i, j: (i, j))
    return pl.pallas_call(
        body, name="ff_fwd", grid=(S // tm, F // tn),
        in_specs=[pl.BlockSpec((tm, D), lambda i, j: (i, 0)), wspec, wspec],
        out_specs=[ospec, ospec, ospec],
        out_shape=[_sds((S, F), F32), _sds((S, F), F32), _sds((S, F), BF16)],
        compiler_params=_params(("parallel", "parallel")),
    )(h, wg, wu)


def mm_down(x, ff, wd):
    S, D = x.shape
    F = ff.shape[1]
    tm = _tile(S, 512)
    tn = _lane_tile(D, 512)
    blk = pl.BlockSpec((tm, tn), lambda i, j: (i, j))
    return pl.pallas_call(
        _mm_body(1, NN, None, 1, True), name="mm_down", grid=(S // tm, D // tn),
        in_specs=[pl.BlockSpec((tm, F), lambda i, j: (i, 0)), pl.BlockSpec((F, tn), lambda i, j: (0, j)), blk],
        out_specs=blk, out_shape=_sds((S, D), F32), compiler_params=_params(("parallel", "parallel")),
    )(ff, wd, x)


def ff_bwd_act(dxb, wd, gate, up, dep):
    S, D = dxb.shape
    F = wd.shape[0]
    tm = _tile(S, 1024)
    tn = _lane_tile(F, 512)

    def body(dx_ref, wd_ref, gate_ref, up_ref, dep_ref, dgate_ref, dup_ref):
        dff = _dot(dx_ref[...], wd_ref[...], NT)
        g = gate_ref[...]
        sg = _sigmoid(g)
        dup_ref[...] = (dff * (g * sg)).astype(dup_ref.dtype)
        dgate_ref[...] = (dff * up_ref[...] * (sg * (1.0 + g * (1.0 - sg)))).astype(dgate_ref.dtype)

    aspec = pl.BlockSpec((tm, tn), lambda i, j: (i, j))
    return pl.pallas_call(
        body, name="ff_bwd_act", grid=(S // tm, F // tn),
        in_specs=[pl.BlockSpec((tm, D), lambda i, j: (i, 0)), pl.BlockSpec((tn, D), lambda i, j: (j, 0)), aspec, aspec,
                  pl.BlockSpec(TOKEN, lambda i, j: (0, 0))],
        out_specs=[aspec, aspec],
        out_shape=[_sds((S, F), BF16), _sds((S, F), BF16)],
        compiler_params=_params(("parallel", "parallel")),
    )(dxb, wd, gate, up, dep)


def dw_down(ff, dxb):
    S, F = ff.shape
    D = dxb.shape[1]
    tf = _lane_tile(F, 256)
    return pl.pallas_call(
        _mm_body(1, TN, None, 1, False), name="dw_down", grid=(F // tf,),
        in_specs=[pl.BlockSpec((S, tf), lambda k: (0, k)), pl.BlockSpec((S, D), lambda k: (0, 0))],
        out_specs=pl.BlockSpec((tf, D), lambda k: (k, 0)),
        out_shape=_sds((F, D), BF16), compiler_params=_params(("parallel",)),
    )(ff, dxb)


def dw_gate_up(h, dgate, dup):
    S, D = h.shape
    F = dgate.shape[1]
    f = _lane_tile(F, 256)

    def body(h_ref, dg_ref, du_ref, og_ref, ou_ref):
        hv = h_ref[...]
        og_ref[...] = _dot(dg_ref[...], hv, TN).astype(og_ref.dtype)
        ou_ref[...] = _dot(du_ref[...], hv, TN).astype(ou_ref.dtype)

    aspec = pl.BlockSpec((S, f), lambda k: (0, k))
    ospec = pl.BlockSpec((f, D), lambda k: (k, 0))
    return pl.pallas_call(
        body, name="dw_gate_up", grid=(F // f,),
        in_specs=[pl.BlockSpec((S, D), lambda k: (0, 0)), aspec, aspec],
        out_specs=[ospec, ospec],
        out_shape=[_sds((F, D), BF16), _sds((F, D), BF16)],
        compiler_params=_params(("parallel",)),
    )(h, dgate, dup)


def dh_ff(dgate, dup, wg, wu):
    S, F = dgate.shape
    D = wg.shape[1]
    tm = _tile(S, 512)
    tn = _lane_tile(D, 256)
    aspec = pl.BlockSpec((tm, F), lambda i, j: (i, 0))
    wspec = pl.BlockSpec((F, tn), lambda i, j: (0, j))
    return pl.pallas_call(
        _mm_body(2, NN, None, 1, False), name="dh_ff", grid=(S // tm, D // tn),
        in_specs=[aspec, wspec, aspec, wspec],
        out_specs=pl.BlockSpec((tm, tn), lambda i, j: (i, j)),
        out_shape=_sds((S, D), F32), compiler_params=_params(("parallel", "parallel")),
    )(dgate, wg, dup, wu)


def dmix_mm(dxb, w, dep):
    S, D = dxb.shape
    tm = _tile(S, 1024)
    tn = _tile(D, 1024)

    def body(dx_ref, w_ref, dep_ref, o_ref):
        o_ref[...] = _dot(dx_ref[...], w_ref[...], NT)

    return pl.pallas_call(
        body, name="dmix_mm", grid=(S // tm, D // tn),
        in_specs=[pl.BlockSpec((tm, D), lambda i, j: (i, 0)), pl.BlockSpec((tn, D), lambda i, j: (j, 0)),
                  pl.BlockSpec(TOKEN, lambda i, j: (0, 0))],
        out_specs=pl.BlockSpec((tm, tn), lambda i, j: (i, j)),
        out_shape=_sds((S, D), F32), compiler_params=_params(("parallel", "parallel")),
    )(dxb, w, dep)


def dw_out(mix_a, mix_g, dxb):
    S, DA = mix_a.shape
    D = dxb.shape[1]
    tn = _lane_tile(D, 512)

    def half(name, m):
        return pl.pallas_call(
            _mm_body(1, TN, None, 1, False), name=name, grid=(D // tn,),
            in_specs=[pl.BlockSpec((S, DA), lambda j: (0, 0)), pl.BlockSpec((S, tn), lambda j: (0, j))],
            out_specs=pl.BlockSpec((DA, tn), lambda j: (0, j)),
            out_shape=_sds((DA, D), BF16), compiler_params=_params(("parallel",)),
        )(m, dxb)

    return half("dw_out_a", mix_a), half("dw_out_g", mix_g)


def dw_in(h, dproj, NS):
    S, D = h.shape
    n = dproj.shape[1] // NS
    tm = _lane_tile(D, 1024)
    return pl.pallas_call(
        _mm_body(1, TN, None, 1, False), name="dw_in", grid=(NS, D // tm),
        in_specs=[pl.BlockSpec((S, tm), lambda k, j: (0, j)), pl.BlockSpec((S, n), lambda k, j: (0, k))],
        out_specs=pl.BlockSpec((None, tm, n), lambda k, j: (k, j, 0)),
        out_shape=_sds((NS, D, n), BF16), compiler_params=_params(("parallel", "parallel")),
    )(h, dproj)


def dh_in(dproj, w):
    S = dproj.shape[0]
    NS, D, n = w.shape
    tm = _tile(S, 512)
    tn = _tile(D, 512)

    def body(dp_ref, w_ref, o_ref):
        acc = None
        for k in range(NS):
            p = _dot(dp_ref[:, k * n:(k + 1) * n], w_ref[k], NT)
            acc = p if acc is None else acc + p
        o_ref[...] = acc

    return pl.pallas_call(
        body, name="dh_in", grid=(S // tm, D // tn),
        in_specs=[pl.BlockSpec((tm, NS * n), lambda i, j: (i, 0)), pl.BlockSpec((NS, tn, n), lambda i, j: (0, j, 0))],
        out_specs=pl.BlockSpec((tm, tn), lambda i, j: (i, j)),
        out_shape=_sds((S, D), F32), compiler_params=_params(("parallel", "parallel")),
    )(dproj, w)


def rope_tables(S):
    pos = jnp.arange(S, dtype=F32)
    inv = ROPE_THETA ** (-jnp.arange(0, HEAD_DIM, 2, dtype=F32) / HEAD_DIM)
    ang = pos[:, None] * inv[None, :]
    cos, sin = jnp.cos(ang), jnp.sin(ang)
    return jnp.concatenate([cos, cos], axis=-1), jnp.concatenate([-sin, sin], axis=-1)


def _rot_half(t):
    return pltpu.roll(t, HEAD_DIM // 2, 1)


def rope_qkv(proj, cos2, sin2, DA):
    S = proj.shape[0]
    tr = _tile(S, 512)
    nh = DA // HEAD_DIM

    def body(q_ref, k_ref, v_ref, cos_ref, sin_ref, o_ref):
        c = cos_ref[...]
        s = sin_ref[...]
        for h in range(nh):
            sl = slice(h * HEAD_DIM, (h + 1) * HEAD_DIM)
            for j, ref in enumerate((q_ref, k_ref)):
                t = ref[:, sl]
                o_ref[:, j * DA + h * HEAD_DIM:j * DA + (h + 1) * HEAD_DIM] = (t * c + _rot_half(t) * s).astype(o_ref.dtype)
        o_ref[:, 2 * DA:3 * DA] = v_ref[...].astype(o_ref.dtype)

    tab = pl.BlockSpec((tr, HEAD_DIM), lambda i: (i, 0))
    return pl.pallas_call(
        body, name="rope_qkv", grid=(S // tr,),
        in_specs=[pl.BlockSpec((tr, DA), lambda i: (i, 0)), pl.BlockSpec((tr, DA), lambda i: (i, 1)),
                  pl.BlockSpec((tr, DA), lambda i: (i, 2)), tab, tab],
        out_specs=pl.BlockSpec((tr, 3 * DA), lambda i: (i, 0)),
        out_shape=_sds((S, 3 * DA), BF16), compiler_params=_params(("parallel",)),
    )(proj, proj, proj, cos2, sin2)


def assemble_dproj(dqs, dks, dvs, duv, cos2, sin2):
    S, DA = dqs[0].shape
    tr = _tile(S, 256)
    nh = DA // HEAD_DIM

    def body(*refs):
        dq_refs, dk_refs, dv_refs = refs[0:3], refs[3:6], refs[6:9]
        duv_ref, cos_ref, sin_ref, o_ref = refs[9:13]
        c = cos_ref[...]
        s = sin_ref[...]
        for j, trio in enumerate((dq_refs, dk_refs)):
            for h in range(nh):
                sl = slice(h * HEAD_DIM, (h + 1) * HEAD_DIM)
                t = trio[0][:, sl] + trio[1][:, sl] + trio[2][:, sl]
                o_ref[:, j * DA + h * HEAD_DIM:j * DA + (h + 1) * HEAD_DIM] = (t * c - _rot_half(t) * s).astype(o_ref.dtype)
        o_ref[:, 2 * DA:3 * DA] = (dv_refs[0][...] + dv_refs[1][...] + dv_refs[2][...]).astype(o_ref.dtype)
        o_ref[:, 3 * DA:5 * DA] = duv_ref[...]

    blk = pl.BlockSpec((tr, DA), lambda i: (i, 0))
    tab = pl.BlockSpec((tr, HEAD_DIM), lambda i: (i, 0))
    return pl.pallas_call(
        body, name="assemble_dproj", grid=(S // tr,),
        in_specs=[blk] * 9 + [pl.BlockSpec((tr, 2 * DA), lambda i: (i, 0)), tab, tab],
        out_specs=pl.BlockSpec((tr, 5 * DA), lambda i: (i, 0)),
        out_shape=_sds((S, 5 * DA), BF16), compiler_params=_params(("parallel",)),
    )(*dqs, *dks, *dvs, duv, cos2, sin2)


def _halo_specs(L, width_blocks, col):
    per = TQ // HALO
    last = L // HALO - 1
    W = width_blocks
    return [
        pl.BlockSpec((HALO, W), lambda r, i: (jnp.maximum(i * per - 1, 0), col(r))),
        pl.BlockSpec((TQ, W), lambda r, i: (i, col(r))),
        pl.BlockSpec((HALO, W), lambda r, i: (jnp.minimum((i + 1) * per, last), col(r))),
    ]


def _cat3(refs, sl):
    return jnp.concatenate([refs[0][:, sl], refs[1][:, sl], refs[2][:, sl]], axis=0)


def attn_fwd(qkv, d, DA):
    S = qkv.shape[0]
    L = S // d
    nh = DA // HEAD_DIM
    scale = HEAD_DIM ** -0.5
    TK = TQ + 2 * HALO

    def body(q_ref, kp_ref, kc_ref, kn_ref, vp_ref, vc_ref, vn_ref, o_ref, lse_ref):
        i = pl.program_id(1)
        row = lax.broadcasted_iota(jnp.int32, (TQ, TK), 0)
        col = lax.broadcasted_iota(jnp.int32, (TQ, TK), 1)
        kpos = i * TQ - HALO + col
        mask = (jnp.abs(col - HALO - row) <= N_SIDE) & (kpos >= 0) & (kpos < L)
        for h in range(nh):
            sl = slice(h * HEAD_DIM, (h + 1) * HEAD_DIM)
            k = _cat3((kp_ref, kc_ref, kn_ref), sl)
            v = _cat3((vp_ref, vc_ref, vn_ref), sl)
            s = _dot(q_ref[:, sl], k, NT) * scale
            s = jnp.where(mask, s, NEG)
            m = jnp.max(s, axis=-1, keepdims=True)
            p = jnp.exp(s - m)
            l = jnp.sum(p, axis=-1, keepdims=True)
            o = _dot(p.astype(v.dtype), v, NN) / l
            o_ref[:, sl] = o
            lse_ref[:, sl] = jnp.broadcast_to(m + jnp.log(l), (TQ, HEAD_DIM))

    out_spec = pl.BlockSpec((TQ, DA), lambda r, i: (i, r))
    o, lse = pl.pallas_call(
        body, name="attn_fwd_d%d" % d, grid=(d, L // TQ),
        in_specs=[pl.BlockSpec((TQ, DA), lambda r, i: (i, 3 * r))]
        + _halo_specs(L, DA, lambda r: 3 * r + 1) + _halo_specs(L, DA, lambda r: 3 * r + 2),
        out_specs=[out_spec, out_spec],
        out_shape=[_sds((L, d * DA), F32), _sds((L, d * DA), F32)],
        compiler_params=_params(("parallel", "parallel")),
    )(*([qkv.reshape(L, d * 3 * DA)] * 7))
    return o.reshape(S, DA), lse.reshape(S, DA)


def attn_merge(os_, lses, g):
    S, DA = os_[0].shape
    tr = _tile(S, 256)

    def body(o0, o1, o2, l0, l1, l2, g_ref, a_ref, lse_ref, mix_ref):
        a0, a1, a2 = l0[...], l1[...], l2[...]
        m = jnp.maximum(jnp.maximum(a0, a1), a2)
        e0, e1, e2 = jnp.exp(a0 - m), jnp.exp(a1 - m), jnp.exp(a2 - m)
        den = e0 + e1 + e2
        a = (e0 / den) * o0[...] + (e1 / den) * o1[...] + (e2 / den) * o2[...]
        a_ref[...] = a
        lse_ref[...] = m + jnp.log(den)
        r = lax.rsqrt(jnp.mean(a * a, axis=-1, keepdims=True) + EPS)
        mix_ref[...] = (a * r * g_ref[...]).astype(mix_ref.dtype)

    blk = pl.BlockSpec((tr, DA), lambda i: (i, 0))
    return pl.pallas_call(
        body, name="attn_merge", grid=(S // tr,),
        in_specs=[blk] * 6 + [pl.BlockSpec((1, DA), lambda i: (0, 0))],
        out_specs=[blk, blk, blk],
        out_shape=[_sds((S, DA), F32), _sds((S, DA), F32), _sds((S, DA), BF16)],
        compiler_params=_params(("parallel",)),
    )(*os_, *lses, g)


def attn_out_bwd(a, g, dmix):
    S, DA = a.shape
    tr = _tile(S, 256)
    nh = DA // HEAD_DIM

    def body(a_ref, g_ref, dy_ref, do_ref, dl_ref, dg_ref):
        av = a_ref[...]
        dx, dg = _rms_bwd_math(av, g_ref[...], dy_ref[...])
        do_ref[...] = dx.astype(do_ref.dtype)
        prod = dx * av
        for h in range(nh):
            sl = slice(h * HEAD_DIM, (h + 1) * HEAD_DIM)
            dl_ref[:, sl] = jnp.broadcast_to(jnp.sum(prod[:, sl], axis=-1, keepdims=True), (tr, HEAD_DIM))

        @pl.when(pl.program_id(0) == 0)
        def _():
            dg_ref[...] = dg

        @pl.when(pl.program_id(0) > 0)
        def _():
            dg_ref[...] += dg

    blk = pl.BlockSpec((tr, DA), lambda i: (i, 0))
    vec = pl.BlockSpec((1, DA), lambda i: (0, 0))
    return pl.pallas_call(
        body, name="attn_out_bwd", grid=(S // tr,),
        in_specs=[blk, vec, blk], out_specs=[blk, blk, vec],
        out_shape=[_sds((S, DA), BF16), _sds((S, DA), F32), _sds((1, DA), F32)],
        compiler_params=_params(("arbitrary",)),
    )(a, g, dmix)


def attn_bwd_dq(qkv, dob, lse, dl, d, DA):
    S = qkv.shape[0]
    L = S // d
    nh = DA // HEAD_DIM
    scale = HEAD_DIM ** -0.5
    TK = TQ + 2 * HALO

    def body(q_ref, kp_ref, kc_ref, kn_ref, vp_ref, vc_ref, vn_ref, do_ref, lse_ref, dl_ref, dq_ref):
        i = pl.program_id(1)
        row = lax.broadcasted_iota(jnp.int32, (TQ, TK), 0)
        col = lax.broadcasted_iota(jnp.int32, (TQ, TK), 1)
        kpos = i * TQ - HALO + col
        mask = (jnp.abs(col - HALO - row) <= N_SIDE) & (kpos >= 0) & (kpos < L)
        for h in range(nh):
            sl = slice(h * HEAD_DIM, (h + 1) * HEAD_DIM)
            k = _cat3((kp_ref, kc_ref, kn_ref), sl)
            v = _cat3((vp_ref, vc_ref, vn_ref), sl)
            s = _dot(q_ref[:, sl], k, NT) * scale
            s = jnp.where(mask, s, NEG)
            p = jnp.exp(s - lse_ref[:, sl][:, 0:1])
            dp = _dot(do_ref[:, sl], v, NT)
            ds = p * (dp - dl_ref[:, sl][:, 0:1])
            dq_ref[:, sl] = _dot(ds.astype(k.dtype), k, NN) * scale

    blk = pl.BlockSpec((TQ, DA), lambda r, i: (i, r))
    dq = pl.pallas_call(
        body, name="attn_bwd_dq_d%d" % d, grid=(d, L // TQ),
        in_specs=[pl.BlockSpec((TQ, DA), lambda r, i: (i, 3 * r))]
        + _halo_specs(L, DA, lambda r: 3 * r + 1) + _halo_specs(L, DA, lambda r: 3 * r + 2) + [blk, blk, blk],
        out_specs=blk, out_shape=_sds((L, d * DA), F32),
        compiler_params=_params(("parallel", "parallel")),
    )(*([qkv.reshape(L, d * 3 * DA)] * 7), dob.reshape(L, d * DA), lse.reshape(L, d * DA), dl.reshape(L, d * DA))
    return dq.reshape(S, DA)


def attn_bwd_dkv(qkv, dob, lse, dl, d, DA):
    S = qkv.shape[0]
    L = S // d
    nh = DA // HEAD_DIM
    scale = HEAD_DIM ** -0.5
    TR = TQ + 2 * HALO

    def body(k_ref, v_ref, qp, qc, qn, dop, doc, don, lp, lc, ln, dp_, dc_, dn_, dk_ref, dv_ref):
        j = pl.program_id(1)
        row = lax.broadcasted_iota(jnp.int32, (TR, TQ), 0)
        col = lax.broadcasted_iota(jnp.int32, (TR, TQ), 1)
        qpos = j * TQ - HALO + row
        mask = (jnp.abs(col - (row - HALO)) <= N_SIDE) & (qpos >= 0) & (qpos < L)
        for h in range(nh):
            sl = slice(h * HEAD_DIM, (h + 1) * HEAD_DIM)
            q = _cat3((qp, qc, qn), sl)
            do = _cat3((dop, doc, don), sl)
            lse_q = _cat3((lp, lc, ln), sl)
            dl_q = _cat3((dp_, dc_, dn_), sl)
            k = k_ref[:, sl]
            v = v_ref[:, sl]
            s = _dot(q, k, NT) * scale
            s = jnp.where(mask, s, NEG)
            p = jnp.exp(s - lse_q)
            dv_ref[:, sl] = _dot(p.astype(do.dtype), do, TN)
            dp = _dot(do, v, NT)
            ds = p * (dp - dl_q)
            dk_ref[:, sl] = _dot(ds.astype(q.dtype), q, TN) * scale

    blk = pl.BlockSpec((TQ, DA), lambda r, i: (i, r))
    ident = lambda r: r
    dk, dv = pl.pallas_call(
        body, name="attn_bwd_dkv_d%d" % d, grid=(d, L // TQ),
        in_specs=[pl.BlockSpec((TQ, DA), lambda r, i: (i, 3 * r + 1)), pl.BlockSpec((TQ, DA), lambda r, i: (i, 3 * r + 2))]
        + _halo_specs(L, DA, lambda r: 3 * r) + _halo_specs(L, DA, ident) + _halo_specs(L, DA, ident) + _halo_specs(L, DA, ident),
        out_specs=[blk, blk], out_shape=[_sds((L, d * DA), F32), _sds((L, d * DA), F32)],
        compiler_params=_params(("parallel", "parallel")),
    )(*([qkv.reshape(L, d * 3 * DA)] * 5), *([dob.reshape(L, d * DA)] * 3), *([lse.reshape(L, d * DA)] * 3),
      *([dl.reshape(L, d * DA)] * 3))
    return dk.reshape(S, DA), dv.reshape(S, DA)


def _gmlp_fwd_math(u_raw, v_raw, ln_g, ws_ref, bcol_ref, n_chunks, ng):
    ug = _gelu(u_raw)
    vg = _gelu(v_raw)
    mu = jnp.mean(vg, axis=-1, keepdims=True)
    xc = vg - mu
    rstd = lax.rsqrt(jnp.mean(xc * xc, axis=-1, keepdims=True) + EPS)
    xhat = xc * rstd
    vn = xhat * ln_g
    rows = []
    for n in range(n_chunks):
        cols = []
        for g in range(ng):
            blk = vn[n * CHUNK:(n + 1) * CHUNK, g * GROUP:(g + 1) * GROUP]
            cols.append(_dot(ws_ref[g].astype(BF16), blk.astype(BF16), NN) + bcol_ref[g])
        rows.append(jnp.concatenate(cols, axis=1))
    mixed = jnp.concatenate(rows, axis=0)
    return ug, xhat, rstd, vn, mixed


def gmlp_fwd(proj, ln_g, w_s, bcol, g_out, DA):
    S = proj.shape[0]
    ng = DA // GROUP
    tr = _tile(S, 2 * CHUNK)

    def body(u_ref, v_ref, ln_ref, ws_ref, bcol_ref, g_ref, G_ref, mix_ref):
        ug, _, _, _, mixed = _gmlp_fwd_math(u_ref[...], v_ref[...], ln_ref[...], ws_ref, bcol_ref, tr // CHUNK, ng)
        G = ug * mixed
        G_ref[...] = G
        r = lax.rsqrt(jnp.mean(G * G, axis=-1, keepdims=True) + EPS)
        mix_ref[...] = (G * r * g_ref[...]).astype(mix_ref.dtype)

    vec = pl.BlockSpec((1, DA), lambda i: (0, 0))
    par = pl.BlockSpec((ng, CHUNK, CHUNK), lambda i: (0, 0, 0))
    blk = pl.BlockSpec((tr, DA), lambda i: (i, 0))
    return pl.pallas_call(
        body, name="gmlp_fwd", grid=(S // tr,),
        in_specs=[pl.BlockSpec((tr, DA), lambda i: (i, 3)), pl.BlockSpec((tr, DA), lambda i: (i, 4)), vec, par, par, vec],
        out_specs=[blk, blk], out_shape=[_sds((S, DA), F32), _sds((S, DA), BF16)],
        compiler_params=_params(("parallel",)),
    )(proj, proj, ln_g, w_s, bcol, g_out)


def gmlp_bwd(proj, ln_g, w_s, w_st, bcol, g_out, dmix, DA):
    S = proj.shape[0]
    ng = DA // GROUP
    tr = _tile(S, 2 * CHUNK)
    nc = tr // CHUNK

    def body(u_ref, v_ref, ln_ref, ws_ref, wst_ref, bcol_ref, g_ref, dy_ref, duv_ref, dg_ref, dln_ref, dws_ref, db_ref):
        u_raw = u_ref[...]
        v_raw = v_ref[...]
        ln_g_v = ln_ref[...]
        ug, xhat, rstd, vn, mixed = _gmlp_fwd_math(u_raw, v_raw, ln_g_v, ws_ref, bcol_ref, nc, ng)
        G = ug * mixed
        dG, dg = _rms_bwd_math(G, g_ref[...], dy_ref[...])
        du_g = dG * mixed
        dmixed = dG * ug
        dws, dbs, rows = [], [], []
        for g in range(ng):
            dws.append(None)
            dbs.append(None)
        for n in range(nc):
            cols = []
            for g in range(ng):
                dm = dmixed[n * CHUNK:(n + 1) * CHUNK, g * GROUP:(g + 1) * GROUP]
                vb = vn[n * CHUNK:(n + 1) * CHUNK, g * GROUP:(g + 1) * GROUP]
                dmb = dm.astype(BF16)
                w = _dot(dmb, vb.astype(BF16), NT)
                b = jnp.broadcast_to(jnp.sum(dm, axis=-1, keepdims=True), (CHUNK, GROUP))
                dws[g] = w if dws[g] is None else dws[g] + w
                dbs[g] = b if dbs[g] is None else dbs[g] + b
                cols.append(_dot(wst_ref[g].astype(BF16), dmb, NN))
            rows.append(jnp.concatenate(cols, axis=1))
        dvn = jnp.concatenate(rows, axis=0)
        dln = jnp.sum(dvn * xhat, axis=0, keepdims=True)
        dxh = dvn * ln_g_v
        dvg = rstd * (dxh - jnp.mean(dxh, axis=-1, keepdims=True) - xhat * jnp.mean(dxh * xhat, axis=-1, keepdims=True))
        duv_ref[:, 0:DA] = (du_g * _gelu_grad(u_raw)).astype(duv_ref.dtype)
        duv_ref[:, DA:2 * DA] = (dvg * _gelu_grad(v_raw)).astype(duv_ref.dtype)

        first = pl.program_id(0) == 0

        @pl.when(first)
        def _():
            dg_ref[...] = dg
            dln_ref[...] = dln
            for g in range(ng):
                dws_ref[g] = dws[g]
                db_ref[g] = dbs[g]

        @pl.when(jnp.logical_not(first))
        def _():
            dg_ref[...] += dg
            dln_ref[...] += dln
            for g in range(ng):
                dws_ref[g] += dws[g]
                db_ref[g] += dbs[g]

    vec = pl.BlockSpec((1, DA), lambda i: (0, 0))
    par = pl.BlockSpec((ng, CHUNK, CHUNK), lambda i: (0, 0, 0))
    return pl.pallas_call(
        body, name="gmlp_bwd", grid=(S // tr,),
        in_specs=[pl.BlockSpec((tr, DA), lambda i: (i, 3)), pl.BlockSpec((tr, DA), lambda i: (i, 4)), vec, par, par, par, vec,
                  pl.BlockSpec((tr, DA), lambda i: (i, 1))],
        out_specs=[pl.BlockSpec((tr, 2 * DA), lambda i: (i, 0)), vec, vec, par, par],
        out_shape=[_sds((S, 2 * DA), BF16), _sds((1, DA), F32), _sds((1, DA), F32),
                   _sds((ng, CHUNK, CHUNK), F32), _sds((ng, CHUNK, CHUNK), F32)],
        compiler_params=_params(("arbitrary",)),
    )(proj, proj, ln_g, w_s, w_st, bcol, g_out, dmix)


def mixer_fwd(x, sm, gw, tabs):
    D = x.shape[1]
    DA = D // 2
    cos2, sin2 = tabs
    h1 = rms_fwd(x, sm["norm1_g"])
    proj = mm_in(h1, gw["w_in"])
    qkv = rope_qkv(proj, cos2, sin2, DA)
    os_, lses = [], []
    for d in DILATIONS:
        o, l = attn_fwd(qkv, d, DA)
        os_.append(o)
        lses.append(l)
    a, lse, mix_a = attn_merge(os_, lses, sm["mix_norm_attn_g"])
    _, mix_g = gmlp_fwd(proj, sm["gmlp_ln_g"], sm["w_spatial"], sm["bcol"], sm["mix_norm_gmlp_g"], DA)
    x2 = mm_out(x, mix_a, mix_g, gw["w_out"])
    return x2, dict(x=x, h1=h1, proj=proj, qkv=qkv, a=a, lse=lse, mix_a=mix_a, mix_g=mix_g)


def ffn_fwd(x2, sm, gw):
    h2 = rms_fwd(x2, sm["norm2_g"])
    gate, up, ff = ff_fwd(h2, gw["w_gate"], gw["w_up"])
    x3 = mm_down(x2, ff, gw["w_down"])
    return x3, dict(x2=x2, h2=h2, gate=gate, up=up, ff=ff)


def ffn_bwd(dx, dxb, sv, sm, gw, dep):
    dgate, dup = ff_bwd_act(dxb, gw["w_down"], sv["gate"], sv["up"], dep)
    g_down = dw_down(sv["ff"], dxb)
    g_gate, g_up = dw_gate_up(sv["h2"], dgate, dup)
    dh2 = dh_ff(dgate, dup, gw["w_gate"], gw["w_up"])
    dx2, dx2b, d_norm2 = rms_bwd_res(sv["x2"], sm["norm2_g"], dh2, dx)
    return dx2, dx2b, dict(w_gate=g_gate, w_up=g_up, w_down=g_down), dict(norm2_g=d_norm2)


def mixer_bwd(dx2, dx2b, sv, sm, gw, tabs, dep):
    D = dx2.shape[1]
    DA = D // 2
    cos2, sin2 = tabs
    NS = gw["w_in"].shape[0]
    dmix = dmix_mm(dx2b, gw["w_out"].reshape(D, D), dep)
    g_out_a, g_out_g = dw_out(sv["mix_a"], sv["mix_g"], dx2b)
    dob, dl, d_mix_a = attn_out_bwd(sv["a"], sm["mix_norm_attn_g"], dmix)
    duv, d_mix_g, d_ln, d_ws, d_bs = gmlp_bwd(sv["proj"], sm["gmlp_ln_g"], sm["w_spatial"], sm["w_spatial_t"], sm["bcol"],
                                              sm["mix_norm_gmlp_g"], dmix, DA)
    dqs, dks, dvs = [], [], []
    for d in DILATIONS:
        dqs.append(attn_bwd_dq(sv["qkv"], dob, sv["lse"], dl, d, DA))
        dk, dv = attn_bwd_dkv(sv["qkv"], dob, sv["lse"], dl, d, DA)
        dks.append(dk)
        dvs.append(dv)
    dproj = assemble_dproj(dqs, dks, dvs, duv, cos2, sin2)
    g_in = dw_in(sv["h1"], dproj, NS)
    dh1 = dh_in(dproj, gw["w_in"])
    dx0, dx0b, d_norm1 = rms_bwd_res(sv["x"], sm["norm1_g"], dh1, dx2)
    big = dict(w_in=g_in, w_out=jnp.concatenate([g_out_a, g_out_g], axis=0))
    small = dict(norm1_g=d_norm1, gmlp_ln_g=d_ln, w_spatial=d_ws, b_spatial=d_bs[:, :, 0], mix_norm_attn_g=d_mix_a,
                 mix_norm_gmlp_g=d_mix_g)
    return dx0, dx0b, big, small


def _place():
    x, y, c = lax.axis_index("x"), lax.axis_index("y"), lax.axis_index("c")
    return x, y, c, [(1 - x, y), (x, 1 - y), (1 - x, 1 - y)]


def _remote(src, dst, send_sems, recv_sems, k, to):
    return pltpu.make_async_remote_copy(src_ref=src, dst_ref=dst, send_sem=send_sems.at[k], recv_sem=recv_sems.at[k],
                                        device_id=to, device_id_type=MESH)


def to_slot(src, layer, me_arr, dep, dtype):
    R, C = src.shape[-2:]
    tr = _tile(R, max(16, 4 * ADAMW_BLOCK_ELEMS // C))

    def body(me_ref, src_ref, dep_ref, o_ref):
        o_ref[...] = src_ref[...].astype(o_ref.dtype)

    if layer is None:
        src_spec = pl.BlockSpec((tr, C), lambda i, me_ref: (i, 0))
    else:
        src_spec = pl.BlockSpec((None, tr, C), lambda i, me_ref: (layer, i, 0))
    return pl.pallas_call(
        body, name="to_slot",
        grid_spec=pltpu.PrefetchScalarGridSpec(
            num_scalar_prefetch=1, grid=(R // tr,),
            in_specs=[src_spec, pl.BlockSpec(TOKEN, lambda i, me_ref: (0, 0))],
            out_specs=pl.BlockSpec((None, tr, C), lambda i, me_ref: (me_ref[0], i, 0))),
        out_shape=_sds((N_DEV, R, C), dtype), compiler_params=_params(("parallel",)),
    )(me_arr, src, dep)


def gather_ici(bufs):
    n = len(bufs)

    def body(*refs):
        outs = refs[n:2 * n]
        send_sems, recv_sems = refs[2 * n:]
        x, y, c, chips = _place()
        mine = [o.at[4 * x + 2 * y + c] for o in outs]
        sent = []
        for a in range(n):
            for j, (px, py) in enumerate(chips):
                cp = _remote(mine[a], mine[a], send_sems, recv_sems, 3 * a + j, (px, py, c))
                cp.start()
                sent.append(cp)
        for a in range(n):
            for j, (px, py) in enumerate(chips):
                slot = outs[a].at[4 * px + 2 * py + c]
                _remote(slot, slot, send_sems, recv_sems, 3 * a + j, (px, py, c)).wait_recv()
        for cp in sent:
            cp.wait_send()

    return pl.pallas_call(
        body, name="gather_ici", in_specs=[ANY] * n, out_specs=[ANY] * n,
        out_shape=[_sds(b.shape, b.dtype) for b in bufs], input_output_aliases={a: a for a in range(n)},
        scratch_shapes=[pltpu.SemaphoreType.DMA((3 * n,)), pltpu.SemaphoreType.DMA((3 * n,))],
    )(*bufs)


HBM_SPEC = pl.BlockSpec(memory_space=pltpu.HBM)
SEM_SPEC = pl.BlockSpec(memory_space=pltpu.SEMAPHORE)
DATAFLOW = pltpu.SideEffectType.DATAFLOW_SIDE_EFFECTING
TOKEN = (8, 128)


def _in_hbm(a):
    return pltpu.with_memory_space_constraint(a, pltpu.HBM)


PLAN_COPIES = dict(gather_ici=3, gather_d2d=N_CHIP, scatter_d2d=N_CHIP, scatter_ici=3)


def _plan(kind):
    x, y, c, chips = _place()
    sibling = (x, y, 1 - c)
    if kind == "gather_ici":
        me = 4 * x + 2 * y + c
        return [(me, me, 4 * px + 2 * py + c, (px, py, c)) for px, py in chips]
    if kind == "gather_d2d":
        return [(2 * j + c, 2 * j + c, 2 * j + 1 - c, sibling) for j in range(N_CHIP)]
    if kind == "scatter_d2d":
        return [(2 * j + 1 - c, j, j, sibling) for j in range(N_CHIP)]
    assert kind == "scatter_ici"
    return [(2 * px + py, 2 * x + y, 2 * px + py, (px, py, c)) for px, py in chips]


def split_start(kind, srcs, dsts):
    n = len(srcs)
    bufs = list(srcs) + ([] if dsts is None else list(dsts))
    nb = len(bufs)
    k = PLAN_COPIES[kind]

    def body(*refs):
        ins = refs[:n]
        outs = ins if dsts is None else refs[n:2 * n]
        send_sems, recv_sems = refs[nb], refs[nb + 1]
        token = refs[-1]
        for a in range(n):
            for j, (src, dst, _, peer) in enumerate(_plan(kind)):
                _remote(ins[a].at[src], outs[a].at[dst], send_sems, recv_sems, k * a + j, peer).start()
        token[...] = jnp.zeros_like(token)

    out = pl.pallas_call(
        body, name=kind + "_start",
        out_shape=(pltpu.SemaphoreType.DMA((k * n,)), pltpu.SemaphoreType.DMA((k * n,)),
                   *[pltpu.HBM(b.shape, b.dtype) for b in bufs], _sds(TOKEN, F32)),
        in_specs=[HBM_SPEC] * nb,
        out_specs=(SEM_SPEC, SEM_SPEC, *[HBM_SPEC] * nb, pl.BlockSpec(memory_space=pltpu.VMEM)),
        input_output_aliases={i: 2 + i for i in range(nb)},
        compiler_params=pltpu.CompilerParams(has_side_effects=DATAFLOW),
    )(*[_in_hbm(b) for b in bufs])
    return kind, out[0], out[1], list(out[2:2 + n]), None if dsts is None else list(out[2 + n:2 + nb]), out[-1]


def split_wait(pending, after):
    kind, send_sems, recv_sems, srcs, dsts, _ = pending
    n = len(srcs)
    bufs = list(srcs) + ([] if dsts is None else list(dsts))
    nb = len(bufs)
    k = PLAN_COPIES[kind]

    def body(*refs):
        ins = refs[:n]
        outs = ins if dsts is None else refs[n:2 * n]
        send_sems, recv_sems = refs[nb], refs[nb + 1]
        for a in range(n):
            for j, (src, _, landed, peer) in enumerate(_plan(kind)):
                cp = _remote(ins[a].at[src], outs[a].at[landed], send_sems, recv_sems, k * a + j, peer)
                cp.wait_send()
                cp.wait_recv()

    out = pl.pallas_call(
        body, name=kind + "_wait",
        out_shape=[pltpu.HBM(b.shape, b.dtype) for b in bufs],
        in_specs=[HBM_SPEC] * nb + [SEM_SPEC, SEM_SPEC, ANY], out_specs=[HBM_SPEC] * nb,
        input_output_aliases={i: i for i in range(nb)},
        compiler_params=pltpu.CompilerParams(has_side_effects=DATAFLOW),
    )(*bufs, send_sems, recv_sems, after)
    return list(out[nb - n:])


def gather_d2d(bufs):
    n = len(bufs)

    def body(*refs):
        outs = refs[n:2 * n]
        token = refs[2 * n]
        send_sems, recv_sems = refs[2 * n + 1:]
        x, y, c, _ = _place()
        sent = []
        for a in range(n):
            for j in range(N_CHIP):
                slot = outs[a].at[2 * j + c]
                cp = _remote(slot, slot, send_sems, recv_sems, N_CHIP * a + j, (x, y, 1 - c))
                cp.start()
                sent.append(cp)
        for a in range(n):
            for j in range(N_CHIP):
                slot = outs[a].at[2 * j + 1 - c]
                _remote(slot, slot, send_sems, recv_sems, N_CHIP * a + j, (x, y, 1 - c)).wait_recv()
        for cp in sent:
            cp.wait_send()
        token[...] = jnp.zeros_like(token)

    out = pl.pallas_call(
        body, name="gather_d2d", in_specs=[ANY] * n,
        out_specs=[ANY] * n + [pl.BlockSpec(memory_space=pltpu.VMEM)],
        out_shape=[_sds(b.shape, b.dtype) for b in bufs] + [_sds(TOKEN, F32)],
        input_output_aliases={a: a for a in range(n)},
        scratch_shapes=[pltpu.SemaphoreType.DMA((N_CHIP * n,)), pltpu.SemaphoreType.DMA((N_CHIP * n,))],
    )(*bufs)
    return list(out[:n]), out[n]


def pair_sum(part, got, c_arr, chip_arr):
    _, R, C = part.shape
    tr = _tile(R, 512)
    p4 = part.reshape(N_CHIP, 2, R, C)

    def body(c_ref, chip_ref, p_ref, g_ref, o_ref, own_ref):
        s = (p_ref[...].astype(F32) + g_ref[...].astype(F32)).astype(o_ref.dtype)
        o_ref[...] = s

        @pl.when(pl.program_id(1) == chip_ref[0])
        def _():
            own_ref[...] = s

    return pl.pallas_call(
        body, name="pair_sum",
        grid_spec=pltpu.PrefetchScalarGridSpec(
            num_scalar_prefetch=2, grid=(R // tr, N_CHIP),
            in_specs=[pl.BlockSpec((None, None, tr, C), lambda i, j, c_ref, chip_ref: (j, c_ref[0], i, 0)),
                      pl.BlockSpec((None, tr, C), lambda i, j, c_ref, chip_ref: (j, i, 0))],
            out_specs=[pl.BlockSpec((None, tr, C), lambda i, j, c_ref, chip_ref: (j, i, 0)),
                       pl.BlockSpec((None, tr, C), lambda i, j, c_ref, chip_ref: (chip_ref[0], i, 0))]),
        out_shape=[_sds((N_CHIP, R, C), part.dtype), _sds((N_CHIP, R, C), part.dtype)],
        compiler_params=_params(("parallel", "arbitrary")),
    )(c_arr, chip_arr, p4, got)


def adamw(w, m, v, parts_per_layer):
    NL, R, C = w.shape
    P = parts_per_layer[0].shape[0]
    tr = _tile(R, max(16, ADAMW_BLOCK_ELEMS // C))
    nb = R // tr

    def body(w_ref, m_ref, v_ref, *rest):
        part_refs = rest[:NL]
        g_ref, d_ref, nm_ref, nv_ref = rest[NL:]
        layer = pl.program_id(0)
        g = None
        for q in range(NL):
            s = part_refs[q][0].astype(F32)
            for t in range(1, P):
                s = s + part_refs[q][t].astype(F32)
            s = jnp.where(layer == q, s, 0.0)
            g = s if g is None else g + s
        wv = w_ref[...]
        nm = ADAM_B1 * m_ref[...] + (1.0 - ADAM_B1) * g
        nv = ADAM_B2 * v_ref[...] + (1.0 - ADAM_B2) * (g * g)
        m_hat = nm / (1.0 - ADAM_B1 ** ADAM_STEP)
        v_hat = nv / (1.0 - ADAM_B2 ** ADAM_STEP)
        g_ref[...] = g
        d_ref[...] = -ADAM_LR * (m_hat / (jnp.sqrt(v_hat) + ADAM_EPS) + ADAM_WD * wv)
        nm_ref[...] = nm
        nv_ref[...] = nv

    def part_spec(q):
        return pl.BlockSpec((P, tr, C), lambda l, i: (0, jnp.where(l == q, i, jnp.where(l < q, 0, nb - 1)), 0))

    blk = pl.BlockSpec((None, tr, C), lambda l, i: (l, i, 0))
    return pl.pallas_call(
        body, name="adamw", grid=(NL, nb),
        in_specs=[blk, blk, blk] + [part_spec(q) for q in range(NL)],
        out_specs=[blk, blk, blk, blk], out_shape=[_sds((NL, R, C), F32)] * 4,
        compiler_params=_params(("arbitrary", "arbitrary")),
    )(w, m, v, *parts_per_layer)


SMALL = ("norm1_g", "gmlp_ln_g", "w_spatial", "b_spatial", "mix_norm_attn_g", "mix_norm_gmlp_g", "norm2_g")
BIG = ("w_in", "w_out", "w_gate", "w_up", "w_down")
TRANSPOSED = ("w_gate", "w_up")
GROUPS = (("w_in", "w_out"), ("w_gate", "w_up", "w_down"))
LANES = 128


def _pack(layers, final):
    flat = [layer[n].reshape(-1) for layer in layers for n in SMALL] + [final.reshape(-1)]
    return jnp.concatenate(flat).reshape(-1, LANES)


def _unpack(packed, like_layers, like_final):
    flat = packed.reshape(-1)
    out, off = [], 0
    for layer in like_layers:
        d = {}
        for n in SMALL:
            size = layer[n].size
            d[n] = flat[off:off + size].reshape(layer[n].shape)
            off += size
        out.append(d)
    return out, flat[off:off + like_final.size].reshape(like_final.shape)


def kernel(x, norm1_g, w_in, gmlp_ln_g, w_spatial, b_spatial, mix_norm_attn_g, mix_norm_gmlp_g, w_out, norm2_g, w_gate, w_up, w_down, final_g, loss_target, m_norm1_g, m_w_in, m_gmlp_ln_g, m_w_spatial, m_b_spatial, m_mix_norm_attn_g, m_mix_norm_gmlp_g, m_w_out, m_norm2_g, m_w_gate, m_w_up, m_w_down, m_final_g, v_norm1_g, v_w_in, v_gmlp_ln_g, v_w_spatial, v_b_spatial, v_mix_norm_attn_g, v_mix_norm_gmlp_g, v_w_out, v_norm2_g, v_w_gate, v_w_up, v_w_down, v_final_g):
    S, D = x.shape[1], x.shape[2]
    NL = norm1_g.shape[0]
    DA = D // 2
    xs = x.reshape(S, D)
    tabs = rope_tables(S)
    ax, ay, ac = lax.axis_index("x"), lax.axis_index("y"), lax.axis_index("c")
    me_arr = (4 * ax + 2 * ay + ac).astype(jnp.int32).reshape(1)
    c_arr = ac.astype(jnp.int32).reshape(1)
    chip_arr = (2 * ax + ay).astype(jnp.int32).reshape(1)
    small_w = dict(norm1_g=norm1_g, gmlp_ln_g=gmlp_ln_g, w_spatial=w_spatial, b_spatial=b_spatial,
                   mix_norm_attn_g=mix_norm_attn_g, mix_norm_gmlp_g=mix_norm_gmlp_g, norm2_g=norm2_g)
    small_m = dict(norm1_g=m_norm1_g, gmlp_ln_g=m_gmlp_ln_g, w_spatial=m_w_spatial, b_spatial=m_b_spatial,
                   mix_norm_attn_g=m_mix_norm_attn_g, mix_norm_gmlp_g=m_mix_norm_gmlp_g, norm2_g=m_norm2_g)
    small_v = dict(norm1_g=v_norm1_g, gmlp_ln_g=v_gmlp_ln_g, w_spatial=v_w_spatial, b_spatial=v_b_spatial,
                   mix_norm_attn_g=v_mix_norm_attn_g, mix_norm_gmlp_g=v_mix_norm_gmlp_g, norm2_g=v_norm2_g)
    def view(n, a):
        return jnp.swapaxes(a, 1, 2) if n in TRANSPOSED else a

    big_w = {n: view(n, a) for n, a in dict(w_in=w_in, w_out=w_out, w_gate=w_gate, w_up=w_up, w_down=w_down).items()}
    big_m = {n: view(n, a) for n, a in dict(w_in=m_w_in, w_out=m_w_out, w_gate=m_w_gate, w_up=m_w_up, w_down=m_w_down).items()}
    big_v = {n: view(n, a) for n, a in dict(w_in=v_w_in, w_out=v_w_out, w_gate=v_w_gate, w_up=v_w_up, w_down=v_w_down).items()}

    def layer_small(l):
        ws = w_spatial[l]
        return dict(norm1_g=norm1_g[l][None], gmlp_ln_g=gmlp_ln_g[l][None], mix_norm_attn_g=mix_norm_attn_g[l][None],
                    mix_norm_gmlp_g=mix_norm_gmlp_g[l][None], norm2_g=norm2_g[l][None], w_spatial=ws,
                    w_spatial_t=jnp.swapaxes(ws, 1, 2), bcol=jnp.broadcast_to(b_spatial[l][:, :, None], ws.shape))

    n_stages = 2 * NL
    zero = jnp.zeros(TOKEN, F32)
    TOK = 5

    def stage_bufs(s, dep):
        return [to_slot(big_w[n], s // 2, me_arr, dep, BF16) for n in GROUPS[s % 2]]

    def stage_weights(s, bufs):
        gw = dict(zip(GROUPS[s % 2], bufs))
        if s % 2 == 0:
            gw["w_out"] = gw["w_out"].reshape(2, DA, D)
        else:
            gw = {n: b.reshape(-1, D) for n, b in gw.items()}
        return gw

    ici = {0: split_start("gather_ici", stage_bufs(0, zero), None)}
    ici[1] = split_start("gather_ici", stage_bufs(1, ici[0][TOK]), None)
    h = xs
    d2d = {0: split_start("gather_d2d", split_wait(ici[0], h), None)}
    ready = {0: split_wait(d2d[0], h)}
    saved, weights = [], []
    for s in range(n_stages):
        deps = [ici[1][TOK]] if s == 0 else []
        if 1 <= s < n_stages - 1:
            d2d[s + 1] = split_start("gather_d2d", split_wait(ici[s + 1], h), None)
            deps.append(d2d[s + 1][TOK])
        if s + 2 < n_stages:
            ici[s + 2] = split_start("gather_ici", stage_bufs(s + 2, deps[-1]), None)
            deps.append(ici[s + 2][TOK])
        dep = sum(deps[1:], deps[0]) if deps else zero
        gw = stage_weights(s, ready[s])
        sm = layer_small(s // 2)
        if s % 2 == 0:
            sm["norm1_g"] = sm["norm1_g"] + dep[0, 0]
            h, sv = mixer_fwd(h, sm, gw, tabs)
        else:
            sm["norm2_g"] = sm["norm2_g"] + dep[0, 0]
            h, sv = ffn_fwd(h, sm, gw)
        saved.append(sv)
        weights.append(gw)
        if s == 0:
            d2d[1] = split_start("gather_d2d", split_wait(ici[1], h), None)
        if s + 1 < n_stages:
            ready[s + 1] = split_wait(d2d[s + 1], h)
    loss_part, dx, dxb, d_final = loss_and_grad(h, final_g[None], loss_target.reshape(S, D))
    loss = lax.psum(loss_part[0, 0], ("x", "y", "c"))

    big_sums = [dict() for _ in range(NL)]
    small_grads = [dict() for _ in range(NL)]
    pending, pending_stage = None, None
    dep = zero
    for s in reversed(range(n_stages)):
        l = s // 2
        if s % 2 == 1:
            dx, dxb, big, small = ffn_bwd(dx, dxb, saved[s], layer_small(l), weights[s], dep)
        else:
            dx, dxb, big, small = mixer_bwd(dx, dxb, saved[s], layer_small(l), weights[s], tabs, dep)
        small_grads[l].update(small)
        if pending is not None:
            big_sums[pending_stage // 2].update(zip(GROUPS[pending_stage % 2], split_wait(pending, dx)))
        parts = [big[n].reshape((N_DEV, -1, big[n].shape[-1])) for n in GROUPS[s % 2]]
        gots = [lax.empty((N_CHIP,) + p.shape[1:], p.dtype) for p in parts]
        got = split_wait(split_start("scatter_d2d", parts, gots), dx)
        pairs = [pair_sum(p, g, c_arr, chip_arr) for p, g in zip(parts, got)]
        pending, pending_stage = split_start("scatter_ici", [t for t, _ in pairs], [o for _, o in pairs]), s
        dep = pending[TOK]

    packed = to_slot(_pack(small_grads, d_final), None, me_arr, dep, F32)
    gathered_small = gather_d2d(gather_ici([packed]))[0][0]
    pw = _pack([{n: small_w[n][l] for n in SMALL} for l in range(NL)], final_g)[None]
    pm = _pack([{n: small_m[n][l] for n in SMALL} for l in range(NL)], m_final_g)[None]
    pv = _pack([{n: small_v[n][l] for n in SMALL} for l in range(NL)], v_final_g)[None]
    like_layers = [{n: small_w[n][l] for n in SMALL} for l in range(NL)]
    small_res = adamw(pw, pm, pv, [gathered_small])
    small_out = [_unpack(t[0], like_layers, final_g) for t in small_res]
    big_sums[pending_stage // 2].update(zip(GROUPS[pending_stage % 2], split_wait(pending, small_res[0])))

    def small_stack(k, n):
        return jnp.stack([small_out[k][0][l][n] for l in range(NL)])

    big_out = {}
    for n in BIG:
        R, C = big_w[n].shape[1], big_w[n].shape[2]
        res = adamw(big_w[n], big_m[n], big_v[n], [big_sums[l][n].reshape(N_CHIP, R, C) for l in range(NL)])
        big_out[n] = [view(n, t) for t in res]

    order = ("norm1_g", "w_in", "gmlp_ln_g", "w_spatial", "b_spatial", "mix_norm_attn_g", "mix_norm_gmlp_g", "w_out",
             "norm2_g", "w_gate", "w_up", "w_down")
    outs = [loss, dx.reshape(x.shape)]
    for k in range(4):
        for n in order:
            outs.append(big_out[n][k] if n in BIG else small_stack(k, n))
        outs.append(small_out[k][1])
    return tuple(outs)
```

```python
import functools

import jax
import jax.numpy as jnp
from jax import lax
from jax.experimental import pallas as pl
from jax.experimental.pallas import tpu as pltpu

F32 = jnp.float32
BF16 = jnp.bfloat16

HEAD_DIM = 128
CHUNK = 128
GROUP = 128
N_SIDE = 64
DILATIONS = (1, 4, 16)
TQ = 128
HALO = 64
EPS = 1e-6
NEG = -1e30
ROPE_THETA = 10000.0
ADAM_LR = 0.001
ADAM_B1 = 0.9
ADAM_B2 = 0.999
ADAM_EPS = 1e-08
ADAM_WD = 0.01
ADAM_STEP = 10
N_DEV = 8
N_CHIP = 4
VMEM_LIMIT_V7X = 56 * 1024 * 1024
ADAMW_BLOCK_ELEMS = 128 * 1024
MESH = pl.DeviceIdType.MESH
ANY = pl.BlockSpec(memory_space=pl.ANY)


def _params(sem=None):
    return pltpu.CompilerParams(dimension_semantics=sem, vmem_limit_bytes=VMEM_LIMIT_V7X)


def _tile(n, want):
    for t in range(min(want, n), 15, -1):
        if n % t == 0 and t % 16 == 0:
            return t
    return n


def _sds(shape, dtype):
    return jax.ShapeDtypeStruct(shape, dtype)


def _dot(a, b, dims):
    return lax.dot_general(a, b, (dims, ((), ())), preferred_element_type=F32)


NN = ((1,), (0,))
NT = ((1,), (1,))
TN = ((0,), (0,))


def _sigmoid(x):
    return 1.0 / (1.0 + jnp.exp(-x))


def _gelu(x):
    return 0.5 * x * (1.0 + lax.erf(x * 0.7071067811865476))


def _gelu_grad(x):
    cdf = 0.5 * (1.0 + lax.erf(x * 0.7071067811865476))
    pdf = 0.3989422804014327 * jnp.exp(-0.5 * x * x)
    return cdf + x * pdf


def rms_fwd(x, g):
    S, D = x.shape
    tr = _tile(S, 512)

    def body(x_ref, g_ref, o_ref):
        xv = x_ref[...]
        r = lax.rsqrt(jnp.mean(xv * xv, axis=-1, keepdims=True) + EPS)
        o_ref[...] = (xv * r * g_ref[...]).astype(o_ref.dtype)

    return pl.pallas_call(
        body, name="rms_fwd", grid=(S // tr,),
        in_specs=[pl.BlockSpec((tr, D), lambda i: (i, 0)), pl.BlockSpec((1, D), lambda i: (0, 0))],
        out_specs=pl.BlockSpec((tr, D), lambda i: (i, 0)),
        out_shape=_sds((S, D), BF16), compiler_params=_params(("parallel",)),
    )(x, g)


def _rms_bwd_math(xv, gv, dy):
    r = lax.rsqrt(jnp.mean(xv * xv, axis=-1, keepdims=True) + EPS)
    dyg = dy * gv
    dx = r * dyg - xv * (r * r * r) * jnp.mean(dyg * xv, axis=-1, keepdims=True)
    dg = jnp.sum(dy * xv * r, axis=0, keepdims=True)
    return dx, dg


def rms_bwd_res(x, g, dy, dres):
    S, D = x.shape
    tr = _tile(S, 512)

    def body(x_ref, g_ref, dy_ref, dres_ref, dx_ref, dxb_ref, dg_ref):
        dx, dg = _rms_bwd_math(x_ref[...], g_ref[...], dy_ref[...])
        tot = dres_ref[...] + dx
        dx_ref[...] = tot
        dxb_ref[...] = tot.astype(dxb_ref.dtype)

        @pl.when(pl.program_id(0) == 0)
        def _():
            dg_ref[...] = dg

        @pl.when(pl.program_id(0) > 0)
        def _():
            dg_ref[...] += dg

    row = pl.BlockSpec((tr, D), lambda i: (i, 0))
    vec = pl.BlockSpec((1, D), lambda i: (0, 0))
    return pl.pallas_call(
        body, name="rms_bwd_res", grid=(S // tr,),
        in_specs=[row, vec, row, row], out_specs=[row, row, vec],
        out_shape=[_sds((S, D), F32), _sds((S, D), BF16), _sds((1, D), F32)],
        compiler_params=_params(("arbitrary",)),
    )(x, g, dy, dres)


def loss_and_grad(x, g, target):
    S, D = x.shape
    tr = _tile(S, 512)

    def body(x_ref, g_ref, t_ref, loss_ref, dx_ref, dxb_ref, dg_ref):
        xv = x_ref[...]
        gv = g_ref[...]
        r = lax.rsqrt(jnp.mean(xv * xv, axis=-1, keepdims=True) + EPS)
        y = xv * r * gv
        diff = y - t_ref[...]
        part = 0.5 * jnp.sum(jnp.mean(diff * diff, axis=-1, keepdims=True), axis=0, keepdims=True)
        dy = diff * (1.0 / D)
        dx, dg = _rms_bwd_math(xv, gv, dy)
        dx_ref[...] = dx
        dxb_ref[...] = dx.astype(dxb_ref.dtype)

        @pl.when(pl.program_id(0) == 0)
        def _():
            dg_ref[...] = dg
            loss_ref[...] = jnp.broadcast_to(part, loss_ref.shape)

        @pl.when(pl.program_id(0) > 0)
        def _():
            dg_ref[...] += dg
            loss_ref[...] += jnp.broadcast_to(part, loss_ref.shape)

    row = pl.BlockSpec((tr, D), lambda i: (i, 0))
    vec = pl.BlockSpec((1, D), lambda i: (0, 0))
    return pl.pallas_call(
        body, name="loss_and_grad", grid=(S // tr,),
        in_specs=[row, vec, row],
        out_specs=[pl.BlockSpec((8, 128), lambda i: (0, 0)), row, row, vec],
        out_shape=[_sds((8, 128), F32), _sds((S, D), F32), _sds((S, D), BF16), _sds((1, D), F32)],
        compiler_params=_params(("arbitrary",)),
    )(x, g, target)


def _mm_body(n_pairs, dims, red_axis, n_red, has_res):
    def body(*refs):
        ins = refs[:2 * n_pairs]
        res_ref = refs[2 * n_pairs] if has_res else None
        o_ref = refs[2 * n_pairs + has_res]
        p = None
        for t in range(n_pairs):
            d = _dot(ins[2 * t][...], ins[2 * t + 1][...], dims)
            p = d if p is None else p + d
        if red_axis is None:
            if has_res:
                p = res_ref[...] + p
            o_ref[...] = p.astype(o_ref.dtype)
            return
        acc_ref = o_ref if o_ref.dtype == F32 else refs[2 * n_pairs + has_res + 1]
        r = pl.program_id(red_axis)

        @pl.when(r == 0)
        def _():
            acc_ref[...] = res_ref[...] + p if has_res else p

        @pl.when(r > 0)
        def _():
            acc_ref[...] += p

        if acc_ref is not o_ref:
            @pl.when(r == n_red - 1)
            def _():
                o_ref[...] = acc_ref[...].astype(o_ref.dtype)

    return body


def mm_in(h, w):
    S, D = h.shape
    NS, _, n = w.shape
    tm = _tile(S, 1024)
    return pl.pallas_call(
        _mm_body(1, NN, None, 1, False), name="mm_in", grid=(S // tm, NS),
        in_specs=[pl.BlockSpec((tm, D), lambda i, k: (i, 0)), pl.BlockSpec((None, D, n), lambda i, k: (k, 0, 0))],
        out_specs=pl.BlockSpec((tm, n), lambda i, k: (i, k)),
        out_shape=_sds((S, NS * n), F32), compiler_params=_params(("parallel", "parallel")),
    )(h, w)


def mm_out(x, mix_a, mix_g, w):
    S, D = x.shape
    DA = mix_a.shape[1]
    tm = _tile(S, 1024)
    tn = _tile(D, 1024)
    act = pl.BlockSpec((tm, DA), lambda i, j: (i, 0))
    return pl.pallas_call(
        _mm_body(2, NN, None, 1, True), name="mm_out", grid=(S // tm, D // tn),
        in_specs=[act, pl.BlockSpec((None, DA, tn), lambda i, j: (0, 0, j)),
                  act, pl.BlockSpec((None, DA, tn), lambda i, j: (1, 0, j)),
                  pl.BlockSpec((tm, tn), lambda i, j: (i, j))],
        out_specs=pl.BlockSpec((tm, tn), lambda i, j: (i, j)),
        out_shape=_sds((S, D), F32), compiler_params=_params(("parallel", "parallel")),
    )(mix_a, w, mix_g, w, x)


def _lane_tile(n, want):
    for t in range(min(want, n) // 128 * 128, 127, -128):
        if n % t == 0:
            return t
    return n


def ff_fwd(h, wg, wu):
    S, D = h.shape
    F = wg.shape[0]
    tm = _tile(S, 1024)
    tn = _lane_tile(F, 512)

    def body(h_ref, wg_ref, wu_ref, gate_ref, up_ref, ff_ref):
        hv = h_ref[...]
        g = _dot(hv, wg_ref[...], NT)
        u = _dot(hv, wu_ref[...], NT)
        gate_ref[...] = g
        up_ref[...] = u
        ff_ref[...] = (g * _sigmoid(g) * u).astype(ff_ref.dtype)

    wspec = pl.BlockSpec((tn, D), lambda i, j: (j, 0))
    ospec = pl.BlockSpec((tm, tn), lambda i, j: (i, j))
    return pl.pallas_call(
        body, name="ff_fwd", grid=(S // tm, F // tn),
        in_specs=[pl.BlockSpec((tm, D), lambda i, j: (i, 0)), wspec, wspec],
        out_specs=[ospec, ospec, ospec],
        out_shape=[_sds((S, F), F32), _sds((S, F), F32), _sds((S, F), BF16)],
        compiler_params=_params(("parallel", "parallel")),
    )(h, wg, wu)


def mm_down(x, ff, wd):
    S, D = x.shape
    F = ff.shape[1]
    tm = _tile(S, 512)
    tn = _lane_tile(D, 512)
    blk = pl.BlockSpec((tm, tn), lambda i, j: (i, j))
    return pl.pallas_call(
        _mm_body(1, NN, None, 1, True), name="mm_down", grid=(S // tm, D // tn),
        in_specs=[pl.BlockSpec((tm, F), lambda i, j: (i, 0)), pl.BlockSpec((F, tn), lambda i, j: (0, j)), blk],
        out_specs=blk, out_shape=_sds((S, D), F32), compiler_params=_params(("parallel", "parallel")),
    )(ff, wd, x)


def ff_bwd_act(dxb, wd, gate, up, dep):
    S, D = dxb.shape
    F = wd.shape[0]
    tm = _tile(S, 1024)
    tn = _lane_tile(F, 512)

    def body(dx_ref, wd_ref, gate_ref, up_ref, dep_ref, dgate_ref, dup_ref):
        dff = _dot(dx_ref[...], wd_ref[...], NT)
        g = gate_ref[...]
        sg = _sigmoid(g)
        dup_ref[...] = (dff * (g * sg)).astype(dup_ref.dtype)
        dgate_ref[...] = (dff * up_ref[...] * (sg * (1.0 + g * (1.0 - sg)))).astype(dgate_ref.dtype)

    aspec = pl.BlockSpec((tm, tn), lambda i, j: (i, j))
    return pl.pallas_call(
        body, name="ff_bwd_act", grid=(S // tm, F // tn),
        in_specs=[pl.BlockSpec((tm, D), lambda i, j: (i, 0)), pl.BlockSpec((tn, D), lambda i, j: (j, 0)), aspec, aspec,
                  pl.BlockSpec(TOKEN, lambda i, j: (0, 0))],
        out_specs=[aspec, aspec],
        out_shape=[_sds((S, F), BF16), _sds((S, F), BF16)],
        compiler_params=_params(("parallel", "parallel")),
    )(dxb, wd, gate, up, dep)


def dw_down(ff, dxb):
    S, F = ff.shape
    D = dxb.shape[1]
    tf = _lane_tile(F, 256)
    return pl.pallas_call(
        _mm_body(1, TN, None, 1, False), name="dw_down", grid=(F // tf,),
        in_specs=[pl.BlockSpec((S, tf), lambda k: (0, k)), pl.BlockSpec((S, D), lambda k: (0, 0))],
        out_specs=pl.BlockSpec((tf, D), lambda k: (k, 0)),
        out_shape=_sds((F, D), BF16), compiler_params=_params(("parallel",)),
    )(ff, dxb)


def dw_gate_up(h, dgate, dup):
    S, D = h.shape
    F = dgate.shape[1]
    f = _lane_tile(F, 256)

    def body(h_ref, dg_ref, du_ref, og_ref, ou_ref):
        hv = h_ref[...]
        og_ref[...] = _dot(dg_ref[...], hv, TN).astype(og_ref.dtype)
        ou_ref[...] = _dot(du_ref[...], hv, TN).astype(ou_ref.dtype)

    aspec = pl.BlockSpec((S, f), lambda k: (0, k))
    ospec = pl.BlockSpec((f, D), lambda k: (k, 0))
    return pl.pallas_call(
        body, name="dw_gate_up", grid=(F // f,),
        in_specs=[pl.BlockSpec((S, D), lambda k: (0, 0)), aspec, aspec],
        out_specs=[ospec, ospec],
        out_shape=[_sds((F, D), BF16), _sds((F, D), BF16)],
        compiler_params=_params(("parallel",)),
    )(h, dgate, dup)


def dh_ff(dgate, dup, wg, wu, dep):
    S, F = dgate.shape
    D = wg.shape[1]
    tm = _tile(S, 512)
    tn = _lane_tile(D, 256)

    def body(dg_ref, wg_ref, du_ref, wu_ref, dep_ref, o_ref):
        o_ref[...] = _dot(dg_ref[...], wg_ref[...], NN) + _dot(du_ref[...], wu_ref[...], NN)

    aspec = pl.BlockSpec((tm, F), lambda i, j: (i, 0))
    wspec = pl.BlockSpec((F, tn), lambda i, j: (0, j))
    return pl.pallas_call(
        body, name="dh_ff", grid=(S // tm, D // tn),
        in_specs=[aspec, wspec, aspec, wspec, pl.BlockSpec(TOKEN, lambda i, j: (0, 0))],
        out_specs=pl.BlockSpec((tm, tn), lambda i, j: (i, j)),
        out_shape=_sds((S, D), F32), compiler_params=_params(("parallel", "parallel")),
    )(dgate, wg, dup, wu, dep)


def dmix_mm(dxb, w, dep):
    S, D = dxb.shape
    tm = _tile(S, 1024)
    tn = _tile(D, 1024)

    def body(dx_ref, w_ref, dep_ref, o_ref):
        o_ref[...] = _dot(dx_ref[...], w_ref[...], NT)

    return pl.pallas_call(
        body, name="dmix_mm", grid=(S // tm, D // tn),
        in_specs=[pl.BlockSpec((tm, D), lambda i, j: (i, 0)), pl.BlockSpec((tn, D), lambda i, j: (j, 0)),
                  pl.BlockSpec(TOKEN, lambda i, j: (0, 0))],
        out_specs=pl.BlockSpec((tm, tn), lambda i, j: (i, j)),
        out_shape=_sds((S, D), F32), compiler_params=_params(("parallel", "parallel")),
    )(dxb, w, dep)


def dw_out(mix_a, mix_g, dxb):
    S, DA = mix_a.shape
    D = dxb.shape[1]
    tn = _lane_tile(D, 512)

    def half(name, m):
        return pl.pallas_call(
            _mm_body(1, TN, None, 1, False), name=name, grid=(D // tn,),
            in_specs=[pl.BlockSpec((S, DA), lambda j: (0, 0)), pl.BlockSpec((S, tn), lambda j: (0, j))],
            out_specs=pl.BlockSpec((DA, tn), lambda j: (0, j)),
            out_shape=_sds((DA, D), BF16), compiler_params=_params(("parallel",)),
        )(m, dxb)

    return half("dw_out_a", mix_a), half("dw_out_g", mix_g)


def dw_in(h, dproj, NS):
    S, D = h.shape
    n = dproj.shape[1] // NS
    tm = _lane_tile(D, 1024)
    return pl.pallas_call(
        _mm_body(1, TN, None, 1, False), name="dw_in", grid=(NS, D // tm),
        in_specs=[pl.BlockSpec((S, tm), lambda k, j: (0, j)), pl.BlockSpec((S, n), lambda k, j: (0, k))],
        out_specs=pl.BlockSpec((None, tm, n), lambda k, j: (k, j, 0)),
        out_shape=_sds((NS, D, n), BF16), compiler_params=_params(("parallel", "parallel")),
    )(h, dproj)


def dh_in(dproj, w, dep):
    S = dproj.shape[0]
    NS, D, n = w.shape
    tm = _tile(S, 512)
    tn = _tile(D, 512)

    def body(dp_ref, w_ref, dep_ref, o_ref):
        acc = None
        for k in range(NS):
            p = _dot(dp_ref[:, k * n:(k + 1) * n], w_ref[k], NT)
            acc = p if acc is None else acc + p
        o_ref[...] = acc

    return pl.pallas_call(
        body, name="dh_in", grid=(S // tm, D // tn),
        in_specs=[pl.BlockSpec((tm, NS * n), lambda i, j: (i, 0)), pl.BlockSpec((NS, tn, n), lambda i, j: (0, j, 0)),
                  pl.BlockSpec(TOKEN, lambda i, j: (0, 0))],
        out_specs=pl.BlockSpec((tm, tn), lambda i, j: (i, j)),
        out_shape=_sds((S, D), F32), compiler_params=_params(("parallel", "parallel")),
    )(dproj, w, dep)


def rope_tables(S):
    pos = jnp.arange(S, dtype=F32)
    inv = ROPE_THETA ** (-jnp.arange(0, HEAD_DIM, 2, dtype=F32) / HEAD_DIM)
    ang = pos[:, None] * inv[None, :]
    cos, sin = jnp.cos(ang), jnp.sin(ang)
    return jnp.concatenate([cos, cos], axis=-1), jnp.concatenate([-sin, sin], axis=-1)


def _rot_half(t):
    return pltpu.roll(t, HEAD_DIM // 2, 1)


def rope_qkv(proj, cos2, sin2, DA):
    S = proj.shape[0]
    tr = _tile(S, 512)
    nh = DA // HEAD_DIM

    def body(q_ref, k_ref, v_ref, cos_ref, sin_ref, o_ref):
        c = cos_ref[...]
        s = sin_ref[...]
        for h in range(nh):
            sl = slice(h * HEAD_DIM, (h + 1) * HEAD_DIM)
            for j, ref in enumerate((q_ref, k_ref)):
                t = ref[:, sl]
                o_ref[:, j * DA + h * HEAD_DIM:j * DA + (h + 1) * HEAD_DIM] = (t * c + _rot_half(t) * s).astype(o_ref.dtype)
        o_ref[:, 2 * DA:3 * DA] = v_ref[...].astype(o_ref.dtype)

    tab = pl.BlockSpec((tr, HEAD_DIM), lambda i: (i, 0))
    return pl.pallas_call(
        body, name="rope_qkv", grid=(S // tr,),
        in_specs=[pl.BlockSpec((tr, DA), lambda i: (i, 0)), pl.BlockSpec((tr, DA), lambda i: (i, 1)),
                  pl.BlockSpec((tr, DA), lambda i: (i, 2)), tab, tab],
        out_specs=pl.BlockSpec((tr, 3 * DA), lambda i: (i, 0)),
        out_shape=_sds((S, 3 * DA), BF16), compiler_params=_params(("parallel",)),
    )(proj, proj, proj, cos2, sin2)


def assemble_dproj(dqs, dks, dvs, duv, cos2, sin2):
    S, DA = dqs[0].shape
    tr = _tile(S, 256)
    nh = DA // HEAD_DIM

    def body(*refs):
        dq_refs, dk_refs, dv_refs = refs[0:3], refs[3:6], refs[6:9]
        duv_ref, cos_ref, sin_ref, o_ref = refs[9:13]
        c = cos_ref[...]
        s = sin_ref[...]
        for j, trio in enumerate((dq_refs, dk_refs)):
            for h in range(nh):
                sl = slice(h * HEAD_DIM, (h + 1) * HEAD_DIM)
                t = trio[0][:, sl] + trio[1][:, sl] + trio[2][:, sl]
                o_ref[:, j * DA + h * HEAD_DIM:j * DA + (h + 1) * HEAD_DIM] = (t * c - _rot_half(t) * s).astype(o_ref.dtype)
        o_ref[:, 2 * DA:3 * DA] = (dv_refs[0][...] + dv_refs[1][...] + dv_refs[2][...]).astype(o_ref.dtype)
        o_ref[:, 3 * DA:5 * DA] = duv_ref[...]

    blk = pl.BlockSpec((tr, DA), lambda i: (i, 0))
    tab = pl.BlockSpec((tr, HEAD_DIM), lambda i: (i, 0))
    return pl.pallas_call(
        body, name="assemble_dproj", grid=(S // tr,),
        in_specs=[blk] * 9 + [pl.BlockSpec((tr, 2 * DA), lambda i: (i, 0)), tab, tab],
        out_specs=pl.BlockSpec((tr, 5 * DA), lambda i: (i, 0)),
        out_shape=_sds((S, 5 * DA), BF16), compiler_params=_params(("parallel",)),
    )(*dqs, *dks, *dvs, duv, cos2, sin2)


STAT_W = 128


def _lanes_per_head(DA):
    return STAT_W // (DA // HEAD_DIM)


def _halo_specs(L, width_blocks, col):
    per = TQ // HALO
    last = L // HALO - 1
    W = width_blocks
    return [
        pl.BlockSpec((HALO, W), lambda r, i: (jnp.maximum(i * per - 1, 0), col(r))),
        pl.BlockSpec((TQ, W), lambda r, i: (i, col(r))),
        pl.BlockSpec((HALO, W), lambda r, i: (jnp.minimum((i + 1) * per, last), col(r))),
    ]


def _cat3(refs, sl):
    return jnp.concatenate([refs[0][:, sl], refs[1][:, sl], refs[2][:, sl]], axis=0)


def attn_fwd(qkv, d, DA):
    S = qkv.shape[0]
    L = S // d
    nh = DA // HEAD_DIM
    lph = _lanes_per_head(DA)
    scale = HEAD_DIM ** -0.5
    TK = TQ + 2 * HALO

    def body(q_ref, kp_ref, kc_ref, kn_ref, vp_ref, vc_ref, vn_ref, o_ref, lse_ref):
        i = pl.program_id(1)
        row = lax.broadcasted_iota(jnp.int32, (TQ, TK), 0)
        col = lax.broadcasted_iota(jnp.int32, (TQ, TK), 1)
        kpos = i * TQ - HALO + col
        mask = (jnp.abs(col - HALO - row) <= N_SIDE) & (kpos >= 0) & (kpos < L)
        for h in range(nh):
            sl = slice(h * HEAD_DIM, (h + 1) * HEAD_DIM)
            k = _cat3((kp_ref, kc_ref, kn_ref), sl)
            v = _cat3((vp_ref, vc_ref, vn_ref), sl)
            s = _dot(q_ref[:, sl], k, NT) * scale
            s = jnp.where(mask, s, NEG)
            m = jnp.max(s, axis=-1, keepdims=True)
            p = jnp.exp(s - m)
            l = jnp.sum(p, axis=-1, keepdims=True)
            o = _dot(p.astype(v.dtype), v, NN) / l
            o_ref[:, sl] = o
            lse_ref[:, h * lph:(h + 1) * lph] = jnp.broadcast_to(m + jnp.log(l), (TQ, lph))

    out_spec = pl.BlockSpec((TQ, DA), lambda r, i: (i, r))
    o, lse = pl.pallas_call(
        body, name="attn_fwd_d%d" % d, grid=(d, L // TQ),
        in_specs=[pl.BlockSpec((TQ, DA), lambda r, i: (i, 3 * r))]
        + _halo_specs(L, DA, lambda r: 3 * r + 1) + _halo_specs(L, DA, lambda r: 3 * r + 2),
        out_specs=[out_spec, pl.BlockSpec((TQ, STAT_W), lambda r, i: (i, r))],
        out_shape=[_sds((L, d * DA), F32), _sds((L, d * STAT_W), F32)],
        compiler_params=_params(("parallel", "parallel")),
    )(*([qkv.reshape(L, d * 3 * DA)] * 7))
    return o.reshape(S, DA), lse.reshape(S, STAT_W)


def attn_merge(os_, lses, g):
    S, DA = os_[0].shape
    tr = _tile(S, 256)
    nh = DA // HEAD_DIM
    lph = _lanes_per_head(DA)

    def body(o0, o1, o2, l0, l1, l2, g_ref, a_ref, lse_ref, mix_ref):
        a0, a1, a2 = l0[...], l1[...], l2[...]
        m = jnp.maximum(jnp.maximum(a0, a1), a2)
        e0, e1, e2 = jnp.exp(a0 - m), jnp.exp(a1 - m), jnp.exp(a2 - m)
        den = e0 + e1 + e2
        w0, w1, w2 = e0 / den, e1 / den, e2 / den
        lse_ref[...] = m + jnp.log(den)
        heads = []
        for h in range(nh):
            sl = slice(h * HEAD_DIM, (h + 1) * HEAD_DIM)
            c = slice(h * lph, h * lph + 1)
            heads.append(w0[:, c] * o0[:, sl] + w1[:, c] * o1[:, sl] + w2[:, c] * o2[:, sl])
        a = jnp.concatenate(heads, axis=1)
        a_ref[...] = a
        r = lax.rsqrt(jnp.mean(a * a, axis=-1, keepdims=True) + EPS)
        mix_ref[...] = (a * r * g_ref[...]).astype(mix_ref.dtype)

    blk = pl.BlockSpec((tr, DA), lambda i: (i, 0))
    stat = pl.BlockSpec((tr, STAT_W), lambda i: (i, 0))
    return pl.pallas_call(
        body, name="attn_merge", grid=(S // tr,),
        in_specs=[blk] * 3 + [stat] * 3 + [pl.BlockSpec((1, DA), lambda i: (0, 0))],
        out_specs=[blk, stat, blk],
        out_shape=[_sds((S, DA), F32), _sds((S, STAT_W), F32), _sds((S, DA), BF16)],
        compiler_params=_params(("parallel",)),
    )(*os_, *lses, g)


def attn_out_bwd(a, g, dmix):
    S, DA = a.shape
    tr = _tile(S, 256)
    nh = DA // HEAD_DIM
    lph = _lanes_per_head(DA)

    def body(a_ref, g_ref, dy_ref, do_ref, dl_ref, dg_ref):
        av = a_ref[...]
        dx, dg = _rms_bwd_math(av, g_ref[...], dy_ref[...])
        do_ref[...] = dx.astype(do_ref.dtype)
        prod = dx * av
        for h in range(nh):
            sl = slice(h * HEAD_DIM, (h + 1) * HEAD_DIM)
            dl_ref[:, h * lph:(h + 1) * lph] = jnp.broadcast_to(jnp.sum(prod[:, sl], axis=-1, keepdims=True), (tr, lph))

        @pl.when(pl.program_id(0) == 0)
        def _():
            dg_ref[...] = dg

        @pl.when(pl.program_id(0) > 0)
        def _():
            dg_ref[...] += dg

    blk = pl.BlockSpec((tr, DA), lambda i: (i, 0))
    vec = pl.BlockSpec((1, DA), lambda i: (0, 0))
    return pl.pallas_call(
        body, name="attn_out_bwd", grid=(S // tr,),
        in_specs=[blk, vec, blk], out_specs=[blk, pl.BlockSpec((tr, STAT_W), lambda i: (i, 0)), vec],
        out_shape=[_sds((S, DA), BF16), _sds((S, STAT_W), F32), _sds((1, DA), F32)],
        compiler_params=_params(("arbitrary",)),
    )(a, g, dmix)


def attn_bwd_dq(qkv, dob, lse, dl, d, DA):
    S = qkv.shape[0]
    L = S // d
    nh = DA // HEAD_DIM
    lph = _lanes_per_head(DA)
    scale = HEAD_DIM ** -0.5
    TK = TQ + 2 * HALO

    def body(q_ref, kp_ref, kc_ref, kn_ref, vp_ref, vc_ref, vn_ref, do_ref, lse_ref, dl_ref, dq_ref):
        i = pl.program_id(1)
        row = lax.broadcasted_iota(jnp.int32, (TQ, TK), 0)
        col = lax.broadcasted_iota(jnp.int32, (TQ, TK), 1)
        kpos = i * TQ - HALO + col
        mask = (jnp.abs(col - HALO - row) <= N_SIDE) & (kpos >= 0) & (kpos < L)
        for h in range(nh):
            sl = slice(h * HEAD_DIM, (h + 1) * HEAD_DIM)
            k = _cat3((kp_ref, kc_ref, kn_ref), sl)
            v = _cat3((vp_ref, vc_ref, vn_ref), sl)
            s = _dot(q_ref[:, sl], k, NT) * scale
            s = jnp.where(mask, s, NEG)
            p = jnp.exp(s - lse_ref[:, h * lph:h * lph + 1])
            dp = _dot(do_ref[:, sl], v, NT)
            ds = p * (dp - dl_ref[:, h * lph:h * lph + 1])
            dq_ref[:, sl] = _dot(ds.astype(k.dtype), k, NN) * scale

    blk = pl.BlockSpec((TQ, DA), lambda r, i: (i, r))
    stat = pl.BlockSpec((TQ, STAT_W), lambda r, i: (i, r))
    dq = pl.pallas_call(
        body, name="attn_bwd_dq_d%d" % d, grid=(d, L // TQ),
        in_specs=[pl.BlockSpec((TQ, DA), lambda r, i: (i, 3 * r))]
        + _halo_specs(L, DA, lambda r: 3 * r + 1) + _halo_specs(L, DA, lambda r: 3 * r + 2) + [blk, stat, stat],
        out_specs=blk, out_shape=_sds((L, d * DA), F32),
        compiler_params=_params(("parallel", "parallel")),
    )(*([qkv.reshape(L, d * 3 * DA)] * 7), dob.reshape(L, d * DA), lse.reshape(L, d * STAT_W), dl.reshape(L, d * STAT_W))
    return dq.reshape(S, DA)


def attn_bwd_dkv(qkv, dob, lse, dl, d, DA):
    S = qkv.shape[0]
    L = S // d
    nh = DA // HEAD_DIM
    lph = _lanes_per_head(DA)
    scale = HEAD_DIM ** -0.5
    TR = TQ + 2 * HALO

    def body(k_ref, v_ref, qp, qc, qn, dop, doc, don, lp, lc, ln, dp_, dc_, dn_, dk_ref, dv_ref):
        j = pl.program_id(1)
        row = lax.broadcasted_iota(jnp.int32, (TR, TQ), 0)
        col = lax.broadcasted_iota(jnp.int32, (TR, TQ), 1)
        qpos = j * TQ - HALO + row
        mask = (jnp.abs(col - (row - HALO)) <= N_SIDE) & (qpos >= 0) & (qpos < L)
        for h in range(nh):
            sl = slice(h * HEAD_DIM, (h + 1) * HEAD_DIM)
            q = _cat3((qp, qc, qn), sl)
            do = _cat3((dop, doc, don), sl)
            stat = slice(h * lph, h * lph + 1)
            lse_q = _cat3((lp, lc, ln), stat)
            dl_q = _cat3((dp_, dc_, dn_), stat)
            k = k_ref[:, sl]
            v = v_ref[:, sl]
            s = _dot(q, k, NT) * scale
            s = jnp.where(mask, s, NEG)
            p = jnp.exp(s - lse_q)
            dv_ref[:, sl] = _dot(p.astype(do.dtype), do, TN)
            dp = _dot(do, v, NT)
            ds = p * (dp - dl_q)
            dk_ref[:, sl] = _dot(ds.astype(q.dtype), q, TN) * scale

    blk = pl.BlockSpec((TQ, DA), lambda r, i: (i, r))
    ident = lambda r: r
    dk, dv = pl.pallas_call(
        body, name="attn_bwd_dkv_d%d" % d, grid=(d, L // TQ),
        in_specs=[pl.BlockSpec((TQ, DA), lambda r, i: (i, 3 * r + 1)), pl.BlockSpec((TQ, DA), lambda r, i: (i, 3 * r + 2))]
        + _halo_specs(L, DA, lambda r: 3 * r) + _halo_specs(L, DA, ident) + _halo_specs(L, STAT_W, ident)
        + _halo_specs(L, STAT_W, ident),
        out_specs=[blk, blk], out_shape=[_sds((L, d * DA), F32), _sds((L, d * DA), F32)],
        compiler_params=_params(("parallel", "parallel")),
    )(*([qkv.reshape(L, d * 3 * DA)] * 5), *([dob.reshape(L, d * DA)] * 3), *([lse.reshape(L, d * STAT_W)] * 3),
      *([dl.reshape(L, d * STAT_W)] * 3))
    return dk.reshape(S, DA), dv.reshape(S, DA)


def _gmlp_fwd_math(u_raw, v_raw, ln_g, ws_ref, bcol_ref, n_chunks, ng):
    ug = _gelu(u_raw)
    vg = _gelu(v_raw)
    mu = jnp.mean(vg, axis=-1, keepdims=True)
    xc = vg - mu
    rstd = lax.rsqrt(jnp.mean(xc * xc, axis=-1, keepdims=True) + EPS)
    xhat = xc * rstd
    vn = xhat * ln_g
    rows = []
    for n in range(n_chunks):
        cols = []
        for g in range(ng):
            blk = vn[n * CHUNK:(n + 1) * CHUNK, g * GROUP:(g + 1) * GROUP]
            cols.append(_dot(ws_ref[g].astype(BF16), blk.astype(BF16), NN) + bcol_ref[g])
        rows.append(jnp.concatenate(cols, axis=1))
    mixed = jnp.concatenate(rows, axis=0)
    return ug, xhat, rstd, vn, mixed


def gmlp_fwd(proj, ln_g, w_s, bcol, g_out, DA):
    S = proj.shape[0]
    ng = DA // GROUP
    tr = _tile(S, 2 * CHUNK)

    def body(u_ref, v_ref, ln_ref, ws_ref, bcol_ref, g_ref, G_ref, mix_ref):
        ug, _, _, _, mixed = _gmlp_fwd_math(u_ref[...], v_ref[...], ln_ref[...], ws_ref, bcol_ref, tr // CHUNK, ng)
        G = ug * mixed
        G_ref[...] = G
        r = lax.rsqrt(jnp.mean(G * G, axis=-1, keepdims=True) + EPS)
        mix_ref[...] = (G * r * g_ref[...]).astype(mix_ref.dtype)

    vec = pl.BlockSpec((1, DA), lambda i: (0, 0))
    par = pl.BlockSpec((ng, CHUNK, CHUNK), lambda i: (0, 0, 0))
    blk = pl.BlockSpec((tr, DA), lambda i: (i, 0))
    return pl.pallas_call(
        body, name="gmlp_fwd", grid=(S // tr,),
        in_specs=[pl.BlockSpec((tr, DA), lambda i: (i, 3)), pl.BlockSpec((tr, DA), lambda i: (i, 4)), vec, par, par, vec],
        out_specs=[blk, blk], out_shape=[_sds((S, DA), F32), _sds((S, DA), BF16)],
        compiler_params=_params(("parallel",)),
    )(proj, proj, ln_g, w_s, bcol, g_out)


def gmlp_bwd(proj, ln_g, w_s, w_st, bcol, g_out, dmix, DA):
    S = proj.shape[0]
    ng = DA // GROUP
    tr = _tile(S, 2 * CHUNK)
    nc = tr // CHUNK

    def body(u_ref, v_ref, ln_ref, ws_ref, wst_ref, bcol_ref, g_ref, dy_ref, duv_ref, dg_ref, dln_ref, dws_ref, db_ref):
        u_raw = u_ref[...]
        v_raw = v_ref[...]
        ln_g_v = ln_ref[...]
        ug, xhat, rstd, vn, mixed = _gmlp_fwd_math(u_raw, v_raw, ln_g_v, ws_ref, bcol_ref, nc, ng)
        G = ug * mixed
        dG, dg = _rms_bwd_math(G, g_ref[...], dy_ref[...])
        du_g = dG * mixed
        dmixed = dG * ug
        dws, dbs, rows = [], [], []
        for g in range(ng):
            dws.append(None)
            dbs.append(None)
        for n in range(nc):
            cols = []
            for g in range(ng):
                dm = dmixed[n * CHUNK:(n + 1) * CHUNK, g * GROUP:(g + 1) * GROUP]
                vb = vn[n * CHUNK:(n + 1) * CHUNK, g * GROUP:(g + 1) * GROUP]
                dmb = dm.astype(BF16)
                w = _dot(dmb, vb.astype(BF16), NT)
                b = jnp.broadcast_to(jnp.sum(dm, axis=-1, keepdims=True), (CHUNK, GROUP))
                dws[g] = w if dws[g] is None else dws[g] + w
                dbs[g] = b if dbs[g] is None else dbs[g] + b
                cols.append(_dot(wst_ref[g].astype(BF16), dmb, NN))
            rows.append(jnp.concatenate(cols, axis=1))
        dvn = jnp.concatenate(rows, axis=0)
        dln = jnp.sum(dvn * xhat, axis=0, keepdims=True)
        dxh = dvn * ln_g_v
        dvg = rstd * (dxh - jnp.mean(dxh, axis=-1, keepdims=True) - xhat * jnp.mean(dxh * xhat, axis=-1, keepdims=True))
        duv_ref[:, 0:DA] = (du_g * _gelu_grad(u_raw)).astype(duv_ref.dtype)
        duv_ref[:, DA:2 * DA] = (dvg * _gelu_grad(v_raw)).astype(duv_ref.dtype)

        first = pl.program_id(0) == 0

        @pl.when(first)
        def _():
            dg_ref[...] = dg
            dln_ref[...] = dln
            for g in range(ng):
                dws_ref[g] = dws[g]
                db_ref[g] = dbs[g]

        @pl.when(jnp.logical_not(first))
        def _():
            dg_ref[...] += dg
            dln_ref[...] += dln
            for g in range(ng):
                dws_ref[g] += dws[g]
                db_ref[g] += dbs[g]

    vec = pl.BlockSpec((1, DA), lambda i: (0, 0))
    par = pl.BlockSpec((ng, CHUNK, CHUNK), lambda i: (0, 0, 0))
    return pl.pallas_call(
        body, name="gmlp_bwd", grid=(S // tr,),
        in_specs=[pl.BlockSpec((tr, DA), lambda i: (i, 3)), pl.BlockSpec((tr, DA), lambda i: (i, 4)), vec, par, par, par, vec,
                  pl.BlockSpec((tr, DA), lambda i: (i, 1))],
        out_specs=[pl.BlockSpec((tr, 2 * DA), lambda i: (i, 0)), vec, vec, par, par],
        out_shape=[_sds((S, 2 * DA), BF16), _sds((1, DA), F32), _sds((1, DA), F32),
                   _sds((ng, CHUNK, CHUNK), F32), _sds((ng, CHUNK, CHUNK), F32)],
        compiler_params=_params(("arbitrary",)),
    )(proj, proj, ln_g, w_s, w_st, bcol, g_out, dmix)


def mixer_fwd(x, sm, gw, tabs):
    D = x.shape[1]
    DA = D // 2
    cos2, sin2 = tabs
    h1 = rms_fwd(x, sm["norm1_g"])
    proj = mm_in(h1, gw["w_in"])
    qkv = rope_qkv(proj, cos2, sin2, DA)
    os_, lses = [], []
    for d in DILATIONS:
        o, l = attn_fwd(qkv, d, DA)
        os_.append(o)
        lses.append(l)
    a, lse, mix_a = attn_merge(os_, lses, sm["mix_norm_attn_g"])
    _, mix_g = gmlp_fwd(proj, sm["gmlp_ln_g"], sm["w_spatial"], sm["bcol"], sm["mix_norm_gmlp_g"], DA)
    x2 = mm_out(x, mix_a, mix_g, gw["w_out"])
    return x2, dict(x=x, h1=h1, proj=proj, qkv=qkv, a=a, lse=lse, mix_a=mix_a, mix_g=mix_g)


def ffn_fwd(x2, sm, gw):
    h2 = rms_fwd(x2, sm["norm2_g"])
    gate, up, ff = ff_fwd(h2, gw["w_gate"], gw["w_up"])
    x3 = mm_down(x2, ff, gw["w_down"])
    return x3, dict(x2=x2, h2=h2, gate=gate, up=up, ff=ff)


def ffn_bwd(dx, dxb, sv, sm, gw, dep, on_grads):
    dgate, dup = ff_bwd_act(dxb, gw["w_down"], sv["gate"], sv["up"], dep)
    g_down = dw_down(sv["ff"], dxb)
    g_gate, g_up = dw_gate_up(sv["h2"], dgate, dup)
    token = on_grads(dict(w_gate=g_gate, w_up=g_up, w_down=g_down))
    dh2 = dh_ff(dgate, dup, gw["w_gate"], gw["w_up"], token)
    dx2, dx2b, d_norm2 = rms_bwd_res(sv["x2"], sm["norm2_g"], dh2, dx)
    return dx2, dx2b, dict(norm2_g=d_norm2)


def mixer_bwd(dx2, dx2b, sv, sm, gw, tabs, dep, on_grads):
    D = dx2.shape[1]
    DA = D // 2
    cos2, sin2 = tabs
    NS = gw["w_in"].shape[0]
    dmix = dmix_mm(dx2b, gw["w_out"].reshape(D, D), dep)
    g_out_a, g_out_g = dw_out(sv["mix_a"], sv["mix_g"], dx2b)
    dob, dl, d_mix_a = attn_out_bwd(sv["a"], sm["mix_norm_attn_g"], dmix)
    duv, d_mix_g, d_ln, d_ws, d_bs = gmlp_bwd(sv["proj"], sm["gmlp_ln_g"], sm["w_spatial"], sm["w_spatial_t"], sm["bcol"],
                                              sm["mix_norm_gmlp_g"], dmix, DA)
    dqs, dks, dvs = [], [], []
    for d in DILATIONS:
        dqs.append(attn_bwd_dq(sv["qkv"], dob, sv["lse"], dl, d, DA))
        dk, dv = attn_bwd_dkv(sv["qkv"], dob, sv["lse"], dl, d, DA)
        dks.append(dk)
        dvs.append(dv)
    dproj = assemble_dproj(dqs, dks, dvs, duv, cos2, sin2)
    g_in = dw_in(sv["h1"], dproj, NS)
    token = on_grads(dict(w_in=g_in, w_out=jnp.concatenate([g_out_a, g_out_g], axis=0)))
    dh1 = dh_in(dproj, gw["w_in"], token)
    dx0, dx0b, d_norm1 = rms_bwd_res(sv["x"], sm["norm1_g"], dh1, dx2)
    small = dict(norm1_g=d_norm1, gmlp_ln_g=d_ln, w_spatial=d_ws, b_spatial=d_bs[:, :, 0], mix_norm_attn_g=d_mix_a,
                 mix_norm_gmlp_g=d_mix_g)
    return dx0, dx0b, small


def _place():
    x, y, c = lax.axis_index("x"), lax.axis_index("y"), lax.axis_index("c")
    return x, y, c, [(1 - x, y), (x, 1 - y), (1 - x, 1 - y)]


def _remote(src, dst, send_sems, recv_sems, k, to):
    return pltpu.make_async_remote_copy(src_ref=src, dst_ref=dst, send_sem=send_sems.at[k], recv_sem=recv_sems.at[k],
                                        device_id=to, device_id_type=MESH)


def to_slot(src, layer, me_arr, dep, dtype):
    R, C = src.shape[-2:]
    tr = _tile(R, max(16, 4 * ADAMW_BLOCK_ELEMS // C))

    def body(me_ref, src_ref, dep_ref, o_ref):
        o_ref[...] = src_ref[...].astype(o_ref.dtype)

    if layer is None:
        src_spec = pl.BlockSpec((tr, C), lambda i, me_ref: (i, 0))
    else:
        src_spec = pl.BlockSpec((None, tr, C), lambda i, me_ref: (layer, i, 0))
    return pl.pallas_call(
        body, name="to_slot",
        grid_spec=pltpu.PrefetchScalarGridSpec(
            num_scalar_prefetch=1, grid=(R // tr,),
            in_specs=[src_spec, pl.BlockSpec(TOKEN, lambda i, me_ref: (0, 0))],
            out_specs=pl.BlockSpec((None, tr, C), lambda i, me_ref: (me_ref[0], i, 0))),
        out_shape=_sds((N_DEV, R, C), dtype), compiler_params=_params(("parallel",)),
    )(me_arr, src, dep)


def gather_ici(bufs):
    n = len(bufs)

    def body(*refs):
        outs = refs[n:2 * n]
        send_sems, recv_sems = refs[2 * n:]
        x, y, c, chips = _place()
        mine = [o.at[4 * x + 2 * y + c] for o in outs]
        sent = []
        for a in range(n):
            for j, (px, py) in enumerate(chips):
                cp = _remote(mine[a], mine[a], send_sems, recv_sems, 3 * a + j, (px, py, c))
                cp.start()
                sent.append(cp)
        for a in range(n):
            for j, (px, py) in enumerate(chips):
                slot = outs[a].at[4 * px + 2 * py + c]
                _remote(slot, slot, send_sems, recv_sems, 3 * a + j, (px, py, c)).wait_recv()
        for cp in sent:
            cp.wait_send()

    return pl.pallas_call(
        body, name="gather_ici", in_specs=[ANY] * n, out_specs=[ANY] * n,
        out_shape=[_sds(b.shape, b.dtype) for b in bufs], input_output_aliases={a: a for a in range(n)},
        scratch_shapes=[pltpu.SemaphoreType.DMA((3 * n,)), pltpu.SemaphoreType.DMA((3 * n,))],
    )(*bufs)


HBM_SPEC = pl.BlockSpec(memory_space=pltpu.HBM)
SEM_SPEC = pl.BlockSpec(memory_space=pltpu.SEMAPHORE)
DATAFLOW = pltpu.SideEffectType.DATAFLOW_SIDE_EFFECTING
TOKEN = (8, 128)


def _in_hbm(a):
    return pltpu.with_memory_space_constraint(a, pltpu.HBM)


PLAN_COPIES = dict(gather_ici=3, gather_d2d=N_CHIP, scatter_d2d=N_CHIP, scatter_ici=3)


def _plan(kind):
    x, y, c, chips = _place()
    sibling = (x, y, 1 - c)
    if kind == "gather_ici":
        me = 4 * x + 2 * y + c
        return [(me, me, 4 * px + 2 * py + c, (px, py, c)) for px, py in chips]
    if kind == "gather_d2d":
        return [(2 * j + c, 2 * j + c, 2 * j + 1 - c, sibling) for j in range(N_CHIP)]
    if kind == "scatter_d2d":
        return [(2 * j + 1 - c, j, j, sibling) for j in range(N_CHIP)]
    assert kind == "scatter_ici"
    return [(2 * px + py, 2 * x + y, 2 * px + py, (px, py, c)) for px, py in chips]


def split_start(kind, srcs, dsts):
    n = len(srcs)
    bufs = list(srcs) + ([] if dsts is None else list(dsts))
    nb = len(bufs)
    k = PLAN_COPIES[kind]

    def body(*refs):
        ins = refs[:n]
        outs = ins if dsts is None else refs[n:2 * n]
        send_sems, recv_sems = refs[nb], refs[nb + 1]
        token = refs[-1]
        for a in range(n):
            for j, (src, dst, _, peer) in enumerate(_plan(kind)):
                _remote(ins[a].at[src], outs[a].at[dst], send_sems, recv_sems, k * a + j, peer).start()
        token[...] = jnp.zeros_like(token)

    out = pl.pallas_call(
        body, name=kind + "_start",
        out_shape=(pltpu.SemaphoreType.DMA((k * n,)), pltpu.SemaphoreType.DMA((k * n,)),
                   *[pltpu.HBM(b.shape, b.dtype) for b in bufs], _sds(TOKEN, F32)),
        in_specs=[HBM_SPEC] * nb,
        out_specs=(SEM_SPEC, SEM_SPEC, *[HBM_SPEC] * nb, pl.BlockSpec(memory_space=pltpu.VMEM)),
        input_output_aliases={i: 2 + i for i in range(nb)},
        compiler_params=pltpu.CompilerParams(has_side_effects=DATAFLOW),
    )(*[_in_hbm(b) for b in bufs])
    return kind, out[0], out[1], list(out[2:2 + n]), None if dsts is None else list(out[2 + n:2 + nb]), out[-1]


def split_wait(pending, after):
    kind, send_sems, recv_sems, srcs, dsts, _ = pending
    n = len(srcs)
    bufs = list(srcs) + ([] if dsts is None else list(dsts))
    nb = len(bufs)
    k = PLAN_COPIES[kind]

    def body(*refs):
        ins = refs[:n]
        outs = ins if dsts is None else refs[n:2 * n]
        send_sems, recv_sems = refs[nb], refs[nb + 1]
        for a in range(n):
            for j, (src, _, landed, peer) in enumerate(_plan(kind)):
                cp = _remote(ins[a].at[src], outs[a].at[landed], send_sems, recv_sems, k * a + j, peer)
                cp.wait_send()
                cp.wait_recv()

    out = pl.pallas_call(
        body, name=kind + "_wait",
        out_shape=[pltpu.HBM(b.shape, b.dtype) for b in bufs],
        in_specs=[HBM_SPEC] * nb + [SEM_SPEC, SEM_SPEC, ANY], out_specs=[HBM_SPEC] * nb,
        input_output_aliases={i: i for i in range(nb)},
        compiler_params=pltpu.CompilerParams(has_side_effects=DATAFLOW),
    )(*bufs, send_sems, recv_sems, after)
    return list(out) if dsts is None else (list(out[:n]), list(out[n:]))


def gather_d2d(bufs):
    n = len(bufs)

    def body(*refs):
        outs = refs[n:2 * n]
        token = refs[2 * n]
        send_sems, recv_sems = refs[2 * n + 1:]
        x, y, c, _ = _place()
        sent = []
        for a in range(n):
            for j in range(N_CHIP):
                slot = outs[a].at[2 * j + c]
                cp = _remote(slot, slot, send_sems, recv_sems, N_CHIP * a + j, (x, y, 1 - c))
                cp.start()
                sent.append(cp)
        for a in range(n):
            for j in range(N_CHIP):
                slot = outs[a].at[2 * j + 1 - c]
                _remote(slot, slot, send_sems, recv_sems, N_CHIP * a + j, (x, y, 1 - c)).wait_recv()
        for cp in sent:
            cp.wait_send()
        token[...] = jnp.zeros_like(token)

    out = pl.pallas_call(
        body, name="gather_d2d", in_specs=[ANY] * n,
        out_specs=[ANY] * n + [pl.BlockSpec(memory_space=pltpu.VMEM)],
        out_shape=[_sds(b.shape, b.dtype) for b in bufs] + [_sds(TOKEN, F32)],
        input_output_aliases={a: a for a in range(n)},
        scratch_shapes=[pltpu.SemaphoreType.DMA((N_CHIP * n,)), pltpu.SemaphoreType.DMA((N_CHIP * n,))],
    )(*bufs)
    return list(out[:n]), out[n]


def pair_sum(part, got, c_arr, chip_arr):
    _, R, C = part.shape
    tr = _tile(R, 512)
    p4 = part.reshape(N_CHIP, 2, R, C)

    def body(c_ref, chip_ref, p_ref, g_ref, o_ref, own_ref):
        s = (p_ref[...].astype(F32) + g_ref[...].astype(F32)).astype(o_ref.dtype)
        o_ref[...] = s

        @pl.when(pl.program_id(1) == chip_ref[0])
        def _():
            own_ref[...] = s

    return pl.pallas_call(
        body, name="pair_sum",
        grid_spec=pltpu.PrefetchScalarGridSpec(
            num_scalar_prefetch=2, grid=(R // tr, N_CHIP),
            in_specs=[pl.BlockSpec((None, None, tr, C), lambda i, j, c_ref, chip_ref: (j, c_ref[0], i, 0)),
                      pl.BlockSpec((None, tr, C), lambda i, j, c_ref, chip_ref: (j, i, 0))],
            out_specs=[pl.BlockSpec((None, tr, C), lambda i, j, c_ref, chip_ref: (j, i, 0)),
                       pl.BlockSpec((None, tr, C), lambda i, j, c_ref, chip_ref: (chip_ref[0], i, 0))]),
        out_shape=[_sds((N_CHIP, R, C), part.dtype), _sds((N_CHIP, R, C), part.dtype)],
        compiler_params=_params(("parallel", "arbitrary")),
    )(c_arr, chip_arr, p4, got)


def adamw(w, m, v, parts_per_layer):
    NL, R, C = w.shape
    P = parts_per_layer[0].shape[0]
    tr = _tile(R, max(16, ADAMW_BLOCK_ELEMS // C))
    nb = R // tr

    def body(w_ref, m_ref, v_ref, *rest):
        part_refs = rest[:NL]
        g_ref, d_ref, nm_ref, nv_ref = rest[NL:]
        layer = pl.program_id(0)
        g = None
        for q in range(NL):
            s = part_refs[q][0].astype(F32)
            for t in range(1, P):
                s = s + part_refs[q][t].astype(F32)
            s = jnp.where(layer == q, s, 0.0)
            g = s if g is None else g + s
        wv = w_ref[...]
        nm = ADAM_B1 * m_ref[...] + (1.0 - ADAM_B1) * g
        nv = ADAM_B2 * v_ref[...] + (1.0 - ADAM_B2) * (g * g)
        m_hat = nm / (1.0 - ADAM_B1 ** ADAM_STEP)
        v_hat = nv / (1.0 - ADAM_B2 ** ADAM_STEP)
        g_ref[...] = g
        d_ref[...] = -ADAM_LR * (m_hat / (jnp.sqrt(v_hat) + ADAM_EPS) + ADAM_WD * wv)
        nm_ref[...] = nm
        nv_ref[...] = nv

    def part_spec(q):
        return pl.BlockSpec((P, tr, C), lambda l, i: (0, jnp.where(l == q, i, jnp.where(l < q, 0, nb - 1)), 0))

    blk = pl.BlockSpec((None, tr, C), lambda l, i: (l, i, 0))
    return pl.pallas_call(
        body, name="adamw", grid=(NL, nb),
        in_specs=[blk, blk, blk] + [part_spec(q) for q in range(NL)],
        out_specs=[blk, blk, blk, blk], out_shape=[_sds((NL, R, C), F32)] * 4,
        compiler_params=_params(("arbitrary", "arbitrary")),
    )(w, m, v, *parts_per_layer)


SMALL = ("norm1_g", "gmlp_ln_g", "w_spatial", "b_spatial", "mix_norm_attn_g", "mix_norm_gmlp_g", "norm2_g")
BIG = ("w_in", "w_out", "w_gate", "w_up", "w_down")
TRANSPOSED = ("w_gate", "w_up")
GROUPS = (("w_in", "w_out"), ("w_gate", "w_up", "w_down"))
LANES = 128


def _pack(layers, final):
    flat = [layer[n].reshape(-1) for layer in layers for n in SMALL] + [final.reshape(-1)]
    return jnp.concatenate(flat).reshape(-1, LANES)


def _unpack(packed, like_layers, like_final):
    flat = packed.reshape(-1)
    out, off = [], 0
    for layer in like_layers:
        d = {}
        for n in SMALL:
            size = layer[n].size
            d[n] = flat[off:off + size].reshape(layer[n].shape)
            off += size
        out.append(d)
    return out, flat[off:off + like_final.size].reshape(like_final.shape)


def kernel(x, norm1_g, w_in, gmlp_ln_g, w_spatial, b_spatial, mix_norm_attn_g, mix_norm_gmlp_g, w_out, norm2_g, w_gate, w_up, w_down, final_g, loss_target, m_norm1_g, m_w_in, m_gmlp_ln_g, m_w_spatial, m_b_spatial, m_mix_norm_attn_g, m_mix_norm_gmlp_g, m_w_out, m_norm2_g, m_w_gate, m_w_up, m_w_down, m_final_g, v_norm1_g, v_w_in, v_gmlp_ln_g, v_w_spatial, v_b_spatial, v_mix_norm_attn_g, v_mix_norm_gmlp_g, v_w_out, v_norm2_g, v_w_gate, v_w_up, v_w_down, v_final_g):
    S, D = x.shape[1], x.shape[2]
    NL = norm1_g.shape[0]
    DA = D // 2
    xs = x.reshape(S, D)
    tabs = rope_tables(S)
    ax, ay, ac = lax.axis_index("x"), lax.axis_index("y"), lax.axis_index("c")
    me_arr = (4 * ax + 2 * ay + ac).astype(jnp.int32).reshape(1)
    c_arr = ac.astype(jnp.int32).reshape(1)
    chip_arr = (2 * ax + ay).astype(jnp.int32).reshape(1)
    small_w = dict(norm1_g=norm1_g, gmlp_ln_g=gmlp_ln_g, w_spatial=w_spatial, b_spatial=b_spatial,
                   mix_norm_attn_g=mix_norm_attn_g, mix_norm_gmlp_g=mix_norm_gmlp_g, norm2_g=norm2_g)
    small_m = dict(norm1_g=m_norm1_g, gmlp_ln_g=m_gmlp_ln_g, w_spatial=m_w_spatial, b_spatial=m_b_spatial,
                   mix_norm_attn_g=m_mix_norm_attn_g, mix_norm_gmlp_g=m_mix_norm_gmlp_g, norm2_g=m_norm2_g)
    small_v = dict(norm1_g=v_norm1_g, gmlp_ln_g=v_gmlp_ln_g, w_spatial=v_w_spatial, b_spatial=v_b_spatial,
                   mix_norm_attn_g=v_mix_norm_attn_g, mix_norm_gmlp_g=v_mix_norm_gmlp_g, norm2_g=v_norm2_g)
    def view(n, a):
        return jnp.swapaxes(a, 1, 2) if n in TRANSPOSED else a

    big_w = {n: view(n, a) for n, a in dict(w_in=w_in, w_out=w_out, w_gate=w_gate, w_up=w_up, w_down=w_down).items()}
    big_m = {n: view(n, a) for n, a in dict(w_in=m_w_in, w_out=m_w_out, w_gate=m_w_gate, w_up=m_w_up, w_down=m_w_down).items()}
    big_v = {n: view(n, a) for n, a in dict(w_in=v_w_in, w_out=v_w_out, w_gate=v_w_gate, w_up=v_w_up, w_down=v_w_down).items()}

    def layer_small(l):
        ws = w_spatial[l]
        return dict(norm1_g=norm1_g[l][None], gmlp_ln_g=gmlp_ln_g[l][None], mix_norm_attn_g=mix_norm_attn_g[l][None],
                    mix_norm_gmlp_g=mix_norm_gmlp_g[l][None], norm2_g=norm2_g[l][None], w_spatial=ws,
                    w_spatial_t=jnp.swapaxes(ws, 1, 2), bcol=jnp.broadcast_to(b_spatial[l][:, :, None], ws.shape))

    n_stages = 2 * NL
    zero = jnp.zeros(TOKEN, F32)
    TOK = 5

    def stage_bufs(s, dep):
        return [to_slot(big_w[n], s // 2, me_arr, dep, BF16) for n in GROUPS[s % 2]]

    def stage_weights(s, bufs):
        gw = dict(zip(GROUPS[s % 2], bufs))
        if s % 2 == 0:
            gw["w_out"] = gw["w_out"].reshape(2, DA, D)
        else:
            gw = {n: b.reshape(-1, D) for n, b in gw.items()}
        return gw

    ici = {0: split_start("gather_ici", stage_bufs(0, zero), None)}
    ici[1] = split_start("gather_ici", stage_bufs(1, ici[0][TOK]), None)
    h = xs
    d2d = {0: split_start("gather_d2d", split_wait(ici[0], h), None)}
    ready = {0: split_wait(d2d[0], h)}
    saved, weights = [], []
    for s in range(n_stages):
        deps = [ici[1][TOK]] if s == 0 else []
        if 1 <= s < n_stages - 1:
            d2d[s + 1] = split_start("gather_d2d", split_wait(ici[s + 1], h), None)
            deps.append(d2d[s + 1][TOK])
        if s + 2 < n_stages:
            ici[s + 2] = split_start("gather_ici", stage_bufs(s + 2, deps[-1]), None)
            deps.append(ici[s + 2][TOK])
        dep = sum(deps[1:], deps[0]) if deps else zero
        gw = stage_weights(s, ready[s])
        sm = layer_small(s // 2)
        if s % 2 == 0:
            sm["norm1_g"] = sm["norm1_g"] + dep[0, 0]
            h, sv = mixer_fwd(h, sm, gw, tabs)
        else:
            sm["norm2_g"] = sm["norm2_g"] + dep[0, 0]
            h, sv = ffn_fwd(h, sm, gw)
        saved.append(sv)
        weights.append(gw)
        if s == 0:
            d2d[1] = split_start("gather_d2d", split_wait(ici[1], h), None)
        if s + 1 < n_stages:
            ready[s + 1] = split_wait(d2d[s + 1], h)
    loss_part, dx, dxb, d_final = loss_and_grad(h, final_g[None], loss_target.reshape(S, D))
    loss = lax.psum(loss_part[0, 0], ("x", "y", "c"))

    big_sums = [dict() for _ in range(NL)]
    small_grads = [dict() for _ in range(NL)]
    pending, pending_stage = None, None
    dep = zero
    for s in reversed(range(n_stages)):
        l = s // 2
        crossing = []

        def on_grads(big, s=s, crossing=crossing):
            parts = [big[n].reshape((N_DEV, -1, big[n].shape[-1])) for n in GROUPS[s % 2]]
            gots = [lax.empty((N_CHIP,) + p.shape[1:], p.dtype) for p in parts]
            crossing.append(split_start("scatter_d2d", parts, gots))
            return crossing[0][TOK]

        if s % 2 == 1:
            dx, dxb, small = ffn_bwd(dx, dxb, saved[s], layer_small(l), weights[s], dep, on_grads)
        else:
            dx, dxb, small = mixer_bwd(dx, dxb, saved[s], layer_small(l), weights[s], tabs, dep, on_grads)
        small_grads[l].update(small)
        if pending is not None:
            big_sums[pending_stage // 2].update(zip(GROUPS[pending_stage % 2], split_wait(pending, dx)[1]))
        parts, got = split_wait(crossing[0], dx)
        pairs = [pair_sum(p, g, c_arr, chip_arr) for p, g in zip(parts, got)]
        pending, pending_stage = split_start("scatter_ici", [t for t, _ in pairs], [o for _, o in pairs]), s
        dep = pending[TOK]

    def update_big(n):
        R, C = big_w[n].shape[1], big_w[n].shape[2]
        res = adamw(big_w[n], big_m[n], big_v[n], [big_sums[l][n].reshape(N_CHIP, R, C) for l in range(NL)])
        return [view(n, t) for t in res], res[0][0, :TOKEN[0], :TOKEN[1]]

    big_out = {}
    for n in GROUPS[1]:
        big_out[n], after_ffn = update_big(n)

    packed = to_slot(_pack(small_grads, d_final), None, me_arr, dep + after_ffn, F32)
    gathered_small = gather_d2d(gather_ici([packed]))[0][0]
    pw = _pack([{n: small_w[n][l] for n in SMALL} for l in range(NL)], final_g)[None]
    pm = _pack([{n: small_m[n][l] for n in SMALL} for l in range(NL)], m_final_g)[None]
    pv = _pack([{n: small_v[n][l] for n in SMALL} for l in range(NL)], v_final_g)[None]
    like_layers = [{n: small_w[n][l] for n in SMALL} for l in range(NL)]
    small_res = adamw(pw, pm, pv, [gathered_small])
    small_out = [_unpack(t[0], like_layers, final_g) for t in small_res]
    big_sums[pending_stage // 2].update(zip(GROUPS[pending_stage % 2], split_wait(pending, small_res[0])[1]))

    def small_stack(k, n):
        return jnp.stack([small_out[k][0][l][n] for l in range(NL)])

    for n in GROUPS[0]:
        big_out[n], _ = update_big(n)

    order = ("norm1_g", "w_in", "gmlp_ln_g", "w_spatial", "b_spatial", "mix_norm_attn_g", "mix_norm_gmlp_g", "w_out",
             "norm2_g", "w_gate", "w_up", "w_down")
    outs = [loss, dx.reshape(x.shape)]
    for k in range(4):
        for n in order:
            outs.append(big_out[n][k] if n in BIG else small_stack(k, n))
        outs.append(small_out[k][1])
    return tuple(outs)
```

```python
import functools

import jax
import jax.numpy as jnp
from jax import lax
from jax.experimental import pallas as pl
from jax.experimental.pallas import tpu as pltpu

F32 = jnp.float32
BF16 = jnp.bfloat16

HEAD_DIM = 128
CHUNK = 128
GROUP = 128
N_SIDE = 64
DILATIONS = (1, 4, 16)
TQ = 128
HALO = 64
EPS = 1e-6
NEG = -1e30
ROPE_THETA = 10000.0
ADAM_LR = 0.001
ADAM_B1 = 0.9
ADAM_B2 = 0.999
ADAM_EPS = 1e-08
ADAM_WD = 0.01
ADAM_STEP = 10
N_DEV = 8
N_CHIP = 4
VMEM_LIMIT_V7X = 56 * 1024 * 1024
ADAMW_BLOCK_ELEMS = 128 * 1024
MESH = pl.DeviceIdType.MESH
ANY = pl.BlockSpec(memory_space=pl.ANY)


def _params(sem=None):
    return pltpu.CompilerParams(dimension_semantics=sem, vmem_limit_bytes=VMEM_LIMIT_V7X)


def _tile(n, want):
    for t in range(min(want, n), 15, -1):
        if n % t == 0 and t % 16 == 0:
            return t
    return n


def _sds(shape, dtype):
    return jax.ShapeDtypeStruct(shape, dtype)


def _dot(a, b, dims):
    return lax.dot_general(a, b, (dims, ((), ())), preferred_element_type=F32)


NN = ((1,), (0,))
NT = ((1,), (1,))
TN = ((0,), (0,))


def _sigmoid(x):
    return 1.0 / (1.0 + jnp.exp(-x))


def _gelu(x):
    return 0.5 * x * (1.0 + lax.erf(x * 0.7071067811865476))


def _gelu_grad(x):
    cdf = 0.5 * (1.0 + lax.erf(x * 0.7071067811865476))
    pdf = 0.3989422804014327 * jnp.exp(-0.5 * x * x)
    return cdf + x * pdf


def rms_fwd(x, g):
    S, D = x.shape
    tr = _tile(S, 512)

    def body(x_ref, g_ref, o_ref):
        xv = x_ref[...]
        r = lax.rsqrt(jnp.mean(xv * xv, axis=-1, keepdims=True) + EPS)
        o_ref[...] = (xv * r * g_ref[...]).astype(o_ref.dtype)

    return pl.pallas_call(
        body, name="rms_fwd", grid=(S // tr,),
        in_specs=[pl.BlockSpec((tr, D), lambda i: (i, 0)), pl.BlockSpec((1, D), lambda i: (0, 0))],
        out_specs=pl.BlockSpec((tr, D), lambda i: (i, 0)),
        out_shape=_sds((S, D), BF16), compiler_params=_params(("parallel",)),
    )(x, g)


def _rms_bwd_math(xv, gv, dy):
    r = lax.rsqrt(jnp.mean(xv * xv, axis=-1, keepdims=True) + EPS)
    dyg = dy * gv
    dx = r * dyg - xv * (r * r * r) * jnp.mean(dyg * xv, axis=-1, keepdims=True)
    dg = jnp.sum(dy * xv * r, axis=0, keepdims=True)
    return dx, dg


def rms_bwd_res(x, g, dy, dres):
    S, D = x.shape
    tr = _tile(S, 512)

    def body(x_ref, g_ref, dy_ref, dres_ref, dx_ref, dxb_ref, dg_ref):
        dx, dg = _rms_bwd_math(x_ref[...], g_ref[...], dy_ref[...])
        tot = dres_ref[...] + dx
        dx_ref[...] = tot
        dxb_ref[...] = tot.astype(dxb_ref.dtype)

        @pl.when(pl.program_id(0) == 0)
        def _():
            dg_ref[...] = dg

        @pl.when(pl.program_id(0) > 0)
        def _():
            dg_ref[...] += dg

    row = pl.BlockSpec((tr, D), lambda i: (i, 0))
    vec = pl.BlockSpec((1, D), lambda i: (0, 0))
    return pl.pallas_call(
        body, name="rms_bwd_res", grid=(S // tr,),
        in_specs=[row, vec, row, row], out_specs=[row, row, vec],
        out_shape=[_sds((S, D), F32), _sds((S, D), BF16), _sds((1, D), F32)],
        compiler_params=_params(("arbitrary",)),
    )(x, g, dy, dres)


def loss_and_grad(x, g, target):
    S, D = x.shape
    tr = _tile(S, 512)

    def body(x_ref, g_ref, t_ref, loss_ref, dx_ref, dxb_ref, dg_ref):
        xv = x_ref[...]
        gv = g_ref[...]
        r = lax.rsqrt(jnp.mean(xv * xv, axis=-1, keepdims=True) + EPS)
        y = xv * r * gv
        diff = y - t_ref[...]
        part = 0.5 * jnp.sum(jnp.mean(diff * diff, axis=-1, keepdims=True), axis=0, keepdims=True)
        dy = diff * (1.0 / D)
        dx, dg = _rms_bwd_math(xv, gv, dy)
        dx_ref[...] = dx
        dxb_ref[...] = dx.astype(dxb_ref.dtype)

        @pl.when(pl.program_id(0) == 0)
        def _():
            dg_ref[...] = dg
            loss_ref[...] = jnp.broadcast_to(part, loss_ref.shape)

        @pl.when(pl.program_id(0) > 0)
        def _():
            dg_ref[...] += dg
            loss_ref[...] += jnp.broadcast_to(part, loss_ref.shape)

    row = pl.BlockSpec((tr, D), lambda i: (i, 0))
    vec = pl.BlockSpec((1, D), lambda i: (0, 0))
    return pl.pallas_call(
        body, name="loss_and_grad", grid=(S // tr,),
        in_specs=[row, vec, row],
        out_specs=[pl.BlockSpec((8, 128), lambda i: (0, 0)), row, row, vec],
        out_shape=[_sds((8, 128), F32), _sds((S, D), F32), _sds((S, D), BF16), _sds((1, D), F32)],
        compiler_params=_params(("arbitrary",)),
    )(x, g, target)


def _mm_body(n_pairs, dims, red_axis, n_red, has_res):
    def body(*refs):
        ins = refs[:2 * n_pairs]
        res_ref = refs[2 * n_pairs] if has_res else None
        o_ref = refs[2 * n_pairs + has_res]
        p = None
        for t in range(n_pairs):
            d = _dot(ins[2 * t][...], ins[2 * t + 1][...], dims)
            p = d if p is None else p + d
        if red_axis is None:
            if has_res:
                p = res_ref[...] + p
            o_ref[...] = p.astype(o_ref.dtype)
            return
        acc_ref = o_ref if o_ref.dtype == F32 else refs[2 * n_pairs + has_res + 1]
        r = pl.program_id(red_axis)

        @pl.when(r == 0)
        def _():
            acc_ref[...] = res_ref[...] + p if has_res else p

        @pl.when(r > 0)
        def _():
            acc_ref[...] += p

        if acc_ref is not o_ref:
            @pl.when(r == n_red - 1)
            def _():
                o_ref[...] = acc_ref[...].astype(o_ref.dtype)

    return body


def mm_in(h, w):
    S, D = h.shape
    NS, _, n = w.shape
    tm = _tile(S, 1024)
    return pl.pallas_call(
        _mm_body(1, NN, None, 1, False), name="mm_in", grid=(S // tm, NS),
        in_specs=[pl.BlockSpec((tm, D), lambda i, k: (i, 0)), pl.BlockSpec((None, D, n), lambda i, k: (k, 0, 0))],
        out_specs=pl.BlockSpec((tm, n), lambda i, k: (i, k)),
        out_shape=_sds((S, NS * n), F32), compiler_params=_params(("parallel", "parallel")),
    )(h, w)


def mm_out(x, mix_a, mix_g, w):
    S, D = x.shape
    DA = mix_a.shape[1]
    tm = _tile(S, 1024)
    tn = _tile(D, 1024)
    act = pl.BlockSpec((tm, DA), lambda i, j: (i, 0))
    return pl.pallas_call(
        _mm_body(2, NN, None, 1, True), name="mm_out", grid=(S // tm, D // tn),
        in_specs=[act, pl.BlockSpec((None, DA, tn), lambda i, j: (0, 0, j)),
                  act, pl.BlockSpec((None, DA, tn), lambda i, j: (1, 0, j)),
                  pl.BlockSpec((tm, tn), lambda i, j: (i, j))],
        out_specs=pl.BlockSpec((tm, tn), lambda i, j: (i, j)),
        out_shape=_sds((S, D), F32), compiler_params=_params(("parallel", "parallel")),
    )(mix_a, w, mix_g, w, x)


def _lane_tile(n, want):
    for t in range(min(want, n) // 128 * 128, 127, -128):
        if n % t == 0:
            return t
    return n


def ff_fwd(h, wg, wu):
    S, D = h.shape
    F = wg.shape[0]
    tm = _tile(S, 1024)
    tn = _lane_tile(F, 512)

    def body(h_ref, wg_ref, wu_ref, dfdu_ref, dfdg_ref, ff_ref):
        hv = h_ref[...]
        g = _dot(hv, wg_ref[...], NT)
        u = _dot(hv, wu_ref[...], NT)
        sg = _sigmoid(g)
        act = g * sg
        dfdu_ref[...] = act.astype(dfdu_ref.dtype)
        dfdg_ref[...] = (u * (sg * (1.0 + g * (1.0 - sg)))).astype(dfdg_ref.dtype)
        ff_ref[...] = (act * u).astype(ff_ref.dtype)

    wspec = pl.BlockSpec((tn, D), lambda i, j: (j, 0))
    ospec = pl.BlockSpec((tm, tn), lambda i, j: (i, j))
    return pl.pallas_call(
        body, name="ff_fwd", grid=(S // tm, F // tn),
        in_specs=[pl.BlockSpec((tm, D), lambda i, j: (i, 0)), wspec, wspec],
        out_specs=[ospec, ospec, ospec],
        out_shape=[_sds((S, F), BF16), _sds((S, F), BF16), _sds((S, F), BF16)],
        compiler_params=_params(("parallel", "parallel")),
    )(h, wg, wu)


def mm_down(x, ff, wd):
    S, D = x.shape
    F = ff.shape[1]
    tm = _tile(S, 1024)
    tn = _lane_tile(D, 512)
    blk = pl.BlockSpec((tm, tn), lambda i, j: (i, j))
    return pl.pallas_call(
        _mm_body(1, NN, None, 1, True), name="mm_down", grid=(S // tm, D // tn),
        in_specs=[pl.BlockSpec((tm, F), lambda i, j: (i, 0)), pl.BlockSpec((F, tn), lambda i, j: (0, j)), blk],
        out_specs=blk, out_shape=_sds((S, D), F32), compiler_params=_params(("parallel", "parallel")),
    )(ff, wd, x)


def ff_bwd_act(dxb, wd, dfdg, dfdu, dep):
    S, D = dxb.shape
    F = wd.shape[0]
    tm = _tile(S, 1024)
    tn = _lane_tile(F, 512)

    def body(dx_ref, wd_ref, dfdg_ref, dfdu_ref, dep_ref, dgate_ref, dup_ref):
        dff = _dot(dx_ref[...], wd_ref[...], NT)
        dup_ref[...] = (dff * dfdu_ref[...].astype(F32)).astype(dup_ref.dtype)
        dgate_ref[...] = (dff * dfdg_ref[...].astype(F32)).astype(dgate_ref.dtype)

    aspec = pl.BlockSpec((tm, tn), lambda i, j: (i, j))
    return pl.pallas_call(
        body, name="ff_bwd_act", grid=(S // tm, F // tn),
        in_specs=[pl.BlockSpec((tm, D), lambda i, j: (i, 0)), pl.BlockSpec((tn, D), lambda i, j: (j, 0)), aspec, aspec,
                  pl.BlockSpec(TOKEN, lambda i, j: (0, 0))],
        out_specs=[aspec, aspec],
        out_shape=[_sds((S, F), BF16), _sds((S, F), BF16)],
        compiler_params=_params(("parallel", "parallel")),
    )(dxb, wd, dfdg, dfdu, dep)


def dw_down(ff, dxb):
    S, F = ff.shape
    D = dxb.shape[1]
    tf = _lane_tile(F, 256)
    return pl.pallas_call(
        _mm_body(1, TN, None, 1, False), name="dw_down", grid=(F // tf,),
        in_specs=[pl.BlockSpec((S, tf), lambda k: (0, k)), pl.BlockSpec((S, D), lambda k: (0, 0))],
        out_specs=pl.BlockSpec((tf, D), lambda k: (k, 0)),
        out_shape=_sds((F, D), BF16), compiler_params=_params(("parallel",)),
    )(ff, dxb)


def dw_gate_up(h, dgate, dup):
    S, D = h.shape
    F = dgate.shape[1]
    f = _lane_tile(F, 256)

    def body(h_ref, dg_ref, du_ref, og_ref, ou_ref):
        hv = h_ref[...]
        og_ref[...] = _dot(dg_ref[...], hv, TN).astype(og_ref.dtype)
        ou_ref[...] = _dot(du_ref[...], hv, TN).astype(ou_ref.dtype)

    aspec = pl.BlockSpec((S, f), lambda k: (0, k))
    ospec = pl.BlockSpec((f, D), lambda k: (k, 0))
    return pl.pallas_call(
        body, name="dw_gate_up", grid=(F // f,),
        in_specs=[pl.BlockSpec((S, D), lambda k: (0, 0)), aspec, aspec],
        out_specs=[ospec, ospec],
        out_shape=[_sds((F, D), BF16), _sds((F, D), BF16)],
        compiler_params=_params(("parallel",)),
    )(h, dgate, dup)


def dh_ff(dgate, dup, wg, wu, dep):
    S, F = dgate.shape
    D = wg.shape[1]
    tm = _tile(S, 512)
    tn = _lane_tile(D, 256)

    def body(dg_ref, wg_ref, du_ref, wu_ref, dep_ref, o_ref):
        o_ref[...] = _dot(dg_ref[...], wg_ref[...], NN) + _dot(du_ref[...], wu_ref[...], NN)

    aspec = pl.BlockSpec((tm, F), lambda i, j: (i, 0))
    wspec = pl.BlockSpec((F, tn), lambda i, j: (0, j))
    return pl.pallas_call(
        body, name="dh_ff", grid=(S // tm, D // tn),
        in_specs=[aspec, wspec, aspec, wspec, pl.BlockSpec(TOKEN, lambda i, j: (0, 0))],
        out_specs=pl.BlockSpec((tm, tn), lambda i, j: (i, j)),
        out_shape=_sds((S, D), F32), compiler_params=_params(("parallel", "parallel")),
    )(dgate, wg, dup, wu, dep)


def dmix_mm(dxb, w, dep):
    S, D = dxb.shape
    tm = _tile(S, 1024)
    tn = _tile(D, 1024)

    def body(dx_ref, w_ref, dep_ref, o_ref):
        o_ref[...] = _dot(dx_ref[...], w_ref[...], NT)

    return pl.pallas_call(
        body, name="dmix_mm", grid=(S // tm, D // tn),
        in_specs=[pl.BlockSpec((tm, D), lambda i, j: (i, 0)), pl.BlockSpec((tn, D), lambda i, j: (j, 0)),
                  pl.BlockSpec(TOKEN, lambda i, j: (0, 0))],
        out_specs=pl.BlockSpec((tm, tn), lambda i, j: (i, j)),
        out_shape=_sds((S, D), F32), compiler_params=_params(("parallel", "parallel")),
    )(dxb, w, dep)


def dw_out(mix_a, mix_g, dxb):
    S, DA = mix_a.shape
    D = dxb.shape[1]
    tn = _lane_tile(D, 512)

    def half(name, m):
        return pl.pallas_call(
            _mm_body(1, TN, None, 1, False), name=name, grid=(D // tn,),
            in_specs=[pl.BlockSpec((S, DA), lambda j: (0, 0)), pl.BlockSpec((S, tn), lambda j: (0, j))],
            out_specs=pl.BlockSpec((DA, tn), lambda j: (0, j)),
            out_shape=_sds((DA, D), BF16), compiler_params=_params(("parallel",)),
        )(m, dxb)

    return half("dw_out_a", mix_a), half("dw_out_g", mix_g)


def dw_in(h, dproj, NS):
    S, D = h.shape
    n = dproj.shape[1] // NS
    tm = _lane_tile(D, 1024)
    return pl.pallas_call(
        _mm_body(1, TN, None, 1, False), name="dw_in", grid=(NS, D // tm),
        in_specs=[pl.BlockSpec((S, tm), lambda k, j: (0, j)), pl.BlockSpec((S, n), lambda k, j: (0, k))],
        out_specs=pl.BlockSpec((None, tm, n), lambda k, j: (k, j, 0)),
        out_shape=_sds((NS, D, n), BF16), compiler_params=_params(("parallel", "parallel")),
    )(h, dproj)


def dh_in(dproj, w, dep):
    S = dproj.shape[0]
    NS, D, n = w.shape
    tm = _tile(S, 512)
    tn = _tile(D, 512)

    def body(dp_ref, w_ref, dep_ref, o_ref):
        acc = None
        for k in range(NS):
            p = _dot(dp_ref[:, k * n:(k + 1) * n], w_ref[k], NT)
            acc = p if acc is None else acc + p
        o_ref[...] = acc

    return pl.pallas_call(
        body, name="dh_in", grid=(S // tm, D // tn),
        in_specs=[pl.BlockSpec((tm, NS * n), lambda i, j: (i, 0)), pl.BlockSpec((NS, tn, n), lambda i, j: (0, j, 0)),
                  pl.BlockSpec(TOKEN, lambda i, j: (0, 0))],
        out_specs=pl.BlockSpec((tm, tn), lambda i, j: (i, j)),
        out_shape=_sds((S, D), F32), compiler_params=_params(("parallel", "parallel")),
    )(dproj, w, dep)


def rope_tables(S):
    pos = jnp.arange(S, dtype=F32)
    inv = ROPE_THETA ** (-jnp.arange(0, HEAD_DIM, 2, dtype=F32) / HEAD_DIM)
    ang = pos[:, None] * inv[None, :]
    cos, sin = jnp.cos(ang), jnp.sin(ang)
    return jnp.concatenate([cos, cos], axis=-1), jnp.concatenate([-sin, sin], axis=-1)


def _rot_half(t):
    return pltpu.roll(t, HEAD_DIM // 2, 1)


PERM_ROWS = 256


def _by_residue(y, d, out_ref):
    tr = y.shape[0]
    n = tr // d
    o = lax.broadcasted_iota(jnp.int32, (tr, tr), 0)
    i = lax.broadcasted_iota(jnp.int32, (tr, tr), 1)
    src = jnp.bitwise_and(o, n - 1) * d + lax.shift_right_logical(o, n.bit_length() - 1)
    z = _dot((src == i).astype(y.dtype), y, NN).astype(out_ref.dtype)
    for r in range(d):
        out_ref[r] = z[r * n:(r + 1) * n]


def _residue_outputs(S, W, dtype):
    shapes = [_sds((d, S // d, W), dtype) for d in DILATIONS[1:]]
    specs = [pl.BlockSpec((d, PERM_ROWS // d, W), lambda i: (0, i, 0)) for d in DILATIONS[1:]]
    return shapes, specs


def rope_qkv(proj, cos2, sin2, DA):
    S = proj.shape[0]
    tr = PERM_ROWS
    nh = DA // HEAD_DIM

    def body(q_ref, k_ref, v_ref, cos_ref, sin_ref, o_ref, *by_res):
        c = cos_ref[...]
        s = sin_ref[...]
        for h in range(nh):
            sl = slice(h * HEAD_DIM, (h + 1) * HEAD_DIM)
            for j, ref in enumerate((q_ref, k_ref)):
                t = ref[:, sl]
                o_ref[:, j * DA + h * HEAD_DIM:j * DA + (h + 1) * HEAD_DIM] = (t * c + _rot_half(t) * s).astype(o_ref.dtype)
        o_ref[:, 2 * DA:3 * DA] = v_ref[...].astype(o_ref.dtype)
        y = o_ref[...]
        for d, ref in zip(DILATIONS[1:], by_res):
            _by_residue(y, d, ref)

    tab = pl.BlockSpec((tr, HEAD_DIM), lambda i: (i, 0))
    shapes, specs = _residue_outputs(S, 3 * DA, BF16)
    out = pl.pallas_call(
        body, name="rope_qkv", grid=(S // tr,),
        in_specs=[pl.BlockSpec((tr, DA), lambda i: (i, 0)), pl.BlockSpec((tr, DA), lambda i: (i, 1)),
                  pl.BlockSpec((tr, DA), lambda i: (i, 2)), tab, tab],
        out_specs=[pl.BlockSpec((tr, 3 * DA), lambda i: (i, 0))] + specs,
        out_shape=[_sds((S, 3 * DA), BF16)] + shapes, compiler_params=_params(("parallel",)),
    )(proj, proj, proj, cos2, sin2)
    return [out[0].reshape(1, S, 3 * DA)] + list(out[1:])


def assemble_dproj(dqs, dks, dvs, duv, cos2, sin2):
    S, DA = dqs[0].shape
    tr = _tile(S, 256)
    nh = DA // HEAD_DIM

    def body(*refs):
        dq_refs, dk_refs, dv_refs = refs[0:3], refs[3:6], refs[6:9]
        duv_ref, cos_ref, sin_ref, o_ref = refs[9:13]
        c = cos_ref[...]
        s = sin_ref[...]
        for j, trio in enumerate((dq_refs, dk_refs)):
            for h in range(nh):
                sl = slice(h * HEAD_DIM, (h + 1) * HEAD_DIM)
                t = trio[0][:, sl] + trio[1][:, sl] + trio[2][:, sl]
                o_ref[:, j * DA + h * HEAD_DIM:j * DA + (h + 1) * HEAD_DIM] = (t * c - _rot_half(t) * s).astype(o_ref.dtype)
        o_ref[:, 2 * DA:3 * DA] = (dv_refs[0][...] + dv_refs[1][...] + dv_refs[2][...]).astype(o_ref.dtype)
        o_ref[:, 3 * DA:5 * DA] = duv_ref[...]

    blk = pl.BlockSpec((tr, DA), lambda i: (i, 0))
    tab = pl.BlockSpec((tr, HEAD_DIM), lambda i: (i, 0))
    return pl.pallas_call(
        body, name="assemble_dproj", grid=(S // tr,),
        in_specs=[blk] * 9 + [pl.BlockSpec((tr, 2 * DA), lambda i: (i, 0)), tab, tab],
        out_specs=pl.BlockSpec((tr, 5 * DA), lambda i: (i, 0)),
        out_shape=_sds((S, 5 * DA), BF16), compiler_params=_params(("parallel",)),
    )(*dqs, *dks, *dvs, duv, cos2, sin2)


STAT_W = 128


def _lanes_per_head(DA):
    return STAT_W // (DA // HEAD_DIM)


def _halo_specs(L, width_blocks, col):
    per = TQ // HALO
    last = L // HALO - 1
    W = width_blocks
    return [
        pl.BlockSpec((HALO, W), lambda r, i: (jnp.maximum(i * per - 1, 0), col(r))),
        pl.BlockSpec((TQ, W), lambda r, i: (i, col(r))),
        pl.BlockSpec((HALO, W), lambda r, i: (jnp.minimum((i + 1) * per, last), col(r))),
    ]


def _halo_specs3(L, W, c):
    per = TQ // HALO
    last = L // HALO - 1
    return [
        pl.BlockSpec((None, HALO, W), lambda r, i: (r, jnp.maximum(i * per - 1, 0), c)),
        pl.BlockSpec((None, TQ, W), lambda r, i: (r, i, c)),
        pl.BlockSpec((None, HALO, W), lambda r, i: (r, jnp.minimum((i + 1) * per, last), c)),
    ]


def _cat3(refs, sl):
    return jnp.concatenate([refs[0][:, sl], refs[1][:, sl], refs[2][:, sl]], axis=0)


def attn_fwd(qkv, d, DA):
    L = qkv.shape[1]
    S = L * d
    nh = DA // HEAD_DIM
    lph = _lanes_per_head(DA)
    scale = HEAD_DIM ** -0.5
    TK = TQ + 2 * HALO

    def body(q_ref, kp_ref, kc_ref, kn_ref, vp_ref, vc_ref, vn_ref, o_ref, lse_ref):
        i = pl.program_id(1)
        row = lax.broadcasted_iota(jnp.int32, (TQ, TK), 0)
        col = lax.broadcasted_iota(jnp.int32, (TQ, TK), 1)
        kpos = i * TQ - HALO + col
        mask = (jnp.abs(col - HALO - row) <= N_SIDE) & (kpos >= 0) & (kpos < L)
        for h in range(nh):
            sl = slice(h * HEAD_DIM, (h + 1) * HEAD_DIM)
            k = _cat3((kp_ref, kc_ref, kn_ref), sl)
            v = _cat3((vp_ref, vc_ref, vn_ref), sl)
            s = _dot(q_ref[:, sl], k, NT) * scale
            s = jnp.where(mask, s, NEG)
            m = jnp.max(s, axis=-1, keepdims=True)
            p = jnp.exp(s - m)
            l = jnp.sum(p, axis=-1, keepdims=True)
            o = _dot(p.astype(v.dtype), v, NN) / l
            o_ref[:, sl] = o
            lse_ref[:, h * lph:(h + 1) * lph] = jnp.broadcast_to(m + jnp.log(l), (TQ, lph))

    out_spec = pl.BlockSpec((TQ, DA), lambda r, i: (i, r))
    o, lse = pl.pallas_call(
        body, name="attn_fwd_d%d" % d, grid=(d, L // TQ),
        in_specs=[pl.BlockSpec((None, TQ, DA), lambda r, i: (r, i, 0))] + _halo_specs3(L, DA, 1) + _halo_specs3(L, DA, 2),
        out_specs=[out_spec, pl.BlockSpec((TQ, STAT_W), lambda r, i: (i, r))],
        out_shape=[_sds((L, d * DA), F32), _sds((L, d * STAT_W), F32)],
        compiler_params=_params(("parallel", "parallel")),
    )(*([qkv] * 7))
    return o.reshape(S, DA), lse.reshape(S, STAT_W)


def attn_merge(os_, lses, g):
    S, DA = os_[0].shape
    tr = _tile(S, 256)
    nh = DA // HEAD_DIM
    lph = _lanes_per_head(DA)

    def body(o0, o1, o2, l0, l1, l2, g_ref, a_ref, lse_ref, mix_ref):
        a0, a1, a2 = l0[...], l1[...], l2[...]
        m = jnp.maximum(jnp.maximum(a0, a1), a2)
        e0, e1, e2 = jnp.exp(a0 - m), jnp.exp(a1 - m), jnp.exp(a2 - m)
        den = e0 + e1 + e2
        w0, w1, w2 = e0 / den, e1 / den, e2 / den
        lse_ref[...] = m + jnp.log(den)
        heads = []
        for h in range(nh):
            sl = slice(h * HEAD_DIM, (h + 1) * HEAD_DIM)
            c = slice(h * lph, h * lph + 1)
            heads.append(w0[:, c] * o0[:, sl] + w1[:, c] * o1[:, sl] + w2[:, c] * o2[:, sl])
        a = jnp.concatenate(heads, axis=1)
        a_ref[...] = a
        r = lax.rsqrt(jnp.mean(a * a, axis=-1, keepdims=True) + EPS)
        mix_ref[...] = (a * r * g_ref[...]).astype(mix_ref.dtype)

    blk = pl.BlockSpec((tr, DA), lambda i: (i, 0))
    stat = pl.BlockSpec((tr, STAT_W), lambda i: (i, 0))
    return pl.pallas_call(
        body, name="attn_merge", grid=(S // tr,),
        in_specs=[blk] * 3 + [stat] * 3 + [pl.BlockSpec((1, DA), lambda i: (0, 0))],
        out_specs=[blk, stat, blk],
        out_shape=[_sds((S, DA), F32), _sds((S, STAT_W), F32), _sds((S, DA), BF16)],
        compiler_params=_params(("parallel",)),
    )(*os_, *lses, g)


def attn_out_bwd(a, g, dmix):
    S, DA = a.shape
    tr = PERM_ROWS
    nh = DA // HEAD_DIM
    lph = _lanes_per_head(DA)

    def body(a_ref, g_ref, dy_ref, do_ref, dl_ref, dg_ref, *by_res):
        av = a_ref[...]
        dx, dg = _rms_bwd_math(av, g_ref[...], dy_ref[...])
        dob = dx.astype(do_ref.dtype)
        do_ref[...] = dob
        for d, ref in zip(DILATIONS[1:], by_res):
            _by_residue(dob, d, ref)
        prod = dx * av
        for h in range(nh):
            sl = slice(h * HEAD_DIM, (h + 1) * HEAD_DIM)
            dl_ref[:, h * lph:(h + 1) * lph] = jnp.broadcast_to(jnp.sum(prod[:, sl], axis=-1, keepdims=True), (tr, lph))

        @pl.when(pl.program_id(0) == 0)
        def _():
            dg_ref[...] = dg

        @pl.when(pl.program_id(0) > 0)
        def _():
            dg_ref[...] += dg

    blk = pl.BlockSpec((tr, DA), lambda i: (i, 0))
    vec = pl.BlockSpec((1, DA), lambda i: (0, 0))
    shapes, specs = _residue_outputs(S, DA, BF16)
    out = pl.pallas_call(
        body, name="attn_out_bwd", grid=(S // tr,),
        in_specs=[blk, vec, blk], out_specs=[blk, pl.BlockSpec((tr, STAT_W), lambda i: (i, 0)), vec] + specs,
        out_shape=[_sds((S, DA), BF16), _sds((S, STAT_W), F32), _sds((1, DA), F32)] + shapes,
        compiler_params=_params(("arbitrary",)),
    )(a, g, dmix)
    return [out[0].reshape(1, S, DA)] + list(out[3:]), out[1], out[2]


def attn_bwd_dq(qkv, dob, lse, dl, d, DA):
    L = qkv.shape[1]
    S = L * d
    nh = DA // HEAD_DIM
    lph = _lanes_per_head(DA)
    scale = HEAD_DIM ** -0.5
    TK = TQ + 2 * HALO

    def body(q_ref, kp_ref, kc_ref, kn_ref, vp_ref, vc_ref, vn_ref, do_ref, lse_ref, dl_ref, dq_ref):
        i = pl.program_id(1)
        row = lax.broadcasted_iota(jnp.int32, (TQ, TK), 0)
        col = lax.broadcasted_iota(jnp.int32, (TQ, TK), 1)
        kpos = i * TQ - HALO + col
        mask = (jnp.abs(col - HALO - row) <= N_SIDE) & (kpos >= 0) & (kpos < L)
        for h in range(nh):
            sl = slice(h * HEAD_DIM, (h + 1) * HEAD_DIM)
            k = _cat3((kp_ref, kc_ref, kn_ref), sl)
            v = _cat3((vp_ref, vc_ref, vn_ref), sl)
            s = _dot(q_ref[:, sl], k, NT) * scale
            s = jnp.where(mask, s, NEG)
            p = jnp.exp(s - lse_ref[:, h * lph:h * lph + 1])
            dp = _dot(do_ref[:, sl], v, NT)
            ds = p * (dp - dl_ref[:, h * lph:h * lph + 1])
            dq_ref[:, sl] = _dot(ds.astype(k.dtype), k, NN) * scale

    blk = pl.BlockSpec((TQ, DA), lambda r, i: (i, r))
    stat = pl.BlockSpec((TQ, STAT_W), lambda r, i: (i, r))
    dq = pl.pallas_call(
        body, name="attn_bwd_dq_d%d" % d, grid=(d, L // TQ),
        in_specs=[pl.BlockSpec((None, TQ, DA), lambda r, i: (r, i, 0))] + _halo_specs3(L, DA, 1) + _halo_specs3(L, DA, 2)
        + [pl.BlockSpec((None, TQ, DA), lambda r, i: (r, i, 0)), stat, stat],
        out_specs=blk, out_shape=_sds((L, d * DA), F32),
        compiler_params=_params(("parallel", "parallel")),
    )(*([qkv] * 7), dob, lse.reshape(L, d * STAT_W), dl.reshape(L, d * STAT_W))
    return dq.reshape(S, DA)


def attn_bwd_dkv(qkv, dob, lse, dl, d, DA):
    L = qkv.shape[1]
    S = L * d
    nh = DA // HEAD_DIM
    lph = _lanes_per_head(DA)
    scale = HEAD_DIM ** -0.5
    TR = TQ + 2 * HALO

    def body(k_ref, v_ref, qp, qc, qn, dop, doc, don, lp, lc, ln, dp_, dc_, dn_, dk_ref, dv_ref):
        j = pl.program_id(1)
        row = lax.broadcasted_iota(jnp.int32, (TR, TQ), 0)
        col = lax.broadcasted_iota(jnp.int32, (TR, TQ), 1)
        qpos = j * TQ - HALO + row
        mask = (jnp.abs(col - (row - HALO)) <= N_SIDE) & (qpos >= 0) & (qpos < L)
        for h in range(nh):
            sl = slice(h * HEAD_DIM, (h + 1) * HEAD_DIM)
            q = _cat3((qp, qc, qn), sl)
            do = _cat3((dop, doc, don), sl)
            stat = slice(h * lph, h * lph + 1)
            lse_q = _cat3((lp, lc, ln), stat)
            dl_q = _cat3((dp_, dc_, dn_), stat)
            k = k_ref[:, sl]
            v = v_ref[:, sl]
            s = _dot(q, k, NT) * scale
            s = jnp.where(mask, s, NEG)
            p = jnp.exp(s - lse_q)
            dv_ref[:, sl] = _dot(p.astype(do.dtype), do, TN)
            dp = _dot(do, v, NT)
            ds = p * (dp - dl_q)
            dk_ref[:, sl] = _dot(ds.astype(q.dtype), q, TN) * scale

    blk = pl.BlockSpec((TQ, DA), lambda r, i: (i, r))
    ident = lambda r: r
    dk, dv = pl.pallas_call(
        body, name="attn_bwd_dkv_d%d" % d, grid=(d, L // TQ),
        in_specs=[pl.BlockSpec((None, TQ, DA), lambda r, i: (r, i, 1)), pl.BlockSpec((None, TQ, DA), lambda r, i: (r, i, 2))]
        + _halo_specs3(L, DA, 0) + _halo_specs3(L, DA, 0) + _halo_specs(L, STAT_W, ident) + _halo_specs(L, STAT_W, ident),
        out_specs=[blk, blk], out_shape=[_sds((L, d * DA), F32), _sds((L, d * DA), F32)],
        compiler_params=_params(("parallel", "parallel")),
    )(*([qkv] * 5), *([dob] * 3), *([lse.reshape(L, d * STAT_W)] * 3), *([dl.reshape(L, d * STAT_W)] * 3))
    return dk.reshape(S, DA), dv.reshape(S, DA)


def _gmlp_fwd_math(u_raw, v_raw, ln_g, ws_ref, bcol_ref, n_chunks, ng):
    ug = _gelu(u_raw)
    vg = _gelu(v_raw)
    mu = jnp.mean(vg, axis=-1, keepdims=True)
    xc = vg - mu
    rstd = lax.rsqrt(jnp.mean(xc * xc, axis=-1, keepdims=True) + EPS)
    xhat = xc * rstd
    vn = xhat * ln_g
    rows = []
    for n in range(n_chunks):
        cols = []
        for g in range(ng):
            blk = vn[n * CHUNK:(n + 1) * CHUNK, g * GROUP:(g + 1) * GROUP]
            cols.append(_dot(ws_ref[g].astype(BF16), blk.astype(BF16), NN) + bcol_ref[g])
        rows.append(jnp.concatenate(cols, axis=1))
    mixed = jnp.concatenate(rows, axis=0)
    return ug, xhat, rstd, vn, mixed


def gmlp_fwd(proj, ln_g, w_s, bcol, g_out, DA):
    S = proj.shape[0]
    ng = DA // GROUP
    tr = _tile(S, 2 * CHUNK)

    def body(u_ref, v_ref, ln_ref, ws_ref, bcol_ref, g_ref, G_ref, mix_ref):
        ug, _, _, _, mixed = _gmlp_fwd_math(u_ref[...], v_ref[...], ln_ref[...], ws_ref, bcol_ref, tr // CHUNK, ng)
        G = ug * mixed
        G_ref[...] = G
        r = lax.rsqrt(jnp.mean(G * G, axis=-1, keepdims=True) + EPS)
        mix_ref[...] = (G * r * g_ref[...]).astype(mix_ref.dtype)

    vec = pl.BlockSpec((1, DA), lambda i: (0, 0))
    par = pl.BlockSpec((ng, CHUNK, CHUNK), lambda i: (0, 0, 0))
    blk = pl.BlockSpec((tr, DA), lambda i: (i, 0))
    return pl.pallas_call(
        body, name="gmlp_fwd", grid=(S // tr,),
        in_specs=[pl.BlockSpec((tr, DA), lambda i: (i, 3)), pl.BlockSpec((tr, DA), lambda i: (i, 4)), vec, par, par, vec],
        out_specs=[blk, blk], out_shape=[_sds((S, DA), F32), _sds((S, DA), BF16)],
        compiler_params=_params(("parallel",)),
    )(proj, proj, ln_g, w_s, bcol, g_out)


def gmlp_bwd(proj, ln_g, w_s, w_st, bcol, g_out, dmix, DA):
    S = proj.shape[0]
    ng = DA // GROUP
    tr = _tile(S, 2 * CHUNK)
    nc = tr // CHUNK

    def body(u_ref, v_ref, ln_ref, ws_ref, wst_ref, bcol_ref, g_ref, dy_ref, duv_ref, dg_ref, dln_ref, dws_ref, db_ref):
        u_raw = u_ref[...]
        v_raw = v_ref[...]
        ln_g_v = ln_ref[...]
        ug, xhat, rstd, vn, mixed = _gmlp_fwd_math(u_raw, v_raw, ln_g_v, ws_ref, bcol_ref, nc, ng)
        G = ug * mixed
        dG, dg = _rms_bwd_math(G, g_ref[...], dy_ref[...])
        du_g = dG * mixed
        dmixed = dG * ug
        dws, dbs, rows = [], [], []
        for g in range(ng):
            dws.append(None)
            dbs.append(None)
        for n in range(nc):
            cols = []
            for g in range(ng):
                dm = dmixed[n * CHUNK:(n + 1) * CHUNK, g * GROUP:(g + 1) * GROUP]
                vb = vn[n * CHUNK:(n + 1) * CHUNK, g * GROUP:(g + 1) * GROUP]
                dmb = dm.astype(BF16)
                w = _dot(dmb, vb.astype(BF16), NT)
                b = jnp.broadcast_to(jnp.sum(dm, axis=-1, keepdims=True), (CHUNK, GROUP))
                dws[g] = w if dws[g] is None else dws[g] + w
                dbs[g] = b if dbs[g] is None else dbs[g] + b
                cols.append(_dot(wst_ref[g].astype(BF16), dmb, NN))
            rows.append(jnp.concatenate(cols, axis=1))
        dvn = jnp.concatenate(rows, axis=0)
        dln = jnp.sum(dvn * xhat, axis=0, keepdims=True)
        dxh = dvn * ln_g_v
        dvg = rstd * (dxh - jnp.mean(dxh, axis=-1, keepdims=True) - xhat * jnp.mean(dxh * xhat, axis=-1, keepdims=True))
        duv_ref[:, 0:DA] = (du_g * _gelu_grad(u_raw)).astype(duv_ref.dtype)
        duv_ref[:, DA:2 * DA] = (dvg * _gelu_grad(v_raw)).astype(duv_ref.dtype)

        first = pl.program_id(0) == 0

        @pl.when(first)
        def _():
            dg_ref[...] = dg
            dln_ref[...] = dln
            for g in range(ng):
                dws_ref[g] = dws[g]
                db_ref[g] = dbs[g]

        @pl.when(jnp.logical_not(first))
        def _():
            dg_ref[...] += dg
            dln_ref[...] += dln
            for g in range(ng):
                dws_ref[g] += dws[g]
                db_ref[g] += dbs[g]

    vec = pl.BlockSpec((1, DA), lambda i: (0, 0))
    par = pl.BlockSpec((ng, CHUNK, CHUNK), lambda i: (0, 0, 0))
    return pl.pallas_call(
        body, name="gmlp_bwd", grid=(S // tr,),
        in_specs=[pl.BlockSpec((tr, DA), lambda i: (i, 3)), pl.BlockSpec((tr, DA), lambda i: (i, 4)), vec, par, par, par, vec,
                  pl.BlockSpec((tr, DA), lambda i: (i, 1))],
        out_specs=[pl.BlockSpec((tr, 2 * DA), lambda i: (i, 0)), vec, vec, par, par],
        out_shape=[_sds((S, 2 * DA), BF16), _sds((1, DA), F32), _sds((1, DA), F32),
                   _sds((ng, CHUNK, CHUNK), F32), _sds((ng, CHUNK, CHUNK), F32)],
        compiler_params=_params(("arbitrary",)),
    )(proj, proj, ln_g, w_s, w_st, bcol, g_out, dmix)


def mixer_fwd(x, sm, gw, tabs):
    D = x.shape[1]
    DA = D // 2
    cos2, sin2 = tabs
    h1 = rms_fwd(x, sm["norm1_g"])
    proj = mm_in(h1, gw["w_in"])
    qkv = rope_qkv(proj, cos2, sin2, DA)
    os_, lses = [], []
    for t, d in enumerate(DILATIONS):
        o, l = attn_fwd(qkv[t], d, DA)
        os_.append(o)
        lses.append(l)
    a, lse, mix_a = attn_merge(os_, lses, sm["mix_norm_attn_g"])
    _, mix_g = gmlp_fwd(proj, sm["gmlp_ln_g"], sm["w_spatial"], sm["bcol"], sm["mix_norm_gmlp_g"], DA)
    x2 = mm_out(x, mix_a, mix_g, gw["w_out"])
    return x2, dict(x=x, h1=h1, proj=proj, qkv=qkv, a=a, lse=lse, mix_a=mix_a, mix_g=mix_g)


def ffn_fwd(x2, sm, gw):
    h2 = rms_fwd(x2, sm["norm2_g"])
    dfdu, dfdg, ff = ff_fwd(h2, gw["w_gate"], gw["w_up"])
    x3 = mm_down(x2, ff, gw["w_down"])
    return x3, dict(x2=x2, h2=h2, dfdg=dfdg, dfdu=dfdu, ff=ff)


def ffn_bwd(dx, dxb, sv, sm, gw, dep, on_grads):
    dgate, dup = ff_bwd_act(dxb, gw["w_down"], sv["dfdg"], sv["dfdu"], dep)
    g_down = dw_down(sv["ff"], dxb)
    g_gate, g_up = dw_gate_up(sv["h2"], dgate, dup)
    token = on_grads(dict(w_gate=g_gate, w_up=g_up, w_down=g_down))
    dh2 = dh_ff(dgate, dup, gw["w_gate"], gw["w_up"], token)
    dx2, dx2b, d_norm2 = rms_bwd_res(sv["x2"], sm["norm2_g"], dh2, dx)
    return dx2, dx2b, dict(norm2_g=d_norm2)


def mixer_bwd(dx2, dx2b, sv, sm, gw, tabs, dep, on_grads):
    D = dx2.shape[1]
    DA = D // 2
    cos2, sin2 = tabs
    NS = gw["w_in"].shape[0]
    dmix = dmix_mm(dx2b, gw["w_out"].reshape(D, D), dep)
    g_out_a, g_out_g = dw_out(sv["mix_a"], sv["mix_g"], dx2b)
    dob, dl, d_mix_a = attn_out_bwd(sv["a"], sm["mix_norm_attn_g"], dmix)
    duv, d_mix_g, d_ln, d_ws, d_bs = gmlp_bwd(sv["proj"], sm["gmlp_ln_g"], sm["w_spatial"], sm["w_spatial_t"], sm["bcol"],
                                              sm["mix_norm_gmlp_g"], dmix, DA)
    dqs, dks, dvs = [], [], []
    for t, d in enumerate(DILATIONS):
        dqs.append(attn_bwd_dq(sv["qkv"][t], dob[t], sv["lse"], dl, d, DA))
        dk, dv = attn_bwd_dkv(sv["qkv"][t], dob[t], sv["lse"], dl, d, DA)
        dks.append(dk)
        dvs.append(dv)
    dproj = assemble_dproj(dqs, dks, dvs, duv, cos2, sin2)
    g_in = dw_in(sv["h1"], dproj, NS)
    token = on_grads(dict(w_in=g_in, w_out=jnp.concatenate([g_out_a, g_out_g], axis=0)))
    dh1 = dh_in(dproj, gw["w_in"], token)
    dx0, dx0b, d_norm1 = rms_bwd_res(sv["x"], sm["norm1_g"], dh1, dx2)
    small = dict(norm1_g=d_norm1, gmlp_ln_g=d_ln, w_spatial=d_ws, b_spatial=d_bs[:, :, 0], mix_norm_attn_g=d_mix_a,
                 mix_norm_gmlp_g=d_mix_g)
    return dx0, dx0b, small


def _place():
    x, y, c = lax.axis_index("x"), lax.axis_index("y"), lax.axis_index("c")
    return x, y, c, [(1 - x, y), (x, 1 - y), (1 - x, 1 - y)]


def _remote(src, dst, send_sems, recv_sems, k, to):
    return pltpu.make_async_remote_copy(src_ref=src, dst_ref=dst, send_sem=send_sems.at[k], recv_sem=recv_sems.at[k],
                                        device_id=to, device_id_type=MESH)


def to_slot(src, layer, me_arr, dep, dtype):
    R, C = src.shape[-2:]
    tr = _tile(R, max(16, 4 * ADAMW_BLOCK_ELEMS // C))

    def body(me_ref, src_ref, dep_ref, o_ref):
        o_ref[...] = src_ref[...].astype(o_ref.dtype)

    if layer is None:
        src_spec = pl.BlockSpec((tr, C), lambda i, me_ref: (i, 0))
    else:
        src_spec = pl.BlockSpec((None, tr, C), lambda i, me_ref: (layer, i, 0))
    return pl.pallas_call(
        body, name="to_slot",
        grid_spec=pltpu.PrefetchScalarGridSpec(
            num_scalar_prefetch=1, grid=(R // tr,),
            in_specs=[src_spec, pl.BlockSpec(TOKEN, lambda i, me_ref: (0, 0))],
            out_specs=pl.BlockSpec((None, tr, C), lambda i, me_ref: (me_ref[0], i, 0))),
        out_shape=_sds((N_DEV, R, C), dtype), compiler_params=_params(("parallel",)),
    )(me_arr, src, dep)


def gather_ici(bufs):
    n = len(bufs)

    def body(*refs):
        outs = refs[n:2 * n]
        send_sems, recv_sems = refs[2 * n:]
        x, y, c, chips = _place()
        mine = [o.at[4 * x + 2 * y + c] for o in outs]
        sent = []
        for a in range(n):
            for j, (px, py) in enumerate(chips):
                cp = _remote(mine[a], mine[a], send_sems, recv_sems, 3 * a + j, (px, py, c))
                cp.start()
                sent.append(cp)
        for a in range(n):
            for j, (px, py) in enumerate(chips):
                slot = outs[a].at[4 * px + 2 * py + c]
                _remote(slot, slot, send_sems, recv_sems, 3 * a + j, (px, py, c)).wait_recv()
        for cp in sent:
            cp.wait_send()

    return pl.pallas_call(
        body, name="gather_ici", in_specs=[ANY] * n, out_specs=[ANY] * n,
        out_shape=[_sds(b.shape, b.dtype) for b in bufs], input_output_aliases={a: a for a in range(n)},
        scratch_shapes=[pltpu.SemaphoreType.DMA((3 * n,)), pltpu.SemaphoreType.DMA((3 * n,))],
    )(*bufs)


HBM_SPEC = pl.BlockSpec(memory_space=pltpu.HBM)
SEM_SPEC = pl.BlockSpec(memory_space=pltpu.SEMAPHORE)
DATAFLOW = pltpu.SideEffectType.DATAFLOW_SIDE_EFFECTING
TOKEN = (8, 128)


def _in_hbm(a):
    return pltpu.with_memory_space_constraint(a, pltpu.HBM)


PLAN_COPIES = dict(gather_ici=3, gather_d2d=N_CHIP, scatter_d2d=N_CHIP, scatter_ici=3)


def _plan(kind):
    x, y, c, chips = _place()
    sibling = (x, y, 1 - c)
    if kind == "gather_ici":
        me = 4 * x + 2 * y + c
        return [(me, me, 4 * px + 2 * py + c, (px, py, c)) for px, py in chips]
    if kind == "gather_d2d":
        return [(2 * j + c, 2 * j + c, 2 * j + 1 - c, sibling) for j in range(N_CHIP)]
    if kind == "scatter_d2d":
        return [(2 * j + 1 - c, j, j, sibling) for j in range(N_CHIP)]
    assert kind == "scatter_ici"
    return [(2 * px + py, 2 * x + y, 2 * px + py, (px, py, c)) for px, py in chips]


def split_start(kind, srcs, dsts):
    n = len(srcs)
    bufs = list(srcs) + ([] if dsts is None else list(dsts))
    nb = len(bufs)
    k = PLAN_COPIES[kind]

    def body(*refs):
        ins = refs[:n]
        outs = ins if dsts is None else refs[n:2 * n]
        send_sems, recv_sems = refs[nb], refs[nb + 1]
        token = refs[-1]
        for a in range(n):
            for j, (src, dst, _, peer) in enumerate(_plan(kind)):
                _remote(ins[a].at[src], outs[a].at[dst], send_sems, recv_sems, k * a + j, peer).start()
        token[...] = jnp.zeros_like(token)

    out = pl.pallas_call(
        body, name=kind + "_start",
        out_shape=(pltpu.SemaphoreType.DMA((k * n,)), pltpu.SemaphoreType.DMA((k * n,)),
                   *[pltpu.HBM(b.shape, b.dtype) for b in bufs], _sds(TOKEN, F32)),
        in_specs=[HBM_SPEC] * nb,
        out_specs=(SEM_SPEC, SEM_SPEC, *[HBM_SPEC] * nb, pl.BlockSpec(memory_space=pltpu.VMEM)),
        input_output_aliases={i: 2 + i for i in range(nb)},
        compiler_params=pltpu.CompilerParams(has_side_effects=DATAFLOW),
    )(*[_in_hbm(b) for b in bufs])
    return kind, out[0], out[1], list(out[2:2 + n]), None if dsts is None else list(out[2 + n:2 + nb]), out[-1]


def split_wait(pending, after):
    kind, send_sems, recv_sems, srcs, dsts, _ = pending
    n = len(srcs)
    bufs = list(srcs) + ([] if dsts is None else list(dsts))
    nb = len(bufs)
    k = PLAN_COPIES[kind]

    def body(*refs):
        ins = refs[:n]
        outs = ins if dsts is None else refs[n:2 * n]
        send_sems, recv_sems = refs[nb], refs[nb + 1]
        for a in range(n):
            for j, (src, _, landed, peer) in enumerate(_plan(kind)):
                cp = _remote(ins[a].at[src], outs[a].at[landed], send_sems, recv_sems, k * a + j, peer)
                cp.wait_send()
                cp.wait_recv()

    out = pl.pallas_call(
        body, name=kind + "_wait",
        out_shape=[pltpu.HBM(b.shape, b.dtype) for b in bufs],
        in_specs=[HBM_SPEC] * nb + [SEM_SPEC, SEM_SPEC, ANY], out_specs=[HBM_SPEC] * nb,
        input_output_aliases={i: i for i in range(nb)},
        compiler_params=pltpu.CompilerParams(has_side_effects=DATAFLOW),
    )(*bufs, send_sems, recv_sems, after)
    return list(out) if dsts is None else (list(out[:n]), list(out[n:]))


def gather_d2d(bufs):
    n = len(bufs)

    def body(*refs):
        outs = refs[n:2 * n]
        token = refs[2 * n]
        send_sems, recv_sems = refs[2 * n + 1:]
        x, y, c, _ = _place()
        sent = []
        for a in range(n):
            for j in range(N_CHIP):
                slot = outs[a].at[2 * j + c]
                cp = _remote(slot, slot, send_sems, recv_sems, N_CHIP * a + j, (x, y, 1 - c))
                cp.start()
                sent.append(cp)
        for a in range(n):
            for j in range(N_CHIP):
                slot = outs[a].at[2 * j + 1 - c]
                _remote(slot, slot, send_sems, recv_sems, N_CHIP * a + j, (x, y, 1 - c)).wait_recv()
        for cp in sent:
            cp.wait_send()
        token[...] = jnp.zeros_like(token)

    out = pl.pallas_call(
        body, name="gather_d2d", in_specs=[ANY] * n,
        out_specs=[ANY] * n + [pl.BlockSpec(memory_space=pltpu.VMEM)],
        out_shape=[_sds(b.shape, b.dtype) for b in bufs] + [_sds(TOKEN, F32)],
        input_output_aliases={a: a for a in range(n)},
        scratch_shapes=[pltpu.SemaphoreType.DMA((N_CHIP * n,)), pltpu.SemaphoreType.DMA((N_CHIP * n,))],
    )(*bufs)
    return list(out[:n]), out[n]


def pair_sum(part, got, c_arr, chip_arr):
    _, R, C = part.shape
    tr = _tile(R, 512)
    p4 = part.reshape(N_CHIP, 2, R, C)

    def body(c_ref, chip_ref, p_ref, g_ref, o_ref, own_ref):
        s = (p_ref[...].astype(F32) + g_ref[...].astype(F32)).astype(o_ref.dtype)
        o_ref[...] = s

        @pl.when(pl.program_id(1) == chip_ref[0])
        def _():
            own_ref[...] = s

    return pl.pallas_call(
        body, name="pair_sum",
        grid_spec=pltpu.PrefetchScalarGridSpec(
            num_scalar_prefetch=2, grid=(R // tr, N_CHIP),
            in_specs=[pl.BlockSpec((None, None, tr, C), lambda i, j, c_ref, chip_ref: (j, c_ref[0], i, 0)),
                      pl.BlockSpec((None, tr, C), lambda i, j, c_ref, chip_ref: (j, i, 0))],
            out_specs=[pl.BlockSpec((None, tr, C), lambda i, j, c_ref, chip_ref: (j, i, 0)),
                       pl.BlockSpec((None, tr, C), lambda i, j, c_ref, chip_ref: (chip_ref[0], i, 0))]),
        out_shape=[_sds((N_CHIP, R, C), part.dtype), _sds((N_CHIP, R, C), part.dtype)],
        compiler_params=_params(("parallel", "arbitrary")),
    )(c_arr, chip_arr, p4, got)


def adamw(w, m, v, parts_per_layer):
    NL, R, C = w.shape
    P = parts_per_layer[0].shape[0]
    tr = _tile(R, max(16, ADAMW_BLOCK_ELEMS // C))
    nb = R // tr

    def body(w_ref, m_ref, v_ref, *rest):
        part_refs = rest[:NL]
        g_ref, d_ref, nm_ref, nv_ref = rest[NL:]
        layer = pl.program_id(0)
        g = None
        for q in range(NL):
            s = part_refs[q][0].astype(F32)
            for t in range(1, P):
                s = s + part_refs[q][t].astype(F32)
            s = jnp.where(layer == q, s, 0.0)
            g = s if g is None else g + s
        wv = w_ref[...]
        nm = ADAM_B1 * m_ref[...] + (1.0 - ADAM_B1) * g
        nv = ADAM_B2 * v_ref[...] + (1.0 - ADAM_B2) * (g * g)
        m_hat = nm / (1.0 - ADAM_B1 ** ADAM_STEP)
        v_hat = nv / (1.0 - ADAM_B2 ** ADAM_STEP)
        g_ref[...] = g
        d_ref[...] = -ADAM_LR * (m_hat / (jnp.sqrt(v_hat) + ADAM_EPS) + ADAM_WD * wv)
        nm_ref[...] = nm
        nv_ref[...] = nv

    def part_spec(q):
        return pl.BlockSpec((P, tr, C), lambda l, i: (0, jnp.where(l == q, i, jnp.where(l < q, 0, nb - 1)), 0))

    blk = pl.BlockSpec((None, tr, C), lambda l, i: (l, i, 0))
    return pl.pallas_call(
        body, name="adamw", grid=(NL, nb),
        in_specs=[blk, blk, blk] + [part_spec(q) for q in range(NL)],
        out_specs=[blk, blk, blk, blk], out_shape=[_sds((NL, R, C), F32)] * 4,
        compiler_params=_params(("arbitrary", "arbitrary")),
    )(w, m, v, *parts_per_layer)


SMALL = ("norm1_g", "gmlp_ln_g", "w_spatial", "b_spatial", "mix_norm_attn_g", "mix_norm_gmlp_g", "norm2_g")
BIG = ("w_in", "w_out", "w_gate", "w_up", "w_down")
TRANSPOSED = ("w_gate", "w_up")
GROUPS = (("w_in", "w_out"), ("w_gate", "w_up", "w_down"))
LANES = 128


def _pack(layers, final):
    flat = [layer[n].reshape(-1) for layer in layers for n in SMALL] + [final.reshape(-1)]
    return jnp.concatenate(flat).reshape(-1, LANES)


def _unpack(packed, like_layers, like_final):
    flat = packed.reshape(-1)
    out, off = [], 0
    for layer in like_layers:
        d = {}
        for n in SMALL:
            size = layer[n].size
            d[n] = flat[off:off + size].reshape(layer[n].shape)
            off += size
        out.append(d)
    return out, flat[off:off + like_final.size].reshape(like_final.shape)


def kernel(x, norm1_g, w_in, gmlp_ln_g, w_spatial, b_spatial, mix_norm_attn_g, mix_norm_gmlp_g, w_out, norm2_g, w_gate, w_up, w_down, final_g, loss_target, m_norm1_g, m_w_in, m_gmlp_ln_g, m_w_spatial, m_b_spatial, m_mix_norm_attn_g, m_mix_norm_gmlp_g, m_w_out, m_norm2_g, m_w_gate, m_w_up, m_w_down, m_final_g, v_norm1_g, v_w_in, v_gmlp_ln_g, v_w_spatial, v_b_spatial, v_mix_norm_attn_g, v_mix_norm_gmlp_g, v_w_out, v_norm2_g, v_w_gate, v_w_up, v_w_down, v_final_g):
    S, D = x.shape[1], x.shape[2]
    NL = norm1_g.shape[0]
    DA = D // 2
    xs = x.reshape(S, D)
    tabs = rope_tables(S)
    ax, ay, ac = lax.axis_index("x"), lax.axis_index("y"), lax.axis_index("c")
    me_arr = (4 * ax + 2 * ay + ac).astype(jnp.int32).reshape(1)
    c_arr = ac.astype(jnp.int32).reshape(1)
    chip_arr = (2 * ax + ay).astype(jnp.int32).reshape(1)
    small_w = dict(norm1_g=norm1_g, gmlp_ln_g=gmlp_ln_g, w_spatial=w_spatial, b_spatial=b_spatial,
                   mix_norm_attn_g=mix_norm_attn_g, mix_norm_gmlp_g=mix_norm_gmlp_g, norm2_g=norm2_g)
    small_m = dict(norm1_g=m_norm1_g, gmlp_ln_g=m_gmlp_ln_g, w_spatial=m_w_spatial, b_spatial=m_b_spatial,
                   mix_norm_attn_g=m_mix_norm_attn_g, mix_norm_gmlp_g=m_mix_norm_gmlp_g, norm2_g=m_norm2_g)
    small_v = dict(norm1_g=v_norm1_g, gmlp_ln_g=v_gmlp_ln_g, w_spatial=v_w_spatial, b_spatial=v_b_spatial,
                   mix_norm_attn_g=v_mix_norm_attn_g, mix_norm_gmlp_g=v_mix_norm_gmlp_g, norm2_g=v_norm2_g)
    def view(n, a):
        return jnp.swapaxes(a, 1, 2) if n in TRANSPOSED else a

    big_w = {n: view(n, a) for n, a in dict(w_in=w_in, w_out=w_out, w_gate=w_gate, w_up=w_up, w_down=w_down).items()}
    big_m = {n: view(n, a) for n, a in dict(w_in=m_w_in, w_out=m_w_out, w_gate=m_w_gate, w_up=m_w_up, w_down=m_w_down).items()}
    big_v = {n: view(n, a) for n, a in dict(w_in=v_w_in, w_out=v_w_out, w_gate=v_w_gate, w_up=v_w_up, w_down=v_w_down).items()}

    def layer_small(l):
        ws = w_spatial[l]
        return dict(norm1_g=norm1_g[l][None], gmlp_ln_g=gmlp_ln_g[l][None], mix_norm_attn_g=mix_norm_attn_g[l][None],
                    mix_norm_gmlp_g=mix_norm_gmlp_g[l][None], norm2_g=norm2_g[l][None], w_spatial=ws,
                    w_spatial_t=jnp.swapaxes(ws, 1, 2), bcol=jnp.broadcast_to(b_spatial[l][:, :, None], ws.shape))

    n_stages = 2 * NL
    zero = jnp.zeros(TOKEN, F32)
    TOK = 5

    def stage_bufs(s, dep):
        return [to_slot(big_w[n], s // 2, me_arr, dep, BF16) for n in GROUPS[s % 2]]

    def stage_weights(s, bufs):
        gw = dict(zip(GROUPS[s % 2], bufs))
        if s % 2 == 0:
            gw["w_out"] = gw["w_out"].reshape(2, DA, D)
        else:
            gw = {n: b.reshape(-1, D) for n, b in gw.items()}
        return gw

    ici = {0: split_start("gather_ici", stage_bufs(0, zero), None)}
    ici[1] = split_start("gather_ici", stage_bufs(1, ici[0][TOK]), None)
    h = xs
    d2d = {0: split_start("gather_d2d", split_wait(ici[0], h), None)}
    ready = {0: split_wait(d2d[0], h)}
    saved, weights = [], []
    for s in range(n_stages):
        deps = [ici[1][TOK]] if s == 0 else []
        if 1 <= s < n_stages - 1:
            d2d[s + 1] = split_start("gather_d2d", split_wait(ici[s + 1], h), None)
            deps.append(d2d[s + 1][TOK])
        if s + 2 < n_stages:
            ici[s + 2] = split_start("gather_ici", stage_bufs(s + 2, deps[-1]), None)
            deps.append(ici[s + 2][TOK])
        dep = sum(deps[1:], deps[0]) if deps else zero
        gw = stage_weights(s, ready[s])
        sm = layer_small(s // 2)
        if s % 2 == 0:
            sm["norm1_g"] = sm["norm1_g"] + dep[0, 0]
            h, sv = mixer_fwd(h, sm, gw, tabs)
        else:
            sm["norm2_g"] = sm["norm2_g"] + dep[0, 0]
            h, sv = ffn_fwd(h, sm, gw)
        saved.append(sv)
        weights.append(gw)
        if s == 0:
            d2d[1] = split_start("gather_d2d", split_wait(ici[1], h), None)
        if s + 1 < n_stages:
            ready[s + 1] = split_wait(d2d[s + 1], h)
    loss_part, dx, dxb, d_final = loss_and_grad(h, final_g[None], loss_target.reshape(S, D))
    loss = lax.psum(loss_part[0, 0], ("x", "y", "c"))

    big_sums = [dict() for _ in range(NL)]
    small_grads = [dict() for _ in range(NL)]
    pending, pending_stage = None, None
    dep = zero
    for s in reversed(range(n_stages)):
        l = s // 2
        crossing = []

        def on_grads(big, s=s, crossing=crossing):
            parts = [big[n].reshape((N_DEV, -1, big[n].shape[-1])) for n in GROUPS[s % 2]]
            gots = [lax.empty((N_CHIP,) + p.shape[1:], p.dtype) for p in parts]
            crossing.append(split_start("scatter_d2d", parts, gots))
            return crossing[0][TOK]

        if s % 2 == 1:
            dx, dxb, small = ffn_bwd(dx, dxb, saved[s], layer_small(l), weights[s], dep, on_grads)
        else:
            dx, dxb, small = mixer_bwd(dx, dxb, saved[s], layer_small(l), weights[s], tabs, dep, on_grads)
        small_grads[l].update(small)
        if pending is not None:
            big_sums[pending_stage // 2].update(zip(GROUPS[pending_stage % 2], split_wait(pending, dx)[1]))
        parts, got = split_wait(crossing[0], dx)
        pairs = [pair_sum(p, g, c_arr, chip_arr) for p, g in zip(parts, got)]
        pending, pending_stage = split_start("scatter_ici", [t for t, _ in pairs], [o for _, o in pairs]), s
        dep = pending[TOK]

    def update_big(n):
        R, C = big_w[n].shape[1], big_w[n].shape[2]
        res = adamw(big_w[n], big_m[n], big_v[n], [big_sums[l][n].reshape(N_CHIP, R, C) for l in range(NL)])
        return [view(n, t) for t in res], res[0][0, :TOKEN[0], :TOKEN[1]]

    big_out = {}
    after_ffn = zero
    for n in GROUPS[1]:
        big_out[n], piece = update_big(n)
        after_ffn = after_ffn + piece

    packed = to_slot(_pack(small_grads, d_final), None, me_arr, dep + after_ffn, F32)
    gathered_small = gather_d2d(gather_ici([packed]))[0][0]
    pw = _pack([{n: small_w[n][l] for n in SMALL} for l in range(NL)], final_g)[None]
    pm = _pack([{n: small_m[n][l] for n in SMALL} for l in range(NL)], m_final_g)[None]
    pv = _pack([{n: small_v[n][l] for n in SMALL} for l in range(NL)], v_final_g)[None]
    like_layers = [{n: small_w[n][l] for n in SMALL} for l in range(NL)]
    small_res = adamw(pw, pm, pv, [gathered_small])
    small_out = [_unpack(t[0], like_layers, final_g) for t in small_res]
    big_sums[pending_stage // 2].update(zip(GROUPS[pending_stage % 2], split_wait(pending, small_res[0])[1]))

    def small_stack(k, n):
        return jnp.stack([small_out[k][0][l][n] for l in range(NL)])

    for n in GROUPS[0]:
        big_out[n], _ = update_big(n)

    order = ("norm1_g", "w_in", "gmlp_ln_g", "w_spatial", "b_spatial", "mix_norm_attn_g", "mix_norm_gmlp_g", "w_out",
             "norm2_g", "w_gate", "w_up", "w_down")
    outs = [loss, dx.reshape(x.shape)]
    for k in range(4):
        for n in order:
            outs.append(big_out[n][k] if n in BIG else small_stack(k, n))
        outs.append(small_out[k][1])
    return tuple(outs)
```

```python
import functools

import jax
import jax.numpy as jnp
from jax import lax
from jax.experimental import pallas as pl
from jax.experimental.pallas import tpu as pltpu

F32 = jnp.float32
BF16 = jnp.bfloat16

HEAD_DIM = 128
CHUNK = 128
GROUP = 128
N_SIDE = 64
DILATIONS = (1, 4, 16)
TQ = 128
HALO = 64
EPS = 1e-6
NEG = -1e30
ROPE_THETA = 10000.0
ADAM_LR = 0.001
ADAM_B1 = 0.9
ADAM_B2 = 0.999
ADAM_EPS = 1e-08
ADAM_WD = 0.01
ADAM_STEP = 10
N_DEV = 8
N_CHIP = 4
VMEM_LIMIT_V7X = 56 * 1024 * 1024
ADAMW_BLOCK_ELEMS = 128 * 1024
MESH = pl.DeviceIdType.MESH
ANY = pl.BlockSpec(memory_space=pl.ANY)


def _params(sem=None):
    return pltpu.CompilerParams(dimension_semantics=sem, vmem_limit_bytes=VMEM_LIMIT_V7X)


def _tile(n, want):
    for t in range(min(want, n), 15, -1):
        if n % t == 0 and t % 16 == 0:
            return t
    return n


def _sds(shape, dtype):
    return jax.ShapeDtypeStruct(shape, dtype)


def _dot(a, b, dims):
    return lax.dot_general(a, b, (dims, ((), ())), preferred_element_type=F32)


NN = ((1,), (0,))
NT = ((1,), (1,))
TN = ((0,), (0,))


def _sigmoid(x):
    return 1.0 / (1.0 + jnp.exp(-x))


def _gelu(x):
    return 0.5 * x * (1.0 + lax.erf(x * 0.7071067811865476))


def _gelu_grad(x):
    cdf = 0.5 * (1.0 + lax.erf(x * 0.7071067811865476))
    pdf = 0.3989422804014327 * jnp.exp(-0.5 * x * x)
    return cdf + x * pdf


def rms_fwd(x, g):
    S, D = x.shape
    tr = _tile(S, 512)

    def body(x_ref, g_ref, o_ref):
        xv = x_ref[...]
        r = lax.rsqrt(jnp.mean(xv * xv, axis=-1, keepdims=True) + EPS)
        o_ref[...] = (xv * r * g_ref[...]).astype(o_ref.dtype)

    return pl.pallas_call(
        body, name="rms_fwd", grid=(S // tr,),
        in_specs=[pl.BlockSpec((tr, D), lambda i: (i, 0)), pl.BlockSpec((1, D), lambda i: (0, 0))],
        out_specs=pl.BlockSpec((tr, D), lambda i: (i, 0)),
        out_shape=_sds((S, D), BF16), compiler_params=_params(("parallel",)),
    )(x, g)


def _rms_bwd_math(xv, gv, dy):
    r = lax.rsqrt(jnp.mean(xv * xv, axis=-1, keepdims=True) + EPS)
    dyg = dy * gv
    dx = r * dyg - xv * (r * r * r) * jnp.mean(dyg * xv, axis=-1, keepdims=True)
    dg = jnp.sum(dy * xv * r, axis=0, keepdims=True)
    return dx, dg


def rms_bwd_res(x, g, dy, dres):
    S, D = x.shape
    tr = _tile(S, 512)

    def body(x_ref, g_ref, dy_ref, dres_ref, dx_ref, dxb_ref, dg_ref):
        dx, dg = _rms_bwd_math(x_ref[...], g_ref[...], dy_ref[...])
        tot = dres_ref[...] + dx
        dx_ref[...] = tot
        dxb_ref[...] = tot.astype(dxb_ref.dtype)

        @pl.when(pl.program_id(0) == 0)
        def _():
            dg_ref[...] = dg

        @pl.when(pl.program_id(0) > 0)
        def _():
            dg_ref[...] += dg

    row = pl.BlockSpec((tr, D), lambda i: (i, 0))
    vec = pl.BlockSpec((1, D), lambda i: (0, 0))
    return pl.pallas_call(
        body, name="rms_bwd_res", grid=(S // tr,),
        in_specs=[row, vec, row, row], out_specs=[row, row, vec],
        out_shape=[_sds((S, D), F32), _sds((S, D), BF16), _sds((1, D), F32)],
        compiler_params=_params(("arbitrary",)),
    )(x, g, dy, dres)


def loss_and_grad(x, g, target):
    S, D = x.shape
    tr = _tile(S, 512)

    def body(x_ref, g_ref, t_ref, loss_ref, dx_ref, dxb_ref, dg_ref):
        xv = x_ref[...]
        gv = g_ref[...]
        r = lax.rsqrt(jnp.mean(xv * xv, axis=-1, keepdims=True) + EPS)
        y = xv * r * gv
        diff = y - t_ref[...]
        part = 0.5 * jnp.sum(jnp.mean(diff * diff, axis=-1, keepdims=True), axis=0, keepdims=True)
        dy = diff * (1.0 / D)
        dx, dg = _rms_bwd_math(xv, gv, dy)
        dx_ref[...] = dx
        dxb_ref[...] = dx.astype(dxb_ref.dtype)

        @pl.when(pl.program_id(0) == 0)
        def _():
            dg_ref[...] = dg
            loss_ref[...] = jnp.broadcast_to(part, loss_ref.shape)

        @pl.when(pl.program_id(0) > 0)
        def _():
            dg_ref[...] += dg
            loss_ref[...] += jnp.broadcast_to(part, loss_ref.shape)

    row = pl.BlockSpec((tr, D), lambda i: (i, 0))
    vec = pl.BlockSpec((1, D), lambda i: (0, 0))
    return pl.pallas_call(
        body, name="loss_and_grad", grid=(S // tr,),
        in_specs=[row, vec, row],
        out_specs=[pl.BlockSpec((8, 128), lambda i: (0, 0)), row, row, vec],
        out_shape=[_sds((8, 128), F32), _sds((S, D), F32), _sds((S, D), BF16), _sds((1, D), F32)],
        compiler_params=_params(("arbitrary",)),
    )(x, g, target)


def _mm_body(n_pairs, dims, red_axis, n_red, has_res):
    def body(*refs):
        ins = refs[:2 * n_pairs]
        res_ref = refs[2 * n_pairs] if has_res else None
        o_ref = refs[2 * n_pairs + has_res]
        p = None
        for t in range(n_pairs):
            d = _dot(ins[2 * t][...], ins[2 * t + 1][...], dims)
            p = d if p is None else p + d
        if red_axis is None:
            if has_res:
                p = res_ref[...] + p
            o_ref[...] = p.astype(o_ref.dtype)
            return
        acc_ref = o_ref if o_ref.dtype == F32 else refs[2 * n_pairs + has_res + 1]
        r = pl.program_id(red_axis)

        @pl.when(r == 0)
        def _():
            acc_ref[...] = res_ref[...] + p if has_res else p

        @pl.when(r > 0)
        def _():
            acc_ref[...] += p

        if acc_ref is not o_ref:
            @pl.when(r == n_red - 1)
            def _():
                o_ref[...] = acc_ref[...].astype(o_ref.dtype)

    return body


def mm_in(h, w):
    S, D = h.shape
    NS, _, n = w.shape
    tm = _tile(S, 1024)
    return pl.pallas_call(
        _mm_body(1, NN, None, 1, False), name="mm_in", grid=(S // tm, NS),
        in_specs=[pl.BlockSpec((tm, D), lambda i, k: (i, 0)), pl.BlockSpec((None, D, n), lambda i, k: (k, 0, 0))],
        out_specs=pl.BlockSpec((tm, n), lambda i, k: (i, k)),
        out_shape=_sds((S, NS * n), F32), compiler_params=_params(("parallel", "parallel")),
    )(h, w)


def mm_out(x, mix_a, mix_g, w):
    S, D = x.shape
    DA = mix_a.shape[1]
    tm = _tile(S, 1024)
    tn = _tile(D, 1024)
    act = pl.BlockSpec((tm, DA), lambda i, j: (i, 0))
    return pl.pallas_call(
        _mm_body(2, NN, None, 1, True), name="mm_out", grid=(S // tm, D // tn),
        in_specs=[act, pl.BlockSpec((None, DA, tn), lambda i, j: (0, 0, j)),
                  act, pl.BlockSpec((None, DA, tn), lambda i, j: (1, 0, j)),
                  pl.BlockSpec((tm, tn), lambda i, j: (i, j))],
        out_specs=pl.BlockSpec((tm, tn), lambda i, j: (i, j)),
        out_shape=_sds((S, D), F32), compiler_params=_params(("parallel", "parallel")),
    )(mix_a, w, mix_g, w, x)


def _lane_tile(n, want):
    for t in range(min(want, n) // 128 * 128, 127, -128):
        if n % t == 0:
            return t
    return n


def ff_fwd(h, wg, wu):
    S, D = h.shape
    F = wg.shape[0]
    tm = _tile(S, 1024)
    tn = _lane_tile(F, 512)

    def body(h_ref, wg_ref, wu_ref, dfdu_ref, dfdg_ref, ff_ref):
        hv = h_ref[...]
        g = _dot(hv, wg_ref[...], NT)
        u = _dot(hv, wu_ref[...], NT)
        sg = _sigmoid(g)
        act = g * sg
        dfdu_ref[...] = act.astype(dfdu_ref.dtype)
        dfdg_ref[...] = (u * (sg * (1.0 + g * (1.0 - sg)))).astype(dfdg_ref.dtype)
        ff_ref[...] = (act * u).astype(ff_ref.dtype)

    wspec = pl.BlockSpec((tn, D), lambda i, j: (j, 0))
    ospec = pl.BlockSpec((tm, tn), lambda i, j: (i, j))
    return pl.pallas_call(
        body, name="ff_fwd", grid=(S // tm, F // tn),
        in_specs=[pl.BlockSpec((tm, D), lambda i, j: (i, 0)), wspec, wspec],
        out_specs=[ospec, ospec, ospec],
        out_shape=[_sds((S, F), BF16), _sds((S, F), BF16), _sds((S, F), BF16)],
        compiler_params=_params(("parallel", "parallel")),
    )(h, wg, wu)


def mm_down(x, ff, wd):
    S, D = x.shape
    F = ff.shape[1]
    tm = _tile(S, 1024)
    tn = _lane_tile(D, 512)
    blk = pl.BlockSpec((tm, tn), lambda i, j: (i, j))
    return pl.pallas_call(
        _mm_body(1, NN, None, 1, True), name="mm_down", grid=(S // tm, D // tn),
        in_specs=[pl.BlockSpec((tm, F), lambda i, j: (i, 0)), pl.BlockSpec((F, tn), lambda i, j: (0, j)), blk],
        out_specs=blk, out_shape=_sds((S, D), F32), compiler_params=_params(("parallel", "parallel")),
    )(ff, wd, x)


def ff_bwd_act(dxb, wd, dfdg, dfdu, dep):
    S, D = dxb.shape
    F = wd.shape[0]
    tm = _tile(S, 1024)
    tn = _lane_tile(F, 512)

    def body(dx_ref, wd_ref, dfdg_ref, dfdu_ref, dep_ref, dgate_ref, dup_ref):
        dff = _dot(dx_ref[...], wd_ref[...], NT)
        dup_ref[...] = (dff * dfdu_ref[...].astype(F32)).astype(dup_ref.dtype)
        dgate_ref[...] = (dff * dfdg_ref[...].astype(F32)).astype(dgate_ref.dtype)

    aspec = pl.BlockSpec((tm, tn), lambda i, j: (i, j))
    return pl.pallas_call(
        body, name="ff_bwd_act", grid=(S // tm, F // tn),
        in_specs=[pl.BlockSpec((tm, D), lambda i, j: (i, 0)), pl.BlockSpec((tn, D), lambda i, j: (j, 0)), aspec, aspec,
                  pl.BlockSpec(TOKEN, lambda i, j: (0, 0))],
        out_specs=[aspec, aspec],
        out_shape=[_sds((S, F), BF16), _sds((S, F), BF16)],
        compiler_params=_params(("parallel", "parallel")),
    )(dxb, wd, dfdg, dfdu, dep)


def dw_down(ff, dxb):
    S, F = ff.shape
    D = dxb.shape[1]
    tf = _lane_tile(F, 256)
    return pl.pallas_call(
        _mm_body(1, TN, None, 1, False), name="dw_down", grid=(F // tf,),
        in_specs=[pl.BlockSpec((S, tf), lambda k: (0, k)), pl.BlockSpec((S, D), lambda k: (0, 0))],
        out_specs=pl.BlockSpec((tf, D), lambda k: (k, 0)),
        out_shape=_sds((F, D), BF16), compiler_params=_params(("parallel",)),
    )(ff, dxb)


def dw_gate_up(h, dgate, dup):
    S, D = h.shape
    F = dgate.shape[1]
    f = _lane_tile(F, 256)

    def body(h_ref, dg_ref, du_ref, og_ref, ou_ref):
        hv = h_ref[...]
        og_ref[...] = _dot(dg_ref[...], hv, TN).astype(og_ref.dtype)
        ou_ref[...] = _dot(du_ref[...], hv, TN).astype(ou_ref.dtype)

    aspec = pl.BlockSpec((S, f), lambda k: (0, k))
    ospec = pl.BlockSpec((f, D), lambda k: (k, 0))
    return pl.pallas_call(
        body, name="dw_gate_up", grid=(F // f,),
        in_specs=[pl.BlockSpec((S, D), lambda k: (0, 0)), aspec, aspec],
        out_specs=[ospec, ospec],
        out_shape=[_sds((F, D), BF16), _sds((F, D), BF16)],
        compiler_params=_params(("parallel",)),
    )(h, dgate, dup)


def dh_ff(dgate, dup, wg, wu, dep):
    S, F = dgate.shape
    D = wg.shape[1]
    tm = _tile(S, 512)
    tn = _lane_tile(D, 256)

    def body(dg_ref, wg_ref, du_ref, wu_ref, dep_ref, o_ref):
        o_ref[...] = _dot(dg_ref[...], wg_ref[...], NN) + _dot(du_ref[...], wu_ref[...], NN)

    aspec = pl.BlockSpec((tm, F), lambda i, j: (i, 0))
    wspec = pl.BlockSpec((F, tn), lambda i, j: (0, j))
    return pl.pallas_call(
        body, name="dh_ff", grid=(S // tm, D // tn),
        in_specs=[aspec, wspec, aspec, wspec, pl.BlockSpec(TOKEN, lambda i, j: (0, 0))],
        out_specs=pl.BlockSpec((tm, tn), lambda i, j: (i, j)),
        out_shape=_sds((S, D), F32), compiler_params=_params(("parallel", "parallel")),
    )(dgate, wg, dup, wu, dep)


def dmix_mm(dxb, w, dep):
    S, D = dxb.shape
    tm = _tile(S, 1024)
    tn = _tile(D, 1024)

    def body(dx_ref, w_ref, dep_ref, o_ref):
        o_ref[...] = _dot(dx_ref[...], w_ref[...], NT)

    return pl.pallas_call(
        body, name="dmix_mm", grid=(S // tm, D // tn),
        in_specs=[pl.BlockSpec((tm, D), lambda i, j: (i, 0)), pl.BlockSpec((tn, D), lambda i, j: (j, 0)),
                  pl.BlockSpec(TOKEN, lambda i, j: (0, 0))],
        out_specs=pl.BlockSpec((tm, tn), lambda i, j: (i, j)),
        out_shape=_sds((S, D), F32), compiler_params=_params(("parallel", "parallel")),
    )(dxb, w, dep)


def dw_out(mix_a, mix_g, dxb):
    S, DA = mix_a.shape
    D = dxb.shape[1]
    tn = _lane_tile(D, 512)

    def half(name, m):
        return pl.pallas_call(
            _mm_body(1, TN, None, 1, False), name=name, grid=(D // tn,),
            in_specs=[pl.BlockSpec((S, DA), lambda j: (0, 0)), pl.BlockSpec((S, tn), lambda j: (0, j))],
            out_specs=pl.BlockSpec((DA, tn), lambda j: (0, j)),
            out_shape=_sds((DA, D), BF16), compiler_params=_params(("parallel",)),
        )(m, dxb)

    return half("dw_out_a", mix_a), half("dw_out_g", mix_g)


def dw_in(h, dproj, NS):
    S, D = h.shape
    n = dproj.shape[1] // NS
    tm = _lane_tile(D, 1024)
    return pl.pallas_call(
        _mm_body(1, TN, None, 1, False), name="dw_in", grid=(NS, D // tm),
        in_specs=[pl.BlockSpec((S, tm), lambda k, j: (0, j)), pl.BlockSpec((S, n), lambda k, j: (0, k))],
        out_specs=pl.BlockSpec((None, tm, n), lambda k, j: (k, j, 0)),
        out_shape=_sds((NS, D, n), BF16), compiler_params=_params(("parallel", "parallel")),
    )(h, dproj)


def dh_in(dproj, w, dep):
    S = dproj.shape[0]
    NS, D, n = w.shape
    tm = _tile(S, 512)
    tn = _tile(D, 512)

    def body(dp_ref, w_ref, dep_ref, o_ref):
        acc = None
        for k in range(NS):
            p = _dot(dp_ref[:, k * n:(k + 1) * n], w_ref[k], NT)
            acc = p if acc is None else acc + p
        o_ref[...] = acc

    return pl.pallas_call(
        body, name="dh_in", grid=(S // tm, D // tn),
        in_specs=[pl.BlockSpec((tm, NS * n), lambda i, j: (i, 0)), pl.BlockSpec((NS, tn, n), lambda i, j: (0, j, 0)),
                  pl.BlockSpec(TOKEN, lambda i, j: (0, 0))],
        out_specs=pl.BlockSpec((tm, tn), lambda i, j: (i, j)),
        out_shape=_sds((S, D), F32), compiler_params=_params(("parallel", "parallel")),
    )(dproj, w, dep)


def rope_tables(S):
    pos = jnp.arange(S, dtype=F32)
    inv = ROPE_THETA ** (-jnp.arange(0, HEAD_DIM, 2, dtype=F32) / HEAD_DIM)
    ang = pos[:, None] * inv[None, :]
    cos, sin = jnp.cos(ang), jnp.sin(ang)
    return jnp.concatenate([cos, cos], axis=-1), jnp.concatenate([-sin, sin], axis=-1)


def _rot_half(t):
    return pltpu.roll(t, HEAD_DIM // 2, 1)


PERM_ROWS = 256


def _by_residue(y, d, out_ref):
    tr = y.shape[0]
    n = tr // d
    o = lax.broadcasted_iota(jnp.int32, (tr, tr), 0)
    i = lax.broadcasted_iota(jnp.int32, (tr, tr), 1)
    src = jnp.bitwise_and(o, n - 1) * d + lax.shift_right_logical(o, n.bit_length() - 1)
    z = _dot((src == i).astype(y.dtype), y, NN).astype(out_ref.dtype)
    for r in range(d):
        out_ref[r] = z[r * n:(r + 1) * n]


def _from_residue(ref, d, terms):
    z = jnp.concatenate([ref[r] for r in range(d)], axis=0)
    tr = z.shape[0]
    n = tr // d
    i = lax.broadcasted_iota(jnp.int32, (tr, tr), 0)
    o = lax.broadcasted_iota(jnp.int32, (tr, tr), 1)
    src = jnp.bitwise_and(i, d - 1) * n + lax.shift_right_logical(i, d.bit_length() - 1)
    pick = (src == o).astype(BF16)
    out = None
    rest = z
    for _ in range(terms):
        piece = rest.astype(BF16)
        got = _dot(pick, piece, NN)
        out = got if out is None else out + got
        rest = rest - piece.astype(F32)
    return out


def _residue_blocks(widths):
    return [pl.BlockSpec((d, PERM_ROWS // d, W), lambda i: (0, i, 0)) for d, W in zip(DILATIONS[1:], widths)]


def _residue_outputs(S, W, dtype):
    shapes = [_sds((d, S // d, W), dtype) for d in DILATIONS[1:]]
    specs = [pl.BlockSpec((d, PERM_ROWS // d, W), lambda i: (0, i, 0)) for d in DILATIONS[1:]]
    return shapes, specs


def rope_qkv(proj, cos2, sin2, DA):
    S = proj.shape[0]
    tr = PERM_ROWS
    nh = DA // HEAD_DIM

    def body(q_ref, k_ref, v_ref, cos_ref, sin_ref, o_ref, *by_res):
        c = cos_ref[...]
        s = sin_ref[...]
        for h in range(nh):
            sl = slice(h * HEAD_DIM, (h + 1) * HEAD_DIM)
            for j, ref in enumerate((q_ref, k_ref)):
                t = ref[:, sl]
                o_ref[:, j * DA + h * HEAD_DIM:j * DA + (h + 1) * HEAD_DIM] = (t * c + _rot_half(t) * s).astype(o_ref.dtype)
        o_ref[:, 2 * DA:3 * DA] = v_ref[...].astype(o_ref.dtype)
        y = o_ref[...]
        for d, ref in zip(DILATIONS[1:], by_res):
            _by_residue(y, d, ref)

    tab = pl.BlockSpec((tr, HEAD_DIM), lambda i: (i, 0))
    shapes, specs = _residue_outputs(S, 3 * DA, BF16)
    out = pl.pallas_call(
        body, name="rope_qkv", grid=(S // tr,),
        in_specs=[pl.BlockSpec((tr, DA), lambda i: (i, 0)), pl.BlockSpec((tr, DA), lambda i: (i, 1)),
                  pl.BlockSpec((tr, DA), lambda i: (i, 2)), tab, tab],
        out_specs=[pl.BlockSpec((tr, 3 * DA), lambda i: (i, 0))] + specs,
        out_shape=[_sds((S, 3 * DA), BF16)] + shapes, compiler_params=_params(("parallel",)),
    )(proj, proj, proj, cos2, sin2)
    return [out[0].reshape(1, S, 3 * DA)] + list(out[1:])


def assemble_dproj(dqs, dks, dvs, duv, cos2, sin2):
    S, DA = dqs[0].shape[1:]
    tr = PERM_ROWS
    nh = DA // HEAD_DIM

    def total(trio):
        t = trio[0][...]
        for d, ref in zip(DILATIONS[1:], trio[1:]):
            t = t + _from_residue(ref, d, 2)
        return t

    def body(*refs):
        dq_refs, dk_refs, dv_refs = refs[0:3], refs[3:6], refs[6:9]
        duv_ref, cos_ref, sin_ref, o_ref = refs[9:13]
        c = cos_ref[...]
        s = sin_ref[...]
        for j, trio in enumerate((dq_refs, dk_refs)):
            t_all = total(trio)
            for h in range(nh):
                t = t_all[:, h * HEAD_DIM:(h + 1) * HEAD_DIM]
                o_ref[:, j * DA + h * HEAD_DIM:j * DA + (h + 1) * HEAD_DIM] = (t * c - _rot_half(t) * s).astype(o_ref.dtype)
        o_ref[:, 2 * DA:3 * DA] = total(dv_refs).astype(o_ref.dtype)
        o_ref[:, 3 * DA:5 * DA] = duv_ref[...]

    trio_specs = [pl.BlockSpec((None, tr, DA), lambda i: (0, i, 0))] + _residue_blocks([DA, DA])
    tab = pl.BlockSpec((tr, HEAD_DIM), lambda i: (i, 0))
    return pl.pallas_call(
        body, name="assemble_dproj", grid=(S // tr,),
        in_specs=trio_specs * 3 + [pl.BlockSpec((tr, 2 * DA), lambda i: (i, 0)), tab, tab],
        out_specs=pl.BlockSpec((tr, 5 * DA), lambda i: (i, 0)),
        out_shape=_sds((S, 5 * DA), BF16), compiler_params=_params(("parallel",)),
    )(*dqs, *dks, *dvs, duv, cos2, sin2)


STAT_W = 128


def _lanes_per_head(DA):
    return STAT_W // (DA // HEAD_DIM)


def _halo_specs(L, width_blocks, col):
    per = TQ // HALO
    last = L // HALO - 1
    W = width_blocks
    return [
        pl.BlockSpec((HALO, W), lambda r, i: (jnp.maximum(i * per - 1, 0), col(r))),
        pl.BlockSpec((TQ, W), lambda r, i: (i, col(r))),
        pl.BlockSpec((HALO, W), lambda r, i: (jnp.minimum((i + 1) * per, last), col(r))),
    ]


def _halo_specs3(L, W, c):
    per = TQ // HALO
    last = L // HALO - 1
    return [
        pl.BlockSpec((None, HALO, W), lambda r, i: (r, jnp.maximum(i * per - 1, 0), c)),
        pl.BlockSpec((None, TQ, W), lambda r, i: (r, i, c)),
        pl.BlockSpec((None, HALO, W), lambda r, i: (r, jnp.minimum((i + 1) * per, last), c)),
    ]


def _cat3(refs, sl):
    return jnp.concatenate([refs[0][:, sl], refs[1][:, sl], refs[2][:, sl]], axis=0)


def attn_fwd(qkv, d, DA):
    L = qkv.shape[1]
    S = L * d
    nh = DA // HEAD_DIM
    lph = _lanes_per_head(DA)
    scale = HEAD_DIM ** -0.5
    TK = TQ + 2 * HALO

    def body(q_ref, kp_ref, kc_ref, kn_ref, vp_ref, vc_ref, vn_ref, o_ref, lse_ref):
        i = pl.program_id(1)
        row = lax.broadcasted_iota(jnp.int32, (TQ, TK), 0)
        col = lax.broadcasted_iota(jnp.int32, (TQ, TK), 1)
        kpos = i * TQ - HALO + col
        mask = (jnp.abs(col - HALO - row) <= N_SIDE) & (kpos >= 0) & (kpos < L)
        for h in range(nh):
            sl = slice(h * HEAD_DIM, (h + 1) * HEAD_DIM)
            k = _cat3((kp_ref, kc_ref, kn_ref), sl)
            v = _cat3((vp_ref, vc_ref, vn_ref), sl)
            s = _dot(q_ref[:, sl], k, NT) * scale
            s = jnp.where(mask, s, NEG)
            m = jnp.max(s, axis=-1, keepdims=True)
            p = jnp.exp(s - m)
            l = jnp.sum(p, axis=-1, keepdims=True)
            o = _dot(p.astype(v.dtype), v, NN) / l
            o_ref[:, sl] = o
            lse_ref[:, h * lph:(h + 1) * lph] = jnp.broadcast_to(m + jnp.log(l), (TQ, lph))

    out_spec = pl.BlockSpec((None, TQ, DA), lambda r, i: (r, i, 0))
    o, lse = pl.pallas_call(
        body, name="attn_fwd_d%d" % d, grid=(d, L // TQ),
        in_specs=[pl.BlockSpec((None, TQ, DA), lambda r, i: (r, i, 0))] + _halo_specs3(L, DA, 1) + _halo_specs3(L, DA, 2),
        out_specs=[out_spec, pl.BlockSpec((TQ, STAT_W), lambda r, i: (i, r))],
        out_shape=[_sds((d, L, DA), F32), _sds((L, d * STAT_W), F32)],
        compiler_params=_params(("parallel", "parallel")),
    )(*([qkv] * 7))
    return o, lse.reshape(S, STAT_W)


def attn_merge(os_, lses, g):
    S, DA = os_[0].shape[1:]
    tr = PERM_ROWS
    nh = DA // HEAD_DIM
    lph = _lanes_per_head(DA)

    def body(o0, o1, o2, l0, l1, l2, g_ref, a_ref, lse_ref, mix_ref):
        a0, a1, a2 = l0[...], l1[...], l2[...]
        m = jnp.maximum(jnp.maximum(a0, a1), a2)
        e0, e1, e2 = jnp.exp(a0 - m), jnp.exp(a1 - m), jnp.exp(a2 - m)
        den = e0 + e1 + e2
        w0, w1, w2 = e0 / den, e1 / den, e2 / den
        lse_ref[...] = m + jnp.log(den)
        v0 = o0[...]
        v1 = _from_residue(o1, DILATIONS[1], 2)
        v2 = _from_residue(o2, DILATIONS[2], 2)
        heads = []
        for h in range(nh):
            sl = slice(h * HEAD_DIM, (h + 1) * HEAD_DIM)
            c = slice(h * lph, h * lph + 1)
            heads.append(w0[:, c] * v0[:, sl] + w1[:, c] * v1[:, sl] + w2[:, c] * v2[:, sl])
        a = jnp.concatenate(heads, axis=1)
        a_ref[...] = a
        r = lax.rsqrt(jnp.mean(a * a, axis=-1, keepdims=True) + EPS)
        mix_ref[...] = (a * r * g_ref[...]).astype(mix_ref.dtype)

    blk = pl.BlockSpec((tr, DA), lambda i: (i, 0))
    stat = pl.BlockSpec((tr, STAT_W), lambda i: (i, 0))
    return pl.pallas_call(
        body, name="attn_merge", grid=(S // tr,),
        in_specs=[pl.BlockSpec((None, tr, DA), lambda i: (0, i, 0))] + _residue_blocks([DA, DA]) + [stat] * 3
        + [pl.BlockSpec((1, DA), lambda i: (0, 0))],
        out_specs=[blk, stat, blk],
        out_shape=[_sds((S, DA), F32), _sds((S, STAT_W), F32), _sds((S, DA), BF16)],
        compiler_params=_params(("parallel",)),
    )(*os_, *lses, g)


def attn_out_bwd(a, g, dmix):
    S, DA = a.shape
    tr = PERM_ROWS
    nh = DA // HEAD_DIM
    lph = _lanes_per_head(DA)

    def body(a_ref, g_ref, dy_ref, do_ref, dl_ref, dg_ref, *by_res):
        av = a_ref[...]
        dx, dg = _rms_bwd_math(av, g_ref[...], dy_ref[...])
        dob = dx.astype(do_ref.dtype)
        do_ref[...] = dob
        for d, ref in zip(DILATIONS[1:], by_res):
            _by_residue(dob, d, ref)
        prod = dx * av
        for h in range(nh):
            sl = slice(h * HEAD_DIM, (h + 1) * HEAD_DIM)
            dl_ref[:, h * lph:(h + 1) * lph] = jnp.broadcast_to(jnp.sum(prod[:, sl], axis=-1, keepdims=True), (tr, lph))

        @pl.when(pl.program_id(0) == 0)
        def _():
            dg_ref[...] = dg

        @pl.when(pl.program_id(0) > 0)
        def _():
            dg_ref[...] += dg

    blk = pl.BlockSpec((tr, DA), lambda i: (i, 0))
    vec = pl.BlockSpec((1, DA), lambda i: (0, 0))
    shapes, specs = _residue_outputs(S, DA, BF16)
    out = pl.pallas_call(
        body, name="attn_out_bwd", grid=(S // tr,),
        in_specs=[blk, vec, blk], out_specs=[blk, pl.BlockSpec((tr, STAT_W), lambda i: (i, 0)), vec] + specs,
        out_shape=[_sds((S, DA), BF16), _sds((S, STAT_W), F32), _sds((1, DA), F32)] + shapes,
        compiler_params=_params(("arbitrary",)),
    )(a, g, dmix)
    return [out[0].reshape(1, S, DA)] + list(out[3:]), out[1], out[2]


def attn_bwd_dq(qkv, dob, lse, dl, d, DA):
    L = qkv.shape[1]
    S = L * d
    nh = DA // HEAD_DIM
    lph = _lanes_per_head(DA)
    scale = HEAD_DIM ** -0.5
    TK = TQ + 2 * HALO

    def body(q_ref, kp_ref, kc_ref, kn_ref, vp_ref, vc_ref, vn_ref, do_ref, lse_ref, dl_ref, dq_ref):
        i = pl.program_id(1)
        row = lax.broadcasted_iota(jnp.int32, (TQ, TK), 0)
        col = lax.broadcasted_iota(jnp.int32, (TQ, TK), 1)
        kpos = i * TQ - HALO + col
        mask = (jnp.abs(col - HALO - row) <= N_SIDE) & (kpos >= 0) & (kpos < L)
        for h in range(nh):
            sl = slice(h * HEAD_DIM, (h + 1) * HEAD_DIM)
            k = _cat3((kp_ref, kc_ref, kn_ref), sl)
            v = _cat3((vp_ref, vc_ref, vn_ref), sl)
            s = _dot(q_ref[:, sl], k, NT) * scale
            s = jnp.where(mask, s, NEG)
            p = jnp.exp(s - lse_ref[:, h * lph:h * lph + 1])
            dp = _dot(do_ref[:, sl], v, NT)
            ds = p * (dp - dl_ref[:, h * lph:h * lph + 1])
            dq_ref[:, sl] = _dot(ds.astype(k.dtype), k, NN) * scale

    blk = pl.BlockSpec((None, TQ, DA), lambda r, i: (r, i, 0))
    stat = pl.BlockSpec((TQ, STAT_W), lambda r, i: (i, r))
    return pl.pallas_call(
        body, name="attn_bwd_dq_d%d" % d, grid=(d, L // TQ),
        in_specs=[blk] + _halo_specs3(L, DA, 1) + _halo_specs3(L, DA, 2) + [blk, stat, stat],
        out_specs=blk, out_shape=_sds((d, L, DA), F32),
        compiler_params=_params(("parallel", "parallel")),
    )(*([qkv] * 7), dob, lse.reshape(L, d * STAT_W), dl.reshape(L, d * STAT_W))


def attn_bwd_dkv(qkv, dob, lse, dl, d, DA):
    L = qkv.shape[1]
    S = L * d
    nh = DA // HEAD_DIM
    lph = _lanes_per_head(DA)
    scale = HEAD_DIM ** -0.5
    TR = TQ + 2 * HALO

    def body(k_ref, v_ref, qp, qc, qn, dop, doc, don, lp, lc, ln, dp_, dc_, dn_, dk_ref, dv_ref):
        j = pl.program_id(1)
        row = lax.broadcasted_iota(jnp.int32, (TR, TQ), 0)
        col = lax.broadcasted_iota(jnp.int32, (TR, TQ), 1)
        qpos = j * TQ - HALO + row
        mask = (jnp.abs(col - (row - HALO)) <= N_SIDE) & (qpos >= 0) & (qpos < L)
        for h in range(nh):
            sl = slice(h * HEAD_DIM, (h + 1) * HEAD_DIM)
            q = _cat3((qp, qc, qn), sl)
            do = _cat3((dop, doc, don), sl)
            stat = slice(h * lph, h * lph + 1)
            lse_q = _cat3((lp, lc, ln), stat)
            dl_q = _cat3((dp_, dc_, dn_), stat)
            k = k_ref[:, sl]
            v = v_ref[:, sl]
            s = _dot(q, k, NT) * scale
            s = jnp.where(mask, s, NEG)
            p = jnp.exp(s - lse_q)
            dv_ref[:, sl] = _dot(p.astype(do.dtype), do, TN)
            dp = _dot(do, v, NT)
            ds = p * (dp - dl_q)
            dk_ref[:, sl] = _dot(ds.astype(q.dtype), q, TN) * scale

    blk = pl.BlockSpec((None, TQ, DA), lambda r, i: (r, i, 0))
    ident = lambda r: r
    dk, dv = pl.pallas_call(
        body, name="attn_bwd_dkv_d%d" % d, grid=(d, L // TQ),
        in_specs=[pl.BlockSpec((None, TQ, DA), lambda r, i: (r, i, 1)), pl.BlockSpec((None, TQ, DA), lambda r, i: (r, i, 2))]
        + _halo_specs3(L, DA, 0) + _halo_specs3(L, DA, 0) + _halo_specs(L, STAT_W, ident) + _halo_specs(L, STAT_W, ident),
        out_specs=[blk, blk], out_shape=[_sds((d, L, DA), F32), _sds((d, L, DA), F32)],
        compiler_params=_params(("parallel", "parallel")),
    )(*([qkv] * 5), *([dob] * 3), *([lse.reshape(L, d * STAT_W)] * 3), *([dl.reshape(L, d * STAT_W)] * 3))
    return dk, dv


def _gmlp_fwd_math(u_raw, v_raw, ln_g, ws_ref, bcol_ref, n_chunks, ng):
    ug = _gelu(u_raw)
    vg = _gelu(v_raw)
    mu = jnp.mean(vg, axis=-1, keepdims=True)
    xc = vg - mu
    rstd = lax.rsqrt(jnp.mean(xc * xc, axis=-1, keepdims=True) + EPS)
    xhat = xc * rstd
    vn = xhat * ln_g
    rows = []
    for n in range(n_chunks):
        cols = []
        for g in range(ng):
            blk = vn[n * CHUNK:(n + 1) * CHUNK, g * GROUP:(g + 1) * GROUP]
            cols.append(_dot(ws_ref[g].astype(BF16), blk.astype(BF16), NN) + bcol_ref[g])
        rows.append(jnp.concatenate(cols, axis=1))
    mixed = jnp.concatenate(rows, axis=0)
    return ug, xhat, rstd, vn, mixed


def gmlp_fwd(proj, ln_g, w_s, bcol, g_out, DA):
    S = proj.shape[0]
    ng = DA // GROUP
    tr = _tile(S, 2 * CHUNK)

    def body(u_ref, v_ref, ln_ref, ws_ref, bcol_ref, g_ref, G_ref, mix_ref):
        ug, _, _, _, mixed = _gmlp_fwd_math(u_ref[...], v_ref[...], ln_ref[...], ws_ref, bcol_ref, tr // CHUNK, ng)
        G = ug * mixed
        G_ref[...] = G
        r = lax.rsqrt(jnp.mean(G * G, axis=-1, keepdims=True) + EPS)
        mix_ref[...] = (G * r * g_ref[...]).astype(mix_ref.dtype)

    vec = pl.BlockSpec((1, DA), lambda i: (0, 0))
    par = pl.BlockSpec((ng, CHUNK, CHUNK), lambda i: (0, 0, 0))
    blk = pl.BlockSpec((tr, DA), lambda i: (i, 0))
    return pl.pallas_call(
        body, name="gmlp_fwd", grid=(S // tr,),
        in_specs=[pl.BlockSpec((tr, DA), lambda i: (i, 3)), pl.BlockSpec((tr, DA), lambda i: (i, 4)), vec, par, par, vec],
        out_specs=[blk, blk], out_shape=[_sds((S, DA), F32), _sds((S, DA), BF16)],
        compiler_params=_params(("parallel",)),
    )(proj, proj, ln_g, w_s, bcol, g_out)


def gmlp_bwd(proj, ln_g, w_s, w_st, bcol, g_out, dmix, DA):
    S = proj.shape[0]
    ng = DA // GROUP
    tr = _tile(S, 2 * CHUNK)
    nc = tr // CHUNK

    def body(u_ref, v_ref, ln_ref, ws_ref, wst_ref, bcol_ref, g_ref, dy_ref, duv_ref, dg_ref, dln_ref, dws_ref, db_ref):
        u_raw = u_ref[...]
        v_raw = v_ref[...]
        ln_g_v = ln_ref[...]
        ug, xhat, rstd, vn, mixed = _gmlp_fwd_math(u_raw, v_raw, ln_g_v, ws_ref, bcol_ref, nc, ng)
        G = ug * mixed
        dG, dg = _rms_bwd_math(G, g_ref[...], dy_ref[...])
        du_g = dG * mixed
        dmixed = dG * ug
        dws, dbs, rows = [], [], []
        for g in range(ng):
            dws.append(None)
            dbs.append(None)
        for n in range(nc):
            cols = []
            for g in range(ng):
                dm = dmixed[n * CHUNK:(n + 1) * CHUNK, g * GROUP:(g + 1) * GROUP]
                vb = vn[n * CHUNK:(n + 1) * CHUNK, g * GROUP:(g + 1) * GROUP]
                dmb = dm.astype(BF16)
                w = _dot(dmb, vb.astype(BF16), NT)
                b = jnp.broadcast_to(jnp.sum(dm, axis=-1, keepdims=True), (CHUNK, GROUP))
                dws[g] = w if dws[g] is None else dws[g] + w
                dbs[g] = b if dbs[g] is None else dbs[g] + b
                cols.append(_dot(wst_ref[g].astype(BF16), dmb, NN))
            rows.append(jnp.concatenate(cols, axis=1))
        dvn = jnp.concatenate(rows, axis=0)
        dln = jnp.sum(dvn * xhat, axis=0, keepdims=True)
        dxh = dvn * ln_g_v
        dvg = rstd * (dxh - jnp.mean(dxh, axis=-1, keepdims=True) - xhat * jnp.mean(dxh * xhat, axis=-1, keepdims=True))
        duv_ref[:, 0:DA] = (du_g * _gelu_grad(u_raw)).astype(duv_ref.dtype)
        duv_ref[:, DA:2 * DA] = (dvg * _gelu_grad(v_raw)).astype(duv_ref.dtype)

        first = pl.program_id(0) == 0

        @pl.when(first)
        def _():
            dg_ref[...] = dg
            dln_ref[...] = dln
            for g in range(ng):
                dws_ref[g] = dws[g]
                db_ref[g] = dbs[g]

        @pl.when(jnp.logical_not(first))
        def _():
            dg_ref[...] += dg
            dln_ref[...] += dln
            for g in range(ng):
                dws_ref[g] += dws[g]
                db_ref[g] += dbs[g]

    vec = pl.BlockSpec((1, DA), lambda i: (0, 0))
    par = pl.BlockSpec((ng, CHUNK, CHUNK), lambda i: (0, 0, 0))
    return pl.pallas_call(
        body, name="gmlp_bwd", grid=(S // tr,),
        in_specs=[pl.BlockSpec((tr, DA), lambda i: (i, 3)), pl.BlockSpec((tr, DA), lambda i: (i, 4)), vec, par, par, par, vec,
                  pl.BlockSpec((tr, DA), lambda i: (i, 1))],
        out_specs=[pl.BlockSpec((tr, 2 * DA), lambda i: (i, 0)), vec, vec, par, par],
        out_shape=[_sds((S, 2 * DA), BF16), _sds((1, DA), F32), _sds((1, DA), F32),
                   _sds((ng, CHUNK, CHUNK), F32), _sds((ng, CHUNK, CHUNK), F32)],
        compiler_params=_params(("arbitrary",)),
    )(proj, proj, ln_g, w_s, w_st, bcol, g_out, dmix)


def mixer_fwd(x, sm, gw, tabs):
    D = x.shape[1]
    DA = D // 2
    cos2, sin2 = tabs
    h1 = rms_fwd(x, sm["norm1_g"])
    proj = mm_in(h1, gw["w_in"])
    qkv = rope_qkv(proj, cos2, sin2, DA)
    os_, lses = [], []
    for t, d in enumerate(DILATIONS):
        o, l = attn_fwd(qkv[t], d, DA)
        os_.append(o)
        lses.append(l)
    a, lse, mix_a = attn_merge(os_, lses, sm["mix_norm_attn_g"])
    _, mix_g = gmlp_fwd(proj, sm["gmlp_ln_g"], sm["w_spatial"], sm["bcol"], sm["mix_norm_gmlp_g"], DA)
    x2 = mm_out(x, mix_a, mix_g, gw["w_out"])
    return x2, dict(x=x, h1=h1, proj=proj, qkv=qkv, a=a, lse=lse, mix_a=mix_a, mix_g=mix_g)


def ffn_fwd(x2, sm, gw):
    h2 = rms_fwd(x2, sm["norm2_g"])
    dfdu, dfdg, ff = ff_fwd(h2, gw["w_gate"], gw["w_up"])
    x3 = mm_down(x2, ff, gw["w_down"])
    return x3, dict(x2=x2, h2=h2, dfdg=dfdg, dfdu=dfdu, ff=ff)


def ffn_bwd(dx, dxb, sv, sm, gw, dep, on_grads):
    dgate, dup = ff_bwd_act(dxb, gw["w_down"], sv["dfdg"], sv["dfdu"], dep)
    g_down = dw_down(sv["ff"], dxb)
    g_gate, g_up = dw_gate_up(sv["h2"], dgate, dup)
    token = on_grads(dict(w_gate=g_gate, w_up=g_up, w_down=g_down))
    dh2 = dh_ff(dgate, dup, gw["w_gate"], gw["w_up"], token)
    dx2, dx2b, d_norm2 = rms_bwd_res(sv["x2"], sm["norm2_g"], dh2, dx)
    return dx2, dx2b, dict(norm2_g=d_norm2)


def mixer_bwd(dx2, dx2b, sv, sm, gw, tabs, dep, on_grads):
    D = dx2.shape[1]
    DA = D // 2
    cos2, sin2 = tabs
    NS = gw["w_in"].shape[0]
    dmix = dmix_mm(dx2b, gw["w_out"].reshape(D, D), dep)
    g_out_a, g_out_g = dw_out(sv["mix_a"], sv["mix_g"], dx2b)
    dob, dl, d_mix_a = attn_out_bwd(sv["a"], sm["mix_norm_attn_g"], dmix)
    duv, d_mix_g, d_ln, d_ws, d_bs = gmlp_bwd(sv["proj"], sm["gmlp_ln_g"], sm["w_spatial"], sm["w_spatial_t"], sm["bcol"],
                                              sm["mix_norm_gmlp_g"], dmix, DA)
    dqs, dks, dvs = [], [], []
    for t, d in enumerate(DILATIONS):
        dqs.append(attn_bwd_dq(sv["qkv"][t], dob[t], sv["lse"], dl, d, DA))
        dk, dv = attn_bwd_dkv(sv["qkv"][t], dob[t], sv["lse"], dl, d, DA)
        dks.append(dk)
        dvs.append(dv)
    dproj = assemble_dproj(dqs, dks, dvs, duv, cos2, sin2)
    g_in = dw_in(sv["h1"], dproj, NS)
    token = on_grads(dict(w_in=g_in, w_out=jnp.concatenate([g_out_a, g_out_g], axis=0)))
    dh1 = dh_in(dproj, gw["w_in"], token)
    dx0, dx0b, d_norm1 = rms_bwd_res(sv["x"], sm["norm1_g"], dh1, dx2)
    small = dict(norm1_g=d_norm1, gmlp_ln_g=d_ln, w_spatial=d_ws, b_spatial=d_bs[:, :, 0], mix_norm_attn_g=d_mix_a,
                 mix_norm_gmlp_g=d_mix_g)
    return dx0, dx0b, small


def _place():
    x, y, c = lax.axis_index("x"), lax.axis_index("y"), lax.axis_index("c")
    return x, y, c, [(1 - x, y), (x, 1 - y), (1 - x, 1 - y)]


def _remote(src, dst, send_sems, recv_sems, k, to):
    return pltpu.make_async_remote_copy(src_ref=src, dst_ref=dst, send_sem=send_sems.at[k], recv_sem=recv_sems.at[k],
                                        device_id=to, device_id_type=MESH)


def to_slot(src, layer, me_arr, dep, dtype):
    R, C = src.shape[-2:]
    tr = _tile(R, max(16, 4 * ADAMW_BLOCK_ELEMS // C))

    def body(me_ref, src_ref, dep_ref, o_ref):
        o_ref[...] = src_ref[...].astype(o_ref.dtype)

    if layer is None:
        src_spec = pl.BlockSpec((tr, C), lambda i, me_ref: (i, 0))
    else:
        src_spec = pl.BlockSpec((None, tr, C), lambda i, me_ref: (layer, i, 0))
    return pl.pallas_call(
        body, name="to_slot",
        grid_spec=pltpu.PrefetchScalarGridSpec(
            num_scalar_prefetch=1, grid=(R // tr,),
            in_specs=[src_spec, pl.BlockSpec(TOKEN, lambda i, me_ref: (0, 0))],
            out_specs=pl.BlockSpec((None, tr, C), lambda i, me_ref: (me_ref[0], i, 0))),
        out_shape=_sds((N_DEV, R, C), dtype), compiler_params=_params(("parallel",)),
    )(me_arr, src, dep)


def gather_ici(bufs):
    n = len(bufs)

    def body(*refs):
        outs = refs[n:2 * n]
        send_sems, recv_sems = refs[2 * n:]
        x, y, c, chips = _place()
        mine = [o.at[4 * x + 2 * y + c] for o in outs]
        sent = []
        for a in range(n):
            for j, (px, py) in enumerate(chips):
                cp = _remote(mine[a], mine[a], send_sems, recv_sems, 3 * a + j, (px, py, c))
                cp.start()
                sent.append(cp)
        for a in range(n):
            for j, (px, py) in enumerate(chips):
                slot = outs[a].at[4 * px + 2 * py + c]
                _remote(slot, slot, send_sems, recv_sems, 3 * a + j, (px, py, c)).wait_recv()
        for cp in sent:
            cp.wait_send()

    return pl.pallas_call(
        body, name="gather_ici", in_specs=[ANY] * n, out_specs=[ANY] * n,
        out_shape=[_sds(b.shape, b.dtype) for b in bufs], input_output_aliases={a: a for a in range(n)},
        scratch_shapes=[pltpu.SemaphoreType.DMA((3 * n,)), pltpu.SemaphoreType.DMA((3 * n,))],
    )(*bufs)


HBM_SPEC = pl.BlockSpec(memory_space=pltpu.HBM)
SEM_SPEC = pl.BlockSpec(memory_space=pltpu.SEMAPHORE)
DATAFLOW = pltpu.SideEffectType.DATAFLOW_SIDE_EFFECTING
TOKEN = (8, 128)


def _in_hbm(a):
    return pltpu.with_memory_space_constraint(a, pltpu.HBM)


PLAN_COPIES = dict(gather_ici=3, gather_d2d=N_CHIP, scatter_d2d=N_CHIP, scatter_ici=3)


def _plan(kind):
    x, y, c, chips = _place()
    sibling = (x, y, 1 - c)
    if kind == "gather_ici":
        me = 4 * x + 2 * y + c
        return [(me, me, 4 * px + 2 * py + c, (px, py, c)) for px, py in chips]
    if kind == "gather_d2d":
        return [(2 * j + c, 2 * j + c, 2 * j + 1 - c, sibling) for j in range(N_CHIP)]
    if kind == "scatter_d2d":
        return [(2 * j + 1 - c, j, j, sibling) for j in range(N_CHIP)]
    assert kind == "scatter_ici"
    return [(2 * px + py, 2 * x + y, 2 * px + py, (px, py, c)) for px, py in chips]


def split_start(kind, srcs, dsts):
    n = len(srcs)
    bufs = list(srcs) + ([] if dsts is None else list(dsts))
    nb = len(bufs)
    k = PLAN_COPIES[kind]

    def body(*refs):
        ins = refs[:n]
        outs = ins if dsts is None else refs[n:2 * n]
        send_sems, recv_sems = refs[nb], refs[nb + 1]
        token = refs[-1]
        for a in range(n):
            for j, (src, dst, _, peer) in enumerate(_plan(kind)):
                _remote(ins[a].at[src], outs[a].at[dst], send_sems, recv_sems, k * a + j, peer).start()
        token[...] = jnp.zeros_like(token)

    out = pl.pallas_call(
        body, name=kind + "_start",
        out_shape=(pltpu.SemaphoreType.DMA((k * n,)), pltpu.SemaphoreType.DMA((k * n,)),
                   *[pltpu.HBM(b.shape, b.dtype) for b in bufs], _sds(TOKEN, F32)),
        in_specs=[HBM_SPEC] * nb,
        out_specs=(SEM_SPEC, SEM_SPEC, *[HBM_SPEC] * nb, pl.BlockSpec(memory_space=pltpu.VMEM)),
        input_output_aliases={i: 2 + i for i in range(nb)},
        compiler_params=pltpu.CompilerParams(has_side_effects=DATAFLOW),
    )(*[_in_hbm(b) for b in bufs])
    return kind, out[0], out[1], list(out[2:2 + n]), None if dsts is None else list(out[2 + n:2 + nb]), out[-1]


def split_wait(pending, after):
    kind, send_sems, recv_sems, srcs, dsts, _ = pending
    n = len(srcs)
    bufs = list(srcs) + ([] if dsts is None else list(dsts))
    nb = len(bufs)
    k = PLAN_COPIES[kind]

    def body(*refs):
        ins = refs[:n]
        outs = ins if dsts is None else refs[n:2 * n]
        send_sems, recv_sems = refs[nb], refs[nb + 1]
        for a in range(n):
            for j, (src, _, landed, peer) in enumerate(_plan(kind)):
                cp = _remote(ins[a].at[src], outs[a].at[landed], send_sems, recv_sems, k * a + j, peer)
                cp.wait_send()
                cp.wait_recv()

    out = pl.pallas_call(
        body, name=kind + "_wait",
        out_shape=[pltpu.HBM(b.shape, b.dtype) for b in bufs],
        in_specs=[HBM_SPEC] * nb + [SEM_SPEC, SEM_SPEC, ANY], out_specs=[HBM_SPEC] * nb,
        input_output_aliases={i: i for i in range(nb)},
        compiler_params=pltpu.CompilerParams(has_side_effects=DATAFLOW),
    )(*bufs, send_sems, recv_sems, after)
    return list(out) if dsts is None else (list(out[:n]), list(out[n:]))


def gather_d2d(bufs):
    n = len(bufs)

    def body(*refs):
        outs = refs[n:2 * n]
        token = refs[2 * n]
        send_sems, recv_sems = refs[2 * n + 1:]
        x, y, c, _ = _place()
        sent = []
        for a in range(n):
            for j in range(N_CHIP):
                slot = outs[a].at[2 * j + c]
                cp = _remote(slot, slot, send_sems, recv_sems, N_CHIP * a + j, (x, y, 1 - c))
                cp.start()
                sent.append(cp)
        for a in range(n):
            for j in range(N_CHIP):
                slot = outs[a].at[2 * j + 1 - c]
                _remote(slot, slot, send_sems, recv_sems, N_CHIP * a + j, (x, y, 1 - c)).wait_recv()
        for cp in sent:
            cp.wait_send()
        token[...] = jnp.zeros_like(token)

    out = pl.pallas_call(
        body, name="gather_d2d", in_specs=[ANY] * n,
        out_specs=[ANY] * n + [pl.BlockSpec(memory_space=pltpu.VMEM)],
        out_shape=[_sds(b.shape, b.dtype) for b in bufs] + [_sds(TOKEN, F32)],
        input_output_aliases={a: a for a in range(n)},
        scratch_shapes=[pltpu.SemaphoreType.DMA((N_CHIP * n,)), pltpu.SemaphoreType.DMA((N_CHIP * n,))],
    )(*bufs)
    return list(out[:n]), out[n]


def pair_sum(part, got, c_arr, chip_arr):
    _, R, C = part.shape
    tr = _tile(R, 512)
    p4 = part.reshape(N_CHIP, 2, R, C)

    def body(c_ref, chip_ref, p_ref, g_ref, o_ref, own_ref):
        s = (p_ref[...].astype(F32) + g_ref[...].astype(F32)).astype(o_ref.dtype)
        o_ref[...] = s

        @pl.when(pl.program_id(1) == chip_ref[0])
        def _():
            own_ref[...] = s

    return pl.pallas_call(
        body, name="pair_sum",
        grid_spec=pltpu.PrefetchScalarGridSpec(
            num_scalar_prefetch=2, grid=(R // tr, N_CHIP),
            in_specs=[pl.BlockSpec((None, None, tr, C), lambda i, j, c_ref, chip_ref: (j, c_ref[0], i, 0)),
                      pl.BlockSpec((None, tr, C), lambda i, j, c_ref, chip_ref: (j, i, 0))],
            out_specs=[pl.BlockSpec((None, tr, C), lambda i, j, c_ref, chip_ref: (j, i, 0)),
                       pl.BlockSpec((None, tr, C), lambda i, j, c_ref, chip_ref: (chip_ref[0], i, 0))]),
        out_shape=[_sds((N_CHIP, R, C), part.dtype), _sds((N_CHIP, R, C), part.dtype)],
        compiler_params=_params(("parallel", "arbitrary")),
    )(c_arr, chip_arr, p4, got)


def adamw(w, m, v, parts_per_layer):
    NL, R, C = w.shape
    P = parts_per_layer[0].shape[0]
    tr = _tile(R, max(16, ADAMW_BLOCK_ELEMS // C))
    nb = R // tr

    def body(w_ref, m_ref, v_ref, *rest):
        part_refs = rest[:NL]
        g_ref, d_ref, nm_ref, nv_ref = rest[NL:]
        layer = pl.program_id(0)
        g = None
        for q in range(NL):
            s = part_refs[q][0].astype(F32)
            for t in range(1, P):
                s = s + part_refs[q][t].astype(F32)
            s = jnp.where(layer == q, s, 0.0)
            g = s if g is None else g + s
        wv = w_ref[...]
        nm = ADAM_B1 * m_ref[...] + (1.0 - ADAM_B1) * g
        nv = ADAM_B2 * v_ref[...] + (1.0 - ADAM_B2) * (g * g)
        m_hat = nm / (1.0 - ADAM_B1 ** ADAM_STEP)
        v_hat = nv / (1.0 - ADAM_B2 ** ADAM_STEP)
        g_ref[...] = g
        d_ref[...] = -ADAM_LR * (m_hat / (jnp.sqrt(v_hat) + ADAM_EPS) + ADAM_WD * wv)
        nm_ref[...] = nm
        nv_ref[...] = nv

    def part_spec(q):
        return pl.BlockSpec((P, tr, C), lambda l, i: (0, jnp.where(l == q, i, jnp.where(l < q, 0, nb - 1)), 0))

    blk = pl.BlockSpec((None, tr, C), lambda l, i: (l, i, 0))
    return pl.pallas_call(
        body, name="adamw", grid=(NL, nb),
        in_specs=[blk, blk, blk] + [part_spec(q) for q in range(NL)],
        out_specs=[blk, blk, blk, blk], out_shape=[_sds((NL, R, C), F32)] * 4,
        compiler_params=_params(("arbitrary", "arbitrary")),
    )(w, m, v, *parts_per_layer)


SMALL = ("norm1_g", "gmlp_ln_g", "w_spatial", "b_spatial", "mix_norm_attn_g", "mix_norm_gmlp_g", "norm2_g")
BIG = ("w_in", "w_out", "w_gate", "w_up", "w_down")
TRANSPOSED = ("w_gate", "w_up")
GROUPS = (("w_in", "w_out"), ("w_gate", "w_up", "w_down"))
LANES = 128


def _pack(layers, final):
    flat = [layer[n].reshape(-1) for layer in layers for n in SMALL] + [final.reshape(-1)]
    return jnp.concatenate(flat).reshape(-1, LANES)


def _unpack(packed, like_layers, like_final):
    flat = packed.reshape(-1)
    out, off = [], 0
    for layer in like_layers:
        d = {}
        for n in SMALL:
            size = layer[n].size
            d[n] = flat[off:off + size].reshape(layer[n].shape)
            off += size
        out.append(d)
    return out, flat[off:off + like_final.size].reshape(like_final.shape)


def kernel(x, norm1_g, w_in, gmlp_ln_g, w_spatial, b_spatial, mix_norm_attn_g, mix_norm_gmlp_g, w_out, norm2_g, w_gate, w_up, w_down, final_g, loss_target, m_norm1_g, m_w_in, m_gmlp_ln_g, m_w_spatial, m_b_spatial, m_mix_norm_attn_g, m_mix_norm_gmlp_g, m_w_out, m_norm2_g, m_w_gate, m_w_up, m_w_down, m_final_g, v_norm1_g, v_w_in, v_gmlp_ln_g, v_w_spatial, v_b_spatial, v_mix_norm_attn_g, v_mix_norm_gmlp_g, v_w_out, v_norm2_g, v_w_gate, v_w_up, v_w_down, v_final_g):
    S, D = x.shape[1], x.shape[2]
    NL = norm1_g.shape[0]
    DA = D // 2
    xs = x.reshape(S, D)
    tabs = rope_tables(S)
    ax, ay, ac = lax.axis_index("x"), lax.axis_index("y"), lax.axis_index("c")
    me_arr = (4 * ax + 2 * ay + ac).astype(jnp.int32).reshape(1)
    c_arr = ac.astype(jnp.int32).reshape(1)
    chip_arr = (2 * ax + ay).astype(jnp.int32).reshape(1)
    small_w = dict(norm1_g=norm1_g, gmlp_ln_g=gmlp_ln_g, w_spatial=w_spatial, b_spatial=b_spatial,
                   mix_norm_attn_g=mix_norm_attn_g, mix_norm_gmlp_g=mix_norm_gmlp_g, norm2_g=norm2_g)
    small_m = dict(norm1_g=m_norm1_g, gmlp_ln_g=m_gmlp_ln_g, w_spatial=m_w_spatial, b_spatial=m_b_spatial,
                   mix_norm_attn_g=m_mix_norm_attn_g, mix_norm_gmlp_g=m_mix_norm_gmlp_g, norm2_g=m_norm2_g)
    small_v = dict(norm1_g=v_norm1_g, gmlp_ln_g=v_gmlp_ln_g, w_spatial=v_w_spatial, b_spatial=v_b_spatial,
                   mix_norm_attn_g=v_mix_norm_attn_g, mix_norm_gmlp_g=v_mix_norm_gmlp_g, norm2_g=v_norm2_g)
    def view(n, a):
        return jnp.swapaxes(a, 1, 2) if n in TRANSPOSED else a

    big_w = {n: view(n, a) for n, a in dict(w_in=w_in, w_out=w_out, w_gate=w_gate, w_up=w_up, w_down=w_down).items()}
    big_m = {n: view(n, a) for n, a in dict(w_in=m_w_in, w_out=m_w_out, w_gate=m_w_gate, w_up=m_w_up, w_down=m_w_down).items()}
    big_v = {n: view(n, a) for n, a in dict(w_in=v_w_in, w_out=v_w_out, w_gate=v_w_gate, w_up=v_w_up, w_down=v_w_down).items()}

    def layer_small(l):
        ws = w_spatial[l]
        return dict(norm1_g=norm1_g[l][None], gmlp_ln_g=gmlp_ln_g[l][None], mix_norm_attn_g=mix_norm_attn_g[l][None],
                    mix_norm_gmlp_g=mix_norm_gmlp_g[l][None], norm2_g=norm2_g[l][None], w_spatial=ws,
                    w_spatial_t=jnp.swapaxes(ws, 1, 2), bcol=jnp.broadcast_to(b_spatial[l][:, :, None], ws.shape))

    n_stages = 2 * NL
    zero = jnp.zeros(TOKEN, F32)
    TOK = 5

    def stage_bufs(s, dep):
        return [to_slot(big_w[n], s // 2, me_arr, dep, BF16) for n in GROUPS[s % 2]]

    def stage_weights(s, bufs):
        gw = dict(zip(GROUPS[s % 2], bufs))
        if s % 2 == 0:
            gw["w_out"] = gw["w_out"].reshape(2, DA, D)
        else:
            gw = {n: b.reshape(-1, D) for n, b in gw.items()}
        return gw

    ici = {0: split_start("gather_ici", stage_bufs(0, zero), None)}
    ici[1] = split_start("gather_ici", stage_bufs(1, ici[0][TOK]), None)
    h = xs
    d2d = {0: split_start("gather_d2d", split_wait(ici[0], h), None)}
    ready = {0: split_wait(d2d[0], h)}
    saved, weights = [], []
    for s in range(n_stages):
        deps = [ici[1][TOK]] if s == 0 else []
        if 1 <= s < n_stages - 1:
            d2d[s + 1] = split_start("gather_d2d", split_wait(ici[s + 1], h), None)
            deps.append(d2d[s + 1][TOK])
        if s + 2 < n_stages:
            ici[s + 2] = split_start("gather_ici", stage_bufs(s + 2, deps[-1]), None)
            deps.append(ici[s + 2][TOK])
        dep = sum(deps[1:], deps[0]) if deps else zero
        gw = stage_weights(s, ready[s])
        sm = layer_small(s // 2)
        if s % 2 == 0:
            sm["norm1_g"] = sm["norm1_g"] + dep[0, 0]
            h, sv = mixer_fwd(h, sm, gw, tabs)
        else:
            sm["norm2_g"] = sm["norm2_g"] + dep[0, 0]
            h, sv = ffn_fwd(h, sm, gw)
        saved.append(sv)
        weights.append(gw)
        if s == 0:
            d2d[1] = split_start("gather_d2d", split_wait(ici[1], h), None)
        if s + 1 < n_stages:
            ready[s + 1] = split_wait(d2d[s + 1], h)
    loss_part, dx, dxb, d_final = loss_and_grad(h, final_g[None], loss_target.reshape(S, D))
    loss = lax.psum(loss_part[0, 0], ("x", "y", "c"))

    big_sums = [dict() for _ in range(NL)]
    small_grads = [dict() for _ in range(NL)]
    pending, pending_stage = None, None
    dep = zero
    for s in reversed(range(n_stages)):
        l = s // 2
        crossing = []

        def on_grads(big, s=s, crossing=crossing):
            parts = [big[n].reshape((N_DEV, -1, big[n].shape[-1])) for n in GROUPS[s % 2]]
            gots = [lax.empty((N_CHIP,) + p.shape[1:], p.dtype) for p in parts]
            crossing.append(split_start("scatter_d2d", parts, gots))
            return crossing[0][TOK]

        if s % 2 == 1:
            dx, dxb, small = ffn_bwd(dx, dxb, saved[s], layer_small(l), weights[s], dep, on_grads)
        else:
            dx, dxb, small = mixer_bwd(dx, dxb, saved[s], layer_small(l), weights[s], tabs, dep, on_grads)
        small_grads[l].update(small)
        if pending is not None:
            big_sums[pending_stage // 2].update(zip(GROUPS[pending_stage % 2], split_wait(pending, dx)[1]))
        parts, got = split_wait(crossing[0], dx)
        pairs = [pair_sum(p, g, c_arr, chip_arr) for p, g in zip(parts, got)]
        pending, pending_stage = split_start("scatter_ici", [t for t, _ in pairs], [o for _, o in pairs]), s
        dep = pending[TOK]

    def update_big(n):
        R, C = big_w[n].shape[1], big_w[n].shape[2]
        res = adamw(big_w[n], big_m[n], big_v[n], [big_sums[l][n].reshape(N_CHIP, R, C) for l in range(NL)])
        return [view(n, t) for t in res], res[0][0, :TOKEN[0], :TOKEN[1]]

    big_out = {}
    after_ffn = zero
    for n in GROUPS[1]:
        big_out[n], piece = update_big(n)
        after_ffn = after_ffn + piece

    packed = to_slot(_pack(small_grads, d_final), None, me_arr, dep + after_ffn, F32)
    gathered_small = gather_d2d(gather_ici([packed]))[0][0]
    pw = _pack([{n: small_w[n][l] for n in SMALL} for l in range(NL)], final_g)[None]
    pm = _pack([{n: small_m[n][l] for n in SMALL} for l in range(NL)], m_final_g)[None]
    pv = _pack([{n: small_v[n][l] for n in SMALL} for l in range(NL)], v_final_g)[None]
    like_layers = [{n: small_w[n][l] for n in SMALL} for l in range(NL)]
    small_res = adamw(pw, pm, pv, [gathered_small])
    small_out = [_unpack(t[0], like_layers, final_g) for t in small_res]
    big_sums[pending_stage // 2].update(zip(GROUPS[pending_stage % 2], split_wait(pending, small_res[0])[1]))

    def small_stack(k, n):
        return jnp.stack([small_out[k][0][l][n] for l in range(NL)])

    for n in GROUPS[0]:
        big_out[n], _ = update_big(n)

    order = ("norm1_g", "w_in", "gmlp_ln_g", "w_spatial", "b_spatial", "mix_norm_attn_g", "mix_norm_gmlp_g", "w_out",
             "norm2_g", "w_gate", "w_up", "w_down")
    outs = [loss, dx.reshape(x.shape)]
    for k in range(4):
        for n in order:
            outs.append(big_out[n][k] if n in BIG else small_stack(k, n))
        outs.append(small_out[k][1])
    return tuple(outs)
```

```python
import functools

import jax
import jax.numpy as jnp
from jax import lax
from jax.experimental import pallas as pl
from jax.experimental.pallas import tpu as pltpu

F32 = jnp.float32
BF16 = jnp.bfloat16

HEAD_DIM = 128
CHUNK = 128
GROUP = 128
N_SIDE = 64
DILATIONS = (1, 4, 16)
TQ = 128
HALO = 64
EPS = 1e-6
NEG = -1e30
ROPE_THETA = 10000.0
ADAM_LR = 0.001
ADAM_B1 = 0.9
ADAM_B2 = 0.999
ADAM_EPS = 1e-08
ADAM_WD = 0.01
ADAM_STEP = 10
N_DEV = 8
N_CHIP = 4
VMEM_LIMIT_V7X = 56 * 1024 * 1024
ADAMW_BLOCK_ELEMS = 128 * 1024
MESH = pl.DeviceIdType.MESH
ANY = pl.BlockSpec(memory_space=pl.ANY)


def _params(sem=None):
    return pltpu.CompilerParams(dimension_semantics=sem, vmem_limit_bytes=VMEM_LIMIT_V7X)


def _tile(n, want):
    for t in range(min(want, n), 15, -1):
        if n % t == 0 and t % 16 == 0:
            return t
    return n


def _sds(shape, dtype):
    return jax.ShapeDtypeStruct(shape, dtype)


def _dot(a, b, dims):
    return lax.dot_general(a, b, (dims, ((), ())), preferred_element_type=F32)


NN = ((1,), (0,))
NT = ((1,), (1,))
TN = ((0,), (0,))


def _sigmoid(x):
    return 1.0 / (1.0 + jnp.exp(-x))


def _gelu(x):
    return 0.5 * x * (1.0 + lax.erf(x * 0.7071067811865476))


def _gelu_grad(x):
    cdf = 0.5 * (1.0 + lax.erf(x * 0.7071067811865476))
    pdf = 0.3989422804014327 * jnp.exp(-0.5 * x * x)
    return cdf + x * pdf


def rms_fwd(x, g):
    S, D = x.shape
    tr = _tile(S, 512)

    def body(x_ref, g_ref, o_ref):
        xv = x_ref[...]
        r = lax.rsqrt(jnp.mean(xv * xv, axis=-1, keepdims=True) + EPS)
        o_ref[...] = (xv * r * g_ref[...]).astype(o_ref.dtype)

    return pl.pallas_call(
        body, name="rms_fwd", grid=(S // tr,),
        in_specs=[pl.BlockSpec((tr, D), lambda i: (i, 0)), pl.BlockSpec((1, D), lambda i: (0, 0))],
        out_specs=pl.BlockSpec((tr, D), lambda i: (i, 0)),
        out_shape=_sds((S, D), BF16), compiler_params=_params(("parallel",)),
    )(x, g)


def _rms_bwd_math(xv, gv, dy):
    r = lax.rsqrt(jnp.mean(xv * xv, axis=-1, keepdims=True) + EPS)
    dyg = dy * gv
    dx = r * dyg - xv * (r * r * r) * jnp.mean(dyg * xv, axis=-1, keepdims=True)
    dg = jnp.sum(dy * xv * r, axis=0, keepdims=True)
    return dx, dg


def rms_bwd_res(x, g, dy, dres):
    S, D = x.shape
    tr = _tile(S, 512)

    def body(x_ref, g_ref, dy_ref, dres_ref, dx_ref, dxb_ref, dg_ref):
        dx, dg = _rms_bwd_math(x_ref[...], g_ref[...], dy_ref[...])
        tot = dres_ref[...] + dx
        dx_ref[...] = tot
        dxb_ref[...] = tot.astype(dxb_ref.dtype)

        @pl.when(pl.program_id(0) == 0)
        def _():
            dg_ref[...] = dg

        @pl.when(pl.program_id(0) > 0)
        def _():
            dg_ref[...] += dg

    row = pl.BlockSpec((tr, D), lambda i: (i, 0))
    vec = pl.BlockSpec((1, D), lambda i: (0, 0))
    return pl.pallas_call(
        body, name="rms_bwd_res", grid=(S // tr,),
        in_specs=[row, vec, row, row], out_specs=[row, row, vec],
        out_shape=[_sds((S, D), F32), _sds((S, D), BF16), _sds((1, D), F32)],
        compiler_params=_params(("arbitrary",)),
    )(x, g, dy, dres)


def loss_and_grad(x, g, target):
    S, D = x.shape
    tr = _tile(S, 512)

    def body(x_ref, g_ref, t_ref, loss_ref, dx_ref, dxb_ref, dg_ref):
        xv = x_ref[...]
        gv = g_ref[...]
        r = lax.rsqrt(jnp.mean(xv * xv, axis=-1, keepdims=True) + EPS)
        y = xv * r * gv
        diff = y - t_ref[...]
        part = 0.5 * jnp.sum(jnp.mean(diff * diff, axis=-1, keepdims=True), axis=0, keepdims=True)
        dy = diff * (1.0 / D)
        dx, dg = _rms_bwd_math(xv, gv, dy)
        dx_ref[...] = dx
        dxb_ref[...] = dx.astype(dxb_ref.dtype)

        @pl.when(pl.program_id(0) == 0)
        def _():
            dg_ref[...] = dg
            loss_ref[...] = jnp.broadcast_to(part, loss_ref.shape)

        @pl.when(pl.program_id(0) > 0)
        def _():
            dg_ref[...] += dg
            loss_ref[...] += jnp.broadcast_to(part, loss_ref.shape)

    row = pl.BlockSpec((tr, D), lambda i: (i, 0))
    vec = pl.BlockSpec((1, D), lambda i: (0, 0))
    return pl.pallas_call(
        body, name="loss_and_grad", grid=(S // tr,),
        in_specs=[row, vec, row],
        out_specs=[pl.BlockSpec((8, 128), lambda i: (0, 0)), row, row, vec],
        out_shape=[_sds((8, 128), F32), _sds((S, D), F32), _sds((S, D), BF16), _sds((1, D), F32)],
        compiler_params=_params(("arbitrary",)),
    )(x, g, target)


def _mm_body(n_pairs, dims, red_axis, n_red, has_res):
    def body(*refs):
        ins = refs[:2 * n_pairs]
        res_ref = refs[2 * n_pairs] if has_res else None
        o_ref = refs[2 * n_pairs + has_res]
        p = None
        for t in range(n_pairs):
            d = _dot(ins[2 * t][...], ins[2 * t + 1][...], dims)
            p = d if p is None else p + d
        if red_axis is None:
            if has_res:
                p = res_ref[...] + p
            o_ref[...] = p.astype(o_ref.dtype)
            return
        acc_ref = o_ref if o_ref.dtype == F32 else refs[2 * n_pairs + has_res + 1]
        r = pl.program_id(red_axis)

        @pl.when(r == 0)
        def _():
            acc_ref[...] = res_ref[...] + p if has_res else p

        @pl.when(r > 0)
        def _():
            acc_ref[...] += p

        if acc_ref is not o_ref:
            @pl.when(r == n_red - 1)
            def _():
                o_ref[...] = acc_ref[...].astype(o_ref.dtype)

    return body


def mm_in(h, w):
    S, D = h.shape
    N = w.shape[0]
    tm = _tile(S, 1024)
    tn = _tile(N, 1024)
    return pl.pallas_call(
        _mm_body(1, NT, None, 1, False), name="mm_in", grid=(S // tm, N // tn),
        in_specs=[pl.BlockSpec((tm, D), lambda i, j: (i, 0)), pl.BlockSpec((tn, D), lambda i, j: (j, 0))],
        out_specs=pl.BlockSpec((tm, tn), lambda i, j: (i, j)),
        out_shape=_sds((S, N), F32), compiler_params=_params(("parallel", "parallel")),
    )(h, w)


def mm_out(x, mix_a, mix_g, w):
    S, D = x.shape
    DA = mix_a.shape[1]
    tm = _tile(S, 1024)
    tn = _tile(D, 1024)
    act = pl.BlockSpec((tm, DA), lambda i, j: (i, 0))
    return pl.pallas_call(
        _mm_body(2, NN, None, 1, True), name="mm_out", grid=(S // tm, D // tn),
        in_specs=[act, pl.BlockSpec((None, DA, tn), lambda i, j: (0, 0, j)),
                  act, pl.BlockSpec((None, DA, tn), lambda i, j: (1, 0, j)),
                  pl.BlockSpec((tm, tn), lambda i, j: (i, j))],
        out_specs=pl.BlockSpec((tm, tn), lambda i, j: (i, j)),
        out_shape=_sds((S, D), F32), compiler_params=_params(("parallel", "parallel")),
    )(mix_a, w, mix_g, w, x)


def _lane_tile(n, want):
    for t in range(min(want, n) // 128 * 128, 127, -128):
        if n % t == 0:
            return t
    return n


def ff_fwd(h, wg, wu):
    S, D = h.shape
    F = wg.shape[0]
    tm = _tile(S, 1024)
    tn = _lane_tile(F, 512)

    def body(h_ref, wg_ref, wu_ref, dfdu_ref, dfdg_ref, ff_ref):
        hv = h_ref[...]
        g = _dot(hv, wg_ref[...], NT)
        u = _dot(hv, wu_ref[...], NT)
        sg = _sigmoid(g)
        act = g * sg
        dfdu_ref[...] = act.astype(dfdu_ref.dtype)
        dfdg_ref[...] = (u * (sg * (1.0 + g * (1.0 - sg)))).astype(dfdg_ref.dtype)
        ff_ref[...] = (act * u).astype(ff_ref.dtype)

    wspec = pl.BlockSpec((tn, D), lambda i, j: (j, 0))
    ospec = pl.BlockSpec((tm, tn), lambda i, j: (i, j))
    return pl.pallas_call(
        body, name="ff_fwd", grid=(S // tm, F // tn),
        in_specs=[pl.BlockSpec((tm, D), lambda i, j: (i, 0)), wspec, wspec],
        out_specs=[ospec, ospec, ospec],
        out_shape=[_sds((S, F), BF16), _sds((S, F), BF16), _sds((S, F), BF16)],
        compiler_params=_params(("parallel", "parallel")),
    )(h, wg, wu)


def mm_down(x, ff, wd):
    S, D = x.shape
    F = ff.shape[1]
    tm = _tile(S, 1024)
    tn = _lane_tile(D, 512)
    blk = pl.BlockSpec((tm, tn), lambda i, j: (i, j))
    return pl.pallas_call(
        _mm_body(1, NN, None, 1, True), name="mm_down", grid=(S // tm, D // tn),
        in_specs=[pl.BlockSpec((tm, F), lambda i, j: (i, 0)), pl.BlockSpec((F, tn), lambda i, j: (0, j)), blk],
        out_specs=blk, out_shape=_sds((S, D), F32), compiler_params=_params(("parallel", "parallel")),
    )(ff, wd, x)


def ff_bwd_act(dxb, wd, dfdg, dfdu, dep):
    S, D = dxb.shape
    F = wd.shape[0]
    tm = _tile(S, 1024)
    tn = _lane_tile(F, 512)

    def body(dx_ref, wd_ref, dfdg_ref, dfdu_ref, dep_ref, dgate_ref, dup_ref):
        dff = _dot(dx_ref[...], wd_ref[...], NT)
        dup_ref[...] = (dff * dfdu_ref[...].astype(F32)).astype(dup_ref.dtype)
        dgate_ref[...] = (dff * dfdg_ref[...].astype(F32)).astype(dgate_ref.dtype)

    aspec = pl.BlockSpec((tm, tn), lambda i, j: (i, j))
    return pl.pallas_call(
        body, name="ff_bwd_act", grid=(S // tm, F // tn),
        in_specs=[pl.BlockSpec((tm, D), lambda i, j: (i, 0)), pl.BlockSpec((tn, D), lambda i, j: (j, 0)), aspec, aspec,
                  pl.BlockSpec(TOKEN, lambda i, j: (0, 0))],
        out_specs=[aspec, aspec],
        out_shape=[_sds((S, F), BF16), _sds((S, F), BF16)],
        compiler_params=_params(("parallel", "parallel")),
    )(dxb, wd, dfdg, dfdu, dep)


def dw_down(ff, dxb):
    S, F = ff.shape
    D = dxb.shape[1]
    tf = _lane_tile(F, 256)
    return pl.pallas_call(
        _mm_body(1, TN, None, 1, False), name="dw_down", grid=(F // tf,),
        in_specs=[pl.BlockSpec((S, tf), lambda k: (0, k)), pl.BlockSpec((S, D), lambda k: (0, 0))],
        out_specs=pl.BlockSpec((tf, D), lambda k: (k, 0)),
        out_shape=_sds((F, D), BF16), compiler_params=_params(("parallel",)),
    )(ff, dxb)


def dw_gate_up(h, dgate, dup):
    S, D = h.shape
    F = dgate.shape[1]
    f = _lane_tile(F, 256)

    def body(h_ref, dg_ref, du_ref, og_ref, ou_ref):
        hv = h_ref[...]
        og_ref[...] = _dot(dg_ref[...], hv, TN).astype(og_ref.dtype)
        ou_ref[...] = _dot(du_ref[...], hv, TN).astype(ou_ref.dtype)

    aspec = pl.BlockSpec((S, f), lambda k: (0, k))
    ospec = pl.BlockSpec((f, D), lambda k: (k, 0))
    return pl.pallas_call(
        body, name="dw_gate_up", grid=(F // f,),
        in_specs=[pl.BlockSpec((S, D), lambda k: (0, 0)), aspec, aspec],
        out_specs=[ospec, ospec],
        out_shape=[_sds((F, D), BF16), _sds((F, D), BF16)],
        compiler_params=_params(("parallel",)),
    )(h, dgate, dup)


def dh_ff(dgate, dup, wg, wu, dep):
    S, F = dgate.shape
    D = wg.shape[1]
    tm = _tile(S, 512)
    tn = _lane_tile(D, 256)

    def body(dg_ref, wg_ref, du_ref, wu_ref, dep_ref, o_ref):
        o_ref[...] = _dot(dg_ref[...], wg_ref[...], NN) + _dot(du_ref[...], wu_ref[...], NN)

    aspec = pl.BlockSpec((tm, F), lambda i, j: (i, 0))
    wspec = pl.BlockSpec((F, tn), lambda i, j: (0, j))
    return pl.pallas_call(
        body, name="dh_ff", grid=(S // tm, D // tn),
        in_specs=[aspec, wspec, aspec, wspec, pl.BlockSpec(TOKEN, lambda i, j: (0, 0))],
        out_specs=pl.BlockSpec((tm, tn), lambda i, j: (i, j)),
        out_shape=_sds((S, D), F32), compiler_params=_params(("parallel", "parallel")),
    )(dgate, wg, dup, wu, dep)


def dmix_mm(dxb, w, dep):
    S, D = dxb.shape
    tm = _tile(S, 1024)
    tn = _tile(D, 1024)

    def body(dx_ref, w_ref, dep_ref, o_ref):
        o_ref[...] = _dot(dx_ref[...], w_ref[...], NT)

    return pl.pallas_call(
        body, name="dmix_mm", grid=(S // tm, D // tn),
        in_specs=[pl.BlockSpec((tm, D), lambda i, j: (i, 0)), pl.BlockSpec((tn, D), lambda i, j: (j, 0)),
                  pl.BlockSpec(TOKEN, lambda i, j: (0, 0))],
        out_specs=pl.BlockSpec((tm, tn), lambda i, j: (i, j)),
        out_shape=_sds((S, D), F32), compiler_params=_params(("parallel", "parallel")),
    )(dxb, w, dep)


def dw_out(mix_a, mix_g, dxb):
    S, DA = mix_a.shape
    D = dxb.shape[1]
    tn = _lane_tile(D, 512)

    def half(name, m):
        return pl.pallas_call(
            _mm_body(1, TN, None, 1, False), name=name, grid=(D // tn,),
            in_specs=[pl.BlockSpec((S, DA), lambda j: (0, 0)), pl.BlockSpec((S, tn), lambda j: (0, j))],
            out_specs=pl.BlockSpec((DA, tn), lambda j: (0, j)),
            out_shape=_sds((DA, D), BF16), compiler_params=_params(("parallel",)),
        )(m, dxb)

    return half("dw_out_a", mix_a), half("dw_out_g", mix_g)


def dw_in(h, dproj):
    S, D = h.shape
    N = dproj.shape[1]
    tn = _lane_tile(N, 256)
    return pl.pallas_call(
        _mm_body(1, TN, None, 1, False), name="dw_in", grid=(N // tn,),
        in_specs=[pl.BlockSpec((S, tn), lambda k: (0, k)), pl.BlockSpec((S, D), lambda k: (0, 0))],
        out_specs=pl.BlockSpec((tn, D), lambda k: (k, 0)),
        out_shape=_sds((N, D), BF16), compiler_params=_params(("parallel",)),
    )(dproj, h)


def dh_in(dproj, w, dep):
    S, N = dproj.shape
    D = w.shape[1]
    tm = _tile(S, 1024)
    tn = _lane_tile(D, 512)

    def body(dp_ref, w_ref, dep_ref, o_ref):
        o_ref[...] = _dot(dp_ref[...], w_ref[...], NN)

    return pl.pallas_call(
        body, name="dh_in", grid=(S // tm, D // tn),
        in_specs=[pl.BlockSpec((tm, N), lambda i, j: (i, 0)), pl.BlockSpec((N, tn), lambda i, j: (0, j)),
                  pl.BlockSpec(TOKEN, lambda i, j: (0, 0))],
        out_specs=pl.BlockSpec((tm, tn), lambda i, j: (i, j)),
        out_shape=_sds((S, D), F32), compiler_params=_params(("parallel", "parallel")),
    )(dproj, w, dep)


def rope_tables(S):
    pos = jnp.arange(S, dtype=F32)
    inv = ROPE_THETA ** (-jnp.arange(0, HEAD_DIM, 2, dtype=F32) / HEAD_DIM)
    ang = pos[:, None] * inv[None, :]
    cos, sin = jnp.cos(ang), jnp.sin(ang)
    return jnp.concatenate([cos, cos], axis=-1), jnp.concatenate([-sin, sin], axis=-1)


def _rot_half(t):
    return pltpu.roll(t, HEAD_DIM // 2, 1)


PERM_ROWS = 256


def _by_residue(y, d, out_ref):
    tr = y.shape[0]
    n = tr // d
    o = lax.broadcasted_iota(jnp.int32, (tr, tr), 0)
    i = lax.broadcasted_iota(jnp.int32, (tr, tr), 1)
    src = jnp.bitwise_and(o, n - 1) * d + lax.shift_right_logical(o, n.bit_length() - 1)
    z = _dot((src == i).astype(y.dtype), y, NN).astype(out_ref.dtype)
    for r in range(d):
        out_ref[r] = z[r * n:(r + 1) * n]


def _from_residue(ref, d, terms):
    z = jnp.concatenate([ref[r] for r in range(d)], axis=0)
    tr = z.shape[0]
    n = tr // d
    i = lax.broadcasted_iota(jnp.int32, (tr, tr), 0)
    o = lax.broadcasted_iota(jnp.int32, (tr, tr), 1)
    src = jnp.bitwise_and(i, d - 1) * n + lax.shift_right_logical(i, d.bit_length() - 1)
    pick = (src == o).astype(BF16)
    out = None
    rest = z
    for _ in range(terms):
        piece = rest.astype(BF16)
        got = _dot(pick, piece, NN)
        out = got if out is None else out + got
        rest = rest - piece.astype(F32)
    return out


def _residue_blocks(widths):
    return [pl.BlockSpec((d, PERM_ROWS // d, W), lambda i: (0, i, 0)) for d, W in zip(DILATIONS[1:], widths)]


def _residue_outputs(S, W, dtype):
    shapes = [_sds((d, S // d, W), dtype) for d in DILATIONS[1:]]
    specs = [pl.BlockSpec((d, PERM_ROWS // d, W), lambda i: (0, i, 0)) for d in DILATIONS[1:]]
    return shapes, specs


def rope_qkv(proj, cos2, sin2, DA):
    S = proj.shape[0]
    tr = PERM_ROWS
    nh = DA // HEAD_DIM

    def body(q_ref, k_ref, v_ref, cos_ref, sin_ref, o_ref, *by_res):
        c = cos_ref[...]
        s = sin_ref[...]
        for h in range(nh):
            sl = slice(h * HEAD_DIM, (h + 1) * HEAD_DIM)
            for j, ref in enumerate((q_ref, k_ref)):
                t = ref[:, sl]
                o_ref[:, j * DA + h * HEAD_DIM:j * DA + (h + 1) * HEAD_DIM] = (t * c + _rot_half(t) * s).astype(o_ref.dtype)
        o_ref[:, 2 * DA:3 * DA] = v_ref[...].astype(o_ref.dtype)
        y = o_ref[...]
        for d, ref in zip(DILATIONS[1:], by_res):
            _by_residue(y, d, ref)

    tab = pl.BlockSpec((tr, HEAD_DIM), lambda i: (i, 0))
    shapes, specs = _residue_outputs(S, 3 * DA, BF16)
    out = pl.pallas_call(
        body, name="rope_qkv", grid=(S // tr,),
        in_specs=[pl.BlockSpec((tr, DA), lambda i: (i, 0)), pl.BlockSpec((tr, DA), lambda i: (i, 1)),
                  pl.BlockSpec((tr, DA), lambda i: (i, 2)), tab, tab],
        out_specs=[pl.BlockSpec((tr, 3 * DA), lambda i: (i, 0))] + specs,
        out_shape=[_sds((S, 3 * DA), BF16)] + shapes, compiler_params=_params(("parallel",)),
    )(proj, proj, proj, cos2, sin2)
    return [out[0].reshape(1, S, 3 * DA)] + list(out[1:])


def assemble_dproj(dqs, dks, dvs, duv, cos2, sin2):
    S, DA = dqs[0].shape[1:]
    tr = PERM_ROWS
    nh = DA // HEAD_DIM

    def total(trio):
        t = trio[0][...]
        for d, ref in zip(DILATIONS[1:], trio[1:]):
            t = t + _from_residue(ref, d, 2)
        return t

    def body(*refs):
        dq_refs, dk_refs, dv_refs = refs[0:3], refs[3:6], refs[6:9]
        duv_ref, cos_ref, sin_ref, o_ref = refs[9:13]
        c = cos_ref[...]
        s = sin_ref[...]
        for j, trio in enumerate((dq_refs, dk_refs)):
            t_all = total(trio)
            for h in range(nh):
                t = t_all[:, h * HEAD_DIM:(h + 1) * HEAD_DIM]
                o_ref[:, j * DA + h * HEAD_DIM:j * DA + (h + 1) * HEAD_DIM] = (t * c - _rot_half(t) * s).astype(o_ref.dtype)
        o_ref[:, 2 * DA:3 * DA] = total(dv_refs).astype(o_ref.dtype)
        o_ref[:, 3 * DA:5 * DA] = duv_ref[...]

    trio_specs = [pl.BlockSpec((None, tr, DA), lambda i: (0, i, 0))] + _residue_blocks([DA, DA])
    tab = pl.BlockSpec((tr, HEAD_DIM), lambda i: (i, 0))
    return pl.pallas_call(
        body, name="assemble_dproj", grid=(S // tr,),
        in_specs=trio_specs * 3 + [pl.BlockSpec((tr, 2 * DA), lambda i: (i, 0)), tab, tab],
        out_specs=pl.BlockSpec((tr, 5 * DA), lambda i: (i, 0)),
        out_shape=_sds((S, 5 * DA), BF16), compiler_params=_params(("parallel",)),
    )(*dqs, *dks, *dvs, duv, cos2, sin2)


STAT_W = 128


def _lanes_per_head(DA):
    return STAT_W // (DA // HEAD_DIM)


def _halo_specs(L, width_blocks, col):
    per = TQ // HALO
    last = L // HALO - 1
    W = width_blocks
    return [
        pl.BlockSpec((HALO, W), lambda r, i: (jnp.maximum(i * per - 1, 0), col(r))),
        pl.BlockSpec((TQ, W), lambda r, i: (i, col(r))),
        pl.BlockSpec((HALO, W), lambda r, i: (jnp.minimum((i + 1) * per, last), col(r))),
    ]


def _halo_specs3(L, W, c):
    per = TQ // HALO
    last = L // HALO - 1
    return [
        pl.BlockSpec((None, HALO, W), lambda r, i: (r, jnp.maximum(i * per - 1, 0), c)),
        pl.BlockSpec((None, TQ, W), lambda r, i: (r, i, c)),
        pl.BlockSpec((None, HALO, W), lambda r, i: (r, jnp.minimum((i + 1) * per, last), c)),
    ]


def _cat3(refs, sl):
    return jnp.concatenate([refs[0][:, sl], refs[1][:, sl], refs[2][:, sl]], axis=0)


def attn_fwd(qkv, d, DA):
    L = qkv.shape[1]
    S = L * d
    nh = DA // HEAD_DIM
    lph = _lanes_per_head(DA)
    scale = HEAD_DIM ** -0.5
    TK = TQ + 2 * HALO

    def body(q_ref, kp_ref, kc_ref, kn_ref, vp_ref, vc_ref, vn_ref, o_ref, lse_ref):
        i = pl.program_id(1)
        row = lax.broadcasted_iota(jnp.int32, (TQ, TK), 0)
        col = lax.broadcasted_iota(jnp.int32, (TQ, TK), 1)
        kpos = i * TQ - HALO + col
        mask = (jnp.abs(col - HALO - row) <= N_SIDE) & (kpos >= 0) & (kpos < L)
        for h in range(nh):
            sl = slice(h * HEAD_DIM, (h + 1) * HEAD_DIM)
            k = _cat3((kp_ref, kc_ref, kn_ref), sl)
            v = _cat3((vp_ref, vc_ref, vn_ref), sl)
            s = _dot(q_ref[:, sl], k, NT) * scale
            s = jnp.where(mask, s, NEG)
            m = jnp.max(s, axis=-1, keepdims=True)
            p = jnp.exp(s - m)
            l = jnp.sum(p, axis=-1, keepdims=True)
            o = _dot(p.astype(v.dtype), v, NN) / l
            o_ref[:, sl] = o
            lse_ref[:, h * lph:(h + 1) * lph] = jnp.broadcast_to(m + jnp.log(l), (TQ, lph))

    out_spec = pl.BlockSpec((None, TQ, DA), lambda r, i: (r, i, 0))
    o, lse = pl.pallas_call(
        body, name="attn_fwd_d%d" % d, grid=(d, L // TQ),
        in_specs=[pl.BlockSpec((None, TQ, DA), lambda r, i: (r, i, 0))] + _halo_specs3(L, DA, 1) + _halo_specs3(L, DA, 2),
        out_specs=[out_spec, pl.BlockSpec((TQ, STAT_W), lambda r, i: (i, r))],
        out_shape=[_sds((d, L, DA), F32), _sds((L, d * STAT_W), F32)],
        compiler_params=_params(("parallel", "parallel")),
    )(*([qkv] * 7))
    return o, lse.reshape(S, STAT_W)


def attn_merge(os_, lses, g):
    S, DA = os_[0].shape[1:]
    tr = PERM_ROWS
    nh = DA // HEAD_DIM
    lph = _lanes_per_head(DA)

    def body(o0, o1, o2, l0, l1, l2, g_ref, a_ref, lse_ref, mix_ref):
        a0, a1, a2 = l0[...], l1[...], l2[...]
        m = jnp.maximum(jnp.maximum(a0, a1), a2)
        e0, e1, e2 = jnp.exp(a0 - m), jnp.exp(a1 - m), jnp.exp(a2 - m)
        den = e0 + e1 + e2
        w0, w1, w2 = e0 / den, e1 / den, e2 / den
        lse_ref[...] = m + jnp.log(den)
        v0 = o0[...]
        v1 = _from_residue(o1, DILATIONS[1], 2)
        v2 = _from_residue(o2, DILATIONS[2], 2)
        heads = []
        for h in range(nh):
            sl = slice(h * HEAD_DIM, (h + 1) * HEAD_DIM)
            c = slice(h * lph, h * lph + 1)
            heads.append(w0[:, c] * v0[:, sl] + w1[:, c] * v1[:, sl] + w2[:, c] * v2[:, sl])
        a = jnp.concatenate(heads, axis=1)
        a_ref[...] = a
        r = lax.rsqrt(jnp.mean(a * a, axis=-1, keepdims=True) + EPS)
        mix_ref[...] = (a * r * g_ref[...]).astype(mix_ref.dtype)

    blk = pl.BlockSpec((tr, DA), lambda i: (i, 0))
    stat = pl.BlockSpec((tr, STAT_W), lambda i: (i, 0))
    return pl.pallas_call(
        body, name="attn_merge", grid=(S // tr,),
        in_specs=[pl.BlockSpec((None, tr, DA), lambda i: (0, i, 0))] + _residue_blocks([DA, DA]) + [stat] * 3
        + [pl.BlockSpec((1, DA), lambda i: (0, 0))],
        out_specs=[blk, stat, blk],
        out_shape=[_sds((S, DA), F32), _sds((S, STAT_W), F32), _sds((S, DA), BF16)],
        compiler_params=_params(("parallel",)),
    )(*os_, *lses, g)


def attn_out_bwd(a, g, dmix):
    S, DA = a.shape
    tr = PERM_ROWS
    nh = DA // HEAD_DIM
    lph = _lanes_per_head(DA)

    def body(a_ref, g_ref, dy_ref, do_ref, dl_ref, dg_ref, *by_res):
        av = a_ref[...]
        dx, dg = _rms_bwd_math(av, g_ref[...], dy_ref[...])
        dob = dx.astype(do_ref.dtype)
        do_ref[...] = dob
        for d, ref in zip(DILATIONS[1:], by_res):
            _by_residue(dob, d, ref)
        prod = dx * av
        for h in range(nh):
            sl = slice(h * HEAD_DIM, (h + 1) * HEAD_DIM)
            dl_ref[:, h * lph:(h + 1) * lph] = jnp.broadcast_to(jnp.sum(prod[:, sl], axis=-1, keepdims=True), (tr, lph))

        @pl.when(pl.program_id(0) == 0)
        def _():
            dg_ref[...] = dg

        @pl.when(pl.program_id(0) > 0)
        def _():
            dg_ref[...] += dg

    blk = pl.BlockSpec((tr, DA), lambda i: (i, 0))
    vec = pl.BlockSpec((1, DA), lambda i: (0, 0))
    shapes, specs = _residue_outputs(S, DA, BF16)
    out = pl.pallas_call(
        body, name="attn_out_bwd", grid=(S // tr,),
        in_specs=[blk, vec, blk], out_specs=[blk, pl.BlockSpec((tr, STAT_W), lambda i: (i, 0)), vec] + specs,
        out_shape=[_sds((S, DA), BF16), _sds((S, STAT_W), F32), _sds((1, DA), F32)] + shapes,
        compiler_params=_params(("arbitrary",)),
    )(a, g, dmix)
    return [out[0].reshape(1, S, DA)] + list(out[3:]), out[1], out[2]


def attn_bwd_dq(qkv, dob, lse, dl, d, DA):
    L = qkv.shape[1]
    S = L * d
    nh = DA // HEAD_DIM
    lph = _lanes_per_head(DA)
    scale = HEAD_DIM ** -0.5
    TK = TQ + 2 * HALO

    def body(q_ref, kp_ref, kc_ref, kn_ref, vp_ref, vc_ref, vn_ref, do_ref, lse_ref, dl_ref, dq_ref):
        i = pl.program_id(1)
        row = lax.broadcasted_iota(jnp.int32, (TQ, TK), 0)
        col = lax.broadcasted_iota(jnp.int32, (TQ, TK), 1)
        kpos = i * TQ - HALO + col
        mask = (jnp.abs(col - HALO - row) <= N_SIDE) & (kpos >= 0) & (kpos < L)
        for h in range(nh):
            sl = slice(h * HEAD_DIM, (h + 1) * HEAD_DIM)
            k = _cat3((kp_ref, kc_ref, kn_ref), sl)
            v = _cat3((vp_ref, vc_ref, vn_ref), sl)
            s = _dot(q_ref[:, sl], k, NT) * scale
            s = jnp.where(mask, s, NEG)
            p = jnp.exp(s - lse_ref[:, h * lph:h * lph + 1])
            dp = _dot(do_ref[:, sl], v, NT)
            ds = p * (dp - dl_ref[:, h * lph:h * lph + 1])
            dq_ref[:, sl] = _dot(ds.astype(k.dtype), k, NN) * scale

    blk = pl.BlockSpec((None, TQ, DA), lambda r, i: (r, i, 0))
    stat = pl.BlockSpec((TQ, STAT_W), lambda r, i: (i, r))
    return pl.pallas_call(
        body, name="attn_bwd_dq_d%d" % d, grid=(d, L // TQ),
        in_specs=[blk] + _halo_specs3(L, DA, 1) + _halo_specs3(L, DA, 2) + [blk, stat, stat],
        out_specs=blk, out_shape=_sds((d, L, DA), F32),
        compiler_params=_params(("parallel", "parallel")),
    )(*([qkv] * 7), dob, lse.reshape(L, d * STAT_W), dl.reshape(L, d * STAT_W))


def attn_bwd_dkv(qkv, dob, lse, dl, d, DA):
    L = qkv.shape[1]
    S = L * d
    nh = DA // HEAD_DIM
    lph = _lanes_per_head(DA)
    scale = HEAD_DIM ** -0.5
    TR = TQ + 2 * HALO

    def body(k_ref, v_ref, qp, qc, qn, dop, doc, don, lp, lc, ln, dp_, dc_, dn_, dk_ref, dv_ref):
        j = pl.program_id(1)
        row = lax.broadcasted_iota(jnp.int32, (TR, TQ), 0)
        col = lax.broadcasted_iota(jnp.int32, (TR, TQ), 1)
        qpos = j * TQ - HALO + row
        mask = (jnp.abs(col - (row - HALO)) <= N_SIDE) & (qpos >= 0) & (qpos < L)
        for h in range(nh):
            sl = slice(h * HEAD_DIM, (h + 1) * HEAD_DIM)
            q = _cat3((qp, qc, qn), sl)
            do = _cat3((dop, doc, don), sl)
            stat = slice(h * lph, h * lph + 1)
            lse_q = _cat3((lp, lc, ln), stat)
            dl_q = _cat3((dp_, dc_, dn_), stat)
            k = k_ref[:, sl]
            v = v_ref[:, sl]
            s = _dot(q, k, NT) * scale
            s = jnp.where(mask, s, NEG)
            p = jnp.exp(s - lse_q)
            dv_ref[:, sl] = _dot(p.astype(do.dtype), do, TN)
            dp = _dot(do, v, NT)
            ds = p * (dp - dl_q)
            dk_ref[:, sl] = _dot(ds.astype(q.dtype), q, TN) * scale

    blk = pl.BlockSpec((None, TQ, DA), lambda r, i: (r, i, 0))
    ident = lambda r: r
    dk, dv = pl.pallas_call(
        body, name="attn_bwd_dkv_d%d" % d, grid=(d, L // TQ),
        in_specs=[pl.BlockSpec((None, TQ, DA), lambda r, i: (r, i, 1)), pl.BlockSpec((None, TQ, DA), lambda r, i: (r, i, 2))]
        + _halo_specs3(L, DA, 0) + _halo_specs3(L, DA, 0) + _halo_specs(L, STAT_W, ident) + _halo_specs(L, STAT_W, ident),
        out_specs=[blk, blk], out_shape=[_sds((d, L, DA), F32), _sds((d, L, DA), F32)],
        compiler_params=_params(("parallel", "parallel")),
    )(*([qkv] * 5), *([dob] * 3), *([lse.reshape(L, d * STAT_W)] * 3), *([dl.reshape(L, d * STAT_W)] * 3))
    return dk, dv


def _gmlp_fwd_math(u_raw, v_raw, ln_g, ws_ref, bcol_ref, n_chunks, ng):
    ug = _gelu(u_raw)
    vg = _gelu(v_raw)
    mu = jnp.mean(vg, axis=-1, keepdims=True)
    xc = vg - mu
    rstd = lax.rsqrt(jnp.mean(xc * xc, axis=-1, keepdims=True) + EPS)
    xhat = xc * rstd
    vn = xhat * ln_g
    rows = []
    for n in range(n_chunks):
        cols = []
        for g in range(ng):
            blk = vn[n * CHUNK:(n + 1) * CHUNK, g * GROUP:(g + 1) * GROUP]
            cols.append(_dot(ws_ref[g].astype(BF16), blk.astype(BF16), NN) + bcol_ref[g])
        rows.append(jnp.concatenate(cols, axis=1))
    mixed = jnp.concatenate(rows, axis=0)
    return ug, xhat, rstd, vn, mixed


def gmlp_fwd(proj, ln_g, w_s, bcol, g_out, DA):
    S = proj.shape[0]
    ng = DA // GROUP
    tr = _tile(S, 2 * CHUNK)

    def body(u_ref, v_ref, ln_ref, ws_ref, bcol_ref, g_ref, G_ref, mix_ref):
        ug, _, _, _, mixed = _gmlp_fwd_math(u_ref[...], v_ref[...], ln_ref[...], ws_ref, bcol_ref, tr // CHUNK, ng)
        G = ug * mixed
        G_ref[...] = G
        r = lax.rsqrt(jnp.mean(G * G, axis=-1, keepdims=True) + EPS)
        mix_ref[...] = (G * r * g_ref[...]).astype(mix_ref.dtype)

    vec = pl.BlockSpec((1, DA), lambda i: (0, 0))
    par = pl.BlockSpec((ng, CHUNK, CHUNK), lambda i: (0, 0, 0))
    blk = pl.BlockSpec((tr, DA), lambda i: (i, 0))
    return pl.pallas_call(
        body, name="gmlp_fwd", grid=(S // tr,),
        in_specs=[pl.BlockSpec((tr, DA), lambda i: (i, 3)), pl.BlockSpec((tr, DA), lambda i: (i, 4)), vec, par, par, vec],
        out_specs=[blk, blk], out_shape=[_sds((S, DA), F32), _sds((S, DA), BF16)],
        compiler_params=_params(("parallel",)),
    )(proj, proj, ln_g, w_s, bcol, g_out)


def gmlp_bwd(proj, ln_g, w_s, w_st, bcol, g_out, dmix, DA):
    S = proj.shape[0]
    ng = DA // GROUP
    tr = _tile(S, 2 * CHUNK)
    nc = tr // CHUNK

    def body(u_ref, v_ref, ln_ref, ws_ref, wst_ref, bcol_ref, g_ref, dy_ref, duv_ref, dg_ref, dln_ref, dws_ref, db_ref):
        u_raw = u_ref[...]
        v_raw = v_ref[...]
        ln_g_v = ln_ref[...]
        ug, xhat, rstd, vn, mixed = _gmlp_fwd_math(u_raw, v_raw, ln_g_v, ws_ref, bcol_ref, nc, ng)
        G = ug * mixed
        dG, dg = _rms_bwd_math(G, g_ref[...], dy_ref[...])
        du_g = dG * mixed
        dmixed = dG * ug
        dws, dbs, rows = [], [], []
        for g in range(ng):
            dws.append(None)
            dbs.append(None)
        for n in range(nc):
            cols = []
            for g in range(ng):
                dm = dmixed[n * CHUNK:(n + 1) * CHUNK, g * GROUP:(g + 1) * GROUP]
                vb = vn[n * CHUNK:(n + 1) * CHUNK, g * GROUP:(g + 1) * GROUP]
                dmb = dm.astype(BF16)
                w = _dot(dmb, vb.astype(BF16), NT)
                b = jnp.broadcast_to(jnp.sum(dm, axis=-1, keepdims=True), (CHUNK, GROUP))
                dws[g] = w if dws[g] is None else dws[g] + w
                dbs[g] = b if dbs[g] is None else dbs[g] + b
                cols.append(_dot(wst_ref[g].astype(BF16), dmb, NN))
            rows.append(jnp.concatenate(cols, axis=1))
        dvn = jnp.concatenate(rows, axis=0)
        dln = jnp.sum(dvn * xhat, axis=0, keepdims=True)
        dxh = dvn * ln_g_v
        dvg = rstd * (dxh - jnp.mean(dxh, axis=-1, keepdims=True) - xhat * jnp.mean(dxh * xhat, axis=-1, keepdims=True))
        duv_ref[:, 0:DA] = (du_g * _gelu_grad(u_raw)).astype(duv_ref.dtype)
        duv_ref[:, DA:2 * DA] = (dvg * _gelu_grad(v_raw)).astype(duv_ref.dtype)

        first = pl.program_id(0) == 0

        @pl.when(first)
        def _():
            dg_ref[...] = dg
            dln_ref[...] = dln
            for g in range(ng):
                dws_ref[g] = dws[g]
                db_ref[g] = dbs[g]

        @pl.when(jnp.logical_not(first))
        def _():
            dg_ref[...] += dg
            dln_ref[...] += dln
            for g in range(ng):
                dws_ref[g] += dws[g]
                db_ref[g] += dbs[g]

    vec = pl.BlockSpec((1, DA), lambda i: (0, 0))
    par = pl.BlockSpec((ng, CHUNK, CHUNK), lambda i: (0, 0, 0))
    return pl.pallas_call(
        body, name="gmlp_bwd", grid=(S // tr,),
        in_specs=[pl.BlockSpec((tr, DA), lambda i: (i, 3)), pl.BlockSpec((tr, DA), lambda i: (i, 4)), vec, par, par, par, vec,
                  pl.BlockSpec((tr, DA), lambda i: (i, 1))],
        out_specs=[pl.BlockSpec((tr, 2 * DA), lambda i: (i, 0)), vec, vec, par, par],
        out_shape=[_sds((S, 2 * DA), BF16), _sds((1, DA), F32), _sds((1, DA), F32),
                   _sds((ng, CHUNK, CHUNK), F32), _sds((ng, CHUNK, CHUNK), F32)],
        compiler_params=_params(("arbitrary",)),
    )(proj, proj, ln_g, w_s, w_st, bcol, g_out, dmix)


def mixer_fwd(x, sm, gw, tabs):
    D = x.shape[1]
    DA = D // 2
    cos2, sin2 = tabs
    h1 = rms_fwd(x, sm["norm1_g"])
    proj = mm_in(h1, gw["w_in"])
    qkv = rope_qkv(proj, cos2, sin2, DA)
    os_, lses = [], []
    for t, d in enumerate(DILATIONS):
        o, l = attn_fwd(qkv[t], d, DA)
        os_.append(o)
        lses.append(l)
    a, lse, mix_a = attn_merge(os_, lses, sm["mix_norm_attn_g"])
    _, mix_g = gmlp_fwd(proj, sm["gmlp_ln_g"], sm["w_spatial"], sm["bcol"], sm["mix_norm_gmlp_g"], DA)
    x2 = mm_out(x, mix_a, mix_g, gw["w_out"])
    return x2, dict(x=x, h1=h1, proj=proj, qkv=qkv, a=a, lse=lse, mix_a=mix_a, mix_g=mix_g)


def ffn_fwd(x2, sm, gw):
    h2 = rms_fwd(x2, sm["norm2_g"])
    dfdu, dfdg, ff = ff_fwd(h2, gw["w_gate"], gw["w_up"])
    x3 = mm_down(x2, ff, gw["w_down"])
    return x3, dict(x2=x2, h2=h2, dfdg=dfdg, dfdu=dfdu, ff=ff)


def ffn_bwd(dx, dxb, sv, sm, gw, dep, on_grads):
    dgate, dup = ff_bwd_act(dxb, gw["w_down"], sv["dfdg"], sv["dfdu"], dep)
    g_down = dw_down(sv["ff"], dxb)
    g_gate, g_up = dw_gate_up(sv["h2"], dgate, dup)
    token = on_grads(dict(w_gate=g_gate, w_up=g_up, w_down=g_down))
    dh2 = dh_ff(dgate, dup, gw["w_gate"], gw["w_up"], token)
    dx2, dx2b, d_norm2 = rms_bwd_res(sv["x2"], sm["norm2_g"], dh2, dx)
    return dx2, dx2b, dict(norm2_g=d_norm2)


def mixer_bwd(dx2, dx2b, sv, sm, gw, tabs, dep, on_grads):
    D = dx2.shape[1]
    DA = D // 2
    cos2, sin2 = tabs
    dmix = dmix_mm(dx2b, gw["w_out"].reshape(D, D), dep)
    g_out_a, g_out_g = dw_out(sv["mix_a"], sv["mix_g"], dx2b)
    dob, dl, d_mix_a = attn_out_bwd(sv["a"], sm["mix_norm_attn_g"], dmix)
    duv, d_mix_g, d_ln, d_ws, d_bs = gmlp_bwd(sv["proj"], sm["gmlp_ln_g"], sm["w_spatial"], sm["w_spatial_t"], sm["bcol"],
                                              sm["mix_norm_gmlp_g"], dmix, DA)
    dqs, dks, dvs = [], [], []
    for t, d in enumerate(DILATIONS):
        dqs.append(attn_bwd_dq(sv["qkv"][t], dob[t], sv["lse"], dl, d, DA))
        dk, dv = attn_bwd_dkv(sv["qkv"][t], dob[t], sv["lse"], dl, d, DA)
        dks.append(dk)
        dvs.append(dv)
    dproj = assemble_dproj(dqs, dks, dvs, duv, cos2, sin2)
    g_in = dw_in(sv["h1"], dproj)
    token = on_grads(dict(w_in=g_in, w_out=jnp.concatenate([g_out_a, g_out_g], axis=0)))
    dh1 = dh_in(dproj, gw["w_in"], token)
    dx0, dx0b, d_norm1 = rms_bwd_res(sv["x"], sm["norm1_g"], dh1, dx2)
    small = dict(norm1_g=d_norm1, gmlp_ln_g=d_ln, w_spatial=d_ws, b_spatial=d_bs[:, :, 0], mix_norm_attn_g=d_mix_a,
                 mix_norm_gmlp_g=d_mix_g)
    return dx0, dx0b, small


def _place():
    x, y, c = lax.axis_index("x"), lax.axis_index("y"), lax.axis_index("c")
    return x, y, c, [(1 - x, y), (x, 1 - y), (1 - x, 1 - y)]


def _remote(src, dst, send_sems, recv_sems, k, to):
    return pltpu.make_async_remote_copy(src_ref=src, dst_ref=dst, send_sem=send_sems.at[k], recv_sem=recv_sems.at[k],
                                        device_id=to, device_id_type=MESH)


def to_slot(src, layer, me_arr, dep, dtype, transpose=False):
    R, C = src.shape[-2:]
    tr = _lane_tile(R, 512) if transpose else _tile(R, max(16, 4 * ADAMW_BLOCK_ELEMS // C))

    def body(me_ref, src_ref, dep_ref, o_ref):
        v = src_ref[...]
        o_ref[...] = (v.T if transpose else v).astype(o_ref.dtype)

    if layer is None:
        src_spec = pl.BlockSpec((tr, C), lambda i, me_ref: (i, 0))
    else:
        src_spec = pl.BlockSpec((None, tr, C), lambda i, me_ref: (layer, i, 0))
    if transpose:
        out_spec = pl.BlockSpec((None, C, tr), lambda i, me_ref: (me_ref[0], 0, i))
    else:
        out_spec = pl.BlockSpec((None, tr, C), lambda i, me_ref: (me_ref[0], i, 0))
    return pl.pallas_call(
        body, name="to_slot",
        grid_spec=pltpu.PrefetchScalarGridSpec(
            num_scalar_prefetch=1, grid=(R // tr,),
            in_specs=[src_spec, pl.BlockSpec(TOKEN, lambda i, me_ref: (0, 0))], out_specs=out_spec),
        out_shape=_sds((N_DEV, C, R) if transpose else (N_DEV, R, C), dtype), compiler_params=_params(("parallel",)),
    )(me_arr, src, dep)


def gather_ici(bufs):
    n = len(bufs)

    def body(*refs):
        outs = refs[n:2 * n]
        send_sems, recv_sems = refs[2 * n:]
        x, y, c, chips = _place()
        mine = [o.at[4 * x + 2 * y + c] for o in outs]
        sent = []
        for a in range(n):
            for j, (px, py) in enumerate(chips):
                cp = _remote(mine[a], mine[a], send_sems, recv_sems, 3 * a + j, (px, py, c))
                cp.start()
                sent.append(cp)
        for a in range(n):
            for j, (px, py) in enumerate(chips):
                slot = outs[a].at[4 * px + 2 * py + c]
                _remote(slot, slot, send_sems, recv_sems, 3 * a + j, (px, py, c)).wait_recv()
        for cp in sent:
            cp.wait_send()

    return pl.pallas_call(
        body, name="gather_ici", in_specs=[ANY] * n, out_specs=[ANY] * n,
        out_shape=[_sds(b.shape, b.dtype) for b in bufs], input_output_aliases={a: a for a in range(n)},
        scratch_shapes=[pltpu.SemaphoreType.DMA((3 * n,)), pltpu.SemaphoreType.DMA((3 * n,))],
    )(*bufs)


HBM_SPEC = pl.BlockSpec(memory_space=pltpu.HBM)
SEM_SPEC = pl.BlockSpec(memory_space=pltpu.SEMAPHORE)
DATAFLOW = pltpu.SideEffectType.DATAFLOW_SIDE_EFFECTING
TOKEN = (8, 128)


def _in_hbm(a):
    return pltpu.with_memory_space_constraint(a, pltpu.HBM)


PLAN_COPIES = dict(gather_ici=3, gather_d2d=N_CHIP, scatter_d2d=N_CHIP, scatter_ici=3)


def _plan(kind):
    x, y, c, chips = _place()
    sibling = (x, y, 1 - c)
    if kind == "gather_ici":
        me = 4 * x + 2 * y + c
        return [(me, me, 4 * px + 2 * py + c, (px, py, c)) for px, py in chips]
    if kind == "gather_d2d":
        return [(2 * j + c, 2 * j + c, 2 * j + 1 - c, sibling) for j in range(N_CHIP)]
    if kind == "scatter_d2d":
        return [(2 * j + 1 - c, j, j, sibling) for j in range(N_CHIP)]
    assert kind == "scatter_ici"
    return [(2 * px + py, 2 * x + y, 2 * px + py, (px, py, c)) for px, py in chips]


def split_start(kind, srcs, dsts):
    n = len(srcs)
    bufs = list(srcs) + ([] if dsts is None else list(dsts))
    nb = len(bufs)
    k = PLAN_COPIES[kind]

    def body(*refs):
        ins = refs[:n]
        outs = ins if dsts is None else refs[n:2 * n]
        send_sems, recv_sems = refs[nb], refs[nb + 1]
        token = refs[-1]
        for a in range(n):
            for j, (src, dst, _, peer) in enumerate(_plan(kind)):
                _remote(ins[a].at[src], outs[a].at[dst], send_sems, recv_sems, k * a + j, peer).start()
        token[...] = jnp.zeros_like(token)

    out = pl.pallas_call(
        body, name=kind + "_start",
        out_shape=(pltpu.SemaphoreType.DMA((k * n,)), pltpu.SemaphoreType.DMA((k * n,)),
                   *[pltpu.HBM(b.shape, b.dtype) for b in bufs], _sds(TOKEN, F32)),
        in_specs=[HBM_SPEC] * nb,
        out_specs=(SEM_SPEC, SEM_SPEC, *[HBM_SPEC] * nb, pl.BlockSpec(memory_space=pltpu.VMEM)),
        input_output_aliases={i: 2 + i for i in range(nb)},
        compiler_params=pltpu.CompilerParams(has_side_effects=DATAFLOW),
    )(*[_in_hbm(b) for b in bufs])
    return kind, out[0], out[1], list(out[2:2 + n]), None if dsts is None else list(out[2 + n:2 + nb]), out[-1]


def split_wait(pending, after):
    kind, send_sems, recv_sems, srcs, dsts, _ = pending
    n = len(srcs)
    bufs = list(srcs) + ([] if dsts is None else list(dsts))
    nb = len(bufs)
    k = PLAN_COPIES[kind]

    def body(*refs):
        ins = refs[:n]
        outs = ins if dsts is None else refs[n:2 * n]
        send_sems, recv_sems = refs[nb], refs[nb + 1]
        for a in range(n):
            for j, (src, _, landed, peer) in enumerate(_plan(kind)):
                cp = _remote(ins[a].at[src], outs[a].at[landed], send_sems, recv_sems, k * a + j, peer)
                cp.wait_send()
                cp.wait_recv()

    out = pl.pallas_call(
        body, name=kind + "_wait",
        out_shape=[pltpu.HBM(b.shape, b.dtype) for b in bufs],
        in_specs=[HBM_SPEC] * nb + [SEM_SPEC, SEM_SPEC, ANY], out_specs=[HBM_SPEC] * nb,
        input_output_aliases={i: i for i in range(nb)},
        compiler_params=pltpu.CompilerParams(has_side_effects=DATAFLOW),
    )(*bufs, send_sems, recv_sems, after)
    return list(out) if dsts is None else (list(out[:n]), list(out[n:]))


def gather_d2d(bufs):
    n = len(bufs)

    def body(*refs):
        outs = refs[n:2 * n]
        token = refs[2 * n]
        send_sems, recv_sems = refs[2 * n + 1:]
        x, y, c, _ = _place()
        sent = []
        for a in range(n):
            for j in range(N_CHIP):
                slot = outs[a].at[2 * j + c]
                cp = _remote(slot, slot, send_sems, recv_sems, N_CHIP * a + j, (x, y, 1 - c))
                cp.start()
                sent.append(cp)
        for a in range(n):
            for j in range(N_CHIP):
                slot = outs[a].at[2 * j + 1 - c]
                _remote(slot, slot, send_sems, recv_sems, N_CHIP * a + j, (x, y, 1 - c)).wait_recv()
        for cp in sent:
            cp.wait_send()
        token[...] = jnp.zeros_like(token)

    out = pl.pallas_call(
        body, name="gather_d2d", in_specs=[ANY] * n,
        out_specs=[ANY] * n + [pl.BlockSpec(memory_space=pltpu.VMEM)],
        out_shape=[_sds(b.shape, b.dtype) for b in bufs] + [_sds(TOKEN, F32)],
        input_output_aliases={a: a for a in range(n)},
        scratch_shapes=[pltpu.SemaphoreType.DMA((N_CHIP * n,)), pltpu.SemaphoreType.DMA((N_CHIP * n,))],
    )(*bufs)
    return list(out[:n]), out[n]


def pair_sum(part, got, c_arr, chip_arr):
    _, R, C = part.shape
    tr = _tile(R, 512)
    p4 = part.reshape(N_CHIP, 2, R, C)

    def body(c_ref, chip_ref, p_ref, g_ref, o_ref, own_ref):
        s = (p_ref[...].astype(F32) + g_ref[...].astype(F32)).astype(o_ref.dtype)
        o_ref[...] = s

        @pl.when(pl.program_id(1) == chip_ref[0])
        def _():
            own_ref[...] = s

    return pl.pallas_call(
        body, name="pair_sum",
        grid_spec=pltpu.PrefetchScalarGridSpec(
            num_scalar_prefetch=2, grid=(R // tr, N_CHIP),
            in_specs=[pl.BlockSpec((None, None, tr, C), lambda i, j, c_ref, chip_ref: (j, c_ref[0], i, 0)),
                      pl.BlockSpec((None, tr, C), lambda i, j, c_ref, chip_ref: (j, i, 0))],
            out_specs=[pl.BlockSpec((None, tr, C), lambda i, j, c_ref, chip_ref: (j, i, 0)),
                       pl.BlockSpec((None, tr, C), lambda i, j, c_ref, chip_ref: (chip_ref[0], i, 0))]),
        out_shape=[_sds((N_CHIP, R, C), part.dtype), _sds((N_CHIP, R, C), part.dtype)],
        compiler_params=_params(("parallel", "arbitrary")),
    )(c_arr, chip_arr, p4, got)


def adamw(w, m, v, parts_per_layer, transposed=False):
    NL, R, C = w.shape
    P = parts_per_layer[0].shape[0]
    want = max(16, ADAMW_BLOCK_ELEMS // C)
    tr = _lane_tile(R, max(want, 128)) if transposed else _tile(R, want)
    nb = R // tr

    def body(w_ref, m_ref, v_ref, *rest):
        part_refs = rest[:NL]
        g_ref, d_ref, nm_ref, nv_ref = rest[NL:]
        layer = pl.program_id(0)
        g = None
        for q in range(NL):
            s = part_refs[q][0].astype(F32)
            for t in range(1, P):
                s = s + part_refs[q][t].astype(F32)
            if transposed:
                s = s.T
            s = jnp.where(layer == q, s, 0.0)
            g = s if g is None else g + s
        wv = w_ref[...]
        nm = ADAM_B1 * m_ref[...] + (1.0 - ADAM_B1) * g
        nv = ADAM_B2 * v_ref[...] + (1.0 - ADAM_B2) * (g * g)
        m_hat = nm / (1.0 - ADAM_B1 ** ADAM_STEP)
        v_hat = nv / (1.0 - ADAM_B2 ** ADAM_STEP)
        g_ref[...] = g
        d_ref[...] = -ADAM_LR * (m_hat / (jnp.sqrt(v_hat) + ADAM_EPS) + ADAM_WD * wv)
        nm_ref[...] = nm
        nv_ref[...] = nv

    def part_spec(q):
        def block(l, i):
            return jnp.where(l == q, i, jnp.where(l < q, 0, nb - 1))

        if transposed:
            return pl.BlockSpec((P, C, tr), lambda l, i: (0, 0, block(l, i)))
        return pl.BlockSpec((P, tr, C), lambda l, i: (0, block(l, i), 0))

    blk = pl.BlockSpec((None, tr, C), lambda l, i: (l, i, 0))
    return pl.pallas_call(
        body, name="adamw", grid=(NL, nb),
        in_specs=[blk, blk, blk] + [part_spec(q) for q in range(NL)],
        out_specs=[blk, blk, blk, blk], out_shape=[_sds((NL, R, C), F32)] * 4,
        compiler_params=_params(("arbitrary", "arbitrary")),
    )(w, m, v, *parts_per_layer)


SMALL = ("norm1_g", "gmlp_ln_g", "w_spatial", "b_spatial", "mix_norm_attn_g", "mix_norm_gmlp_g", "norm2_g")
BIG = ("w_in", "w_out", "w_gate", "w_up", "w_down")
TRANSPOSED = ("w_gate", "w_up")
GATHERED_TRANSPOSED = ("w_in",)
GROUPS = (("w_in", "w_out"), ("w_gate", "w_up", "w_down"))
LANES = 128


def _pack(layers, final):
    flat = [layer[n].reshape(-1) for layer in layers for n in SMALL] + [final.reshape(-1)]
    return jnp.concatenate(flat).reshape(-1, LANES)


def _unpack(packed, like_layers, like_final):
    flat = packed.reshape(-1)
    out, off = [], 0
    for layer in like_layers:
        d = {}
        for n in SMALL:
            size = layer[n].size
            d[n] = flat[off:off + size].reshape(layer[n].shape)
            off += size
        out.append(d)
    return out, flat[off:off + like_final.size].reshape(like_final.shape)


def kernel(x, norm1_g, w_in, gmlp_ln_g, w_spatial, b_spatial, mix_norm_attn_g, mix_norm_gmlp_g, w_out, norm2_g, w_gate, w_up, w_down, final_g, loss_target, m_norm1_g, m_w_in, m_gmlp_ln_g, m_w_spatial, m_b_spatial, m_mix_norm_attn_g, m_mix_norm_gmlp_g, m_w_out, m_norm2_g, m_w_gate, m_w_up, m_w_down, m_final_g, v_norm1_g, v_w_in, v_gmlp_ln_g, v_w_spatial, v_b_spatial, v_mix_norm_attn_g, v_mix_norm_gmlp_g, v_w_out, v_norm2_g, v_w_gate, v_w_up, v_w_down, v_final_g):
    S, D = x.shape[1], x.shape[2]
    NL = norm1_g.shape[0]
    DA = D // 2
    xs = x.reshape(S, D)
    tabs = rope_tables(S)
    ax, ay, ac = lax.axis_index("x"), lax.axis_index("y"), lax.axis_index("c")
    me_arr = (4 * ax + 2 * ay + ac).astype(jnp.int32).reshape(1)
    c_arr = ac.astype(jnp.int32).reshape(1)
    chip_arr = (2 * ax + ay).astype(jnp.int32).reshape(1)
    small_w = dict(norm1_g=norm1_g, gmlp_ln_g=gmlp_ln_g, w_spatial=w_spatial, b_spatial=b_spatial,
                   mix_norm_attn_g=mix_norm_attn_g, mix_norm_gmlp_g=mix_norm_gmlp_g, norm2_g=norm2_g)
    small_m = dict(norm1_g=m_norm1_g, gmlp_ln_g=m_gmlp_ln_g, w_spatial=m_w_spatial, b_spatial=m_b_spatial,
                   mix_norm_attn_g=m_mix_norm_attn_g, mix_norm_gmlp_g=m_mix_norm_gmlp_g, norm2_g=m_norm2_g)
    small_v = dict(norm1_g=v_norm1_g, gmlp_ln_g=v_gmlp_ln_g, w_spatial=v_w_spatial, b_spatial=v_b_spatial,
                   mix_norm_attn_g=v_mix_norm_attn_g, mix_norm_gmlp_g=v_mix_norm_gmlp_g, norm2_g=v_norm2_g)
    def view(n, a):
        return jnp.swapaxes(a, 1, 2) if n in TRANSPOSED else a

    big_w = {n: view(n, a) for n, a in dict(w_in=w_in, w_out=w_out, w_gate=w_gate, w_up=w_up, w_down=w_down).items()}
    big_m = {n: view(n, a) for n, a in dict(w_in=m_w_in, w_out=m_w_out, w_gate=m_w_gate, w_up=m_w_up, w_down=m_w_down).items()}
    big_v = {n: view(n, a) for n, a in dict(w_in=v_w_in, w_out=v_w_out, w_gate=v_w_gate, w_up=v_w_up, w_down=v_w_down).items()}

    def layer_small(l):
        ws = w_spatial[l]
        return dict(norm1_g=norm1_g[l][None], gmlp_ln_g=gmlp_ln_g[l][None], mix_norm_attn_g=mix_norm_attn_g[l][None],
                    mix_norm_gmlp_g=mix_norm_gmlp_g[l][None], norm2_g=norm2_g[l][None], w_spatial=ws,
                    w_spatial_t=jnp.swapaxes(ws, 1, 2), bcol=jnp.broadcast_to(b_spatial[l][:, :, None], ws.shape))

    n_stages = 2 * NL
    zero = jnp.zeros(TOKEN, F32)
    TOK = 5

    def stage_bufs(s, dep):
        return [to_slot(big_w[n], s // 2, me_arr, dep, BF16, transpose=(n in GATHERED_TRANSPOSED)) for n in GROUPS[s % 2]]

    def stage_weights(s, bufs):
        gw = dict(zip(GROUPS[s % 2], bufs))
        if s % 2 == 0:
            gw["w_in"] = gw["w_in"].reshape(-1, D)
            gw["w_out"] = gw["w_out"].reshape(2, DA, D)
        else:
            gw = {n: b.reshape(-1, D) for n, b in gw.items()}
        return gw

    ici = {0: split_start("gather_ici", stage_bufs(0, zero), None)}
    ici[1] = split_start("gather_ici", stage_bufs(1, ici[0][TOK]), None)
    h = xs
    d2d = {0: split_start("gather_d2d", split_wait(ici[0], h), None)}
    ready = {0: split_wait(d2d[0], h)}
    saved, weights = [], []
    for s in range(n_stages):
        deps = [ici[1][TOK]] if s == 0 else []
        if 1 <= s < n_stages - 1:
            d2d[s + 1] = split_start("gather_d2d", split_wait(ici[s + 1], h), None)
            deps.append(d2d[s + 1][TOK])
        if s + 2 < n_stages:
            ici[s + 2] = split_start("gather_ici", stage_bufs(s + 2, deps[-1]), None)
            deps.append(ici[s + 2][TOK])
        dep = sum(deps[1:], deps[0]) if deps else zero
        gw = stage_weights(s, ready[s])
        sm = layer_small(s // 2)
        if s % 2 == 0:
            sm["norm1_g"] = sm["norm1_g"] + dep[0, 0]
            h, sv = mixer_fwd(h, sm, gw, tabs)
        else:
            sm["norm2_g"] = sm["norm2_g"] + dep[0, 0]
            h, sv = ffn_fwd(h, sm, gw)
        saved.append(sv)
        weights.append(gw)
        if s == 0:
            d2d[1] = split_start("gather_d2d", split_wait(ici[1], h), None)
        if s + 1 < n_stages:
            ready[s + 1] = split_wait(d2d[s + 1], h)
    loss_part, dx, dxb, d_final = loss_and_grad(h, final_g[None], loss_target.reshape(S, D))
    loss = lax.psum(loss_part[0, 0], ("x", "y", "c"))

    big_sums = [dict() for _ in range(NL)]
    small_grads = [dict() for _ in range(NL)]
    pending, pending_stage = None, None
    dep = zero
    for s in reversed(range(n_stages)):
        l = s // 2
        crossing = []

        def on_grads(big, s=s, crossing=crossing):
            parts = [big[n].reshape((N_DEV, -1, big[n].shape[-1])) for n in GROUPS[s % 2]]
            gots = [lax.empty((N_CHIP,) + p.shape[1:], p.dtype) for p in parts]
            crossing.append(split_start("scatter_d2d", parts, gots))
            return crossing[0][TOK]

        if s % 2 == 1:
            dx, dxb, small = ffn_bwd(dx, dxb, saved[s], layer_small(l), weights[s], dep, on_grads)
        else:
            dx, dxb, small = mixer_bwd(dx, dxb, saved[s], layer_small(l), weights[s], tabs, dep, on_grads)
        small_grads[l].update(small)
        if pending is not None:
            big_sums[pending_stage // 2].update(zip(GROUPS[pending_stage % 2], split_wait(pending, dx)[1]))
        parts, got = split_wait(crossing[0], dx)
        pairs = [pair_sum(p, g, c_arr, chip_arr) for p, g in zip(parts, got)]
        pending, pending_stage = split_start("scatter_ici", [t for t, _ in pairs], [o for _, o in pairs]), s
        dep = pending[TOK]

    def update_big(n):
        res = adamw(big_w[n], big_m[n], big_v[n], [big_sums[l][n] for l in range(NL)], n in GATHERED_TRANSPOSED)
        return [view(n, t) for t in res], res[0][0, :TOKEN[0], :TOKEN[1]]

    big_out = {}
    after_ffn = zero
    for n in GROUPS[1]:
        big_out[n], piece = update_big(n)
        after_ffn = after_ffn + piece

    packed = to_slot(_pack(small_grads, d_final), None, me_arr, dep + after_ffn, F32)
    gathered_small = gather_d2d(gather_ici([packed]))[0][0]
    pw = _pack([{n: small_w[n][l] for n in SMALL} for l in range(NL)], final_g)[None]
    pm = _pack([{n: small_m[n][l] for n in SMALL} for l in range(NL)], m_final_g)[None]
    pv = _pack([{n: small_v[n][l] for n in SMALL} for l in range(NL)], v_final_g)[None]
    like_layers = [{n: small_w[n][l] for n in SMALL} for l in range(NL)]
    small_res = adamw(pw, pm, pv, [gathered_small])
    small_out = [_unpack(t[0], like_layers, final_g) for t in small_res]
    big_sums[pending_stage // 2].update(zip(GROUPS[pending_stage % 2], split_wait(pending, small_res[0])[1]))

    def small_stack(k, n):
        return jnp.stack([small_out[k][0][l][n] for l in range(NL)])

    for n in GROUPS[0]:
        big_out[n], _ = update_big(n)

    order = ("norm1_g", "w_in", "gmlp_ln_g", "w_spatial", "b_spatial", "mix_norm_attn_g", "mix_norm_gmlp_g", "w_out",
             "norm2_g", "w_gate", "w_up", "w_down")
    outs = [loss, dx.reshape(x.shape)]
    for k in range(4):
        for n in order:
            outs.append(big_out[n][k] if n in BIG else small_stack(k, n))
        outs.append(small_out[k][1])
    return tuple(outs)
```

```python
import functools

import jax
import jax.numpy as jnp
from jax import lax
from jax.experimental import pallas as pl
from jax.experimental.pallas import tpu as pltpu

F32 = jnp.float32
BF16 = jnp.bfloat16

HEAD_DIM = 128
CHUNK = 128
GROUP = 128
N_SIDE = 64
DILATIONS = (1, 4, 16)
TQ = 128
HALO = 64
EPS = 1e-6
NEG = -1e30
ROPE_THETA = 10000.0
ADAM_LR = 0.001
ADAM_B1 = 0.9
ADAM_B2 = 0.999
ADAM_EPS = 1e-08
ADAM_WD = 0.01
ADAM_STEP = 10
N_DEV = 8
N_CHIP = 4
VMEM_LIMIT_V7X = 56 * 1024 * 1024
ADAMW_BLOCK_ELEMS = 128 * 1024
MESH = pl.DeviceIdType.MESH
ANY = pl.BlockSpec(memory_space=pl.ANY)


def _params(sem=None):
    return pltpu.CompilerParams(dimension_semantics=sem, vmem_limit_bytes=VMEM_LIMIT_V7X)


def _tile(n, want):
    for t in range(min(want, n), 15, -1):
        if n % t == 0 and t % 16 == 0:
            return t
    return n


def _sds(shape, dtype):
    return jax.ShapeDtypeStruct(shape, dtype)


def _dot(a, b, dims):
    return lax.dot_general(a, b, (dims, ((), ())), preferred_element_type=F32)


NN = ((1,), (0,))
NT = ((1,), (1,))
TN = ((0,), (0,))


def _sigmoid(x):
    return 1.0 / (1.0 + jnp.exp(-x))


def _gelu(x):
    return 0.5 * x * (1.0 + lax.erf(x * 0.7071067811865476))


def _gelu_grad(x):
    cdf = 0.5 * (1.0 + lax.erf(x * 0.7071067811865476))
    pdf = 0.3989422804014327 * jnp.exp(-0.5 * x * x)
    return cdf + x * pdf


def rms_fwd(x, g):
    S, D = x.shape
    tr = _tile(S, 512)

    def body(x_ref, g_ref, o_ref):
        xv = x_ref[...]
        r = lax.rsqrt(jnp.mean(xv * xv, axis=-1, keepdims=True) + EPS)
        o_ref[...] = (xv * r * g_ref[...]).astype(o_ref.dtype)

    return pl.pallas_call(
        body, name="rms_fwd", grid=(S // tr,),
        in_specs=[pl.BlockSpec((tr, D), lambda i: (i, 0)), pl.BlockSpec((1, D), lambda i: (0, 0))],
        out_specs=pl.BlockSpec((tr, D), lambda i: (i, 0)),
        out_shape=_sds((S, D), BF16), compiler_params=_params(("parallel",)),
    )(x, g)


def _rms_bwd_math(xv, gv, dy):
    r = lax.rsqrt(jnp.mean(xv * xv, axis=-1, keepdims=True) + EPS)
    dyg = dy * gv
    dx = r * dyg - xv * (r * r * r) * jnp.mean(dyg * xv, axis=-1, keepdims=True)
    dg = jnp.sum(dy * xv * r, axis=0, keepdims=True)
    return dx, dg


def rms_bwd_res(x, g, dy, dres):
    S, D = x.shape
    tr = _tile(S, 512)

    def body(x_ref, g_ref, dy_ref, dres_ref, dx_ref, dxb_ref, dg_ref):
        dx, dg = _rms_bwd_math(x_ref[...], g_ref[...], dy_ref[...])
        tot = dres_ref[...] + dx
        dx_ref[...] = tot
        dxb_ref[...] = tot.astype(dxb_ref.dtype)

        @pl.when(pl.program_id(0) == 0)
        def _():
            dg_ref[...] = dg

        @pl.when(pl.program_id(0) > 0)
        def _():
            dg_ref[...] += dg

    row = pl.BlockSpec((tr, D), lambda i: (i, 0))
    vec = pl.BlockSpec((1, D), lambda i: (0, 0))
    return pl.pallas_call(
        body, name="rms_bwd_res", grid=(S // tr,),
        in_specs=[row, vec, row, row], out_specs=[row, row, vec],
        out_shape=[_sds((S, D), F32), _sds((S, D), BF16), _sds((1, D), F32)],
        compiler_params=_params(("arbitrary",)),
    )(x, g, dy, dres)


def loss_and_grad(x, g, target):
    S, D = x.shape
    tr = _tile(S, 512)

    def body(x_ref, g_ref, t_ref, loss_ref, dx_ref, dxb_ref, dg_ref):
        xv = x_ref[...]
        gv = g_ref[...]
        r = lax.rsqrt(jnp.mean(xv * xv, axis=-1, keepdims=True) + EPS)
        y = xv * r * gv
        diff = y - t_ref[...]
        part = 0.5 * jnp.sum(jnp.mean(diff * diff, axis=-1, keepdims=True), axis=0, keepdims=True)
        dy = diff * (1.0 / D)
        dx, dg = _rms_bwd_math(xv, gv, dy)
        dx_ref[...] = dx
        dxb_ref[...] = dx.astype(dxb_ref.dtype)

        @pl.when(pl.program_id(0) == 0)
        def _():
            dg_ref[...] = dg
            loss_ref[...] = jnp.broadcast_to(part, loss_ref.shape)

        @pl.when(pl.program_id(0) > 0)
        def _():
            dg_ref[...] += dg
            loss_ref[...] += jnp.broadcast_to(part, loss_ref.shape)

    row = pl.BlockSpec((tr, D), lambda i: (i, 0))
    vec = pl.BlockSpec((1, D), lambda i: (0, 0))
    return pl.pallas_call(
        body, name="loss_and_grad", grid=(S // tr,),
        in_specs=[row, vec, row],
        out_specs=[pl.BlockSpec((8, 128), lambda i: (0, 0)), row, row, vec],
        out_shape=[_sds((8, 128), F32), _sds((S, D), F32), _sds((S, D), BF16), _sds((1, D), F32)],
        compiler_params=_params(("arbitrary",)),
    )(x, g, target)


def _mm_body(n_pairs, dims, red_axis, n_red, has_res):
    def body(*refs):
        ins = refs[:2 * n_pairs]
        res_ref = refs[2 * n_pairs] if has_res else None
        o_ref = refs[2 * n_pairs + has_res]
        p = None
        for t in range(n_pairs):
            d = _dot(ins[2 * t][...], ins[2 * t + 1][...], dims)
            p = d if p is None else p + d
        if red_axis is None:
            if has_res:
                p = res_ref[...] + p
            o_ref[...] = p.astype(o_ref.dtype)
            return
        acc_ref = o_ref if o_ref.dtype == F32 else refs[2 * n_pairs + has_res + 1]
        r = pl.program_id(red_axis)

        @pl.when(r == 0)
        def _():
            acc_ref[...] = res_ref[...] + p if has_res else p

        @pl.when(r > 0)
        def _():
            acc_ref[...] += p

        if acc_ref is not o_ref:
            @pl.when(r == n_red - 1)
            def _():
                o_ref[...] = acc_ref[...].astype(o_ref.dtype)

    return body


def mm_in(h, w):
    S, D = h.shape
    N = w.shape[0]
    tm = _tile(S, 1024)
    tn = _tile(N, 1024)
    return pl.pallas_call(
        _mm_body(1, NT, None, 1, False), name="mm_in", grid=(S // tm, N // tn),
        in_specs=[pl.BlockSpec((tm, D), lambda i, j: (i, 0)), pl.BlockSpec((tn, D), lambda i, j: (j, 0))],
        out_specs=pl.BlockSpec((tm, tn), lambda i, j: (i, j)),
        out_shape=_sds((S, N), F32), compiler_params=_params(("parallel", "parallel")),
    )(h, w)


def mm_out(x, mix_a, mix_g, w):
    S, D = x.shape
    DA = mix_a.shape[1]
    tm = _tile(S, 1024)
    tn = _tile(D, 1024)
    act = pl.BlockSpec((tm, DA), lambda i, j: (i, 0))
    return pl.pallas_call(
        _mm_body(2, NN, None, 1, True), name="mm_out", grid=(S // tm, D // tn),
        in_specs=[act, pl.BlockSpec((None, DA, tn), lambda i, j: (0, 0, j)),
                  act, pl.BlockSpec((None, DA, tn), lambda i, j: (1, 0, j)),
                  pl.BlockSpec((tm, tn), lambda i, j: (i, j))],
        out_specs=pl.BlockSpec((tm, tn), lambda i, j: (i, j)),
        out_shape=_sds((S, D), F32), compiler_params=_params(("parallel", "parallel")),
    )(mix_a, w, mix_g, w, x)


def _lane_tile(n, want):
    for t in range(min(want, n) // 128 * 128, 127, -128):
        if n % t == 0:
            return t
    return n


def ff_fwd(h, wg, wu):
    S, D = h.shape
    F = wg.shape[0]
    tm = _tile(S, 1024)
    tn = _lane_tile(F, 512)

    def body(h_ref, wg_ref, wu_ref, dfdu_ref, dfdg_ref, ff_ref):
        hv = h_ref[...]
        g = _dot(hv, wg_ref[...], NT)
        u = _dot(hv, wu_ref[...], NT)
        sg = _sigmoid(g)
        act = g * sg
        dfdu_ref[...] = act.astype(dfdu_ref.dtype)
        dfdg_ref[...] = (u * (sg * (1.0 + g * (1.0 - sg)))).astype(dfdg_ref.dtype)
        ff_ref[...] = (act * u).astype(ff_ref.dtype)

    wspec = pl.BlockSpec((tn, D), lambda i, j: (j, 0))
    ospec = pl.BlockSpec((tm, tn), lambda i, j: (i, j))
    return pl.pallas_call(
        body, name="ff_fwd", grid=(S // tm, F // tn),
        in_specs=[pl.BlockSpec((tm, D), lambda i, j: (i, 0)), wspec, wspec],
        out_specs=[ospec, ospec, ospec],
        out_shape=[_sds((S, F), BF16), _sds((S, F), BF16), _sds((S, F), BF16)],
        compiler_params=_params(("parallel", "parallel")),
    )(h, wg, wu)


def mm_down(x, ff, wd):
    S, D = x.shape
    F = ff.shape[1]
    tm = _tile(S, 1024)
    tn = _lane_tile(D, 512)
    blk = pl.BlockSpec((tm, tn), lambda i, j: (i, j))
    return pl.pallas_call(
        _mm_body(1, NN, None, 1, True), name="mm_down", grid=(S // tm, D // tn),
        in_specs=[pl.BlockSpec((tm, F), lambda i, j: (i, 0)), pl.BlockSpec((F, tn), lambda i, j: (0, j)), blk],
        out_specs=blk, out_shape=_sds((S, D), F32), compiler_params=_params(("parallel", "parallel")),
    )(ff, wd, x)


def ff_bwd_act(dxb, wd, dfdg, dfdu, dep):
    S, D = dxb.shape
    F = wd.shape[0]
    tm = _tile(S, 1024)
    tn = _lane_tile(F, 512)

    def body(dx_ref, wd_ref, dfdg_ref, dfdu_ref, dep_ref, dgate_ref, dup_ref):
        dff = _dot(dx_ref[...], wd_ref[...], NT)
        dup_ref[...] = (dff * dfdu_ref[...].astype(F32)).astype(dup_ref.dtype)
        dgate_ref[...] = (dff * dfdg_ref[...].astype(F32)).astype(dgate_ref.dtype)

    aspec = pl.BlockSpec((tm, tn), lambda i, j: (i, j))
    return pl.pallas_call(
        body, name="ff_bwd_act", grid=(S // tm, F // tn),
        in_specs=[pl.BlockSpec((tm, D), lambda i, j: (i, 0)), pl.BlockSpec((tn, D), lambda i, j: (j, 0)), aspec, aspec,
                  pl.BlockSpec(TOKEN, lambda i, j: (0, 0))],
        out_specs=[aspec, aspec],
        out_shape=[_sds((S, F), BF16), _sds((S, F), BF16)],
        compiler_params=_params(("parallel", "parallel")),
    )(dxb, wd, dfdg, dfdu, dep)


def dw_down(ff, dxb):
    S, F = ff.shape
    D = dxb.shape[1]
    tf = _lane_tile(F, 256)
    return pl.pallas_call(
        _mm_body(1, TN, None, 1, False), name="dw_down", grid=(F // tf,),
        in_specs=[pl.BlockSpec((S, tf), lambda k: (0, k)), pl.BlockSpec((S, D), lambda k: (0, 0))],
        out_specs=pl.BlockSpec((tf, D), lambda k: (k, 0)),
        out_shape=_sds((F, D), BF16), compiler_params=_params(("parallel",)),
    )(ff, dxb)


def dw_gate_up(h, dgate, dup):
    S, D = h.shape
    F = dgate.shape[1]
    f = _lane_tile(F, 256)

    def body(h_ref, dg_ref, du_ref, og_ref, ou_ref):
        hv = h_ref[...]
        og_ref[...] = _dot(dg_ref[...], hv, TN).astype(og_ref.dtype)
        ou_ref[...] = _dot(du_ref[...], hv, TN).astype(ou_ref.dtype)

    aspec = pl.BlockSpec((S, f), lambda k: (0, k))
    ospec = pl.BlockSpec((f, D), lambda k: (k, 0))
    return pl.pallas_call(
        body, name="dw_gate_up", grid=(F // f,),
        in_specs=[pl.BlockSpec((S, D), lambda k: (0, 0)), aspec, aspec],
        out_specs=[ospec, ospec],
        out_shape=[_sds((F, D), BF16), _sds((F, D), BF16)],
        compiler_params=_params(("parallel",)),
    )(h, dgate, dup)


def dh_ff(dgate, dup, wg, wu, dep):
    S, F = dgate.shape
    D = wg.shape[1]
    tm = _tile(S, 1024)
    tn = _lane_tile(D, 256)
    blk = pl.BlockSpec((tm, tn), lambda i, j: (i, j))

    def part(name, act, w, prev):
        def body(a_ref, w_ref, dep_ref, *rest):
            p = _dot(a_ref[...], w_ref[...], NN)
            rest[-1][...] = p if prev is None else rest[0][...] + p

        return pl.pallas_call(
            body, name=name, grid=(S // tm, D // tn),
            in_specs=[pl.BlockSpec((tm, F), lambda i, j: (i, 0)), pl.BlockSpec((F, tn), lambda i, j: (0, j)),
                      pl.BlockSpec(TOKEN, lambda i, j: (0, 0))] + ([] if prev is None else [blk]),
            out_specs=blk, out_shape=_sds((S, D), F32), input_output_aliases={} if prev is None else {3: 0},
            compiler_params=_params(("parallel", "parallel")),
        )(act, w, dep, *([] if prev is None else [prev]))

    return part("dh_ff_up", dup, wu, part("dh_ff_gate", dgate, wg, None))


def dmix_mm(dxb, w, dep):
    S, D = dxb.shape
    tm = _tile(S, 1024)
    tn = _tile(D, 1024)

    def body(dx_ref, w_ref, dep_ref, o_ref):
        o_ref[...] = _dot(dx_ref[...], w_ref[...], NT)

    return pl.pallas_call(
        body, name="dmix_mm", grid=(S // tm, D // tn),
        in_specs=[pl.BlockSpec((tm, D), lambda i, j: (i, 0)), pl.BlockSpec((tn, D), lambda i, j: (j, 0)),
                  pl.BlockSpec(TOKEN, lambda i, j: (0, 0))],
        out_specs=pl.BlockSpec((tm, tn), lambda i, j: (i, j)),
        out_shape=_sds((S, D), F32), compiler_params=_params(("parallel", "parallel")),
    )(dxb, w, dep)


def dw_out(mix_a, mix_g, dxb):
    S, DA = mix_a.shape
    D = dxb.shape[1]
    tn = _lane_tile(D, 512)

    def half(name, m):
        return pl.pallas_call(
            _mm_body(1, TN, None, 1, False), name=name, grid=(D // tn,),
            in_specs=[pl.BlockSpec((S, DA), lambda j: (0, 0)), pl.BlockSpec((S, tn), lambda j: (0, j))],
            out_specs=pl.BlockSpec((DA, tn), lambda j: (0, j)),
            out_shape=_sds((DA, D), BF16), compiler_params=_params(("parallel",)),
        )(m, dxb)

    return half("dw_out_a", mix_a), half("dw_out_g", mix_g)


def dw_in(h, dproj):
    S, D = h.shape
    N = dproj.shape[1]
    tn = _lane_tile(N, 256)
    return pl.pallas_call(
        _mm_body(1, TN, None, 1, False), name="dw_in", grid=(N // tn,),
        in_specs=[pl.BlockSpec((S, tn), lambda k: (0, k)), pl.BlockSpec((S, D), lambda k: (0, 0))],
        out_specs=pl.BlockSpec((tn, D), lambda k: (k, 0)),
        out_shape=_sds((N, D), BF16), compiler_params=_params(("parallel",)),
    )(dproj, h)


def dh_in(dproj, w, dep):
    S, N = dproj.shape
    D = w.shape[1]
    tm = _tile(S, 1024)
    tn = _lane_tile(D, 512)

    def body(dp_ref, w_ref, dep_ref, o_ref):
        o_ref[...] = _dot(dp_ref[...], w_ref[...], NN)

    return pl.pallas_call(
        body, name="dh_in", grid=(S // tm, D // tn),
        in_specs=[pl.BlockSpec((tm, N), lambda i, j: (i, 0)), pl.BlockSpec((N, tn), lambda i, j: (0, j)),
                  pl.BlockSpec(TOKEN, lambda i, j: (0, 0))],
        out_specs=pl.BlockSpec((tm, tn), lambda i, j: (i, j)),
        out_shape=_sds((S, D), F32), compiler_params=_params(("parallel", "parallel")),
    )(dproj, w, dep)


def rope_tables(S):
    pos = jnp.arange(S, dtype=F32)
    inv = ROPE_THETA ** (-jnp.arange(0, HEAD_DIM, 2, dtype=F32) / HEAD_DIM)
    ang = pos[:, None] * inv[None, :]
    cos, sin = jnp.cos(ang), jnp.sin(ang)
    return jnp.concatenate([cos, cos], axis=-1), jnp.concatenate([-sin, sin], axis=-1)


def _rot_half(t):
    return pltpu.roll(t, HEAD_DIM // 2, 1)


PERM_ROWS = 256


def _by_residue(y, d, out_ref):
    tr = y.shape[0]
    n = tr // d
    o = lax.broadcasted_iota(jnp.int32, (tr, tr), 0)
    i = lax.broadcasted_iota(jnp.int32, (tr, tr), 1)
    src = jnp.bitwise_and(o, n - 1) * d + lax.shift_right_logical(o, n.bit_length() - 1)
    z = _dot((src == i).astype(y.dtype), y, NN).astype(out_ref.dtype)
    for r in range(d):
        out_ref[r] = z[r * n:(r + 1) * n]


def _from_residue(ref, d, terms):
    z = jnp.concatenate([ref[r] for r in range(d)], axis=0)
    tr = z.shape[0]
    n = tr // d
    i = lax.broadcasted_iota(jnp.int32, (tr, tr), 0)
    o = lax.broadcasted_iota(jnp.int32, (tr, tr), 1)
    src = jnp.bitwise_and(i, d - 1) * n + lax.shift_right_logical(i, d.bit_length() - 1)
    pick = (src == o).astype(BF16)
    out = None
    rest = z
    for _ in range(terms):
        piece = rest.astype(BF16)
        got = _dot(pick, piece, NN)
        out = got if out is None else out + got
        rest = rest - piece.astype(F32)
    return out


def _residue_blocks(widths):
    return [pl.BlockSpec((d, PERM_ROWS // d, W), lambda i: (0, i, 0)) for d, W in zip(DILATIONS[1:], widths)]


def _residue_outputs(S, W, dtype):
    shapes = [_sds((d, S // d, W), dtype) for d in DILATIONS[1:]]
    specs = [pl.BlockSpec((d, PERM_ROWS // d, W), lambda i: (0, i, 0)) for d in DILATIONS[1:]]
    return shapes, specs


def rope_qkv(proj, cos2, sin2, DA):
    S = proj.shape[0]
    tr = PERM_ROWS
    nh = DA // HEAD_DIM

    def body(q_ref, k_ref, v_ref, cos_ref, sin_ref, o_ref, *by_res):
        c = cos_ref[...]
        s = sin_ref[...]
        for h in range(nh):
            sl = slice(h * HEAD_DIM, (h + 1) * HEAD_DIM)
            for j, ref in enumerate((q_ref, k_ref)):
                t = ref[:, sl]
                o_ref[:, j * DA + h * HEAD_DIM:j * DA + (h + 1) * HEAD_DIM] = (t * c + _rot_half(t) * s).astype(o_ref.dtype)
        o_ref[:, 2 * DA:3 * DA] = v_ref[...].astype(o_ref.dtype)
        y = o_ref[...]
        for d, ref in zip(DILATIONS[1:], by_res):
            _by_residue(y, d, ref)

    tab = pl.BlockSpec((tr, HEAD_DIM), lambda i: (i, 0))
    shapes, specs = _residue_outputs(S, 3 * DA, BF16)
    out = pl.pallas_call(
        body, name="rope_qkv", grid=(S // tr,),
        in_specs=[pl.BlockSpec((tr, DA), lambda i: (i, 0)), pl.BlockSpec((tr, DA), lambda i: (i, 1)),
                  pl.BlockSpec((tr, DA), lambda i: (i, 2)), tab, tab],
        out_specs=[pl.BlockSpec((tr, 3 * DA), lambda i: (i, 0))] + specs,
        out_shape=[_sds((S, 3 * DA), BF16)] + shapes, compiler_params=_params(("parallel",)),
    )(proj, proj, proj, cos2, sin2)
    return [out[0].reshape(1, S, 3 * DA)] + list(out[1:])


def assemble_dproj(dqs, dks, dvs, duv, cos2, sin2):
    S, DA = dqs[0].shape[1:]
    tr = PERM_ROWS
    nh = DA // HEAD_DIM

    def total(trio):
        t = trio[0][...]
        for d, ref in zip(DILATIONS[1:], trio[1:]):
            t = t + _from_residue(ref, d, 2)
        return t

    def body(*refs):
        dq_refs, dk_refs, dv_refs = refs[0:3], refs[3:6], refs[6:9]
        duv_ref, cos_ref, sin_ref, o_ref = refs[9:13]
        c = cos_ref[...]
        s = sin_ref[...]
        for j, trio in enumerate((dq_refs, dk_refs)):
            t_all = total(trio)
            for h in range(nh):
                t = t_all[:, h * HEAD_DIM:(h + 1) * HEAD_DIM]
                o_ref[:, j * DA + h * HEAD_DIM:j * DA + (h + 1) * HEAD_DIM] = (t * c - _rot_half(t) * s).astype(o_ref.dtype)
        o_ref[:, 2 * DA:3 * DA] = total(dv_refs).astype(o_ref.dtype)
        o_ref[:, 3 * DA:5 * DA] = duv_ref[...]

    trio_specs = [pl.BlockSpec((None, tr, DA), lambda i: (0, i, 0))] + _residue_blocks([DA, DA])
    tab = pl.BlockSpec((tr, HEAD_DIM), lambda i: (i, 0))
    return pl.pallas_call(
        body, name="assemble_dproj", grid=(S // tr,),
        in_specs=trio_specs * 3 + [pl.BlockSpec((tr, 2 * DA), lambda i: (i, 0)), tab, tab],
        out_specs=pl.BlockSpec((tr, 5 * DA), lambda i: (i, 0)),
        out_shape=_sds((S, 5 * DA), BF16), compiler_params=_params(("parallel",)),
    )(*dqs, *dks, *dvs, duv, cos2, sin2)


STAT_W = 128


def _lanes_per_head(DA):
    return STAT_W // (DA // HEAD_DIM)


def _halo_specs(L, width_blocks, col):
    per = TQ // HALO
    last = L // HALO - 1
    W = width_blocks
    return [
        pl.BlockSpec((HALO, W), lambda r, i: (jnp.maximum(i * per - 1, 0), col(r))),
        pl.BlockSpec((TQ, W), lambda r, i: (i, col(r))),
        pl.BlockSpec((HALO, W), lambda r, i: (jnp.minimum((i + 1) * per, last), col(r))),
    ]


def _halo_specs3(L, W, c):
    per = TQ // HALO
    last = L // HALO - 1
    return [
        pl.BlockSpec((None, HALO, W), lambda r, i: (r, jnp.maximum(i * per - 1, 0), c)),
        pl.BlockSpec((None, TQ, W), lambda r, i: (r, i, c)),
        pl.BlockSpec((None, HALO, W), lambda r, i: (r, jnp.minimum((i + 1) * per, last), c)),
    ]


def _cat3(refs, sl):
    return jnp.concatenate([refs[0][:, sl], refs[1][:, sl], refs[2][:, sl]], axis=0)


def attn_fwd(qkv, d, DA):
    L = qkv.shape[1]
    S = L * d
    nh = DA // HEAD_DIM
    lph = _lanes_per_head(DA)
    scale = HEAD_DIM ** -0.5
    TK = TQ + 2 * HALO

    def body(q_ref, kp_ref, kc_ref, kn_ref, vp_ref, vc_ref, vn_ref, o_ref, lse_ref):
        i = pl.program_id(1)
        row = lax.broadcasted_iota(jnp.int32, (TQ, TK), 0)
        col = lax.broadcasted_iota(jnp.int32, (TQ, TK), 1)
        kpos = i * TQ - HALO + col
        mask = (jnp.abs(col - HALO - row) <= N_SIDE) & (kpos >= 0) & (kpos < L)
        for h in range(nh):
            sl = slice(h * HEAD_DIM, (h + 1) * HEAD_DIM)
            k = _cat3((kp_ref, kc_ref, kn_ref), sl)
            v = _cat3((vp_ref, vc_ref, vn_ref), sl)
            s = _dot(q_ref[:, sl], k, NT) * scale
            s = jnp.where(mask, s, NEG)
            m = jnp.max(s, axis=-1, keepdims=True)
            p = jnp.exp(s - m)
            l = jnp.sum(p, axis=-1, keepdims=True)
            o = _dot(p.astype(v.dtype), v, NN) / l
            o_ref[:, sl] = o
            lse_ref[:, h * lph:(h + 1) * lph] = jnp.broadcast_to(m + jnp.log(l), (TQ, lph))

    out_spec = pl.BlockSpec((None, TQ, DA), lambda r, i: (r, i, 0))
    o, lse = pl.pallas_call(
        body, name="attn_fwd_d%d" % d, grid=(d, L // TQ),
        in_specs=[pl.BlockSpec((None, TQ, DA), lambda r, i: (r, i, 0))] + _halo_specs3(L, DA, 1) + _halo_specs3(L, DA, 2),
        out_specs=[out_spec, pl.BlockSpec((TQ, STAT_W), lambda r, i: (i, r))],
        out_shape=[_sds((d, L, DA), F32), _sds((L, d * STAT_W), F32)],
        compiler_params=_params(("parallel", "parallel")),
    )(*([qkv] * 7))
    return o, lse.reshape(S, STAT_W)


def attn_merge(os_, lses, g):
    S, DA = os_[0].shape[1:]
    tr = PERM_ROWS
    nh = DA // HEAD_DIM
    lph = _lanes_per_head(DA)

    def body(o0, o1, o2, l0, l1, l2, g_ref, a_ref, lse_ref, mix_ref):
        a0, a1, a2 = l0[...], l1[...], l2[...]
        m = jnp.maximum(jnp.maximum(a0, a1), a2)
        e0, e1, e2 = jnp.exp(a0 - m), jnp.exp(a1 - m), jnp.exp(a2 - m)
        den = e0 + e1 + e2
        w0, w1, w2 = e0 / den, e1 / den, e2 / den
        lse_ref[...] = m + jnp.log(den)
        v0 = o0[...]
        v1 = _from_residue(o1, DILATIONS[1], 2)
        v2 = _from_residue(o2, DILATIONS[2], 2)
        heads = []
        for h in range(nh):
            sl = slice(h * HEAD_DIM, (h + 1) * HEAD_DIM)
            c = slice(h * lph, h * lph + 1)
            heads.append(w0[:, c] * v0[:, sl] + w1[:, c] * v1[:, sl] + w2[:, c] * v2[:, sl])
        a = jnp.concatenate(heads, axis=1)
        a_ref[...] = a
        r = lax.rsqrt(jnp.mean(a * a, axis=-1, keepdims=True) + EPS)
        mix_ref[...] = (a * r * g_ref[...]).astype(mix_ref.dtype)

    blk = pl.BlockSpec((tr, DA), lambda i: (i, 0))
    stat = pl.BlockSpec((tr, STAT_W), lambda i: (i, 0))
    return pl.pallas_call(
        body, name="attn_merge", grid=(S // tr,),
        in_specs=[pl.BlockSpec((None, tr, DA), lambda i: (0, i, 0))] + _residue_blocks([DA, DA]) + [stat] * 3
        + [pl.BlockSpec((1, DA), lambda i: (0, 0))],
        out_specs=[blk, stat, blk],
        out_shape=[_sds((S, DA), F32), _sds((S, STAT_W), F32), _sds((S, DA), BF16)],
        compiler_params=_params(("parallel",)),
    )(*os_, *lses, g)


def attn_out_bwd(a, g, dmix):
    S, DA = a.shape
    tr = PERM_ROWS
    nh = DA // HEAD_DIM
    lph = _lanes_per_head(DA)

    def body(a_ref, g_ref, dy_ref, do_ref, dl_ref, dg_ref, *by_res):
        av = a_ref[...]
        dx, dg = _rms_bwd_math(av, g_ref[...], dy_ref[...])
        dob = dx.astype(do_ref.dtype)
        do_ref[...] = dob
        for d, ref in zip(DILATIONS[1:], by_res):
            _by_residue(dob, d, ref)
        prod = dx * av
        for h in range(nh):
            sl = slice(h * HEAD_DIM, (h + 1) * HEAD_DIM)
            dl_ref[:, h * lph:(h + 1) * lph] = jnp.broadcast_to(jnp.sum(prod[:, sl], axis=-1, keepdims=True), (tr, lph))

        @pl.when(pl.program_id(0) == 0)
        def _():
            dg_ref[...] = dg

        @pl.when(pl.program_id(0) > 0)
        def _():
            dg_ref[...] += dg

    blk = pl.BlockSpec((tr, DA), lambda i: (i, 0))
    vec = pl.BlockSpec((1, DA), lambda i: (0, 0))
    shapes, specs = _residue_outputs(S, DA, BF16)
    out = pl.pallas_call(
        body, name="attn_out_bwd", grid=(S // tr,),
        in_specs=[blk, vec, blk], out_specs=[blk, pl.BlockSpec((tr, STAT_W), lambda i: (i, 0)), vec] + specs,
        out_shape=[_sds((S, DA), BF16), _sds((S, STAT_W), F32), _sds((1, DA), F32)] + shapes,
        compiler_params=_params(("arbitrary",)),
    )(a, g, dmix)
    return [out[0].reshape(1, S, DA)] + list(out[3:]), out[1], out[2]


def attn_bwd_dq(qkv, dob, lse, dl, d, DA):
    L = qkv.shape[1]
    S = L * d
    nh = DA // HEAD_DIM
    lph = _lanes_per_head(DA)
    scale = HEAD_DIM ** -0.5
    TK = TQ + 2 * HALO

    def body(q_ref, kp_ref, kc_ref, kn_ref, vp_ref, vc_ref, vn_ref, do_ref, lse_ref, dl_ref, dq_ref):
        i = pl.program_id(1)
        row = lax.broadcasted_iota(jnp.int32, (TQ, TK), 0)
        col = lax.broadcasted_iota(jnp.int32, (TQ, TK), 1)
        kpos = i * TQ - HALO + col
        mask = (jnp.abs(col - HALO - row) <= N_SIDE) & (kpos >= 0) & (kpos < L)
        for h in range(nh):
            sl = slice(h * HEAD_DIM, (h + 1) * HEAD_DIM)
            k = _cat3((kp_ref, kc_ref, kn_ref), sl)
            v = _cat3((vp_ref, vc_ref, vn_ref), sl)
            s = _dot(q_ref[:, sl], k, NT) * scale
            s = jnp.where(mask, s, NEG)
            p = jnp.exp(s - lse_ref[:, h * lph:h * lph + 1])
            dp = _dot(do_ref[:, sl], v, NT)
            ds = p * (dp - dl_ref[:, h * lph:h * lph + 1])
            dq_ref[:, sl] = _dot(ds.astype(k.dtype), k, NN) * scale

    blk = pl.BlockSpec((None, TQ, DA), lambda r, i: (r, i, 0))
    stat = pl.BlockSpec((TQ, STAT_W), lambda r, i: (i, r))
    return pl.pallas_call(
        body, name="attn_bwd_dq_d%d" % d, grid=(d, L // TQ),
        in_specs=[blk] + _halo_specs3(L, DA, 1) + _halo_specs3(L, DA, 2) + [blk, stat, stat],
        out_specs=blk, out_shape=_sds((d, L, DA), F32),
        compiler_params=_params(("parallel", "parallel")),
    )(*([qkv] * 7), dob, lse.reshape(L, d * STAT_W), dl.reshape(L, d * STAT_W))


def attn_bwd_dkv(qkv, dob, lse, dl, d, DA):
    L = qkv.shape[1]
    S = L * d
    nh = DA // HEAD_DIM
    lph = _lanes_per_head(DA)
    scale = HEAD_DIM ** -0.5
    TR = TQ + 2 * HALO

    def body(k_ref, v_ref, qp, qc, qn, dop, doc, don, lp, lc, ln, dp_, dc_, dn_, dk_ref, dv_ref):
        j = pl.program_id(1)
        row = lax.broadcasted_iota(jnp.int32, (TR, TQ), 0)
        col = lax.broadcasted_iota(jnp.int32, (TR, TQ), 1)
        qpos = j * TQ - HALO + row
        mask = (jnp.abs(col - (row - HALO)) <= N_SIDE) & (qpos >= 0) & (qpos < L)
        for h in range(nh):
            sl = slice(h * HEAD_DIM, (h + 1) * HEAD_DIM)
            q = _cat3((qp, qc, qn), sl)
            do = _cat3((dop, doc, don), sl)
            stat = slice(h * lph, h * lph + 1)
            lse_q = _cat3((lp, lc, ln), stat)
            dl_q = _cat3((dp_, dc_, dn_), stat)
            k = k_ref[:, sl]
            v = v_ref[:, sl]
            s = _dot(q, k, NT) * scale
            s = jnp.where(mask, s, NEG)
            p = jnp.exp(s - lse_q)
            dv_ref[:, sl] = _dot(p.astype(do.dtype), do, TN)
            dp = _dot(do, v, NT)
            ds = p * (dp - dl_q)
            dk_ref[:, sl] = _dot(ds.astype(q.dtype), q, TN) * scale

    blk = pl.BlockSpec((None, TQ, DA), lambda r, i: (r, i, 0))
    ident = lambda r: r
    dk, dv = pl.pallas_call(
        body, name="attn_bwd_dkv_d%d" % d, grid=(d, L // TQ),
        in_specs=[pl.BlockSpec((None, TQ, DA), lambda r, i: (r, i, 1)), pl.BlockSpec((None, TQ, DA), lambda r, i: (r, i, 2))]
        + _halo_specs3(L, DA, 0) + _halo_specs3(L, DA, 0) + _halo_specs(L, STAT_W, ident) + _halo_specs(L, STAT_W, ident),
        out_specs=[blk, blk], out_shape=[_sds((d, L, DA), F32), _sds((d, L, DA), F32)],
        compiler_params=_params(("parallel", "parallel")),
    )(*([qkv] * 5), *([dob] * 3), *([lse.reshape(L, d * STAT_W)] * 3), *([dl.reshape(L, d * STAT_W)] * 3))
    return dk, dv


def _gmlp_fwd_math(u_raw, v_raw, ln_g, ws_ref, bcol_ref, n_chunks, ng):
    ug = _gelu(u_raw)
    vg = _gelu(v_raw)
    mu = jnp.mean(vg, axis=-1, keepdims=True)
    xc = vg - mu
    rstd = lax.rsqrt(jnp.mean(xc * xc, axis=-1, keepdims=True) + EPS)
    xhat = xc * rstd
    vn = xhat * ln_g
    rows = []
    for n in range(n_chunks):
        cols = []
        for g in range(ng):
            blk = vn[n * CHUNK:(n + 1) * CHUNK, g * GROUP:(g + 1) * GROUP]
            cols.append(_dot(ws_ref[g].astype(BF16), blk.astype(BF16), NN) + bcol_ref[g])
        rows.append(jnp.concatenate(cols, axis=1))
    mixed = jnp.concatenate(rows, axis=0)
    return ug, xhat, rstd, vn, mixed


def gmlp_fwd(proj, ln_g, w_s, bcol, g_out, DA):
    S = proj.shape[0]
    ng = DA // GROUP
    tr = _tile(S, 2 * CHUNK)

    def body(u_ref, v_ref, ln_ref, ws_ref, bcol_ref, g_ref, G_ref, mix_ref):
        ug, _, _, _, mixed = _gmlp_fwd_math(u_ref[...], v_ref[...], ln_ref[...], ws_ref, bcol_ref, tr // CHUNK, ng)
        G = ug * mixed
        G_ref[...] = G
        r = lax.rsqrt(jnp.mean(G * G, axis=-1, keepdims=True) + EPS)
        mix_ref[...] = (G * r * g_ref[...]).astype(mix_ref.dtype)

    vec = pl.BlockSpec((1, DA), lambda i: (0, 0))
    par = pl.BlockSpec((ng, CHUNK, CHUNK), lambda i: (0, 0, 0))
    blk = pl.BlockSpec((tr, DA), lambda i: (i, 0))
    return pl.pallas_call(
        body, name="gmlp_fwd", grid=(S // tr,),
        in_specs=[pl.BlockSpec((tr, DA), lambda i: (i, 3)), pl.BlockSpec((tr, DA), lambda i: (i, 4)), vec, par, par, vec],
        out_specs=[blk, blk], out_shape=[_sds((S, DA), F32), _sds((S, DA), BF16)],
        compiler_params=_params(("parallel",)),
    )(proj, proj, ln_g, w_s, bcol, g_out)


def gmlp_bwd(proj, ln_g, w_s, w_st, bcol, g_out, dmix, DA):
    S = proj.shape[0]
    ng = DA // GROUP
    tr = _tile(S, 2 * CHUNK)
    nc = tr // CHUNK

    def body(u_ref, v_ref, ln_ref, ws_ref, wst_ref, bcol_ref, g_ref, dy_ref, duv_ref, dg_ref, dln_ref, dws_ref, db_ref):
        u_raw = u_ref[...]
        v_raw = v_ref[...]
        ln_g_v = ln_ref[...]
        ug, xhat, rstd, vn, mixed = _gmlp_fwd_math(u_raw, v_raw, ln_g_v, ws_ref, bcol_ref, nc, ng)
        G = ug * mixed
        dG, dg = _rms_bwd_math(G, g_ref[...], dy_ref[...])
        du_g = dG * mixed
        dmixed = dG * ug
        dws, dbs, rows = [], [], []
        for g in range(ng):
            dws.append(None)
            dbs.append(None)
        for n in range(nc):
            cols = []
            for g in range(ng):
                dm = dmixed[n * CHUNK:(n + 1) * CHUNK, g * GROUP:(g + 1) * GROUP]
                vb = vn[n * CHUNK:(n + 1) * CHUNK, g * GROUP:(g + 1) * GROUP]
                dmb = dm.astype(BF16)
                w = _dot(dmb, vb.astype(BF16), NT)
                b = jnp.broadcast_to(jnp.sum(dm, axis=-1, keepdims=True), (CHUNK, GROUP))
                dws[g] = w if dws[g] is None else dws[g] + w
                dbs[g] = b if dbs[g] is None else dbs[g] + b
                cols.append(_dot(wst_ref[g].astype(BF16), dmb, NN))
            rows.append(jnp.concatenate(cols, axis=1))
        dvn = jnp.concatenate(rows, axis=0)
        dln = jnp.sum(dvn * xhat, axis=0, keepdims=True)
        dxh = dvn * ln_g_v
        dvg = rstd * (dxh - jnp.mean(dxh, axis=-1, keepdims=True) - xhat * jnp.mean(dxh * xhat, axis=-1, keepdims=True))
        duv_ref[:, 0:DA] = (du_g * _gelu_grad(u_raw)).astype(duv_ref.dtype)
        duv_ref[:, DA:2 * DA] = (dvg * _gelu_grad(v_raw)).astype(duv_ref.dtype)

        first = pl.program_id(0) == 0

        @pl.when(first)
        def _():
            dg_ref[...] = dg
            dln_ref[...] = dln
            for g in range(ng):
                dws_ref[g] = dws[g]
                db_ref[g] = dbs[g]

        @pl.when(jnp.logical_not(first))
        def _():
            dg_ref[...] += dg
            dln_ref[...] += dln
            for g in range(ng):
                dws_ref[g] += dws[g]
                db_ref[g] += dbs[g]

    vec = pl.BlockSpec((1, DA), lambda i: (0, 0))
    par = pl.BlockSpec((ng, CHUNK, CHUNK), lambda i: (0, 0, 0))
    return pl.pallas_call(
        body, name="gmlp_bwd", grid=(S // tr,),
        in_specs=[pl.BlockSpec((tr, DA), lambda i: (i, 3)), pl.BlockSpec((tr, DA), lambda i: (i, 4)), vec, par, par, par, vec,
                  pl.BlockSpec((tr, DA), lambda i: (i, 1))],
        out_specs=[pl.BlockSpec((tr, 2 * DA), lambda i: (i, 0)), vec, vec, par, par],
        out_shape=[_sds((S, 2 * DA), BF16), _sds((1, DA), F32), _sds((1, DA), F32),
                   _sds((ng, CHUNK, CHUNK), F32), _sds((ng, CHUNK, CHUNK), F32)],
        compiler_params=_params(("arbitrary",)),
    )(proj, proj, ln_g, w_s, w_st, bcol, g_out, dmix)


def mixer_fwd(x, sm, gw, tabs):
    D = x.shape[1]
    DA = D // 2
    cos2, sin2 = tabs
    h1 = rms_fwd(x, sm["norm1_g"])
    proj = mm_in(h1, gw["w_in"])
    qkv = rope_qkv(proj, cos2, sin2, DA)
    os_, lses = [], []
    for t, d in enumerate(DILATIONS):
        o, l = attn_fwd(qkv[t], d, DA)
        os_.append(o)
        lses.append(l)
    a, lse, mix_a = attn_merge(os_, lses, sm["mix_norm_attn_g"])
    _, mix_g = gmlp_fwd(proj, sm["gmlp_ln_g"], sm["w_spatial"], sm["bcol"], sm["mix_norm_gmlp_g"], DA)
    x2 = mm_out(x, mix_a, mix_g, gw["w_out"])
    return x2, dict(x=x, h1=h1, proj=proj, qkv=qkv, a=a, lse=lse, mix_a=mix_a, mix_g=mix_g)


def ffn_fwd(x2, sm, gw):
    h2 = rms_fwd(x2, sm["norm2_g"])
    dfdu, dfdg, ff = ff_fwd(h2, gw["w_gate"], gw["w_up"])
    x3 = mm_down(x2, ff, gw["w_down"])
    return x3, dict(x2=x2, h2=h2, dfdg=dfdg, dfdu=dfdu, ff=ff)


def ffn_bwd(dx, dxb, sv, sm, gw, dep, on_grads):
    dgate, dup = ff_bwd_act(dxb, gw["w_down"], sv["dfdg"], sv["dfdu"], dep)
    g_down = dw_down(sv["ff"], dxb)
    g_gate, g_up = dw_gate_up(sv["h2"], dgate, dup)
    token = on_grads(dict(w_gate=g_gate, w_up=g_up, w_down=g_down))
    dh2 = dh_ff(dgate, dup, gw["w_gate"], gw["w_up"], token)
    dx2, dx2b, d_norm2 = rms_bwd_res(sv["x2"], sm["norm2_g"], dh2, dx)
    return dx2, dx2b, dict(norm2_g=d_norm2)


def mixer_bwd(dx2, dx2b, sv, sm, gw, tabs, dep, on_grads):
    D = dx2.shape[1]
    DA = D // 2
    cos2, sin2 = tabs
    dmix = dmix_mm(dx2b, gw["w_out"].reshape(D, D), dep)
    g_out_a, g_out_g = dw_out(sv["mix_a"], sv["mix_g"], dx2b)
    dob, dl, d_mix_a = attn_out_bwd(sv["a"], sm["mix_norm_attn_g"], dmix)
    duv, d_mix_g, d_ln, d_ws, d_bs = gmlp_bwd(sv["proj"], sm["gmlp_ln_g"], sm["w_spatial"], sm["w_spatial_t"], sm["bcol"],
                                              sm["mix_norm_gmlp_g"], dmix, DA)
    dqs, dks, dvs = [], [], []
    for t, d in enumerate(DILATIONS):
        dqs.append(attn_bwd_dq(sv["qkv"][t], dob[t], sv["lse"], dl, d, DA))
        dk, dv = attn_bwd_dkv(sv["qkv"][t], dob[t], sv["lse"], dl, d, DA)
        dks.append(dk)
        dvs.append(dv)
    dproj = assemble_dproj(dqs, dks, dvs, duv, cos2, sin2)
    g_in = dw_in(sv["h1"], dproj)
    token = on_grads(dict(w_in=g_in, w_out=jnp.concatenate([g_out_a, g_out_g], axis=0)))
    dh1 = dh_in(dproj, gw["w_in"], token)
    dx0, dx0b, d_norm1 = rms_bwd_res(sv["x"], sm["norm1_g"], dh1, dx2)
    small = dict(norm1_g=d_norm1, gmlp_ln_g=d_ln, w_spatial=d_ws, b_spatial=d_bs[:, :, 0], mix_norm_attn_g=d_mix_a,
                 mix_norm_gmlp_g=d_mix_g)
    return dx0, dx0b, small


def _place():
    x, y, c = lax.axis_index("x"), lax.axis_index("y"), lax.axis_index("c")
    return x, y, c, [(1 - x, y), (x, 1 - y), (1 - x, 1 - y)]


def _remote(src, dst, send_sems, recv_sems, k, to):
    return pltpu.make_async_remote_copy(src_ref=src, dst_ref=dst, send_sem=send_sems.at[k], recv_sem=recv_sems.at[k],
                                        device_id=to, device_id_type=MESH)


def to_slot(src, layer, me_arr, dep, dtype, transpose=False):
    R, C = src.shape[-2:]
    tr = _lane_tile(R, 512) if transpose else _tile(R, max(16, 4 * ADAMW_BLOCK_ELEMS // C))

    def body(me_ref, src_ref, dep_ref, o_ref):
        v = src_ref[...]
        o_ref[...] = (v.T if transpose else v).astype(o_ref.dtype)

    if layer is None:
        src_spec = pl.BlockSpec((tr, C), lambda i, me_ref: (i, 0))
    else:
        src_spec = pl.BlockSpec((None, tr, C), lambda i, me_ref: (layer, i, 0))
    if transpose:
        out_spec = pl.BlockSpec((None, C, tr), lambda i, me_ref: (me_ref[0], 0, i))
    else:
        out_spec = pl.BlockSpec((None, tr, C), lambda i, me_ref: (me_ref[0], i, 0))
    return pl.pallas_call(
        body, name="to_slot",
        grid_spec=pltpu.PrefetchScalarGridSpec(
            num_scalar_prefetch=1, grid=(R // tr,),
            in_specs=[src_spec, pl.BlockSpec(TOKEN, lambda i, me_ref: (0, 0))], out_specs=out_spec),
        out_shape=_sds((N_DEV, C, R) if transpose else (N_DEV, R, C), dtype), compiler_params=_params(("parallel",)),
    )(me_arr, src, dep)


def gather_ici(bufs):
    n = len(bufs)

    def body(*refs):
        outs = refs[n:2 * n]
        send_sems, recv_sems = refs[2 * n:]
        x, y, c, chips = _place()
        mine = [o.at[4 * x + 2 * y + c] for o in outs]
        sent = []
        for a in range(n):
            for j, (px, py) in enumerate(chips):
                cp = _remote(mine[a], mine[a], send_sems, recv_sems, 3 * a + j, (px, py, c))
                cp.start()
                sent.append(cp)
        for a in range(n):
            for j, (px, py) in enumerate(chips):
                slot = outs[a].at[4 * px + 2 * py + c]
                _remote(slot, slot, send_sems, recv_sems, 3 * a + j, (px, py, c)).wait_recv()
        for cp in sent:
            cp.wait_send()

    return pl.pallas_call(
        body, name="gather_ici", in_specs=[ANY] * n, out_specs=[ANY] * n,
        out_shape=[_sds(b.shape, b.dtype) for b in bufs], input_output_aliases={a: a for a in range(n)},
        scratch_shapes=[pltpu.SemaphoreType.DMA((3 * n,)), pltpu.SemaphoreType.DMA((3 * n,))],
    )(*bufs)


HBM_SPEC = pl.BlockSpec(memory_space=pltpu.HBM)
SEM_SPEC = pl.BlockSpec(memory_space=pltpu.SEMAPHORE)
DATAFLOW = pltpu.SideEffectType.DATAFLOW_SIDE_EFFECTING
TOKEN = (8, 128)


def _in_hbm(a):
    return pltpu.with_memory_space_constraint(a, pltpu.HBM)


PLAN_COPIES = dict(gather_ici=3, gather_d2d=N_CHIP, scatter_d2d=N_CHIP, scatter_ici=3)


def _plan(kind):
    x, y, c, chips = _place()
    sibling = (x, y, 1 - c)
    if kind == "gather_ici":
        me = 4 * x + 2 * y + c
        return [(me, me, 4 * px + 2 * py + c, (px, py, c)) for px, py in chips]
    if kind == "gather_d2d":
        return [(2 * j + c, 2 * j + c, 2 * j + 1 - c, sibling) for j in range(N_CHIP)]
    if kind == "scatter_d2d":
        return [(2 * j + 1 - c, j, j, sibling) for j in range(N_CHIP)]
    assert kind == "scatter_ici"
    return [(2 * px + py, 2 * x + y, 2 * px + py, (px, py, c)) for px, py in chips]


def split_start(kind, srcs, dsts):
    n = len(srcs)
    bufs = list(srcs) + ([] if dsts is None else list(dsts))
    nb = len(bufs)
    k = PLAN_COPIES[kind]

    def body(*refs):
        ins = refs[:n]
        outs = ins if dsts is None else refs[n:2 * n]
        send_sems, recv_sems = refs[nb], refs[nb + 1]
        token = refs[-1]
        for a in range(n):
            for j, (src, dst, _, peer) in enumerate(_plan(kind)):
                _remote(ins[a].at[src], outs[a].at[dst], send_sems, recv_sems, k * a + j, peer).start()
        token[...] = jnp.zeros_like(token)

    out = pl.pallas_call(
        body, name=kind + "_start",
        out_shape=(pltpu.SemaphoreType.DMA((k * n,)), pltpu.SemaphoreType.DMA((k * n,)),
                   *[pltpu.HBM(b.shape, b.dtype) for b in bufs], _sds(TOKEN, F32)),
        in_specs=[HBM_SPEC] * nb,
        out_specs=(SEM_SPEC, SEM_SPEC, *[HBM_SPEC] * nb, pl.BlockSpec(memory_space=pltpu.VMEM)),
        input_output_aliases={i: 2 + i for i in range(nb)},
        compiler_params=pltpu.CompilerParams(has_side_effects=DATAFLOW),
    )(*[_in_hbm(b) for b in bufs])
    return kind, out[0], out[1], list(out[2:2 + n]), None if dsts is None else list(out[2 + n:2 + nb]), out[-1]


def split_wait(pending, after):
    kind, send_sems, recv_sems, srcs, dsts, _ = pending
    n = len(srcs)
    bufs = list(srcs) + ([] if dsts is None else list(dsts))
    nb = len(bufs)
    k = PLAN_COPIES[kind]

    def body(*refs):
        ins = refs[:n]
        outs = ins if dsts is None else refs[n:2 * n]
        send_sems, recv_sems = refs[nb], refs[nb + 1]
        for a in range(n):
            for j, (src, _, landed, peer) in enumerate(_plan(kind)):
                cp = _remote(ins[a].at[src], outs[a].at[landed], send_sems, recv_sems, k * a + j, peer)
                cp.wait_send()
                cp.wait_recv()

    out = pl.pallas_call(
        body, name=kind + "_wait",
        out_shape=[pltpu.HBM(b.shape, b.dtype) for b in bufs],
        in_specs=[HBM_SPEC] * nb + [SEM_SPEC, SEM_SPEC, ANY], out_specs=[HBM_SPEC] * nb,
        input_output_aliases={i: i for i in range(nb)},
        compiler_params=pltpu.CompilerParams(has_side_effects=DATAFLOW),
    )(*bufs, send_sems, recv_sems, after)
    return list(out) if dsts is None else (list(out[:n]), list(out[n:]))


def gather_d2d(bufs):
    n = len(bufs)

    def body(*refs):
        outs = refs[n:2 * n]
        token = refs[2 * n]
        send_sems, recv_sems = refs[2 * n + 1:]
        x, y, c, _ = _place()
        sent = []
        for a in range(n):
            for j in range(N_CHIP):
                slot = outs[a].at[2 * j + c]
                cp = _remote(slot, slot, send_sems, recv_sems, N_CHIP * a + j, (x, y, 1 - c))
                cp.start()
                sent.append(cp)
        for a in range(n):
            for j in range(N_CHIP):
                slot = outs[a].at[2 * j + 1 - c]
                _remote(slot, slot, send_sems, recv_sems, N_CHIP * a + j, (x, y, 1 - c)).wait_recv()
        for cp in sent:
            cp.wait_send()
        token[...] = jnp.zeros_like(token)

    out = pl.pallas_call(
        body, name="gather_d2d", in_specs=[ANY] * n,
        out_specs=[ANY] * n + [pl.BlockSpec(memory_space=pltpu.VMEM)],
        out_shape=[_sds(b.shape, b.dtype) for b in bufs] + [_sds(TOKEN, F32)],
        input_output_aliases={a: a for a in range(n)},
        scratch_shapes=[pltpu.SemaphoreType.DMA((N_CHIP * n,)), pltpu.SemaphoreType.DMA((N_CHIP * n,))],
    )(*bufs)
    return list(out[:n]), out[n]


def pair_sum(part, got, c_arr, chip_arr):
    _, R, C = part.shape
    tr = _tile(R, 512)
    p4 = part.reshape(N_CHIP, 2, R, C)

    def body(c_ref, chip_ref, p_ref, g_ref, o_ref, own_ref):
        s = (p_ref[...].astype(F32) + g_ref[...].astype(F32)).astype(o_ref.dtype)
        o_ref[...] = s

        @pl.when(pl.program_id(1) == chip_ref[0])
        def _():
            own_ref[...] = s

    return pl.pallas_call(
        body, name="pair_sum",
        grid_spec=pltpu.PrefetchScalarGridSpec(
            num_scalar_prefetch=2, grid=(R // tr, N_CHIP),
            in_specs=[pl.BlockSpec((None, None, tr, C), lambda i, j, c_ref, chip_ref: (j, c_ref[0], i, 0)),
                      pl.BlockSpec((None, tr, C), lambda i, j, c_ref, chip_ref: (j, i, 0))],
            out_specs=[pl.BlockSpec((None, tr, C), lambda i, j, c_ref, chip_ref: (j, i, 0)),
                       pl.BlockSpec((None, tr, C), lambda i, j, c_ref, chip_ref: (chip_ref[0], i, 0))]),
        out_shape=[_sds((N_CHIP, R, C), part.dtype), _sds((N_CHIP, R, C), part.dtype)],
        compiler_params=_params(("parallel", "arbitrary")),
    )(c_arr, chip_arr, p4, got)


def adamw(w, m, v, parts_per_layer, transposed=False):
    NL, R, C = w.shape
    P = parts_per_layer[0].shape[0]
    want = max(16, ADAMW_BLOCK_ELEMS // C)
    tr = _lane_tile(R, max(want, 128)) if transposed else _tile(R, want)
    nb = R // tr

    def body(w_ref, m_ref, v_ref, *rest):
        part_refs = rest[:NL]
        g_ref, d_ref, nm_ref, nv_ref = rest[NL:]
        layer = pl.program_id(0)
        g = None
        for q in range(NL):
            s = part_refs[q][0].astype(F32)
            for t in range(1, P):
                s = s + part_refs[q][t].astype(F32)
            if transposed:
                s = s.T
            s = jnp.where(layer == q, s, 0.0)
            g = s if g is None else g + s
        wv = w_ref[...]
        nm = ADAM_B1 * m_ref[...] + (1.0 - ADAM_B1) * g
        nv = ADAM_B2 * v_ref[...] + (1.0 - ADAM_B2) * (g * g)
        m_hat = nm / (1.0 - ADAM_B1 ** ADAM_STEP)
        v_hat = nv / (1.0 - ADAM_B2 ** ADAM_STEP)
        g_ref[...] = g
        d_ref[...] = -ADAM_LR * (m_hat / (jnp.sqrt(v_hat) + ADAM_EPS) + ADAM_WD * wv)
        nm_ref[...] = nm
        nv_ref[...] = nv

    def part_spec(q):
        def block(l, i):
            return jnp.where(l == q, i, jnp.where(l < q, 0, nb - 1))

        if transposed:
            return pl.BlockSpec((P, C, tr), lambda l, i: (0, 0, block(l, i)))
        return pl.BlockSpec((P, tr, C), lambda l, i: (0, block(l, i), 0))

    blk = pl.BlockSpec((None, tr, C), lambda l, i: (l, i, 0))
    return pl.pallas_call(
        body, name="adamw", grid=(NL, nb),
        in_specs=[blk, blk, blk] + [part_spec(q) for q in range(NL)],
        out_specs=[blk, blk, blk, blk], out_shape=[_sds((NL, R, C), F32)] * 4,
        compiler_params=_params(("arbitrary", "arbitrary")),
    )(w, m, v, *parts_per_layer)


SMALL = ("norm1_g", "gmlp_ln_g", "w_spatial", "b_spatial", "mix_norm_attn_g", "mix_norm_gmlp_g", "norm2_g")
BIG = ("w_in", "w_out", "w_gate", "w_up", "w_down")
TRANSPOSED = ("w_gate", "w_up")
GATHERED_TRANSPOSED = ("w_in",)
GROUPS = (("w_in", "w_out"), ("w_gate", "w_up", "w_down"))
LANES = 128


def _pack(layers, final):
    flat = [layer[n].reshape(-1) for layer in layers for n in SMALL] + [final.reshape(-1)]
    return jnp.concatenate(flat).reshape(-1, LANES)


def _unpack(packed, like_layers, like_final):
    flat = packed.reshape(-1)
    out, off = [], 0
    for layer in like_layers:
        d = {}
        for n in SMALL:
            size = layer[n].size
            d[n] = flat[off:off + size].reshape(layer[n].shape)
            off += size
        out.append(d)
    return out, flat[off:off + like_final.size].reshape(like_final.shape)


def kernel(x, norm1_g, w_in, gmlp_ln_g, w_spatial, b_spatial, mix_norm_attn_g, mix_norm_gmlp_g, w_out, norm2_g, w_gate, w_up, w_down, final_g, loss_target, m_norm1_g, m_w_in, m_gmlp_ln_g, m_w_spatial, m_b_spatial, m_mix_norm_attn_g, m_mix_norm_gmlp_g, m_w_out, m_norm2_g, m_w_gate, m_w_up, m_w_down, m_final_g, v_norm1_g, v_w_in, v_gmlp_ln_g, v_w_spatial, v_b_spatial, v_mix_norm_attn_g, v_mix_norm_gmlp_g, v_w_out, v_norm2_g, v_w_gate, v_w_up, v_w_down, v_final_g):
    S, D = x.shape[1], x.shape[2]
    NL = norm1_g.shape[0]
    DA = D // 2
    xs = x.reshape(S, D)
    tabs = rope_tables(S)
    ax, ay, ac = lax.axis_index("x"), lax.axis_index("y"), lax.axis_index("c")
    me_arr = (4 * ax + 2 * ay + ac).astype(jnp.int32).reshape(1)
    c_arr = ac.astype(jnp.int32).reshape(1)
    chip_arr = (2 * ax + ay).astype(jnp.int32).reshape(1)
    small_w = dict(norm1_g=norm1_g, gmlp_ln_g=gmlp_ln_g, w_spatial=w_spatial, b_spatial=b_spatial,
                   mix_norm_attn_g=mix_norm_attn_g, mix_norm_gmlp_g=mix_norm_gmlp_g, norm2_g=norm2_g)
    small_m = dict(norm1_g=m_norm1_g, gmlp_ln_g=m_gmlp_ln_g, w_spatial=m_w_spatial, b_spatial=m_b_spatial,
                   mix_norm_attn_g=m_mix_norm_attn_g, mix_norm_gmlp_g=m_mix_norm_gmlp_g, norm2_g=m_norm2_g)
    small_v = dict(norm1_g=v_norm1_g, gmlp_ln_g=v_gmlp_ln_g, w_spatial=v_w_spatial, b_spatial=v_b_spatial,
                   mix_norm_attn_g=v_mix_norm_attn_g, mix_norm_gmlp_g=v_mix_norm_gmlp_g, norm2_g=v_norm2_g)
    def view(n, a):
        return jnp.swapaxes(a, 1, 2) if n in TRANSPOSED else a

    big_w = {n: view(n, a) for n, a in dict(w_in=w_in, w_out=w_out, w_gate=w_gate, w_up=w_up, w_down=w_down).items()}
    big_m = {n: view(n, a) for n, a in dict(w_in=m_w_in, w_out=m_w_out, w_gate=m_w_gate, w_up=m_w_up, w_down=m_w_down).items()}
    big_v = {n: view(n, a) for n, a in dict(w_in=v_w_in, w_out=v_w_out, w_gate=v_w_gate, w_up=v_w_up, w_down=v_w_down).items()}

    def layer_small(l):
        ws = w_spatial[l]
        return dict(norm1_g=norm1_g[l][None], gmlp_ln_g=gmlp_ln_g[l][None], mix_norm_attn_g=mix_norm_attn_g[l][None],
                    mix_norm_gmlp_g=mix_norm_gmlp_g[l][None], norm2_g=norm2_g[l][None], w_spatial=ws,
                    w_spatial_t=jnp.swapaxes(ws, 1, 2), bcol=jnp.broadcast_to(b_spatial[l][:, :, None], ws.shape))

    n_stages = 2 * NL
    zero = jnp.zeros(TOKEN, F32)
    TOK = 5

    def stage_bufs(s, dep):
        return [to_slot(big_w[n], s // 2, me_arr, dep, BF16, transpose=(n in GATHERED_TRANSPOSED)) for n in GROUPS[s % 2]]

    def stage_weights(s, bufs):
        gw = dict(zip(GROUPS[s % 2], bufs))
        if s % 2 == 0:
            gw["w_in"] = gw["w_in"].reshape(-1, D)
            gw["w_out"] = gw["w_out"].reshape(2, DA, D)
        else:
            gw = {n: b.reshape(-1, D) for n, b in gw.items()}
        return gw

    ici = {0: split_start("gather_ici", stage_bufs(0, zero), None)}
    ici[1] = split_start("gather_ici", stage_bufs(1, ici[0][TOK]), None)
    h = xs
    d2d = {0: split_start("gather_d2d", split_wait(ici[0], h), None)}
    ready = {0: split_wait(d2d[0], h)}
    saved, weights = [], []
    for s in range(n_stages):
        deps = [ici[1][TOK]] if s == 0 else []
        if 1 <= s < n_stages - 1:
            d2d[s + 1] = split_start("gather_d2d", split_wait(ici[s + 1], h), None)
            deps.append(d2d[s + 1][TOK])
        if s + 2 < n_stages:
            ici[s + 2] = split_start("gather_ici", stage_bufs(s + 2, deps[-1]), None)
            deps.append(ici[s + 2][TOK])
        dep = sum(deps[1:], deps[0]) if deps else zero
        gw = stage_weights(s, ready[s])
        sm = layer_small(s // 2)
        if s % 2 == 0:
            sm["norm1_g"] = sm["norm1_g"] + dep[0, 0]
            h, sv = mixer_fwd(h, sm, gw, tabs)
        else:
            sm["norm2_g"] = sm["norm2_g"] + dep[0, 0]
            h, sv = ffn_fwd(h, sm, gw)
        saved.append(sv)
        weights.append(gw)
        if s == 0:
            d2d[1] = split_start("gather_d2d", split_wait(ici[1], h), None)
        if s + 1 < n_stages:
            ready[s + 1] = split_wait(d2d[s + 1], h)
    loss_part, dx, dxb, d_final = loss_and_grad(h, final_g[None], loss_target.reshape(S, D))
    loss = lax.psum(loss_part[0, 0], ("x", "y", "c"))

    big_sums = [dict() for _ in range(NL)]
    small_grads = [dict() for _ in range(NL)]
    pending, pending_stage = None, None
    dep = zero
    for s in reversed(range(n_stages)):
        l = s // 2
        crossing = []

        def on_grads(big, s=s, crossing=crossing):
            parts = [big[n].reshape((N_DEV, -1, big[n].shape[-1])) for n in GROUPS[s % 2]]
            gots = [lax.empty((N_CHIP,) + p.shape[1:], p.dtype) for p in parts]
            crossing.append(split_start("scatter_d2d", parts, gots))
            return crossing[0][TOK]

        if s % 2 == 1:
            dx, dxb, small = ffn_bwd(dx, dxb, saved[s], layer_small(l), weights[s], dep, on_grads)
        else:
            dx, dxb, small = mixer_bwd(dx, dxb, saved[s], layer_small(l), weights[s], tabs, dep, on_grads)
        small_grads[l].update(small)
        if pending is not None:
            big_sums[pending_stage // 2].update(zip(GROUPS[pending_stage % 2], split_wait(pending, dx)[1]))
        parts, got = split_wait(crossing[0], dx)
        pairs = [pair_sum(p, g, c_arr, chip_arr) for p, g in zip(parts, got)]
        pending, pending_stage = split_start("scatter_ici", [t for t, _ in pairs], [o for _, o in pairs]), s
        dep = pending[TOK]

    def update_big(n):
        res = adamw(big_w[n], big_m[n], big_v[n], [big_sums[l][n] for l in range(NL)], n in GATHERED_TRANSPOSED)
        return [view(n, t) for t in res], res[0][0, :TOKEN[0], :TOKEN[1]]

    big_out = {}
    after_ffn = zero
    for n in GROUPS[1]:
        big_out[n], piece = update_big(n)
        after_ffn = after_ffn + piece

    packed = to_slot(_pack(small_grads, d_final), None, me_arr, dep + after_ffn, F32)
    gathered_small = gather_d2d(gather_ici([packed]))[0][0]
    pw = _pack([{n: small_w[n][l] for n in SMALL} for l in range(NL)], final_g)[None]
    pm = _pack([{n: small_m[n][l] for n in SMALL} for l in range(NL)], m_final_g)[None]
    pv = _pack([{n: small_v[n][l] for n in SMALL} for l in range(NL)], v_final_g)[None]
    like_layers = [{n: small_w[n][l] for n in SMALL} for l in range(NL)]
    small_res = adamw(pw, pm, pv, [gathered_small])
    small_out = [_unpack(t[0], like_layers, final_g) for t in small_res]
    big_sums[pending_stage // 2].update(zip(GROUPS[pending_stage % 2], split_wait(pending, small_res[0])[1]))

    def small_stack(k, n):
        return jnp.stack([small_out[k][0][l][n] for l in range(NL)])

    for n in GROUPS[0]:
        big_out[n], _ = update_big(n)

    order = ("norm1_g", "w_in", "gmlp_ln_g", "w_spatial", "b_spatial", "mix_norm_attn_g", "mix_norm_gmlp_g", "w_out",
             "norm2_g", "w_gate", "w_up", "w_down")
    outs = [loss, dx.reshape(x.shape)]
    for k in range(4):
        for n in order:
            outs.append(big_out[n][k] if n in BIG else small_stack(k, n))
        outs.append(small_out[k][1])
    return tuple(outs)
```

```python
import functools

import jax
import jax.numpy as jnp
from jax import lax
from jax.experimental import pallas as pl
from jax.experimental.pallas import tpu as pltpu

F32 = jnp.float32
BF16 = jnp.bfloat16

HEAD_DIM = 128
CHUNK = 128
GROUP = 128
N_SIDE = 64
DILATIONS = (1, 4, 16)
TQ = 128
HALO = 64
EPS = 1e-6
NEG = -1e30
ROPE_THETA = 10000.0
ADAM_LR = 0.001
ADAM_B1 = 0.9
ADAM_B2 = 0.999
ADAM_EPS = 1e-08
ADAM_WD = 0.01
ADAM_STEP = 10
N_DEV = 8
N_CHIP = 4
VMEM_LIMIT_V7X = 56 * 1024 * 1024
ADAMW_BLOCK_ELEMS = 128 * 1024
MESH = pl.DeviceIdType.MESH
ANY = pl.BlockSpec(memory_space=pl.ANY)


def _params(sem=None):
    return pltpu.CompilerParams(dimension_semantics=sem, vmem_limit_bytes=VMEM_LIMIT_V7X)


def _tile(n, want):
    for t in range(min(want, n), 15, -1):
        if n % t == 0 and t % 16 == 0:
            return t
    return n


def _sds(shape, dtype):
    return jax.ShapeDtypeStruct(shape, dtype)


def _dot(a, b, dims):
    return lax.dot_general(a, b, (dims, ((), ())), preferred_element_type=F32)


NN = ((1,), (0,))
NT = ((1,), (1,))
TN = ((0,), (0,))


def _sigmoid(x):
    return 1.0 / (1.0 + jnp.exp(-x))


def _gelu(x):
    return 0.5 * x * (1.0 + lax.erf(x * 0.7071067811865476))


def _gelu_grad(x):
    cdf = 0.5 * (1.0 + lax.erf(x * 0.7071067811865476))
    pdf = 0.3989422804014327 * jnp.exp(-0.5 * x * x)
    return cdf + x * pdf


def rms_fwd(x, g):
    S, D = x.shape
    tr = _tile(S, 512)

    def body(x_ref, g_ref, o_ref):
        xv = x_ref[...]
        r = lax.rsqrt(jnp.mean(xv * xv, axis=-1, keepdims=True) + EPS)
        o_ref[...] = (xv * r * g_ref[...]).astype(o_ref.dtype)

    return pl.pallas_call(
        body, name="rms_fwd", grid=(S // tr,),
        in_specs=[pl.BlockSpec((tr, D), lambda i: (i, 0)), pl.BlockSpec((1, D), lambda i: (0, 0))],
        out_specs=pl.BlockSpec((tr, D), lambda i: (i, 0)),
        out_shape=_sds((S, D), BF16), compiler_params=_params(("parallel",)),
    )(x, g)


def _rms_bwd_math(xv, gv, dy):
    r = lax.rsqrt(jnp.mean(xv * xv, axis=-1, keepdims=True) + EPS)
    dyg = dy * gv
    dx = r * dyg - xv * (r * r * r) * jnp.mean(dyg * xv, axis=-1, keepdims=True)
    dg = jnp.sum(dy * xv * r, axis=0, keepdims=True)
    return dx, dg


def rms_bwd_res(x, g, dy, dres):
    S, D = x.shape
    tr = _tile(S, 512)

    def body(x_ref, g_ref, dy_ref, dres_ref, dx_ref, dxb_ref, dg_ref):
        dx, dg = _rms_bwd_math(x_ref[...], g_ref[...], dy_ref[...])
        tot = dres_ref[...] + dx
        dx_ref[...] = tot
        dxb_ref[...] = tot.astype(dxb_ref.dtype)

        @pl.when(pl.program_id(0) == 0)
        def _():
            dg_ref[...] = dg

        @pl.when(pl.program_id(0) > 0)
        def _():
            dg_ref[...] += dg

    row = pl.BlockSpec((tr, D), lambda i: (i, 0))
    vec = pl.BlockSpec((1, D), lambda i: (0, 0))
    return pl.pallas_call(
        body, name="rms_bwd_res", grid=(S // tr,),
        in_specs=[row, vec, row, row], out_specs=[row, row, vec],
        out_shape=[_sds((S, D), F32), _sds((S, D), BF16), _sds((1, D), F32)],
        compiler_params=_params(("arbitrary",)),
    )(x, g, dy, dres)


def loss_and_grad(x, g, target):
    S, D = x.shape
    tr = _tile(S, 512)

    def body(x_ref, g_ref, t_ref, loss_ref, dx_ref, dxb_ref, dg_ref):
        xv = x_ref[...]
        gv = g_ref[...]
        r = lax.rsqrt(jnp.mean(xv * xv, axis=-1, keepdims=True) + EPS)
        y = xv * r * gv
        diff = y - t_ref[...]
        part = 0.5 * jnp.sum(jnp.mean(diff * diff, axis=-1, keepdims=True), axis=0, keepdims=True)
        dy = diff * (1.0 / D)
        dx, dg = _rms_bwd_math(xv, gv, dy)
        dx_ref[...] = dx
        dxb_ref[...] = dx.astype(dxb_ref.dtype)

        @pl.when(pl.program_id(0) == 0)
        def _():
            dg_ref[...] = dg
            loss_ref[...] = jnp.broadcast_to(part, loss_ref.shape)

        @pl.when(pl.program_id(0) > 0)
        def _():
            dg_ref[...] += dg
            loss_ref[...] += jnp.broadcast_to(part, loss_ref.shape)

    row = pl.BlockSpec((tr, D), lambda i: (i, 0))
    vec = pl.BlockSpec((1, D), lambda i: (0, 0))
    return pl.pallas_call(
        body, name="loss_and_grad", grid=(S // tr,),
        in_specs=[row, vec, row],
        out_specs=[pl.BlockSpec((8, 128), lambda i: (0, 0)), row, row, vec],
        out_shape=[_sds((8, 128), F32), _sds((S, D), F32), _sds((S, D), BF16), _sds((1, D), F32)],
        compiler_params=_params(("arbitrary",)),
    )(x, g, target)


def _mm_body(n_pairs, dims, red_axis, n_red, has_res):
    def body(*refs):
        ins = refs[:2 * n_pairs]
        res_ref = refs[2 * n_pairs] if has_res else None
        o_ref = refs[2 * n_pairs + has_res]
        p = None
        for t in range(n_pairs):
            d = _dot(ins[2 * t][...], ins[2 * t + 1][...], dims)
            p = d if p is None else p + d
        if red_axis is None:
            if has_res:
                p = res_ref[...] + p
            o_ref[...] = p.astype(o_ref.dtype)
            return
        acc_ref = o_ref if o_ref.dtype == F32 else refs[2 * n_pairs + has_res + 1]
        r = pl.program_id(red_axis)

        @pl.when(r == 0)
        def _():
            acc_ref[...] = res_ref[...] + p if has_res else p

        @pl.when(r > 0)
        def _():
            acc_ref[...] += p

        if acc_ref is not o_ref:
            @pl.when(r == n_red - 1)
            def _():
                o_ref[...] = acc_ref[...].astype(o_ref.dtype)

    return body


def mm_in(h, w):
    S, D = h.shape
    N = w.shape[0]
    tm = _tile(S, 1024)
    tn = _tile(N, 1024)
    return pl.pallas_call(
        _mm_body(1, NT, None, 1, False), name="mm_in", grid=(S // tm, N // tn),
        in_specs=[pl.BlockSpec((tm, D), lambda i, j: (i, 0)), pl.BlockSpec((tn, D), lambda i, j: (j, 0))],
        out_specs=pl.BlockSpec((tm, tn), lambda i, j: (i, j)),
        out_shape=_sds((S, N), F32), compiler_params=_params(("parallel", "parallel")),
    )(h, w)


def mm_out(x, mix_a, mix_g, w):
    S, D = x.shape
    DA = mix_a.shape[1]
    tm = _tile(S, 1024)
    tn = _tile(D, 1024)
    act = pl.BlockSpec((tm, DA), lambda i, j: (i, 0))
    return pl.pallas_call(
        _mm_body(2, NN, None, 1, True), name="mm_out", grid=(S // tm, D // tn),
        in_specs=[act, pl.BlockSpec((None, DA, tn), lambda i, j: (0, 0, j)),
                  act, pl.BlockSpec((None, DA, tn), lambda i, j: (1, 0, j)),
                  pl.BlockSpec((tm, tn), lambda i, j: (i, j))],
        out_specs=pl.BlockSpec((tm, tn), lambda i, j: (i, j)),
        out_shape=_sds((S, D), F32), compiler_params=_params(("parallel", "parallel")),
    )(mix_a, w, mix_g, w, x)


def _lane_tile(n, want):
    for t in range(min(want, n) // 128 * 128, 127, -128):
        if n % t == 0:
            return t
    return n


def ff_fwd(h, wg, wu):
    S, D = h.shape
    F = wg.shape[0]
    tm = _tile(S, 1024)
    tn = _lane_tile(F, 512)

    def body(h_ref, wg_ref, wu_ref, dfdu_ref, dfdg_ref, ff_ref):
        hv = h_ref[...]
        g = _dot(hv, wg_ref[...], NT)
        u = _dot(hv, wu_ref[...], NT)
        sg = _sigmoid(g)
        act = g * sg
        dfdu_ref[...] = act.astype(dfdu_ref.dtype)
        dfdg_ref[...] = (u * (sg * (1.0 + g * (1.0 - sg)))).astype(dfdg_ref.dtype)
        ff_ref[...] = (act * u).astype(ff_ref.dtype)

    wspec = pl.BlockSpec((tn, D), lambda i, j: (j, 0))
    ospec = pl.BlockSpec((tm, tn), lambda i, j: (i, j))
    return pl.pallas_call(
        body, name="ff_fwd", grid=(S // tm, F // tn),
        in_specs=[pl.BlockSpec((tm, D), lambda i, j: (i, 0)), wspec, wspec],
        out_specs=[ospec, ospec, ospec],
        out_shape=[_sds((S, F), BF16), _sds((S, F), BF16), _sds((S, F), BF16)],
        compiler_params=_params(("parallel", "parallel")),
    )(h, wg, wu)


def mm_down(x, ff, wd):
    S, D = x.shape
    F = ff.shape[1]
    tm = _tile(S, 1024)
    tn = _lane_tile(D, 512)
    blk = pl.BlockSpec((tm, tn), lambda i, j: (i, j))
    return pl.pallas_call(
        _mm_body(1, NN, None, 1, True), name="mm_down", grid=(S // tm, D // tn),
        in_specs=[pl.BlockSpec((tm, F), lambda i, j: (i, 0)), pl.BlockSpec((F, tn), lambda i, j: (0, j)), blk],
        out_specs=blk, out_shape=_sds((S, D), F32), compiler_params=_params(("parallel", "parallel")),
    )(ff, wd, x)


def ff_bwd_act(dxb, wd, dfdg, dfdu, dep):
    S, D = dxb.shape
    F = wd.shape[0]
    tm = _tile(S, 1024)
    tn = _lane_tile(F, 512)

    def body(dx_ref, wd_ref, dfdg_ref, dfdu_ref, dep_ref, dgate_ref, dup_ref):
        dff = _dot(dx_ref[...], wd_ref[...], NT)
        dup_ref[...] = (dff * dfdu_ref[...].astype(F32)).astype(dup_ref.dtype)
        dgate_ref[...] = (dff * dfdg_ref[...].astype(F32)).astype(dgate_ref.dtype)

    aspec = pl.BlockSpec((tm, tn), lambda i, j: (i, j))
    return pl.pallas_call(
        body, name="ff_bwd_act", grid=(S // tm, F // tn),
        in_specs=[pl.BlockSpec((tm, D), lambda i, j: (i, 0)), pl.BlockSpec((tn, D), lambda i, j: (j, 0)), aspec, aspec,
                  pl.BlockSpec(TOKEN, lambda i, j: (0, 0))],
        out_specs=[aspec, aspec],
        out_shape=[_sds((S, F), BF16), _sds((S, F), BF16)],
        compiler_params=_params(("parallel", "parallel")),
    )(dxb, wd, dfdg, dfdu, dep)


def dw_down(ff, dxb):
    S, F = ff.shape
    D = dxb.shape[1]
    tf = _lane_tile(F, 256)
    return pl.pallas_call(
        _mm_body(1, TN, None, 1, False), name="dw_down", grid=(F // tf,),
        in_specs=[pl.BlockSpec((S, tf), lambda k: (0, k)), pl.BlockSpec((S, D), lambda k: (0, 0))],
        out_specs=pl.BlockSpec((tf, D), lambda k: (k, 0)),
        out_shape=_sds((F, D), BF16), compiler_params=_params(("parallel",)),
    )(ff, dxb)


def dw_gate_up(h, dgate, dup):
    S, D = h.shape
    F = dgate.shape[1]
    f = _lane_tile(F, 256)

    def body(h_ref, dg_ref, du_ref, og_ref, ou_ref):
        hv = h_ref[...]
        og_ref[...] = _dot(dg_ref[...], hv, TN).astype(og_ref.dtype)
        ou_ref[...] = _dot(du_ref[...], hv, TN).astype(ou_ref.dtype)

    aspec = pl.BlockSpec((S, f), lambda k: (0, k))
    ospec = pl.BlockSpec((f, D), lambda k: (k, 0))
    return pl.pallas_call(
        body, name="dw_gate_up", grid=(F // f,),
        in_specs=[pl.BlockSpec((S, D), lambda k: (0, 0)), aspec, aspec],
        out_specs=[ospec, ospec],
        out_shape=[_sds((F, D), BF16), _sds((F, D), BF16)],
        compiler_params=_params(("parallel",)),
    )(h, dgate, dup)


def dh_ff(dgate, dup, wg, wu, dep):
    S, F = dgate.shape
    D = wg.shape[1]
    tm = _tile(S, 512)
    tn = _lane_tile(D, 256)

    def body(dg_ref, wg_ref, du_ref, wu_ref, dep_ref, o_ref):
        o_ref[...] = _dot(dg_ref[...], wg_ref[...], NN) + _dot(du_ref[...], wu_ref[...], NN)

    aspec = pl.BlockSpec((tm, F), lambda i, j: (i, 0))
    wspec = pl.BlockSpec((F, tn), lambda i, j: (0, j))
    return pl.pallas_call(
        body, name="dh_ff", grid=(S // tm, D // tn),
        in_specs=[aspec, wspec, aspec, wspec, pl.BlockSpec(TOKEN, lambda i, j: (0, 0))],
        out_specs=pl.BlockSpec((tm, tn), lambda i, j: (i, j)),
        out_shape=_sds((S, D), F32), compiler_params=_params(("parallel", "parallel")),
    )(dgate, wg, dup, wu, dep)


def dmix_mm(dxb, w, dep):
    S, D = dxb.shape
    tm = _tile(S, 1024)
    tn = _tile(D, 1024)

    def body(dx_ref, w_ref, dep_ref, o_ref):
        o_ref[...] = _dot(dx_ref[...], w_ref[...], NT)

    return pl.pallas_call(
        body, name="dmix_mm", grid=(S // tm, D // tn),
        in_specs=[pl.BlockSpec((tm, D), lambda i, j: (i, 0)), pl.BlockSpec((tn, D), lambda i, j: (j, 0)),
                  pl.BlockSpec(TOKEN, lambda i, j: (0, 0))],
        out_specs=pl.BlockSpec((tm, tn), lambda i, j: (i, j)),
        out_shape=_sds((S, D), F32), compiler_params=_params(("parallel", "parallel")),
    )(dxb, w, dep)


def dw_out(mix_a, mix_g, dxb):
    S, DA = mix_a.shape
    D = dxb.shape[1]
    tn = _lane_tile(D, 512)

    def half(name, m):
        return pl.pallas_call(
            _mm_body(1, TN, None, 1, False), name=name, grid=(D // tn,),
            in_specs=[pl.BlockSpec((S, DA), lambda j: (0, 0)), pl.BlockSpec((S, tn), lambda j: (0, j))],
            out_specs=pl.BlockSpec((DA, tn), lambda j: (0, j)),
            out_shape=_sds((DA, D), BF16), compiler_params=_params(("parallel",)),
        )(m, dxb)

    return half("dw_out_a", mix_a), half("dw_out_g", mix_g)


def dw_in(h, dproj):
    S, D = h.shape
    N = dproj.shape[1]
    tn = _lane_tile(N, 256)
    return pl.pallas_call(
        _mm_body(1, TN, None, 1, False), name="dw_in", grid=(N // tn,),
        in_specs=[pl.BlockSpec((S, tn), lambda k: (0, k)), pl.BlockSpec((S, D), lambda k: (0, 0))],
        out_specs=pl.BlockSpec((tn, D), lambda k: (k, 0)),
        out_shape=_sds((N, D), BF16), compiler_params=_params(("parallel",)),
    )(dproj, h)


def dh_in(dproj, w, dep):
    S, N = dproj.shape
    D = w.shape[1]
    tm = _tile(S, 1024)
    tn = _lane_tile(D, 512)

    def body(dp_ref, w_ref, dep_ref, o_ref):
        o_ref[...] = _dot(dp_ref[...], w_ref[...], NN)

    return pl.pallas_call(
        body, name="dh_in", grid=(S // tm, D // tn),
        in_specs=[pl.BlockSpec((tm, N), lambda i, j: (i, 0)), pl.BlockSpec((N, tn), lambda i, j: (0, j)),
                  pl.BlockSpec(TOKEN, lambda i, j: (0, 0))],
        out_specs=pl.BlockSpec((tm, tn), lambda i, j: (i, j)),
        out_shape=_sds((S, D), F32), compiler_params=_params(("parallel", "parallel")),
    )(dproj, w, dep)


def rope_tables(S):
    pos = jnp.arange(S, dtype=F32)
    inv = ROPE_THETA ** (-jnp.arange(0, HEAD_DIM, 2, dtype=F32) / HEAD_DIM)
    ang = pos[:, None] * inv[None, :]
    cos, sin = jnp.cos(ang), jnp.sin(ang)
    return jnp.concatenate([cos, cos], axis=-1), jnp.concatenate([-sin, sin], axis=-1)


def _rot_half(t):
    return pltpu.roll(t, HEAD_DIM // 2, 1)


PERM_ROWS = 256


def _by_residue(y, d, out_ref):
    tr = y.shape[0]
    n = tr // d
    o = lax.broadcasted_iota(jnp.int32, (tr, tr), 0)
    i = lax.broadcasted_iota(jnp.int32, (tr, tr), 1)
    src = jnp.bitwise_and(o, n - 1) * d + lax.shift_right_logical(o, n.bit_length() - 1)
    z = _dot((src == i).astype(y.dtype), y, NN).astype(out_ref.dtype)
    for r in range(d):
        out_ref[r] = z[r * n:(r + 1) * n]


def _from_residue(ref, d, terms):
    z = jnp.concatenate([ref[r] for r in range(d)], axis=0)
    tr = z.shape[0]
    n = tr // d
    i = lax.broadcasted_iota(jnp.int32, (tr, tr), 0)
    o = lax.broadcasted_iota(jnp.int32, (tr, tr), 1)
    src = jnp.bitwise_and(i, d - 1) * n + lax.shift_right_logical(i, d.bit_length() - 1)
    pick = (src == o).astype(BF16)
    out = None
    rest = z
    for _ in range(terms):
        piece = rest.astype(BF16)
        got = _dot(pick, piece, NN)
        out = got if out is None else out + got
        rest = rest - piece.astype(F32)
    return out


def _residue_blocks(widths):
    return [pl.BlockSpec((d, PERM_ROWS // d, W), lambda i: (0, i, 0)) for d, W in zip(DILATIONS[1:], widths)]


def _residue_outputs(S, W, dtype):
    shapes = [_sds((d, S // d, W), dtype) for d in DILATIONS[1:]]
    specs = [pl.BlockSpec((d, PERM_ROWS // d, W), lambda i: (0, i, 0)) for d in DILATIONS[1:]]
    return shapes, specs


def rope_qkv(proj, cos2, sin2, DA):
    S = proj.shape[0]
    tr = PERM_ROWS
    nh = DA // HEAD_DIM

    def body(q_ref, k_ref, v_ref, cos_ref, sin_ref, o_ref, *by_res):
        c = cos_ref[...]
        s = sin_ref[...]
        for h in range(nh):
            sl = slice(h * HEAD_DIM, (h + 1) * HEAD_DIM)
            for j, ref in enumerate((q_ref, k_ref)):
                t = ref[:, sl]
                o_ref[:, j * DA + h * HEAD_DIM:j * DA + (h + 1) * HEAD_DIM] = (t * c + _rot_half(t) * s).astype(o_ref.dtype)
        o_ref[:, 2 * DA:3 * DA] = v_ref[...].astype(o_ref.dtype)
        y = o_ref[...]
        for d, ref in zip(DILATIONS[1:], by_res):
            _by_residue(y, d, ref)

    tab = pl.BlockSpec((tr, HEAD_DIM), lambda i: (i, 0))
    shapes, specs = _residue_outputs(S, 3 * DA, BF16)
    out = pl.pallas_call(
        body, name="rope_qkv", grid=(S // tr,),
        in_specs=[pl.BlockSpec((tr, DA), lambda i: (i, 0)), pl.BlockSpec((tr, DA), lambda i: (i, 1)),
                  pl.BlockSpec((tr, DA), lambda i: (i, 2)), tab, tab],
        out_specs=[pl.BlockSpec((tr, 3 * DA), lambda i: (i, 0))] + specs,
        out_shape=[_sds((S, 3 * DA), BF16)] + shapes, compiler_params=_params(("parallel",)),
    )(proj, proj, proj, cos2, sin2)
    return [out[0].reshape(1, S, 3 * DA)] + list(out[1:])


def assemble_dproj(dqs, dks, dvs, duv, cos2, sin2):
    S, DA = dqs[0].shape[1:]
    tr = PERM_ROWS
    nh = DA // HEAD_DIM

    def total(trio):
        t = trio[0][...]
        for d, ref in zip(DILATIONS[1:], trio[1:]):
            t = t + _from_residue(ref, d, 2)
        return t

    def body(*refs):
        dq_refs, dk_refs, dv_refs = refs[0:3], refs[3:6], refs[6:9]
        duv_ref, cos_ref, sin_ref, o_ref = refs[9:13]
        c = cos_ref[...]
        s = sin_ref[...]
        for j, trio in enumerate((dq_refs, dk_refs)):
            t_all = total(trio)
            for h in range(nh):
                t = t_all[:, h * HEAD_DIM:(h + 1) * HEAD_DIM]
                o_ref[:, j * DA + h * HEAD_DIM:j * DA + (h + 1) * HEAD_DIM] = (t * c - _rot_half(t) * s).astype(o_ref.dtype)
        o_ref[:, 2 * DA:3 * DA] = total(dv_refs).astype(o_ref.dtype)
        o_ref[:, 3 * DA:5 * DA] = duv_ref[...]

    trio_specs = [pl.BlockSpec((None, tr, DA), lambda i: (0, i, 0))] + _residue_blocks([DA, DA])
    tab = pl.BlockSpec((tr, HEAD_DIM), lambda i: (i, 0))
    return pl.pallas_call(
        body, name="assemble_dproj", grid=(S // tr,),
        in_specs=trio_specs * 3 + [pl.BlockSpec((tr, 2 * DA), lambda i: (i, 0)), tab, tab],
        out_specs=pl.BlockSpec((tr, 5 * DA), lambda i: (i, 0)),
        out_shape=_sds((S, 5 * DA), BF16), compiler_params=_params(("parallel",)),
    )(*dqs, *dks, *dvs, duv, cos2, sin2)


STAT_W = 128


def _lanes_per_head(DA):
    return STAT_W // (DA // HEAD_DIM)


def _halo_specs(L, width_blocks, col):
    per = TQ // HALO
    last = L // HALO - 1
    W = width_blocks
    return [
        pl.BlockSpec((HALO, W), lambda r, i: (jnp.maximum(i * per - 1, 0), col(r))),
        pl.BlockSpec((TQ, W), lambda r, i: (i, col(r))),
        pl.BlockSpec((HALO, W), lambda r, i: (jnp.minimum((i + 1) * per, last), col(r))),
    ]


def _halo_specs3(L, W, c):
    per = TQ // HALO
    last = L // HALO - 1
    return [
        pl.BlockSpec((None, HALO, W), lambda r, i: (r, jnp.maximum(i * per - 1, 0), c)),
        pl.BlockSpec((None, TQ, W), lambda r, i: (r, i, c)),
        pl.BlockSpec((None, HALO, W), lambda r, i: (r, jnp.minimum((i + 1) * per, last), c)),
    ]


def _cat3(refs, sl):
    return jnp.concatenate([refs[0][:, sl], refs[1][:, sl], refs[2][:, sl]], axis=0)


def attn_fwd(qkv, d, DA):
    L = qkv.shape[1]
    S = L * d
    nh = DA // HEAD_DIM
    lph = _lanes_per_head(DA)
    scale = HEAD_DIM ** -0.5
    TK = TQ + 2 * HALO

    def body(q_ref, kp_ref, kc_ref, kn_ref, vp_ref, vc_ref, vn_ref, o_ref, lse_ref):
        i = pl.program_id(1)
        row = lax.broadcasted_iota(jnp.int32, (TQ, TK), 0)
        col = lax.broadcasted_iota(jnp.int32, (TQ, TK), 1)
        kpos = i * TQ - HALO + col
        mask = (jnp.abs(col - HALO - row) <= N_SIDE) & (kpos >= 0) & (kpos < L)
        for h in range(nh):
            sl = slice(h * HEAD_DIM, (h + 1) * HEAD_DIM)
            k = _cat3((kp_ref, kc_ref, kn_ref), sl)
            v = _cat3((vp_ref, vc_ref, vn_ref), sl)
            s = _dot(q_ref[:, sl], k, NT) * scale
            s = jnp.where(mask, s, NEG)
            m = jnp.max(s, axis=-1, keepdims=True)
            p = jnp.exp(s - m)
            l = jnp.sum(p, axis=-1, keepdims=True)
            o = _dot(p.astype(v.dtype), v, NN) / l
            o_ref[:, sl] = o
            lse_ref[:, h * lph:(h + 1) * lph] = jnp.broadcast_to(m + jnp.log(l), (TQ, lph))

    out_spec = pl.BlockSpec((None, TQ, DA), lambda r, i: (r, i, 0))
    o, lse = pl.pallas_call(
        body, name="attn_fwd_d%d" % d, grid=(d, L // TQ),
        in_specs=[pl.BlockSpec((None, TQ, DA), lambda r, i: (r, i, 0))] + _halo_specs3(L, DA, 1) + _halo_specs3(L, DA, 2),
        out_specs=[out_spec, pl.BlockSpec((TQ, STAT_W), lambda r, i: (i, r))],
        out_shape=[_sds((d, L, DA), F32), _sds((L, d * STAT_W), F32)],
        compiler_params=_params(("parallel", "parallel")),
    )(*([qkv] * 7))
    return o, lse.reshape(S, STAT_W)


def attn_merge(os_, lses, g):
    S, DA = os_[0].shape[1:]
    tr = PERM_ROWS
    nh = DA // HEAD_DIM
    lph = _lanes_per_head(DA)

    def body(o0, o1, o2, l0, l1, l2, g_ref, a_ref, lse_ref, mix_ref):
        a0, a1, a2 = l0[...], l1[...], l2[...]
        m = jnp.maximum(jnp.maximum(a0, a1), a2)
        e0, e1, e2 = jnp.exp(a0 - m), jnp.exp(a1 - m), jnp.exp(a2 - m)
        den = e0 + e1 + e2
        w0, w1, w2 = e0 / den, e1 / den, e2 / den
        lse_ref[...] = m + jnp.log(den)
        v0 = o0[...]
        v1 = _from_residue(o1, DILATIONS[1], 2)
        v2 = _from_residue(o2, DILATIONS[2], 2)
        heads = []
        for h in range(nh):
            sl = slice(h * HEAD_DIM, (h + 1) * HEAD_DIM)
            c = slice(h * lph, h * lph + 1)
            heads.append(w0[:, c] * v0[:, sl] + w1[:, c] * v1[:, sl] + w2[:, c] * v2[:, sl])
        a = jnp.concatenate(heads, axis=1)
        a_ref[...] = a
        r = lax.rsqrt(jnp.mean(a * a, axis=-1, keepdims=True) + EPS)
        mix_ref[...] = (a * r * g_ref[...]).astype(mix_ref.dtype)

    blk = pl.BlockSpec((tr, DA), lambda i: (i, 0))
    stat = pl.BlockSpec((tr, STAT_W), lambda i: (i, 0))
    return pl.pallas_call(
        body, name="attn_merge", grid=(S // tr,),
        in_specs=[pl.BlockSpec((None, tr, DA), lambda i: (0, i, 0))] + _residue_blocks([DA, DA]) + [stat] * 3
        + [pl.BlockSpec((1, DA), lambda i: (0, 0))],
        out_specs=[blk, stat, blk],
        out_shape=[_sds((S, DA), F32), _sds((S, STAT_W), F32), _sds((S, DA), BF16)],
        compiler_params=_params(("parallel",)),
    )(*os_, *lses, g)


def attn_out_bwd(a, g, dmix):
    S, DA = a.shape
    tr = PERM_ROWS
    nh = DA // HEAD_DIM
    lph = _lanes_per_head(DA)

    def body(a_ref, g_ref, dy_ref, do_ref, dl_ref, dg_ref, *by_res):
        av = a_ref[...]
        dx, dg = _rms_bwd_math(av, g_ref[...], dy_ref[...])
        dob = dx.astype(do_ref.dtype)
        do_ref[...] = dob
        for d, ref in zip(DILATIONS[1:], by_res):
            _by_residue(dob, d, ref)
        prod = dx * av
        for h in range(nh):
            sl = slice(h * HEAD_DIM, (h + 1) * HEAD_DIM)
            dl_ref[:, h * lph:(h + 1) * lph] = jnp.broadcast_to(jnp.sum(prod[:, sl], axis=-1, keepdims=True), (tr, lph))

        @pl.when(pl.program_id(0) == 0)
        def _():
            dg_ref[...] = dg

        @pl.when(pl.program_id(0) > 0)
        def _():
            dg_ref[...] += dg

    blk = pl.BlockSpec((tr, DA), lambda i: (i, 0))
    vec = pl.BlockSpec((1, DA), lambda i: (0, 0))
    shapes, specs = _residue_outputs(S, DA, BF16)
    out = pl.pallas_call(
        body, name="attn_out_bwd", grid=(S // tr,),
        in_specs=[blk, vec, blk], out_specs=[blk, pl.BlockSpec((tr, STAT_W), lambda i: (i, 0)), vec] + specs,
        out_shape=[_sds((S, DA), BF16), _sds((S, STAT_W), F32), _sds((1, DA), F32)] + shapes,
        compiler_params=_params(("arbitrary",)),
    )(a, g, dmix)
    return [out[0].reshape(1, S, DA)] + list(out[3:]), out[1], out[2]


def attn_bwd_dq(qkv, dob, lse, dl, d, DA):
    L = qkv.shape[1]
    S = L * d
    nh = DA // HEAD_DIM
    lph = _lanes_per_head(DA)
    scale = HEAD_DIM ** -0.5
    TK = TQ + 2 * HALO

    def body(q_ref, kp_ref, kc_ref, kn_ref, vp_ref, vc_ref, vn_ref, do_ref, lse_ref, dl_ref, dq_ref):
        i = pl.program_id(1)
        row = lax.broadcasted_iota(jnp.int32, (TQ, TK), 0)
        col = lax.broadcasted_iota(jnp.int32, (TQ, TK), 1)
        kpos = i * TQ - HALO + col
        mask = (jnp.abs(col - HALO - row) <= N_SIDE) & (kpos >= 0) & (kpos < L)
        for h in range(nh):
            sl = slice(h * HEAD_DIM, (h + 1) * HEAD_DIM)
            k = _cat3((kp_ref, kc_ref, kn_ref), sl)
            v = _cat3((vp_ref, vc_ref, vn_ref), sl)
            s = _dot(q_ref[:, sl], k, NT) * scale
            s = jnp.where(mask, s, NEG)
            p = jnp.exp(s - lse_ref[:, h * lph:h * lph + 1])
            dp = _dot(do_ref[:, sl], v, NT)
            ds = p * (dp - dl_ref[:, h * lph:h * lph + 1])
            dq_ref[:, sl] = _dot(ds.astype(k.dtype), k, NN) * scale

    blk = pl.BlockSpec((None, TQ, DA), lambda r, i: (r, i, 0))
    stat = pl.BlockSpec((TQ, STAT_W), lambda r, i: (i, r))
    return pl.pallas_call(
        body, name="attn_bwd_dq_d%d" % d, grid=(d, L // TQ),
        in_specs=[blk] + _halo_specs3(L, DA, 1) + _halo_specs3(L, DA, 2) + [blk, stat, stat],
        out_specs=blk, out_shape=_sds((d, L, DA), F32),
        compiler_params=_params(("parallel", "parallel")),
    )(*([qkv] * 7), dob, lse.reshape(L, d * STAT_W), dl.reshape(L, d * STAT_W))


def attn_bwd_dkv(qkv, dob, lse, dl, d, DA):
    L = qkv.shape[1]
    S = L * d
    nh = DA // HEAD_DIM
    lph = _lanes_per_head(DA)
    scale = HEAD_DIM ** -0.5
    TR = TQ + 2 * HALO

    def body(k_ref, v_ref, qp, qc, qn, dop, doc, don, lp, lc, ln, dp_, dc_, dn_, dk_ref, dv_ref):
        j = pl.program_id(1)
        row = lax.broadcasted_iota(jnp.int32, (TR, TQ), 0)
        col = lax.broadcasted_iota(jnp.int32, (TR, TQ), 1)
        qpos = j * TQ - HALO + row
        mask = (jnp.abs(col - (row - HALO)) <= N_SIDE) & (qpos >= 0) & (qpos < L)
        for h in range(nh):
            sl = slice(h * HEAD_DIM, (h + 1) * HEAD_DIM)
            q = _cat3((qp, qc, qn), sl)
            do = _cat3((dop, doc, don), sl)
            stat = slice(h * lph, h * lph + 1)
            lse_q = _cat3((lp, lc, ln), stat)
            dl_q = _cat3((dp_, dc_, dn_), stat)
            k = k_ref[:, sl]
            v = v_ref[:, sl]
            s = _dot(q, k, NT) * scale
            s = jnp.where(mask, s, NEG)
            p = jnp.exp(s - lse_q)
            dv_ref[:, sl] = _dot(p.astype(do.dtype), do, TN)
            dp = _dot(do, v, NT)
            ds = p * (dp - dl_q)
            dk_ref[:, sl] = _dot(ds.astype(q.dtype), q, TN) * scale

    blk = pl.BlockSpec((None, TQ, DA), lambda r, i: (r, i, 0))
    ident = lambda r: r
    dk, dv = pl.pallas_call(
        body, name="attn_bwd_dkv_d%d" % d, grid=(d, L // TQ),
        in_specs=[pl.BlockSpec((None, TQ, DA), lambda r, i: (r, i, 1)), pl.BlockSpec((None, TQ, DA), lambda r, i: (r, i, 2))]
        + _halo_specs3(L, DA, 0) + _halo_specs3(L, DA, 0) + _halo_specs(L, STAT_W, ident) + _halo_specs(L, STAT_W, ident),
        out_specs=[blk, blk], out_shape=[_sds((d, L, DA), F32), _sds((d, L, DA), F32)],
        compiler_params=_params(("parallel", "parallel")),
    )(*([qkv] * 5), *([dob] * 3), *([lse.reshape(L, d * STAT_W)] * 3), *([dl.reshape(L, d * STAT_W)] * 3))
    return dk, dv


def _gmlp_fwd_math(u_raw, v_raw, ln_g, ws_ref, bcol_ref, n_chunks, ng):
    ug = _gelu(u_raw)
    vg = _gelu(v_raw)
    mu = jnp.mean(vg, axis=-1, keepdims=True)
    xc = vg - mu
    rstd = lax.rsqrt(jnp.mean(xc * xc, axis=-1, keepdims=True) + EPS)
    xhat = xc * rstd
    vn = xhat * ln_g
    rows = []
    for n in range(n_chunks):
        cols = []
        for g in range(ng):
            blk = vn[n * CHUNK:(n + 1) * CHUNK, g * GROUP:(g + 1) * GROUP]
            cols.append(_dot(ws_ref[g].astype(BF16), blk.astype(BF16), NN) + bcol_ref[g])
        rows.append(jnp.concatenate(cols, axis=1))
    mixed = jnp.concatenate(rows, axis=0)
    return ug, xhat, rstd, vn, mixed


def gmlp_fwd(proj, ln_g, w_s, bcol, g_out, DA):
    S = proj.shape[0]
    ng = DA // GROUP
    tr = _tile(S, 2 * CHUNK)

    def body(u_ref, v_ref, ln_ref, ws_ref, bcol_ref, g_ref, G_ref, mix_ref):
        ug, _, _, _, mixed = _gmlp_fwd_math(u_ref[...], v_ref[...], ln_ref[...], ws_ref, bcol_ref, tr // CHUNK, ng)
        G = ug * mixed
        G_ref[...] = G
        r = lax.rsqrt(jnp.mean(G * G, axis=-1, keepdims=True) + EPS)
        mix_ref[...] = (G * r * g_ref[...]).astype(mix_ref.dtype)

    vec = pl.BlockSpec((1, DA), lambda i: (0, 0))
    par = pl.BlockSpec((ng, CHUNK, CHUNK), lambda i: (0, 0, 0))
    blk = pl.BlockSpec((tr, DA), lambda i: (i, 0))
    return pl.pallas_call(
        body, name="gmlp_fwd", grid=(S // tr,),
        in_specs=[pl.BlockSpec((tr, DA), lambda i: (i, 3)), pl.BlockSpec((tr, DA), lambda i: (i, 4)), vec, par, par, vec],
        out_specs=[blk, blk], out_shape=[_sds((S, DA), F32), _sds((S, DA), BF16)],
        compiler_params=_params(("parallel",)),
    )(proj, proj, ln_g, w_s, bcol, g_out)


def gmlp_bwd(proj, ln_g, w_s, w_st, bcol, g_out, dmix, DA):
    S = proj.shape[0]
    ng = DA // GROUP
    tr = _tile(S, 2 * CHUNK)
    nc = tr // CHUNK

    def body(u_ref, v_ref, ln_ref, ws_ref, wst_ref, bcol_ref, g_ref, dy_ref, duv_ref, dg_ref, dln_ref, dws_ref, db_ref):
        u_raw = u_ref[...]
        v_raw = v_ref[...]
        ln_g_v = ln_ref[...]
        ug, xhat, rstd, vn, mixed = _gmlp_fwd_math(u_raw, v_raw, ln_g_v, ws_ref, bcol_ref, nc, ng)
        G = ug * mixed
        dG, dg = _rms_bwd_math(G, g_ref[...], dy_ref[...])
        du_g = dG * mixed
        dmixed = dG * ug
        dws, dbs, rows = [], [], []
        for g in range(ng):
            dws.append(None)
            dbs.append(None)
        for n in range(nc):
            cols = []
            for g in range(ng):
                dm = dmixed[n * CHUNK:(n + 1) * CHUNK, g * GROUP:(g + 1) * GROUP]
                vb = vn[n * CHUNK:(n + 1) * CHUNK, g * GROUP:(g + 1) * GROUP]
                dmb = dm.astype(BF16)
                w = _dot(dmb, vb.astype(BF16), NT)
                b = jnp.broadcast_to(jnp.sum(dm, axis=-1, keepdims=True), (CHUNK, GROUP))
                dws[g] = w if dws[g] is None else dws[g] + w
                dbs[g] = b if dbs[g] is None else dbs[g] + b
                cols.append(_dot(wst_ref[g].astype(BF16), dmb, NN))
            rows.append(jnp.concatenate(cols, axis=1))
        dvn = jnp.concatenate(rows, axis=0)
        dln = jnp.sum(dvn * xhat, axis=0, keepdims=True)
        dxh = dvn * ln_g_v
        dvg = rstd * (dxh - jnp.mean(dxh, axis=-1, keepdims=True) - xhat * jnp.mean(dxh * xhat, axis=-1, keepdims=True))
        duv_ref[:, 0:DA] = (du_g * _gelu_grad(u_raw)).astype(duv_ref.dtype)
        duv_ref[:, DA:2 * DA] = (dvg * _gelu_grad(v_raw)).astype(duv_ref.dtype)

        first = pl.program_id(0) == 0

        @pl.when(first)
        def _():
            dg_ref[...] = dg
            dln_ref[...] = dln
            for g in range(ng):
                dws_ref[g] = dws[g]
                db_ref[g] = dbs[g]

        @pl.when(jnp.logical_not(first))
        def _():
            dg_ref[...] += dg
            dln_ref[...] += dln
            for g in range(ng):
                dws_ref[g] += dws[g]
                db_ref[g] += dbs[g]

    vec = pl.BlockSpec((1, DA), lambda i: (0, 0))
    par = pl.BlockSpec((ng, CHUNK, CHUNK), lambda i: (0, 0, 0))
    return pl.pallas_call(
        body, name="gmlp_bwd", grid=(S // tr,),
        in_specs=[pl.BlockSpec((tr, DA), lambda i: (i, 3)), pl.BlockSpec((tr, DA), lambda i: (i, 4)), vec, par, par, par, vec,
                  pl.BlockSpec((tr, DA), lambda i: (i, 1))],
        out_specs=[pl.BlockSpec((tr, 2 * DA), lambda i: (i, 0)), vec, vec, par, par],
        out_shape=[_sds((S, 2 * DA), BF16), _sds((1, DA), F32), _sds((1, DA), F32),
                   _sds((ng, CHUNK, CHUNK), F32), _sds((ng, CHUNK, CHUNK), F32)],
        compiler_params=_params(("arbitrary",)),
    )(proj, proj, ln_g, w_s, w_st, bcol, g_out, dmix)


def mixer_fwd(x, sm, gw, tabs):
    D = x.shape[1]
    DA = D // 2
    cos2, sin2 = tabs
    h1 = rms_fwd(x, sm["norm1_g"])
    proj = mm_in(h1, gw["w_in"])
    qkv = rope_qkv(proj, cos2, sin2, DA)
    os_, lses = [], []
    for t, d in enumerate(DILATIONS):
        o, l = attn_fwd(qkv[t], d, DA)
        os_.append(o)
        lses.append(l)
    a, lse, mix_a = attn_merge(os_, lses, sm["mix_norm_attn_g"])
    _, mix_g = gmlp_fwd(proj, sm["gmlp_ln_g"], sm["w_spatial"], sm["bcol"], sm["mix_norm_gmlp_g"], DA)
    x2 = mm_out(x, mix_a, mix_g, gw["w_out"])
    return x2, dict(x=x, h1=h1, proj=proj, qkv=qkv, a=a, lse=lse, mix_a=mix_a, mix_g=mix_g)


def ffn_fwd(x2, sm, gw):
    h2 = rms_fwd(x2, sm["norm2_g"])
    dfdu, dfdg, ff = ff_fwd(h2, gw["w_gate"], gw["w_up"])
    x3 = mm_down(x2, ff, gw["w_down"])
    return x3, dict(x2=x2, h2=h2, dfdg=dfdg, dfdu=dfdu, ff=ff)


def ffn_bwd(dx, dxb, sv, sm, gw, dep, on_grads):
    dgate, dup = ff_bwd_act(dxb, gw["w_down"], sv["dfdg"], sv["dfdu"], dep)
    g_down = dw_down(sv["ff"], dxb)
    g_gate, g_up = dw_gate_up(sv["h2"], dgate, dup)
    token = on_grads(dict(w_gate=g_gate, w_up=g_up, w_down=g_down))
    dh2 = dh_ff(dgate, dup, gw["w_gate"], gw["w_up"], token)
    dx2, dx2b, d_norm2 = rms_bwd_res(sv["x2"], sm["norm2_g"], dh2, dx)
    return dx2, dx2b, dict(norm2_g=d_norm2)


def mixer_bwd(dx2, dx2b, sv, sm, gw, tabs, dep, on_grads):
    D = dx2.shape[1]
    DA = D // 2
    cos2, sin2 = tabs
    dmix = dmix_mm(dx2b, gw["w_out"].reshape(D, D), dep)
    g_out_a, g_out_g = dw_out(sv["mix_a"], sv["mix_g"], dx2b)
    dob, dl, d_mix_a = attn_out_bwd(sv["a"], sm["mix_norm_attn_g"], dmix)
    duv, d_mix_g, d_ln, d_ws, d_bs = gmlp_bwd(sv["proj"], sm["gmlp_ln_g"], sm["w_spatial"], sm["w_spatial_t"], sm["bcol"],
                                              sm["mix_norm_gmlp_g"], dmix, DA)
    dqs, dks, dvs = [], [], []
    for t, d in enumerate(DILATIONS):
        dqs.append(attn_bwd_dq(sv["qkv"][t], dob[t], sv["lse"], dl, d, DA))
        dk, dv = attn_bwd_dkv(sv["qkv"][t], dob[t], sv["lse"], dl, d, DA)
        dks.append(dk)
        dvs.append(dv)
    dproj = assemble_dproj(dqs, dks, dvs, duv, cos2, sin2)
    g_in = dw_in(sv["h1"], dproj)
    token = on_grads(dict(w_in=g_in, w_out=jnp.concatenate([g_out_a, g_out_g], axis=0)))
    dh1 = dh_in(dproj, gw["w_in"], token)
    dx0, dx0b, d_norm1 = rms_bwd_res(sv["x"], sm["norm1_g"], dh1, dx2)
    small = dict(norm1_g=d_norm1, gmlp_ln_g=d_ln, w_spatial=d_ws, b_spatial=d_bs[:, :, 0], mix_norm_attn_g=d_mix_a,
                 mix_norm_gmlp_g=d_mix_g)
    return dx0, dx0b, small


def _place():
    x, y, c = lax.axis_index("x"), lax.axis_index("y"), lax.axis_index("c")
    return x, y, c, [(1 - x, y), (x, 1 - y), (1 - x, 1 - y)]


def _remote(src, dst, send_sems, recv_sems, k, to):
    return pltpu.make_async_remote_copy(src_ref=src, dst_ref=dst, send_sem=send_sems.at[k], recv_sem=recv_sems.at[k],
                                        device_id=to, device_id_type=MESH)


def to_slot(src, layer, me_arr, dep, dtype, transpose=False):
    R, C = src.shape[-2:]
    tr = _lane_tile(R, 512) if transpose else _tile(R, max(16, 4 * ADAMW_BLOCK_ELEMS // C))

    def body(me_ref, src_ref, dep_ref, o_ref):
        v = src_ref[...]
        o_ref[...] = (v.T if transpose else v).astype(o_ref.dtype)

    if layer is None:
        src_spec = pl.BlockSpec((tr, C), lambda i, me_ref: (i, 0))
    else:
        src_spec = pl.BlockSpec((None, tr, C), lambda i, me_ref: (layer, i, 0))
    if transpose:
        out_spec = pl.BlockSpec((None, C, tr), lambda i, me_ref: (me_ref[0], 0, i))
    else:
        out_spec = pl.BlockSpec((None, tr, C), lambda i, me_ref: (me_ref[0], i, 0))
    return pl.pallas_call(
        body, name="to_slot",
        grid_spec=pltpu.PrefetchScalarGridSpec(
            num_scalar_prefetch=1, grid=(R // tr,),
            in_specs=[src_spec, pl.BlockSpec(TOKEN, lambda i, me_ref: (0, 0))], out_specs=out_spec),
        out_shape=_sds((N_DEV, C, R) if transpose else (N_DEV, R, C), dtype), compiler_params=_params(("parallel",)),
    )(me_arr, src, dep)


HBM_SPEC = pl.BlockSpec(memory_space=pltpu.HBM)
SEM_SPEC = pl.BlockSpec(memory_space=pltpu.SEMAPHORE)
DATAFLOW = pltpu.SideEffectType.DATAFLOW_SIDE_EFFECTING
TOKEN = (8, 128)


def _in_hbm(a):
    return pltpu.with_memory_space_constraint(a, pltpu.HBM)


PLAN_COPIES = dict(gather_ici=3, gather_d2d=N_CHIP, scatter_d2d=N_CHIP, scatter_ici=3)


def _plan(kind):
    x, y, c, chips = _place()
    sibling = (x, y, 1 - c)
    if kind == "gather_ici":
        me = 4 * x + 2 * y + c
        return [(me, me, 4 * px + 2 * py + c, (px, py, c)) for px, py in chips]
    if kind == "gather_d2d":
        return [(2 * j + c, 2 * j + c, 2 * j + 1 - c, sibling) for j in range(N_CHIP)]
    if kind == "scatter_d2d":
        return [(2 * j + 1 - c, j, j, sibling) for j in range(N_CHIP)]
    assert kind == "scatter_ici"
    return [(2 * px + py, 2 * x + y, 2 * px + py, (px, py, c)) for px, py in chips]


def split_start(kind, srcs, dsts):
    n = len(srcs)
    bufs = list(srcs) + ([] if dsts is None else list(dsts))
    nb = len(bufs)
    k = PLAN_COPIES[kind]

    def body(*refs):
        ins = refs[:n]
        outs = ins if dsts is None else refs[n:2 * n]
        send_sems, recv_sems = refs[nb], refs[nb + 1]
        token = refs[-1]
        for a in range(n):
            for j, (src, dst, _, peer) in enumerate(_plan(kind)):
                _remote(ins[a].at[src], outs[a].at[dst], send_sems, recv_sems, k * a + j, peer).start()
        token[...] = jnp.zeros_like(token)

    out = pl.pallas_call(
        body, name=kind + "_start",
        out_shape=(pltpu.SemaphoreType.DMA((k * n,)), pltpu.SemaphoreType.DMA((k * n,)),
                   *[pltpu.HBM(b.shape, b.dtype) for b in bufs], _sds(TOKEN, F32)),
        in_specs=[HBM_SPEC] * nb,
        out_specs=(SEM_SPEC, SEM_SPEC, *[HBM_SPEC] * nb, pl.BlockSpec(memory_space=pltpu.VMEM)),
        input_output_aliases={i: 2 + i for i in range(nb)},
        compiler_params=pltpu.CompilerParams(has_side_effects=DATAFLOW),
    )(*[_in_hbm(b) for b in bufs])
    return kind, out[0], out[1], list(out[2:2 + n]), None if dsts is None else list(out[2 + n:2 + nb]), out[-1]


def split_wait(pending, after):
    kind, send_sems, recv_sems, srcs, dsts, _ = pending
    n = len(srcs)
    bufs = list(srcs) + ([] if dsts is None else list(dsts))
    nb = len(bufs)
    k = PLAN_COPIES[kind]

    def body(*refs):
        ins = refs[:n]
        outs = ins if dsts is None else refs[n:2 * n]
        send_sems, recv_sems = refs[nb], refs[nb + 1]
        for a in range(n):
            for j, (src, _, landed, peer) in enumerate(_plan(kind)):
                cp = _remote(ins[a].at[src], outs[a].at[landed], send_sems, recv_sems, k * a + j, peer)
                cp.wait_send()
                cp.wait_recv()

    out = pl.pallas_call(
        body, name=kind + "_wait",
        out_shape=[pltpu.HBM(b.shape, b.dtype) for b in bufs],
        in_specs=[HBM_SPEC] * nb + [SEM_SPEC, SEM_SPEC, ANY], out_specs=[HBM_SPEC] * nb,
        input_output_aliases={i: i for i in range(nb)},
        compiler_params=pltpu.CompilerParams(has_side_effects=DATAFLOW),
    )(*bufs, send_sems, recv_sems, after)
    return list(out) if dsts is None else (list(out[:n]), list(out[n:]))


def gather_d2d(bufs):
    n = len(bufs)

    def body(*refs):
        outs = refs[n:2 * n]
        token = refs[2 * n]
        send_sems, recv_sems = refs[2 * n + 1:]
        x, y, c, _ = _place()
        sent = []
        for a in range(n):
            for j in range(N_CHIP):
                slot = outs[a].at[2 * j + c]
                cp = _remote(slot, slot, send_sems, recv_sems, N_CHIP * a + j, (x, y, 1 - c))
                cp.start()
                sent.append(cp)
        for a in range(n):
            for j in range(N_CHIP):
                slot = outs[a].at[2 * j + 1 - c]
                _remote(slot, slot, send_sems, recv_sems, N_CHIP * a + j, (x, y, 1 - c)).wait_recv()
        for cp in sent:
            cp.wait_send()
        token[...] = jnp.zeros_like(token)

    out = pl.pallas_call(
        body, name="gather_d2d", in_specs=[ANY] * n,
        out_specs=[ANY] * n + [pl.BlockSpec(memory_space=pltpu.VMEM)],
        out_shape=[_sds(b.shape, b.dtype) for b in bufs] + [_sds(TOKEN, F32)],
        input_output_aliases={a: a for a in range(n)},
        scratch_shapes=[pltpu.SemaphoreType.DMA((N_CHIP * n,)), pltpu.SemaphoreType.DMA((N_CHIP * n,))],
    )(*bufs)
    return list(out[:n]), out[n]


def pair_sum(part, got, c_arr, chip_arr):
    _, R, C = part.shape
    tr = _tile(R, 512)
    p4 = part.reshape(N_CHIP, 2, R, C)

    def body(c_ref, chip_ref, p_ref, g_ref, o_ref, own_ref):
        s = (p_ref[...].astype(F32) + g_ref[...].astype(F32)).astype(o_ref.dtype)
        o_ref[...] = s

        @pl.when(pl.program_id(1) == chip_ref[0])
        def _():
            own_ref[...] = s

    return pl.pallas_call(
        body, name="pair_sum",
        grid_spec=pltpu.PrefetchScalarGridSpec(
            num_scalar_prefetch=2, grid=(R // tr, N_CHIP),
            in_specs=[pl.BlockSpec((None, None, tr, C), lambda i, j, c_ref, chip_ref: (j, c_ref[0], i, 0)),
                      pl.BlockSpec((None, tr, C), lambda i, j, c_ref, chip_ref: (j, i, 0))],
            out_specs=[pl.BlockSpec((None, tr, C), lambda i, j, c_ref, chip_ref: (j, i, 0)),
                       pl.BlockSpec((None, tr, C), lambda i, j, c_ref, chip_ref: (chip_ref[0], i, 0))]),
        out_shape=[_sds((N_CHIP, R, C), part.dtype), _sds((N_CHIP, R, C), part.dtype)],
        compiler_params=_params(("parallel", "arbitrary")),
    )(c_arr, chip_arr, p4, got)


def adamw(w, m, v, parts_per_layer, transposed=False):
    NL, R, C = w.shape
    P = parts_per_layer[0].shape[0]
    want = max(16, ADAMW_BLOCK_ELEMS // C)
    tr = _lane_tile(R, max(want, 128)) if transposed else _tile(R, want)
    nb = R // tr

    def body(w_ref, m_ref, v_ref, *rest):
        part_refs = rest[:NL]
        g_ref, d_ref, nm_ref, nv_ref = rest[NL:]
        layer = pl.program_id(0)
        g = None
        for q in range(NL):
            s = part_refs[q][0].astype(F32)
            for t in range(1, P):
                s = s + part_refs[q][t].astype(F32)
            if transposed:
                s = s.T
            s = jnp.where(layer == q, s, 0.0)
            g = s if g is None else g + s
        wv = w_ref[...]
        nm = ADAM_B1 * m_ref[...] + (1.0 - ADAM_B1) * g
        nv = ADAM_B2 * v_ref[...] + (1.0 - ADAM_B2) * (g * g)
        m_hat = nm / (1.0 - ADAM_B1 ** ADAM_STEP)
        v_hat = nv / (1.0 - ADAM_B2 ** ADAM_STEP)
        g_ref[...] = g
        d_ref[...] = -ADAM_LR * (m_hat / (jnp.sqrt(v_hat) + ADAM_EPS) + ADAM_WD * wv)
        nm_ref[...] = nm
        nv_ref[...] = nv

    def part_spec(q):
        def block(l, i):
            return jnp.where(l == q, i, jnp.where(l < q, 0, nb - 1))

        if transposed:
            return pl.BlockSpec((P, C, tr), lambda l, i: (0, 0, block(l, i)))
        return pl.BlockSpec((P, tr, C), lambda l, i: (0, block(l, i), 0))

    blk = pl.BlockSpec((None, tr, C), lambda l, i: (l, i, 0))
    return pl.pallas_call(
        body, name="adamw", grid=(NL, nb),
        in_specs=[blk, blk, blk] + [part_spec(q) for q in range(NL)],
        out_specs=[blk, blk, blk, blk], out_shape=[_sds((NL, R, C), F32)] * 4,
        compiler_params=_params(("arbitrary", "arbitrary")),
    )(w, m, v, *parts_per_layer)


SMALL = ("norm1_g", "gmlp_ln_g", "w_spatial", "b_spatial", "mix_norm_attn_g", "mix_norm_gmlp_g", "norm2_g")
BIG = ("w_in", "w_out", "w_gate", "w_up", "w_down")
TRANSPOSED = ("w_gate", "w_up")
GATHERED_TRANSPOSED = ("w_in",)
GROUPS = (("w_in", "w_out"), ("w_gate", "w_up", "w_down"))
LANES = 128


def _pack(layers, final):
    flat = [layer[n].reshape(-1) for layer in layers for n in SMALL] + [final.reshape(-1)]
    return jnp.concatenate(flat).reshape(-1, LANES)


def _unpack(packed, like_layers, like_final):
    flat = packed.reshape(-1)
    out, off = [], 0
    for layer in like_layers:
        d = {}
        for n in SMALL:
            size = layer[n].size
            d[n] = flat[off:off + size].reshape(layer[n].shape)
            off += size
        out.append(d)
    return out, flat[off:off + like_final.size].reshape(like_final.shape)


def kernel(x, norm1_g, w_in, gmlp_ln_g, w_spatial, b_spatial, mix_norm_attn_g, mix_norm_gmlp_g, w_out, norm2_g, w_gate, w_up, w_down, final_g, loss_target, m_norm1_g, m_w_in, m_gmlp_ln_g, m_w_spatial, m_b_spatial, m_mix_norm_attn_g, m_mix_norm_gmlp_g, m_w_out, m_norm2_g, m_w_gate, m_w_up, m_w_down, m_final_g, v_norm1_g, v_w_in, v_gmlp_ln_g, v_w_spatial, v_b_spatial, v_mix_norm_attn_g, v_mix_norm_gmlp_g, v_w_out, v_norm2_g, v_w_gate, v_w_up, v_w_down, v_final_g):
    S, D = x.shape[1], x.shape[2]
    NL = norm1_g.shape[0]
    DA = D // 2
    xs = x.reshape(S, D)
    tabs = rope_tables(S)
    ax, ay, ac = lax.axis_index("x"), lax.axis_index("y"), lax.axis_index("c")
    me_arr = (4 * ax + 2 * ay + ac).astype(jnp.int32).reshape(1)
    c_arr = ac.astype(jnp.int32).reshape(1)
    chip_arr = (2 * ax + ay).astype(jnp.int32).reshape(1)
    small_w = dict(norm1_g=norm1_g, gmlp_ln_g=gmlp_ln_g, w_spatial=w_spatial, b_spatial=b_spatial,
                   mix_norm_attn_g=mix_norm_attn_g, mix_norm_gmlp_g=mix_norm_gmlp_g, norm2_g=norm2_g)
    small_m = dict(norm1_g=m_norm1_g, gmlp_ln_g=m_gmlp_ln_g, w_spatial=m_w_spatial, b_spatial=m_b_spatial,
                   mix_norm_attn_g=m_mix_norm_attn_g, mix_norm_gmlp_g=m_mix_norm_gmlp_g, norm2_g=m_norm2_g)
    small_v = dict(norm1_g=v_norm1_g, gmlp_ln_g=v_gmlp_ln_g, w_spatial=v_w_spatial, b_spatial=v_b_spatial,
                   mix_norm_attn_g=v_mix_norm_attn_g, mix_norm_gmlp_g=v_mix_norm_gmlp_g, norm2_g=v_norm2_g)
    def view(n, a):
        return jnp.swapaxes(a, 1, 2) if n in TRANSPOSED else a

    big_w = {n: view(n, a) for n, a in dict(w_in=w_in, w_out=w_out, w_gate=w_gate, w_up=w_up, w_down=w_down).items()}
    big_m = {n: view(n, a) for n, a in dict(w_in=m_w_in, w_out=m_w_out, w_gate=m_w_gate, w_up=m_w_up, w_down=m_w_down).items()}
    big_v = {n: view(n, a) for n, a in dict(w_in=v_w_in, w_out=v_w_out, w_gate=v_w_gate, w_up=v_w_up, w_down=v_w_down).items()}

    def layer_small(l):
        ws = w_spatial[l]
        return dict(norm1_g=norm1_g[l][None], gmlp_ln_g=gmlp_ln_g[l][None], mix_norm_attn_g=mix_norm_attn_g[l][None],
                    mix_norm_gmlp_g=mix_norm_gmlp_g[l][None], norm2_g=norm2_g[l][None], w_spatial=ws,
                    w_spatial_t=jnp.swapaxes(ws, 1, 2), bcol=jnp.broadcast_to(b_spatial[l][:, :, None], ws.shape))

    n_stages = 2 * NL
    zero = jnp.zeros(TOKEN, F32)
    TOK = 5

    def stage_bufs(s, dep):
        return [to_slot(big_w[n], s // 2, me_arr, dep, BF16, transpose=(n in GATHERED_TRANSPOSED)) for n in GROUPS[s % 2]]

    def stage_weights(s, bufs):
        gw = dict(zip(GROUPS[s % 2], bufs))
        if s % 2 == 0:
            gw["w_in"] = gw["w_in"].reshape(-1, D)
            gw["w_out"] = gw["w_out"].reshape(2, DA, D)
        else:
            gw = {n: b.reshape(-1, D) for n, b in gw.items()}
        return gw

    ici = {0: split_start("gather_ici", stage_bufs(0, zero), None)}
    ici[1] = split_start("gather_ici", stage_bufs(1, ici[0][TOK]), None)
    h = xs
    d2d = {0: split_start("gather_d2d", split_wait(ici[0], h), None)}
    ready = {0: split_wait(d2d[0], h)}
    saved, weights = [], []
    for s in range(n_stages):
        deps = [ici[1][TOK]] if s == 0 else []
        if 1 <= s < n_stages - 1:
            d2d[s + 1] = split_start("gather_d2d", split_wait(ici[s + 1], h), None)
            deps.append(d2d[s + 1][TOK])
        if s + 2 < n_stages:
            ici[s + 2] = split_start("gather_ici", stage_bufs(s + 2, deps[-1]), None)
            deps.append(ici[s + 2][TOK])
        dep = sum(deps[1:], deps[0]) if deps else zero
        gw = stage_weights(s, ready[s])
        sm = layer_small(s // 2)
        if s % 2 == 0:
            sm["norm1_g"] = sm["norm1_g"] + dep[0, 0]
            h, sv = mixer_fwd(h, sm, gw, tabs)
        else:
            sm["norm2_g"] = sm["norm2_g"] + dep[0, 0]
            h, sv = ffn_fwd(h, sm, gw)
        saved.append(sv)
        weights.append(gw)
        if s == 0:
            d2d[1] = split_start("gather_d2d", split_wait(ici[1], h), None)
        if s + 1 < n_stages:
            ready[s + 1] = split_wait(d2d[s + 1], h)
    loss_part, dx, dxb, d_final = loss_and_grad(h, final_g[None], loss_target.reshape(S, D))
    loss = lax.psum(loss_part[0, 0], ("x", "y", "c"))

    big_sums = [dict() for _ in range(NL)]
    small_grads = [dict() for _ in range(NL)]
    pending, pending_stage = None, None
    dep = zero
    for s in reversed(range(n_stages)):
        l = s // 2
        crossing = []

        def on_grads(big, s=s, crossing=crossing):
            parts = [big[n].reshape((N_DEV, -1, big[n].shape[-1])) for n in GROUPS[s % 2]]
            gots = [lax.empty((N_CHIP,) + p.shape[1:], p.dtype) for p in parts]
            crossing.append(split_start("scatter_d2d", parts, gots))
            return crossing[0][TOK]

        if s % 2 == 1:
            dx, dxb, small = ffn_bwd(dx, dxb, saved[s], layer_small(l), weights[s], dep, on_grads)
        else:
            dx, dxb, small = mixer_bwd(dx, dxb, saved[s], layer_small(l), weights[s], tabs, dep, on_grads)
        small_grads[l].update(small)
        if pending is not None:
            big_sums[pending_stage // 2].update(zip(GROUPS[pending_stage % 2], split_wait(pending, dx)[1]))
        parts, got = split_wait(crossing[0], dx)
        pairs = [pair_sum(p, g, c_arr, chip_arr) for p, g in zip(parts, got)]
        pending, pending_stage = split_start("scatter_ici", [t for t, _ in pairs], [o for _, o in pairs]), s
        dep = pending[TOK]

    def update_big(n):
        res = adamw(big_w[n], big_m[n], big_v[n], [big_sums[l][n] for l in range(NL)], n in GATHERED_TRANSPOSED)
        return [view(n, t) for t in res], res[0][0, :TOKEN[0], :TOKEN[1]]

    big_out = {}
    after_ffn = zero
    for n in GROUPS[1]:
        big_out[n], piece = update_big(n)
        after_ffn = after_ffn + piece

    packed = to_slot(_pack(small_grads, d_final), None, me_arr, dep + after_ffn, F32)
    small_pending = split_start("gather_ici", [packed], None)
    big_sums[pending_stage // 2].update(zip(GROUPS[pending_stage % 2], split_wait(pending, small_pending[TOK])[1]))
    after_mixer = zero
    for n in GROUPS[0]:
        big_out[n], piece = update_big(n)
        after_mixer = after_mixer + piece
    gathered_small = gather_d2d(split_wait(small_pending, after_mixer))[0][0]
    pw = _pack([{n: small_w[n][l] for n in SMALL} for l in range(NL)], final_g)[None]
    pm = _pack([{n: small_m[n][l] for n in SMALL} for l in range(NL)], m_final_g)[None]
    pv = _pack([{n: small_v[n][l] for n in SMALL} for l in range(NL)], v_final_g)[None]
    like_layers = [{n: small_w[n][l] for n in SMALL} for l in range(NL)]
    small_res = adamw(pw, pm, pv, [gathered_small])
    small_out = [_unpack(t[0], like_layers, final_g) for t in small_res]

    def small_stack(k, n):
        return jnp.stack([small_out[k][0][l][n] for l in range(NL)])

    order = ("norm1_g", "w_in", "gmlp_ln_g", "w_spatial", "b_spatial", "mix_norm_attn_g", "mix_norm_gmlp_g", "w_out",
             "norm2_g", "w_gate", "w_up", "w_down")
    outs = [loss, dx.reshape(x.shape)]
    for k in range(4):
        for n in order:
            outs.append(big_out[n][k] if n in BIG else small_stack(k, n))
        outs.append(small_out[k][1])
    return tuple(outs)
```

```python
import functools

import jax
import jax.numpy as jnp
from jax import lax
from jax.experimental import pallas as pl
from jax.experimental.pallas import tpu as pltpu

F32 = jnp.float32
BF16 = jnp.bfloat16

HEAD_DIM = 128
CHUNK = 128
GROUP = 128
N_SIDE = 64
DILATIONS = (1, 4, 16)
TQ = 128
HALO = 64
EPS = 1e-6
NEG = -1e30
ROPE_THETA = 10000.0
ADAM_LR = 0.001
ADAM_B1 = 0.9
ADAM_B2 = 0.999
ADAM_EPS = 1e-08
ADAM_WD = 0.01
ADAM_STEP = 10
N_DEV = 8
N_CHIP = 4
VMEM_LIMIT_V7X = 56 * 1024 * 1024
ADAMW_BLOCK_ELEMS = 128 * 1024
MESH = pl.DeviceIdType.MESH
ANY = pl.BlockSpec(memory_space=pl.ANY)


def _params(sem=None):
    return pltpu.CompilerParams(dimension_semantics=sem, vmem_limit_bytes=VMEM_LIMIT_V7X)


def _tile(n, want):
    for t in range(min(want, n), 15, -1):
        if n % t == 0 and t % 16 == 0:
            return t
    return n


def _sds(shape, dtype):
    return jax.ShapeDtypeStruct(shape, dtype)


def _dot(a, b, dims):
    return lax.dot_general(a, b, (dims, ((), ())), preferred_element_type=F32)


NN = ((1,), (0,))
NT = ((1,), (1,))
TN = ((0,), (0,))


def _sigmoid(x):
    return 1.0 / (1.0 + jnp.exp(-x))


def _gelu(x):
    return 0.5 * x * (1.0 + lax.erf(x * 0.7071067811865476))


def _gelu_grad(x):
    cdf = 0.5 * (1.0 + lax.erf(x * 0.7071067811865476))
    pdf = 0.3989422804014327 * jnp.exp(-0.5 * x * x)
    return cdf + x * pdf


def rms_fwd(x, g):
    S, D = x.shape
    tr = _tile(S, 512)

    def body(x_ref, g_ref, o_ref):
        xv = x_ref[...]
        r = lax.rsqrt(jnp.mean(xv * xv, axis=-1, keepdims=True) + EPS)
        o_ref[...] = (xv * r * g_ref[...]).astype(o_ref.dtype)

    return pl.pallas_call(
        body, name="rms_fwd", grid=(S // tr,),
        in_specs=[pl.BlockSpec((tr, D), lambda i: (i, 0)), pl.BlockSpec((1, D), lambda i: (0, 0))],
        out_specs=pl.BlockSpec((tr, D), lambda i: (i, 0)),
        out_shape=_sds((S, D), BF16), compiler_params=_params(("parallel",)),
    )(x, g)


def _rms_bwd_math(xv, gv, dy):
    r = lax.rsqrt(jnp.mean(xv * xv, axis=-1, keepdims=True) + EPS)
    dyg = dy * gv
    dx = r * dyg - xv * (r * r * r) * jnp.mean(dyg * xv, axis=-1, keepdims=True)
    dg = jnp.sum(dy * xv * r, axis=0, keepdims=True)
    return dx, dg


def rms_bwd_res(x, g, dy, dres):
    S, D = x.shape
    tr = _tile(S, 512)

    def body(x_ref, g_ref, dy_ref, dres_ref, dx_ref, dxb_ref, dg_ref):
        dx, dg = _rms_bwd_math(x_ref[...], g_ref[...], dy_ref[...])
        tot = dres_ref[...] + dx
        dx_ref[...] = tot
        dxb_ref[...] = tot.astype(dxb_ref.dtype)

        @pl.when(pl.program_id(0) == 0)
        def _():
            dg_ref[...] = dg

        @pl.when(pl.program_id(0) > 0)
        def _():
            dg_ref[...] += dg

    row = pl.BlockSpec((tr, D), lambda i: (i, 0))
    vec = pl.BlockSpec((1, D), lambda i: (0, 0))
    return pl.pallas_call(
        body, name="rms_bwd_res", grid=(S // tr,),
        in_specs=[row, vec, row, row], out_specs=[row, row, vec],
        out_shape=[_sds((S, D), F32), _sds((S, D), BF16), _sds((1, D), F32)],
        compiler_params=_params(("arbitrary",)),
    )(x, g, dy, dres)


def loss_and_grad(x, g, target):
    S, D = x.shape
    tr = _tile(S, 512)

    def body(x_ref, g_ref, t_ref, loss_ref, dx_ref, dxb_ref, dg_ref):
        xv = x_ref[...]
        gv = g_ref[...]
        r = lax.rsqrt(jnp.mean(xv * xv, axis=-1, keepdims=True) + EPS)
        y = xv * r * gv
        diff = y - t_ref[...]
        part = 0.5 * jnp.sum(jnp.mean(diff * diff, axis=-1, keepdims=True), axis=0, keepdims=True)
        dy = diff * (1.0 / D)
        dx, dg = _rms_bwd_math(xv, gv, dy)
        dx_ref[...] = dx
        dxb_ref[...] = dx.astype(dxb_ref.dtype)

        @pl.when(pl.program_id(0) == 0)
        def _():
            dg_ref[...] = dg
            loss_ref[...] = jnp.broadcast_to(part, loss_ref.shape)

        @pl.when(pl.program_id(0) > 0)
        def _():
            dg_ref[...] += dg
            loss_ref[...] += jnp.broadcast_to(part, loss_ref.shape)

    row = pl.BlockSpec((tr, D), lambda i: (i, 0))
    vec = pl.BlockSpec((1, D), lambda i: (0, 0))
    return pl.pallas_call(
        body, name="loss_and_grad", grid=(S // tr,),
        in_specs=[row, vec, row],
        out_specs=[pl.BlockSpec((8, 128), lambda i: (0, 0)), row, row, vec],
        out_shape=[_sds((8, 128), F32), _sds((S, D), F32), _sds((S, D), BF16), _sds((1, D), F32)],
        compiler_params=_params(("arbitrary",)),
    )(x, g, target)


def _mm_body(n_pairs, dims, red_axis, n_red, has_res):
    def body(*refs):
        ins = refs[:2 * n_pairs]
        res_ref = refs[2 * n_pairs] if has_res else None
        o_ref = refs[2 * n_pairs + has_res]
        p = None
        for t in range(n_pairs):
            d = _dot(ins[2 * t][...], ins[2 * t + 1][...], dims)
            p = d if p is None else p + d
        if red_axis is None:
            if has_res:
                p = res_ref[...] + p
            o_ref[...] = p.astype(o_ref.dtype)
            return
        acc_ref = o_ref if o_ref.dtype == F32 else refs[2 * n_pairs + has_res + 1]
        r = pl.program_id(red_axis)

        @pl.when(r == 0)
        def _():
            acc_ref[...] = res_ref[...] + p if has_res else p

        @pl.when(r > 0)
        def _():
            acc_ref[...] += p

        if acc_ref is not o_ref:
            @pl.when(r == n_red - 1)
            def _():
                o_ref[...] = acc_ref[...].astype(o_ref.dtype)

    return body


def mm_in(h, w):
    S, D = h.shape
    N = w.shape[0]
    tm = _tile(S, 1024)
    tn = _tile(N, 1024)
    return pl.pallas_call(
        _mm_body(1, NT, None, 1, False), name="mm_in", grid=(S // tm, N // tn),
        in_specs=[pl.BlockSpec((tm, D), lambda i, j: (i, 0)), pl.BlockSpec((tn, D), lambda i, j: (j, 0))],
        out_specs=pl.BlockSpec((tm, tn), lambda i, j: (i, j)),
        out_shape=_sds((S, N), F32), compiler_params=_params(("parallel", "parallel")),
    )(h, w)


def mm_out(x, mix_a, mix_g, w):
    S, D = x.shape
    DA = mix_a.shape[1]
    tm = _tile(S, 1024)
    tn = _tile(D, 1024)
    act = pl.BlockSpec((tm, DA), lambda i, j: (i, 0))
    return pl.pallas_call(
        _mm_body(2, NN, None, 1, True), name="mm_out", grid=(S // tm, D // tn),
        in_specs=[act, pl.BlockSpec((None, DA, tn), lambda i, j: (0, 0, j)),
                  act, pl.BlockSpec((None, DA, tn), lambda i, j: (1, 0, j)),
                  pl.BlockSpec((tm, tn), lambda i, j: (i, j))],
        out_specs=pl.BlockSpec((tm, tn), lambda i, j: (i, j)),
        out_shape=_sds((S, D), F32), compiler_params=_params(("parallel", "parallel")),
    )(mix_a, w, mix_g, w, x)


def _lane_tile(n, want):
    for t in range(min(want, n) // 128 * 128, 127, -128):
        if n % t == 0:
            return t
    return n


def ff_fwd(h, wg, wu):
    S, D = h.shape
    F = wg.shape[0]
    tm = _tile(S, 1024)
    tn = _lane_tile(F, 512)

    def body(h_ref, wg_ref, wu_ref, dfdu_ref, dfdg_ref, ff_ref):
        hv = h_ref[...]
        g = _dot(hv, wg_ref[...], NT)
        u = _dot(hv, wu_ref[...], NT)
        sg = _sigmoid(g)
        act = g * sg
        dfdu_ref[...] = act.astype(dfdu_ref.dtype)
        dfdg_ref[...] = (u * (sg * (1.0 + g * (1.0 - sg)))).astype(dfdg_ref.dtype)
        ff_ref[...] = (act * u).astype(ff_ref.dtype)

    wspec = pl.BlockSpec((tn, D), lambda i, j: (j, 0))
    ospec = pl.BlockSpec((tm, tn), lambda i, j: (i, j))
    return pl.pallas_call(
        body, name="ff_fwd", grid=(S // tm, F // tn),
        in_specs=[pl.BlockSpec((tm, D), lambda i, j: (i, 0)), wspec, wspec],
        out_specs=[ospec, ospec, ospec],
        out_shape=[_sds((S, F), BF16), _sds((S, F), BF16), _sds((S, F), BF16)],
        compiler_params=_params(("parallel", "parallel")),
    )(h, wg, wu)


def mm_down(x, ff, wd):
    S, D = x.shape
    F = ff.shape[1]
    tm = _tile(S, 1024)
    tn = _lane_tile(D, 512)
    blk = pl.BlockSpec((tm, tn), lambda i, j: (i, j))
    return pl.pallas_call(
        _mm_body(1, NN, None, 1, True), name="mm_down", grid=(S // tm, D // tn),
        in_specs=[pl.BlockSpec((tm, F), lambda i, j: (i, 0)), pl.BlockSpec((F, tn), lambda i, j: (0, j)), blk],
        out_specs=blk, out_shape=_sds((S, D), F32), compiler_params=_params(("parallel", "parallel")),
    )(ff, wd, x)


def ff_bwd_act(dxb, wd, dfdg, dfdu, dep):
    S, D = dxb.shape
    F = wd.shape[0]
    tm = _tile(S, 1024)
    tn = _lane_tile(F, 512)

    def body(dx_ref, wd_ref, dfdg_ref, dfdu_ref, dep_ref, dgate_ref, dup_ref):
        dff = _dot(dx_ref[...], wd_ref[...], NT)
        dup_ref[...] = (dff * dfdu_ref[...].astype(F32)).astype(dup_ref.dtype)
        dgate_ref[...] = (dff * dfdg_ref[...].astype(F32)).astype(dgate_ref.dtype)

    aspec = pl.BlockSpec((tm, tn), lambda i, j: (i, j))
    return pl.pallas_call(
        body, name="ff_bwd_act", grid=(S // tm, F // tn),
        in_specs=[pl.BlockSpec((tm, D), lambda i, j: (i, 0)), pl.BlockSpec((tn, D), lambda i, j: (j, 0)), aspec, aspec,
                  pl.BlockSpec(TOKEN, lambda i, j: (0, 0))],
        out_specs=[aspec, aspec],
        out_shape=[_sds((S, F), BF16), _sds((S, F), BF16)],
        compiler_params=_params(("parallel", "parallel")),
    )(dxb, wd, dfdg, dfdu, dep)


def dw_down(ff, dxb):
    S, F = ff.shape
    D = dxb.shape[1]
    tf = _lane_tile(F, 256)
    return pl.pallas_call(
        _mm_body(1, TN, None, 1, False), name="dw_down", grid=(F // tf,),
        in_specs=[pl.BlockSpec((S, tf), lambda k: (0, k)), pl.BlockSpec((S, D), lambda k: (0, 0))],
        out_specs=pl.BlockSpec((tf, D), lambda k: (k, 0)),
        out_shape=_sds((F, D), BF16), compiler_params=_params(("parallel",)),
    )(ff, dxb)


def dw_gate_up(h, dgate, dup):
    S, D = h.shape
    F = dgate.shape[1]
    f = _lane_tile(F, 256)

    def body(h_ref, dg_ref, du_ref, og_ref, ou_ref):
        hv = h_ref[...]
        og_ref[...] = _dot(dg_ref[...], hv, TN).astype(og_ref.dtype)
        ou_ref[...] = _dot(du_ref[...], hv, TN).astype(ou_ref.dtype)

    aspec = pl.BlockSpec((S, f), lambda k: (0, k))
    ospec = pl.BlockSpec((f, D), lambda k: (k, 0))
    return pl.pallas_call(
        body, name="dw_gate_up", grid=(F // f,),
        in_specs=[pl.BlockSpec((S, D), lambda k: (0, 0)), aspec, aspec],
        out_specs=[ospec, ospec],
        out_shape=[_sds((F, D), BF16), _sds((F, D), BF16)],
        compiler_params=_params(("parallel",)),
    )(h, dgate, dup)


def dh_ff(dgate, dup, wg, wu, dep):
    S, F = dgate.shape
    D = wg.shape[1]
    tm = _tile(S, 512)
    tn = _lane_tile(D, 256)

    def body(dg_ref, wg_ref, du_ref, wu_ref, dep_ref, o_ref):
        o_ref[...] = _dot(dg_ref[...], wg_ref[...], NN) + _dot(du_ref[...], wu_ref[...], NN)

    aspec = pl.BlockSpec((tm, F), lambda i, j: (i, 0))
    wspec = pl.BlockSpec((F, tn), lambda i, j: (0, j))
    return pl.pallas_call(
        body, name="dh_ff", grid=(S // tm, D // tn),
        in_specs=[aspec, wspec, aspec, wspec, pl.BlockSpec(TOKEN, lambda i, j: (0, 0))],
        out_specs=pl.BlockSpec((tm, tn), lambda i, j: (i, j)),
        out_shape=_sds((S, D), F32), compiler_params=_params(("parallel", "parallel")),
    )(dgate, wg, dup, wu, dep)


def dmix_mm(dxb, w, dep):
    S, D = dxb.shape
    tm = _tile(S, 1024)
    tn = _tile(D, 1024)

    def body(dx_ref, w_ref, dep_ref, o_ref):
        o_ref[...] = _dot(dx_ref[...], w_ref[...], NT)

    return pl.pallas_call(
        body, name="dmix_mm", grid=(S // tm, D // tn),
        in_specs=[pl.BlockSpec((tm, D), lambda i, j: (i, 0)), pl.BlockSpec((tn, D), lambda i, j: (j, 0)),
                  pl.BlockSpec(TOKEN, lambda i, j: (0, 0))],
        out_specs=pl.BlockSpec((tm, tn), lambda i, j: (i, j)),
        out_shape=_sds((S, D), F32), compiler_params=_params(("parallel", "parallel")),
    )(dxb, w, dep)


def dw_out(mix_a, mix_g, dxb):
    S, DA = mix_a.shape
    D = dxb.shape[1]
    tn = _lane_tile(D, 512)

    def half(name, m):
        return pl.pallas_call(
            _mm_body(1, TN, None, 1, False), name=name, grid=(D // tn,),
            in_specs=[pl.BlockSpec((S, DA), lambda j: (0, 0)), pl.BlockSpec((S, tn), lambda j: (0, j))],
            out_specs=pl.BlockSpec((DA, tn), lambda j: (0, j)),
            out_shape=_sds((DA, D), BF16), compiler_params=_params(("parallel",)),
        )(m, dxb)

    return half("dw_out_a", mix_a), half("dw_out_g", mix_g)


def dw_in(h, dproj):
    S, D = h.shape
    N = dproj.shape[1]
    tn = _lane_tile(N, 256)
    return pl.pallas_call(
        _mm_body(1, TN, None, 1, False), name="dw_in", grid=(N // tn,),
        in_specs=[pl.BlockSpec((S, tn), lambda k: (0, k)), pl.BlockSpec((S, D), lambda k: (0, 0))],
        out_specs=pl.BlockSpec((tn, D), lambda k: (k, 0)),
        out_shape=_sds((N, D), BF16), compiler_params=_params(("parallel",)),
    )(dproj, h)


def dh_in(dproj, w, dep):
    S, N = dproj.shape
    D = w.shape[1]
    tm = _tile(S, 1024)
    tn = _lane_tile(D, 512)

    def body(dp_ref, w_ref, dep_ref, o_ref):
        o_ref[...] = _dot(dp_ref[...], w_ref[...], NN)

    return pl.pallas_call(
        body, name="dh_in", grid=(S // tm, D // tn),
        in_specs=[pl.BlockSpec((tm, N), lambda i, j: (i, 0)), pl.BlockSpec((N, tn), lambda i, j: (0, j)),
                  pl.BlockSpec(TOKEN, lambda i, j: (0, 0))],
        out_specs=pl.BlockSpec((tm, tn), lambda i, j: (i, j)),
        out_shape=_sds((S, D), F32), compiler_params=_params(("parallel", "parallel")),
    )(dproj, w, dep)


def rope_tables(S):
    pos = jnp.arange(S, dtype=F32)
    inv = ROPE_THETA ** (-jnp.arange(0, HEAD_DIM, 2, dtype=F32) / HEAD_DIM)
    ang = pos[:, None] * inv[None, :]
    cos, sin = jnp.cos(ang), jnp.sin(ang)
    return jnp.concatenate([cos, cos], axis=-1), jnp.concatenate([-sin, sin], axis=-1)


def _rot_half(t):
    return pltpu.roll(t, HEAD_DIM // 2, 1)


PERM_ROWS = 256


def _by_residue(y, d, out_ref):
    tr = y.shape[0]
    n = tr // d
    o = lax.broadcasted_iota(jnp.int32, (tr, tr), 0)
    i = lax.broadcasted_iota(jnp.int32, (tr, tr), 1)
    src = jnp.bitwise_and(o, n - 1) * d + lax.shift_right_logical(o, n.bit_length() - 1)
    z = _dot((src == i).astype(y.dtype), y, NN).astype(out_ref.dtype)
    for r in range(d):
        out_ref[r] = z[r * n:(r + 1) * n]


def _from_residue(ref, d, terms):
    z = jnp.concatenate([ref[r] for r in range(d)], axis=0)
    tr = z.shape[0]
    n = tr // d
    i = lax.broadcasted_iota(jnp.int32, (tr, tr), 0)
    o = lax.broadcasted_iota(jnp.int32, (tr, tr), 1)
    src = jnp.bitwise_and(i, d - 1) * n + lax.shift_right_logical(i, d.bit_length() - 1)
    pick = (src == o).astype(BF16)
    out = None
    rest = z
    for _ in range(terms):
        piece = rest.astype(BF16)
        got = _dot(pick, piece, NN)
        out = got if out is None else out + got
        rest = rest - piece.astype(F32)
    return out


def _residue_blocks(widths):
    return [pl.BlockSpec((d, PERM_ROWS // d, W), lambda i: (0, i, 0)) for d, W in zip(DILATIONS[1:], widths)]


def _residue_outputs(S, W, dtype):
    shapes = [_sds((d, S // d, W), dtype) for d in DILATIONS[1:]]
    specs = [pl.BlockSpec((d, PERM_ROWS // d, W), lambda i: (0, i, 0)) for d in DILATIONS[1:]]
    return shapes, specs


def rope_qkv(proj, cos2, sin2, DA):
    S = proj.shape[0]
    tr = PERM_ROWS
    nh = DA // HEAD_DIM

    def body(q_ref, k_ref, v_ref, cos_ref, sin_ref, o_ref, *by_res):
        c = cos_ref[...]
        s = sin_ref[...]
        for h in range(nh):
            sl = slice(h * HEAD_DIM, (h + 1) * HEAD_DIM)
            for j, ref in enumerate((q_ref, k_ref)):
                t = ref[:, sl]
                o_ref[:, j * DA + h * HEAD_DIM:j * DA + (h + 1) * HEAD_DIM] = (t * c + _rot_half(t) * s).astype(o_ref.dtype)
        o_ref[:, 2 * DA:3 * DA] = v_ref[...].astype(o_ref.dtype)
        y = o_ref[...]
        for d, ref in zip(DILATIONS[1:], by_res):
            _by_residue(y, d, ref)

    tab = pl.BlockSpec((tr, HEAD_DIM), lambda i: (i, 0))
    shapes, specs = _residue_outputs(S, 3 * DA, BF16)
    out = pl.pallas_call(
        body, name="rope_qkv", grid=(S // tr,),
        in_specs=[pl.BlockSpec((tr, DA), lambda i: (i, 0)), pl.BlockSpec((tr, DA), lambda i: (i, 1)),
                  pl.BlockSpec((tr, DA), lambda i: (i, 2)), tab, tab],
        out_specs=[pl.BlockSpec((tr, 3 * DA), lambda i: (i, 0))] + specs,
        out_shape=[_sds((S, 3 * DA), BF16)] + shapes, compiler_params=_params(("parallel",)),
    )(proj, proj, proj, cos2, sin2)
    return [out[0].reshape(1, S, 3 * DA)] + list(out[1:])


def assemble_dproj(dqs, dks, dvs, duv, cos2, sin2):
    S, DA = dqs[0].shape[1:]
    tr = PERM_ROWS
    nh = DA // HEAD_DIM

    def total(trio):
        t = trio[0][...]
        for d, ref in zip(DILATIONS[1:], trio[1:]):
            t = t + _from_residue(ref, d, 2)
        return t

    def body(*refs):
        dq_refs, dk_refs, dv_refs = refs[0:3], refs[3:6], refs[6:9]
        duv_ref, cos_ref, sin_ref, o_ref = refs[9:13]
        c = cos_ref[...]
        s = sin_ref[...]
        for j, trio in enumerate((dq_refs, dk_refs)):
            t_all = total(trio)
            for h in range(nh):
                t = t_all[:, h * HEAD_DIM:(h + 1) * HEAD_DIM]
                o_ref[:, j * DA + h * HEAD_DIM:j * DA + (h + 1) * HEAD_DIM] = (t * c - _rot_half(t) * s).astype(o_ref.dtype)
        o_ref[:, 2 * DA:3 * DA] = total(dv_refs).astype(o_ref.dtype)
        o_ref[:, 3 * DA:5 * DA] = duv_ref[...]

    trio_specs = [pl.BlockSpec((None, tr, DA), lambda i: (0, i, 0))] + _residue_blocks([DA, DA])
    tab = pl.BlockSpec((tr, HEAD_DIM), lambda i: (i, 0))
    return pl.pallas_call(
        body, name="assemble_dproj", grid=(S // tr,),
        in_specs=trio_specs * 3 + [pl.BlockSpec((tr, 2 * DA), lambda i: (i, 0)), tab, tab],
        out_specs=pl.BlockSpec((tr, 5 * DA), lambda i: (i, 0)),
        out_shape=_sds((S, 5 * DA), BF16), compiler_params=_params(("parallel",)),
    )(*dqs, *dks, *dvs, duv, cos2, sin2)


STAT_W = 128


def _lanes_per_head(DA):
    return STAT_W // (DA // HEAD_DIM)


def _halo_specs(L, width_blocks, col):
    per = TQ // HALO
    last = L // HALO - 1
    W = width_blocks
    return [
        pl.BlockSpec((HALO, W), lambda r, i: (jnp.maximum(i * per - 1, 0), col(r))),
        pl.BlockSpec((TQ, W), lambda r, i: (i, col(r))),
        pl.BlockSpec((HALO, W), lambda r, i: (jnp.minimum((i + 1) * per, last), col(r))),
    ]


def _halo_specs3(L, W, c):
    per = TQ // HALO
    last = L // HALO - 1
    return [
        pl.BlockSpec((None, HALO, W), lambda r, i: (r, jnp.maximum(i * per - 1, 0), c)),
        pl.BlockSpec((None, TQ, W), lambda r, i: (r, i, c)),
        pl.BlockSpec((None, HALO, W), lambda r, i: (r, jnp.minimum((i + 1) * per, last), c)),
    ]


def _cat3(refs, sl):
    return jnp.concatenate([refs[0][:, sl], refs[1][:, sl], refs[2][:, sl]], axis=0)


def attn_fwd(qkv, d, DA):
    L = qkv.shape[1]
    S = L * d
    nh = DA // HEAD_DIM
    lph = _lanes_per_head(DA)
    scale = HEAD_DIM ** -0.5
    TK = TQ + 2 * HALO

    def body(q_ref, kp_ref, kc_ref, kn_ref, vp_ref, vc_ref, vn_ref, o_ref, lse_ref):
        i = pl.program_id(1)
        row = lax.broadcasted_iota(jnp.int32, (TQ, TK), 0)
        col = lax.broadcasted_iota(jnp.int32, (TQ, TK), 1)
        kpos = i * TQ - HALO + col
        mask = (jnp.abs(col - HALO - row) <= N_SIDE) & (kpos >= 0) & (kpos < L)
        for h in range(nh):
            sl = slice(h * HEAD_DIM, (h + 1) * HEAD_DIM)
            k = _cat3((kp_ref, kc_ref, kn_ref), sl)
            v = _cat3((vp_ref, vc_ref, vn_ref), sl)
            s = _dot(q_ref[:, sl], k, NT) * scale
            s = jnp.where(mask, s, NEG)
            m = jnp.max(s, axis=-1, keepdims=True)
            p = jnp.exp(s - m)
            l = jnp.sum(p, axis=-1, keepdims=True)
            o = _dot(p.astype(v.dtype), v, NN) / l
            o_ref[:, sl] = o
            lse_ref[:, h * lph:(h + 1) * lph] = jnp.broadcast_to(m + jnp.log(l), (TQ, lph))

    out_spec = pl.BlockSpec((None, TQ, DA), lambda r, i: (r, i, 0))
    o, lse = pl.pallas_call(
        body, name="attn_fwd_d%d" % d, grid=(d, L // TQ),
        in_specs=[pl.BlockSpec((None, TQ, DA), lambda r, i: (r, i, 0))] + _halo_specs3(L, DA, 1) + _halo_specs3(L, DA, 2),
        out_specs=[out_spec, pl.BlockSpec((TQ, STAT_W), lambda r, i: (i, r))],
        out_shape=[_sds((d, L, DA), F32), _sds((L, d * STAT_W), F32)],
        compiler_params=_params(("parallel", "parallel")),
    )(*([qkv] * 7))
    return o, lse.reshape(S, STAT_W)


def attn_merge(os_, lses, g):
    S, DA = os_[0].shape[1:]
    tr = PERM_ROWS
    nh = DA // HEAD_DIM
    lph = _lanes_per_head(DA)

    def body(o0, o1, o2, l0, l1, l2, g_ref, a_ref, lse_ref, mix_ref):
        a0, a1, a2 = l0[...], l1[...], l2[...]
        m = jnp.maximum(jnp.maximum(a0, a1), a2)
        e0, e1, e2 = jnp.exp(a0 - m), jnp.exp(a1 - m), jnp.exp(a2 - m)
        den = e0 + e1 + e2
        w0, w1, w2 = e0 / den, e1 / den, e2 / den
        lse_ref[...] = m + jnp.log(den)
        v0 = o0[...]
        v1 = _from_residue(o1, DILATIONS[1], 2)
        v2 = _from_residue(o2, DILATIONS[2], 2)
        heads = []
        for h in range(nh):
            sl = slice(h * HEAD_DIM, (h + 1) * HEAD_DIM)
            c = slice(h * lph, h * lph + 1)
            heads.append(w0[:, c] * v0[:, sl] + w1[:, c] * v1[:, sl] + w2[:, c] * v2[:, sl])
        a = jnp.concatenate(heads, axis=1)
        a_ref[...] = a
        r = lax.rsqrt(jnp.mean(a * a, axis=-1, keepdims=True) + EPS)
        mix_ref[...] = (a * r * g_ref[...]).astype(mix_ref.dtype)

    blk = pl.BlockSpec((tr, DA), lambda i: (i, 0))
    stat = pl.BlockSpec((tr, STAT_W), lambda i: (i, 0))
    return pl.pallas_call(
        body, name="attn_merge", grid=(S // tr,),
        in_specs=[pl.BlockSpec((None, tr, DA), lambda i: (0, i, 0))] + _residue_blocks([DA, DA]) + [stat] * 3
        + [pl.BlockSpec((1, DA), lambda i: (0, 0))],
        out_specs=[blk, stat, blk],
        out_shape=[_sds((S, DA), F32), _sds((S, STAT_W), F32), _sds((S, DA), BF16)],
        compiler_params=_params(("parallel",)),
    )(*os_, *lses, g)


def attn_out_bwd(a, g, dmix):
    S, DA = a.shape
    tr = PERM_ROWS
    nh = DA // HEAD_DIM
    lph = _lanes_per_head(DA)

    def body(a_ref, g_ref, dy_ref, do_ref, dl_ref, dg_ref, *by_res):
        av = a_ref[...]
        dx, dg = _rms_bwd_math(av, g_ref[...], dy_ref[...])
        dob = dx.astype(do_ref.dtype)
        do_ref[...] = dob
        for d, ref in zip(DILATIONS[1:], by_res):
            _by_residue(dob, d, ref)
        prod = dx * av
        for h in range(nh):
            sl = slice(h * HEAD_DIM, (h + 1) * HEAD_DIM)
            dl_ref[:, h * lph:(h + 1) * lph] = jnp.broadcast_to(jnp.sum(prod[:, sl], axis=-1, keepdims=True), (tr, lph))

        @pl.when(pl.program_id(0) == 0)
        def _():
            dg_ref[...] = dg

        @pl.when(pl.program_id(0) > 0)
        def _():
            dg_ref[...] += dg

    blk = pl.BlockSpec((tr, DA), lambda i: (i, 0))
    vec = pl.BlockSpec((1, DA), lambda i: (0, 0))
    shapes, specs = _residue_outputs(S, DA, BF16)
    out = pl.pallas_call(
        body, name="attn_out_bwd", grid=(S // tr,),
        in_specs=[blk, vec, blk], out_specs=[blk, pl.BlockSpec((tr, STAT_W), lambda i: (i, 0)), vec] + specs,
        out_shape=[_sds((S, DA), BF16), _sds((S, STAT_W), F32), _sds((1, DA), F32)] + shapes,
        compiler_params=_params(("arbitrary",)),
    )(a, g, dmix)
    return [out[0].reshape(1, S, DA)] + list(out[3:]), out[1], out[2]


def attn_bwd_dq(qkv, dob, lse, dl, d, DA):
    L = qkv.shape[1]
    S = L * d
    nh = DA // HEAD_DIM
    lph = _lanes_per_head(DA)
    scale = HEAD_DIM ** -0.5
    TK = TQ + 2 * HALO

    def body(q_ref, kp_ref, kc_ref, kn_ref, vp_ref, vc_ref, vn_ref, do_ref, lse_ref, dl_ref, dq_ref):
        i = pl.program_id(1)
        row = lax.broadcasted_iota(jnp.int32, (TQ, TK), 0)
        col = lax.broadcasted_iota(jnp.int32, (TQ, TK), 1)
        kpos = i * TQ - HALO + col
        mask = (jnp.abs(col - HALO - row) <= N_SIDE) & (kpos >= 0) & (kpos < L)
        for h in range(nh):
            sl = slice(h * HEAD_DIM, (h + 1) * HEAD_DIM)
            k = _cat3((kp_ref, kc_ref, kn_ref), sl)
            v = _cat3((vp_ref, vc_ref, vn_ref), sl)
            s = _dot(q_ref[:, sl], k, NT) * scale
            s = jnp.where(mask, s, NEG)
            p = jnp.exp(s - lse_ref[:, h * lph:h * lph + 1])
            dp = _dot(do_ref[:, sl], v, NT)
            ds = p * (dp - dl_ref[:, h * lph:h * lph + 1])
            dq_ref[:, sl] = _dot(ds.astype(k.dtype), k, NN) * scale

    blk = pl.BlockSpec((None, TQ, DA), lambda r, i: (r, i, 0))
    stat = pl.BlockSpec((TQ, STAT_W), lambda r, i: (i, r))
    return pl.pallas_call(
        body, name="attn_bwd_dq_d%d" % d, grid=(d, L // TQ),
        in_specs=[blk] + _halo_specs3(L, DA, 1) + _halo_specs3(L, DA, 2) + [blk, stat, stat],
        out_specs=blk, out_shape=_sds((d, L, DA), F32),
        compiler_params=_params(("parallel", "parallel")),
    )(*([qkv] * 7), dob, lse.reshape(L, d * STAT_W), dl.reshape(L, d * STAT_W))


def attn_bwd_dkv(qkv, dob, lse, dl, d, DA):
    L = qkv.shape[1]
    S = L * d
    nh = DA // HEAD_DIM
    lph = _lanes_per_head(DA)
    scale = HEAD_DIM ** -0.5
    TR = TQ + 2 * HALO

    def body(k_ref, v_ref, qp, qc, qn, dop, doc, don, lp, lc, ln, dp_, dc_, dn_, dk_ref, dv_ref):
        j = pl.program_id(1)
        row = lax.broadcasted_iota(jnp.int32, (TR, TQ), 0)
        col = lax.broadcasted_iota(jnp.int32, (TR, TQ), 1)
        qpos = j * TQ - HALO + row
        mask = (jnp.abs(col - (row - HALO)) <= N_SIDE) & (qpos >= 0) & (qpos < L)
        for h in range(nh):
            sl = slice(h * HEAD_DIM, (h + 1) * HEAD_DIM)
            q = _cat3((qp, qc, qn), sl)
            do = _cat3((dop, doc, don), sl)
            stat = slice(h * lph, h * lph + 1)
            lse_q = _cat3((lp, lc, ln), stat)
            dl_q = _cat3((dp_, dc_, dn_), stat)
            k = k_ref[:, sl]
            v = v_ref[:, sl]
            s = _dot(q, k, NT) * scale
            s = jnp.where(mask, s, NEG)
            p = jnp.exp(s - lse_q)
            dv_ref[:, sl] = _dot(p.astype(do.dtype), do, TN)
            dp = _dot(do, v, NT)
            ds = p * (dp - dl_q)
            dk_ref[:, sl] = _dot(ds.astype(q.dtype), q, TN) * scale

    blk = pl.BlockSpec((None, TQ, DA), lambda r, i: (r, i, 0))
    ident = lambda r: r
    dk, dv = pl.pallas_call(
        body, name="attn_bwd_dkv_d%d" % d, grid=(d, L // TQ),
        in_specs=[pl.BlockSpec((None, TQ, DA), lambda r, i: (r, i, 1)), pl.BlockSpec((None, TQ, DA), lambda r, i: (r, i, 2))]
        + _halo_specs3(L, DA, 0) + _halo_specs3(L, DA, 0) + _halo_specs(L, STAT_W, ident) + _halo_specs(L, STAT_W, ident),
        out_specs=[blk, blk], out_shape=[_sds((d, L, DA), F32), _sds((d, L, DA), F32)],
        compiler_params=_params(("parallel", "parallel")),
    )(*([qkv] * 5), *([dob] * 3), *([lse.reshape(L, d * STAT_W)] * 3), *([dl.reshape(L, d * STAT_W)] * 3))
    return dk, dv


def _gmlp_fwd_math(u_raw, v_raw, ln_g, ws_ref, bcol_ref, n_chunks, ng):
    ug = _gelu(u_raw)
    vg = _gelu(v_raw)
    mu = jnp.mean(vg, axis=-1, keepdims=True)
    xc = vg - mu
    rstd = lax.rsqrt(jnp.mean(xc * xc, axis=-1, keepdims=True) + EPS)
    xhat = xc * rstd
    vn = xhat * ln_g
    rows = []
    for n in range(n_chunks):
        cols = []
        for g in range(ng):
            blk = vn[n * CHUNK:(n + 1) * CHUNK, g * GROUP:(g + 1) * GROUP]
            cols.append(_dot(ws_ref[g].astype(BF16), blk.astype(BF16), NN) + bcol_ref[g])
        rows.append(jnp.concatenate(cols, axis=1))
    mixed = jnp.concatenate(rows, axis=0)
    return ug, xhat, rstd, vn, mixed


def gmlp_fwd(proj, ln_g, w_s, bcol, g_out, DA):
    S = proj.shape[0]
    ng = DA // GROUP
    tr = _tile(S, 2 * CHUNK)

    def body(u_ref, v_ref, ln_ref, ws_ref, bcol_ref, g_ref, G_ref, mix_ref):
        ug, _, _, _, mixed = _gmlp_fwd_math(u_ref[...], v_ref[...], ln_ref[...], ws_ref, bcol_ref, tr // CHUNK, ng)
        G = ug * mixed
        G_ref[...] = G
        r = lax.rsqrt(jnp.mean(G * G, axis=-1, keepdims=True) + EPS)
        mix_ref[...] = (G * r * g_ref[...]).astype(mix_ref.dtype)

    vec = pl.BlockSpec((1, DA), lambda i: (0, 0))
    par = pl.BlockSpec((ng, CHUNK, CHUNK), lambda i: (0, 0, 0))
    blk = pl.BlockSpec((tr, DA), lambda i: (i, 0))
    return pl.pallas_call(
        body, name="gmlp_fwd", grid=(S // tr,),
        in_specs=[pl.BlockSpec((tr, DA), lambda i: (i, 3)), pl.BlockSpec((tr, DA), lambda i: (i, 4)), vec, par, par, vec],
        out_specs=[blk, blk], out_shape=[_sds((S, DA), F32), _sds((S, DA), BF16)],
        compiler_params=_params(("parallel",)),
    )(proj, proj, ln_g, w_s, bcol, g_out)


def gmlp_bwd(proj, ln_g, w_s, w_st, bcol, g_out, dmix, DA):
    S = proj.shape[0]
    ng = DA // GROUP
    tr = _tile(S, 2 * CHUNK)
    nc = tr // CHUNK

    def body(u_ref, v_ref, ln_ref, ws_ref, wst_ref, bcol_ref, g_ref, dy_ref, duv_ref, dg_ref, dln_ref, dws_ref, db_ref):
        u_raw = u_ref[...]
        v_raw = v_ref[...]
        ln_g_v = ln_ref[...]
        ug, xhat, rstd, vn, mixed = _gmlp_fwd_math(u_raw, v_raw, ln_g_v, ws_ref, bcol_ref, nc, ng)
        G = ug * mixed
        dG, dg = _rms_bwd_math(G, g_ref[...], dy_ref[...])
        du_g = dG * mixed
        dmixed = dG * ug
        dws, dbs, rows = [], [], []
        for g in range(ng):
            dws.append(None)
            dbs.append(None)
        for n in range(nc):
            cols = []
            for g in range(ng):
                dm = dmixed[n * CHUNK:(n + 1) * CHUNK, g * GROUP:(g + 1) * GROUP]
                vb = vn[n * CHUNK:(n + 1) * CHUNK, g * GROUP:(g + 1) * GROUP]
                dmb = dm.astype(BF16)
                w = _dot(dmb, vb.astype(BF16), NT)
                b = jnp.broadcast_to(jnp.sum(dm, axis=-1, keepdims=True), (CHUNK, GROUP))
                dws[g] = w if dws[g] is None else dws[g] + w
                dbs[g] = b if dbs[g] is None else dbs[g] + b
                cols.append(_dot(wst_ref[g].astype(BF16), dmb, NN))
            rows.append(jnp.concatenate(cols, axis=1))
        dvn = jnp.concatenate(rows, axis=0)
        dln = jnp.sum(dvn * xhat, axis=0, keepdims=True)
        dxh = dvn * ln_g_v
        dvg = rstd * (dxh - jnp.mean(dxh, axis=-1, keepdims=True) - xhat * jnp.mean(dxh * xhat, axis=-1, keepdims=True))
        duv_ref[:, 0:DA] = (du_g * _gelu_grad(u_raw)).astype(duv_ref.dtype)
        duv_ref[:, DA:2 * DA] = (dvg * _gelu_grad(v_raw)).astype(duv_ref.dtype)

        first = pl.program_id(0) == 0

        @pl.when(first)
        def _():
            dg_ref[...] = dg
            dln_ref[...] = dln
            for g in range(ng):
                dws_ref[g] = dws[g]
                db_ref[g] = dbs[g]

        @pl.when(jnp.logical_not(first))
        def _():
            dg_ref[...] += dg
            dln_ref[...] += dln
            for g in range(ng):
                dws_ref[g] += dws[g]
                db_ref[g] += dbs[g]

    vec = pl.BlockSpec((1, DA), lambda i: (0, 0))
    par = pl.BlockSpec((ng, CHUNK, CHUNK), lambda i: (0, 0, 0))
    return pl.pallas_call(
        body, name="gmlp_bwd", grid=(S // tr,),
        in_specs=[pl.BlockSpec((tr, DA), lambda i: (i, 3)), pl.BlockSpec((tr, DA), lambda i: (i, 4)), vec, par, par, par, vec,
                  pl.BlockSpec((tr, DA), lambda i: (i, 1))],
        out_specs=[pl.BlockSpec((tr, 2 * DA), lambda i: (i, 0)), vec, vec, par, par],
        out_shape=[_sds((S, 2 * DA), BF16), _sds((1, DA), F32), _sds((1, DA), F32),
                   _sds((ng, CHUNK, CHUNK), F32), _sds((ng, CHUNK, CHUNK), F32)],
        compiler_params=_params(("arbitrary",)),
    )(proj, proj, ln_g, w_s, w_st, bcol, g_out, dmix)


def mixer_fwd(x, sm, gw, tabs):
    D = x.shape[1]
    DA = D // 2
    cos2, sin2 = tabs
    h1 = rms_fwd(x, sm["norm1_g"])
    proj = mm_in(h1, gw["w_in"])
    qkv = rope_qkv(proj, cos2, sin2, DA)
    os_, lses = [], []
    for t, d in enumerate(DILATIONS):
        o, l = attn_fwd(qkv[t], d, DA)
        os_.append(o)
        lses.append(l)
    a, lse, mix_a = attn_merge(os_, lses, sm["mix_norm_attn_g"])
    _, mix_g = gmlp_fwd(proj, sm["gmlp_ln_g"], sm["w_spatial"], sm["bcol"], sm["mix_norm_gmlp_g"], DA)
    x2 = mm_out(x, mix_a, mix_g, gw["w_out"])
    return x2, dict(x=x, h1=h1, proj=proj, qkv=qkv, a=a, lse=lse, mix_a=mix_a, mix_g=mix_g)


def ffn_fwd(x2, sm, gw):
    h2 = rms_fwd(x2, sm["norm2_g"])
    dfdu, dfdg, ff = ff_fwd(h2, gw["w_gate"], gw["w_up"])
    x3 = mm_down(x2, ff, gw["w_down"])
    return x3, dict(x2=x2, h2=h2, dfdg=dfdg, dfdu=dfdu, ff=ff)


def ffn_bwd(dx, dxb, sv, sm, gw, dep, on_grads):
    dgate, dup = ff_bwd_act(dxb, gw["w_down"], sv["dfdg"], sv["dfdu"], dep)
    g_down = dw_down(sv["ff"], dxb)
    g_gate, g_up = dw_gate_up(sv["h2"], dgate, dup)
    token = on_grads(dict(w_gate=g_gate, w_up=g_up, w_down=g_down))
    dh2 = dh_ff(dgate, dup, gw["w_gate"], gw["w_up"], token)
    dx2, dx2b, d_norm2 = rms_bwd_res(sv["x2"], sm["norm2_g"], dh2, dx)
    return dx2, dx2b, dict(norm2_g=d_norm2)


def mixer_bwd(dx2, dx2b, sv, sm, gw, tabs, dep, on_grads):
    D = dx2.shape[1]
    DA = D // 2
    cos2, sin2 = tabs
    dmix = dmix_mm(dx2b, gw["w_out"].reshape(D, D), dep)
    g_out_a, g_out_g = dw_out(sv["mix_a"], sv["mix_g"], dx2b)
    dob, dl, d_mix_a = attn_out_bwd(sv["a"], sm["mix_norm_attn_g"], dmix)
    duv, d_mix_g, d_ln, d_ws, d_bs = gmlp_bwd(sv["proj"], sm["gmlp_ln_g"], sm["w_spatial"], sm["w_spatial_t"], sm["bcol"],
                                              sm["mix_norm_gmlp_g"], dmix, DA)
    dqs, dks, dvs = [], [], []
    for t, d in enumerate(DILATIONS):
        dqs.append(attn_bwd_dq(sv["qkv"][t], dob[t], sv["lse"], dl, d, DA))
        dk, dv = attn_bwd_dkv(sv["qkv"][t], dob[t], sv["lse"], dl, d, DA)
        dks.append(dk)
        dvs.append(dv)
    dproj = assemble_dproj(dqs, dks, dvs, duv, cos2, sin2)
    g_in = dw_in(sv["h1"], dproj)
    token = on_grads(dict(w_in=g_in, w_out=jnp.concatenate([g_out_a, g_out_g], axis=0)))
    dh1 = dh_in(dproj, gw["w_in"], token)
    dx0, dx0b, d_norm1 = rms_bwd_res(sv["x"], sm["norm1_g"], dh1, dx2)
    small = dict(norm1_g=d_norm1, gmlp_ln_g=d_ln, w_spatial=d_ws, b_spatial=d_bs[:, :, 0], mix_norm_attn_g=d_mix_a,
                 mix_norm_gmlp_g=d_mix_g)
    return dx0, dx0b, small


def _place():
    x, y, c = lax.axis_index("x"), lax.axis_index("y"), lax.axis_index("c")
    return x, y, c, [(1 - x, y), (x, 1 - y), (1 - x, 1 - y)]


def _remote(src, dst, send_sems, recv_sems, k, to):
    return pltpu.make_async_remote_copy(src_ref=src, dst_ref=dst, send_sem=send_sems.at[k], recv_sem=recv_sems.at[k],
                                        device_id=to, device_id_type=MESH)


def to_slot(src, layer, me_arr, dep, dtype, transpose=False):
    R, C = src.shape[-2:]
    tr = _lane_tile(R, 512) if transpose else _tile(R, max(16, 4 * ADAMW_BLOCK_ELEMS // C))

    def body(me_ref, src_ref, dep_ref, o_ref):
        v = src_ref[...]
        o_ref[...] = (v.T if transpose else v).astype(o_ref.dtype)

    if layer is None:
        src_spec = pl.BlockSpec((tr, C), lambda i, me_ref: (i, 0))
    else:
        src_spec = pl.BlockSpec((None, tr, C), lambda i, me_ref: (layer, i, 0))
    if transpose:
        out_spec = pl.BlockSpec((None, C, tr), lambda i, me_ref: (me_ref[0], 0, i))
    else:
        out_spec = pl.BlockSpec((None, tr, C), lambda i, me_ref: (me_ref[0], i, 0))
    return pl.pallas_call(
        body, name="to_slot",
        grid_spec=pltpu.PrefetchScalarGridSpec(
            num_scalar_prefetch=1, grid=(R // tr,),
            in_specs=[src_spec, pl.BlockSpec(TOKEN, lambda i, me_ref: (0, 0))], out_specs=out_spec),
        out_shape=_sds((N_DEV, C, R) if transpose else (N_DEV, R, C), dtype), compiler_params=_params(("parallel",)),
    )(me_arr, src, dep)


HBM_SPEC = pl.BlockSpec(memory_space=pltpu.HBM)
SEM_SPEC = pl.BlockSpec(memory_space=pltpu.SEMAPHORE)
DATAFLOW = pltpu.SideEffectType.DATAFLOW_SIDE_EFFECTING
TOKEN = (8, 128)


def _in_hbm(a):
    return pltpu.with_memory_space_constraint(a, pltpu.HBM)


PLAN_COPIES = dict(gather_ici=3, gather_d2d=N_CHIP, scatter_d2d=N_CHIP, scatter_ici=3)


def _plan(kind):
    x, y, c, chips = _place()
    sibling = (x, y, 1 - c)
    if kind == "gather_ici":
        me = 4 * x + 2 * y + c
        return [(me, me, 4 * px + 2 * py + c, (px, py, c)) for px, py in chips]
    if kind == "gather_d2d":
        return [(2 * j + c, 2 * j + c, 2 * j + 1 - c, sibling) for j in range(N_CHIP)]
    if kind == "scatter_d2d":
        return [(2 * j + 1 - c, j, j, sibling) for j in range(N_CHIP)]
    assert kind == "scatter_ici"
    return [(2 * px + py, 2 * x + y, 2 * px + py, (px, py, c)) for px, py in chips]


def split_start(kind, srcs, dsts):
    n = len(srcs)
    bufs = list(srcs) + ([] if dsts is None else list(dsts))
    nb = len(bufs)
    k = PLAN_COPIES[kind]

    def body(*refs):
        ins = refs[:n]
        outs = ins if dsts is None else refs[n:2 * n]
        send_sems, recv_sems = refs[nb], refs[nb + 1]
        token = refs[-1]
        for a in range(n):
            for j, (src, dst, _, peer) in enumerate(_plan(kind)):
                _remote(ins[a].at[src], outs[a].at[dst], send_sems, recv_sems, k * a + j, peer).start()
        token[...] = jnp.zeros_like(token)

    out = pl.pallas_call(
        body, name=kind + "_start",
        out_shape=(pltpu.SemaphoreType.DMA((k * n,)), pltpu.SemaphoreType.DMA((k * n,)),
                   *[pltpu.HBM(b.shape, b.dtype) for b in bufs], _sds(TOKEN, F32)),
        in_specs=[HBM_SPEC] * nb,
        out_specs=(SEM_SPEC, SEM_SPEC, *[HBM_SPEC] * nb, pl.BlockSpec(memory_space=pltpu.VMEM)),
        input_output_aliases={i: 2 + i for i in range(nb)},
        compiler_params=pltpu.CompilerParams(has_side_effects=DATAFLOW),
    )(*[_in_hbm(b) for b in bufs])
    return kind, out[0], out[1], list(out[2:2 + n]), None if dsts is None else list(out[2 + n:2 + nb]), out[-1]


def split_wait(pending, after):
    kind, send_sems, recv_sems, srcs, dsts, _ = pending
    n = len(srcs)
    bufs = list(srcs) + ([] if dsts is None else list(dsts))
    nb = len(bufs)
    k = PLAN_COPIES[kind]

    def body(*refs):
        ins = refs[:n]
        outs = ins if dsts is None else refs[n:2 * n]
        send_sems, recv_sems = refs[nb], refs[nb + 1]
        for a in range(n):
            for j, (src, _, landed, peer) in enumerate(_plan(kind)):
                cp = _remote(ins[a].at[src], outs[a].at[landed], send_sems, recv_sems, k * a + j, peer)
                cp.wait_send()
                cp.wait_recv()

    out = pl.pallas_call(
        body, name=kind + "_wait",
        out_shape=[pltpu.HBM(b.shape, b.dtype) for b in bufs],
        in_specs=[HBM_SPEC] * nb + [SEM_SPEC, SEM_SPEC, ANY], out_specs=[HBM_SPEC] * nb,
        input_output_aliases={i: i for i in range(nb)},
        compiler_params=pltpu.CompilerParams(has_side_effects=DATAFLOW),
    )(*bufs, send_sems, recv_sems, after)
    return list(out) if dsts is None else (list(out[:n]), list(out[n:]))


def gather_d2d(bufs):
    n = len(bufs)

    def body(*refs):
        outs = refs[n:2 * n]
        token = refs[2 * n]
        send_sems, recv_sems = refs[2 * n + 1:]
        x, y, c, _ = _place()
        sent = []
        for a in range(n):
            for j in range(N_CHIP):
                slot = outs[a].at[2 * j + c]
                cp = _remote(slot, slot, send_sems, recv_sems, N_CHIP * a + j, (x, y, 1 - c))
                cp.start()
                sent.append(cp)
        for a in range(n):
            for j in range(N_CHIP):
                slot = outs[a].at[2 * j + 1 - c]
                _remote(slot, slot, send_sems, recv_sems, N_CHIP * a + j, (x, y, 1 - c)).wait_recv()
        for cp in sent:
            cp.wait_send()
        token[...] = jnp.zeros_like(token)

    out = pl.pallas_call(
        body, name="gather_d2d", in_specs=[ANY] * n,
        out_specs=[ANY] * n + [pl.BlockSpec(memory_space=pltpu.VMEM)],
        out_shape=[_sds(b.shape, b.dtype) for b in bufs] + [_sds(TOKEN, F32)],
        input_output_aliases={a: a for a in range(n)},
        scratch_shapes=[pltpu.SemaphoreType.DMA((N_CHIP * n,)), pltpu.SemaphoreType.DMA((N_CHIP * n,))],
    )(*bufs)
    return list(out[:n]), out[n]


def pair_sum(part, got, c_arr, chip_arr):
    _, R, C = part.shape
    tr = _tile(R, 512)
    p4 = part.reshape(N_CHIP, 2, R, C)

    def body(c_ref, chip_ref, p_ref, g_ref, o_ref, own_ref):
        s = (p_ref[...].astype(F32) + g_ref[...].astype(F32)).astype(o_ref.dtype)
        o_ref[...] = s

        @pl.when(pl.program_id(1) == chip_ref[0])
        def _():
            own_ref[...] = s

    return pl.pallas_call(
        body, name="pair_sum",
        grid_spec=pltpu.PrefetchScalarGridSpec(
            num_scalar_prefetch=2, grid=(R // tr, N_CHIP),
            in_specs=[pl.BlockSpec((None, None, tr, C), lambda i, j, c_ref, chip_ref: (j, c_ref[0], i, 0)),
                      pl.BlockSpec((None, tr, C), lambda i, j, c_ref, chip_ref: (j, i, 0))],
            out_specs=[pl.BlockSpec((None, tr, C), lambda i, j, c_ref, chip_ref: (j, i, 0)),
                       pl.BlockSpec((None, tr, C), lambda i, j, c_ref, chip_ref: (chip_ref[0], i, 0))]),
        out_shape=[_sds((N_CHIP, R, C), part.dtype), _sds((N_CHIP, R, C), part.dtype)],
        compiler_params=_params(("parallel", "arbitrary")),
    )(c_arr, chip_arr, p4, got)


def adamw(w, m, v, parts_per_layer, transposed=False):
    NL, R, C = w.shape
    P = parts_per_layer[0].shape[0]
    want = max(16, ADAMW_BLOCK_ELEMS // C)
    tr = _lane_tile(R, max(want, 128)) if transposed else _tile(R, want)
    nb = R // tr

    def body(w_ref, m_ref, v_ref, *rest):
        part_refs = rest[:NL]
        g_ref, d_ref, nm_ref, nv_ref = rest[NL:]
        layer = pl.program_id(0)

        def update(parts_ref):
            g = parts_ref[0].astype(F32)
            for t in range(1, P):
                g = g + parts_ref[t].astype(F32)
            if transposed:
                g = g.T
            wv = w_ref[...]
            nm = ADAM_B1 * m_ref[...] + (1.0 - ADAM_B1) * g
            nv = ADAM_B2 * v_ref[...] + (1.0 - ADAM_B2) * (g * g)
            m_hat = nm / (1.0 - ADAM_B1 ** ADAM_STEP)
            v_hat = nv / (1.0 - ADAM_B2 ** ADAM_STEP)
            g_ref[...] = g
            d_ref[...] = -ADAM_LR * (m_hat / (jnp.sqrt(v_hat) + ADAM_EPS) + ADAM_WD * wv)
            nm_ref[...] = nm
            nv_ref[...] = nv

        for q in range(NL):
            pl.when(layer == q)(functools.partial(update, part_refs[q]))

    def part_spec(q):
        def block(l, i):
            return jnp.where(l == q, i, jnp.where(l < q, 0, nb - 1))

        if transposed:
            return pl.BlockSpec((P, C, tr), lambda l, i: (0, 0, block(l, i)))
        return pl.BlockSpec((P, tr, C), lambda l, i: (0, block(l, i), 0))

    blk = pl.BlockSpec((None, tr, C), lambda l, i: (l, i, 0))
    return pl.pallas_call(
        body, name="adamw", grid=(NL, nb),
        in_specs=[blk, blk, blk] + [part_spec(q) for q in range(NL)],
        out_specs=[blk, blk, blk, blk], out_shape=[_sds((NL, R, C), F32)] * 4,
        compiler_params=_params(("arbitrary", "arbitrary")),
    )(w, m, v, *parts_per_layer)


SMALL = ("norm1_g", "gmlp_ln_g", "w_spatial", "b_spatial", "mix_norm_attn_g", "mix_norm_gmlp_g", "norm2_g")
BIG = ("w_in", "w_out", "w_gate", "w_up", "w_down")
TRANSPOSED = ("w_gate", "w_up")
GATHERED_TRANSPOSED = ("w_in",)
GROUPS = (("w_in", "w_out"), ("w_gate", "w_up", "w_down"))
LANES = 128


def _pack(layers, final):
    flat = [layer[n].reshape(-1) for layer in layers for n in SMALL] + [final.reshape(-1)]
    return jnp.concatenate(flat).reshape(-1, LANES)


def _unpack(packed, like_layers, like_final):
    flat = packed.reshape(-1)
    out, off = [], 0
    for layer in like_layers:
        d = {}
        for n in SMALL:
            size = layer[n].size
            d[n] = flat[off:off + size].reshape(layer[n].shape)
            off += size
        out.append(d)
    return out, flat[off:off + like_final.size].reshape(like_final.shape)


def kernel(x, norm1_g, w_in, gmlp_ln_g, w_spatial, b_spatial, mix_norm_attn_g, mix_norm_gmlp_g, w_out, norm2_g, w_gate, w_up, w_down, final_g, loss_target, m_norm1_g, m_w_in, m_gmlp_ln_g, m_w_spatial, m_b_spatial, m_mix_norm_attn_g, m_mix_norm_gmlp_g, m_w_out, m_norm2_g, m_w_gate, m_w_up, m_w_down, m_final_g, v_norm1_g, v_w_in, v_gmlp_ln_g, v_w_spatial, v_b_spatial, v_mix_norm_attn_g, v_mix_norm_gmlp_g, v_w_out, v_norm2_g, v_w_gate, v_w_up, v_w_down, v_final_g):
    S, D = x.shape[1], x.shape[2]
    NL = norm1_g.shape[0]
    DA = D // 2
    xs = x.reshape(S, D)
    tabs = rope_tables(S)
    ax, ay, ac = lax.axis_index("x"), lax.axis_index("y"), lax.axis_index("c")
    me_arr = (4 * ax + 2 * ay + ac).astype(jnp.int32).reshape(1)
    c_arr = ac.astype(jnp.int32).reshape(1)
    chip_arr = (2 * ax + ay).astype(jnp.int32).reshape(1)
    small_w = dict(norm1_g=norm1_g, gmlp_ln_g=gmlp_ln_g, w_spatial=w_spatial, b_spatial=b_spatial,
                   mix_norm_attn_g=mix_norm_attn_g, mix_norm_gmlp_g=mix_norm_gmlp_g, norm2_g=norm2_g)
    small_m = dict(norm1_g=m_norm1_g, gmlp_ln_g=m_gmlp_ln_g, w_spatial=m_w_spatial, b_spatial=m_b_spatial,
                   mix_norm_attn_g=m_mix_norm_attn_g, mix_norm_gmlp_g=m_mix_norm_gmlp_g, norm2_g=m_norm2_g)
    small_v = dict(norm1_g=v_norm1_g, gmlp_ln_g=v_gmlp_ln_g, w_spatial=v_w_spatial, b_spatial=v_b_spatial,
                   mix_norm_attn_g=v_mix_norm_attn_g, mix_norm_gmlp_g=v_mix_norm_gmlp_g, norm2_g=v_norm2_g)
    def view(n, a):
        return jnp.swapaxes(a, 1, 2) if n in TRANSPOSED else a

    big_w = {n: view(n, a) for n, a in dict(w_in=w_in, w_out=w_out, w_gate=w_gate, w_up=w_up, w_down=w_down).items()}
    big_m = {n: view(n, a) for n, a in dict(w_in=m_w_in, w_out=m_w_out, w_gate=m_w_gate, w_up=m_w_up, w_down=m_w_down).items()}
    big_v = {n: view(n, a) for n, a in dict(w_in=v_w_in, w_out=v_w_out, w_gate=v_w_gate, w_up=v_w_up, w_down=v_w_down).items()}

    def layer_small(l):
        ws = w_spatial[l]
        return dict(norm1_g=norm1_g[l][None], gmlp_ln_g=gmlp_ln_g[l][None], mix_norm_attn_g=mix_norm_attn_g[l][None],
                    mix_norm_gmlp_g=mix_norm_gmlp_g[l][None], norm2_g=norm2_g[l][None], w_spatial=ws,
                    w_spatial_t=jnp.swapaxes(ws, 1, 2), bcol=jnp.broadcast_to(b_spatial[l][:, :, None], ws.shape))

    n_stages = 2 * NL
    zero = jnp.zeros(TOKEN, F32)
    TOK = 5

    def stage_bufs(s, dep):
        return [to_slot(big_w[n], s // 2, me_arr, dep, BF16, transpose=(n in GATHERED_TRANSPOSED)) for n in GROUPS[s % 2]]

    def stage_weights(s, bufs):
        gw = dict(zip(GROUPS[s % 2], bufs))
        if s % 2 == 0:
            gw["w_in"] = gw["w_in"].reshape(-1, D)
            gw["w_out"] = gw["w_out"].reshape(2, DA, D)
        else:
            gw = {n: b.reshape(-1, D) for n, b in gw.items()}
        return gw

    ici = {0: split_start("gather_ici", stage_bufs(0, zero), None)}
    ici[1] = split_start("gather_ici", stage_bufs(1, ici[0][TOK]), None)
    h = xs
    d2d = {0: split_start("gather_d2d", split_wait(ici[0], h), None)}
    ready = {0: split_wait(d2d[0], h)}
    saved, weights = [], []
    for s in range(n_stages):
        deps = [ici[1][TOK]] if s == 0 else []
        if 1 <= s < n_stages - 1:
            d2d[s + 1] = split_start("gather_d2d", split_wait(ici[s + 1], h), None)
            deps.append(d2d[s + 1][TOK])
        if s + 2 < n_stages:
            ici[s + 2] = split_start("gather_ici", stage_bufs(s + 2, deps[-1]), None)
            deps.append(ici[s + 2][TOK])
        dep = sum(deps[1:], deps[0]) if deps else zero
        gw = stage_weights(s, ready[s])
        sm = layer_small(s // 2)
        if s % 2 == 0:
            sm["norm1_g"] = sm["norm1_g"] + dep[0, 0]
            h, sv = mixer_fwd(h, sm, gw, tabs)
        else:
            sm["norm2_g"] = sm["norm2_g"] + dep[0, 0]
            h, sv = ffn_fwd(h, sm, gw)
        saved.append(sv)
        weights.append(gw)
        if s == 0:
            d2d[1] = split_start("gather_d2d", split_wait(ici[1], h), None)
        if s + 1 < n_stages:
            ready[s + 1] = split_wait(d2d[s + 1], h)
    loss_part, dx, dxb, d_final = loss_and_grad(h, final_g[None], loss_target.reshape(S, D))
    loss = lax.psum(loss_part[0, 0], ("x", "y", "c"))

    big_sums = [dict() for _ in range(NL)]
    small_grads = [dict() for _ in range(NL)]
    pending, pending_stage = None, None
    dep = zero
    for s in reversed(range(n_stages)):
        l = s // 2
        crossing = []

        def on_grads(big, s=s, crossing=crossing):
            parts = [big[n].reshape((N_DEV, -1, big[n].shape[-1])) for n in GROUPS[s % 2]]
            gots = [lax.empty((N_CHIP,) + p.shape[1:], p.dtype) for p in parts]
            crossing.append(split_start("scatter_d2d", parts, gots))
            return crossing[0][TOK]

        if s % 2 == 1:
            dx, dxb, small = ffn_bwd(dx, dxb, saved[s], layer_small(l), weights[s], dep, on_grads)
        else:
            dx, dxb, small = mixer_bwd(dx, dxb, saved[s], layer_small(l), weights[s], tabs, dep, on_grads)
        small_grads[l].update(small)
        if pending is not None:
            big_sums[pending_stage // 2].update(zip(GROUPS[pending_stage % 2], split_wait(pending, dx)[1]))
        parts, got = split_wait(crossing[0], dx)
        pairs = [pair_sum(p, g, c_arr, chip_arr) for p, g in zip(parts, got)]
        pending, pending_stage = split_start("scatter_ici", [t for t, _ in pairs], [o for _, o in pairs]), s
        dep = pending[TOK]

    def update_big(n):
        res = adamw(big_w[n], big_m[n], big_v[n], [big_sums[l][n] for l in range(NL)], n in GATHERED_TRANSPOSED)
        return [view(n, t) for t in res], res[0][0, :TOKEN[0], :TOKEN[1]]

    big_out = {}
    after_ffn = zero
    for n in GROUPS[1]:
        big_out[n], piece = update_big(n)
        after_ffn = after_ffn + piece

    packed = to_slot(_pack(small_grads, d_final), None, me_arr, dep + after_ffn, F32)
    small_pending = split_start("gather_ici", [packed], None)
    big_sums[pending_stage // 2].update(zip(GROUPS[pending_stage % 2], split_wait(pending, small_pending[TOK])[1]))
    after_mixer = zero
    for n in GROUPS[0]:
        big_out[n], piece = update_big(n)
        after_mixer = after_mixer + piece
    gathered_small = gather_d2d(split_wait(small_pending, after_mixer))[0][0]
    pw = _pack([{n: small_w[n][l] for n in SMALL} for l in range(NL)], final_g)[None]
    pm = _pack([{n: small_m[n][l] for n in SMALL} for l in range(NL)], m_final_g)[None]
    pv = _pack([{n: small_v[n][l] for n in SMALL} for l in range(NL)], v_final_g)[None]
    like_layers = [{n: small_w[n][l] for n in SMALL} for l in range(NL)]
    small_res = adamw(pw, pm, pv, [gathered_small])
    small_out = [_unpack(t[0], like_layers, final_g) for t in small_res]

    def small_stack(k, n):
        return jnp.stack([small_out[k][0][l][n] for l in range(NL)])

    order = ("norm1_g", "w_in", "gmlp_ln_g", "w_spatial", "b_spatial", "mix_norm_attn_g", "mix_norm_gmlp_g", "w_out",
             "norm2_g", "w_gate", "w_up", "w_down")
    outs = [loss, dx.reshape(x.shape)]
    for k in range(4):
        for n in order:
            outs.append(big_out[n][k] if n in BIG else small_stack(k, n))
        outs.append(small_out[k][1])
    return tuple(outs)
```

```python
import functools

import jax
import jax.numpy as jnp
from jax import lax
from jax.experimental import pallas as pl
from jax.experimental.pallas import tpu as pltpu

F32 = jnp.float32
BF16 = jnp.bfloat16

HEAD_DIM = 128
CHUNK = 128
GROUP = 128
N_SIDE = 64
DILATIONS = (1, 4, 16)
TQ = 128
HALO = 64
EPS = 1e-6
NEG = -1e30
ROPE_THETA = 10000.0
ADAM_LR = 0.001
ADAM_B1 = 0.9
ADAM_B2 = 0.999
ADAM_EPS = 1e-08
ADAM_WD = 0.01
ADAM_STEP = 10
N_DEV = 8
N_CHIP = 4
VMEM_LIMIT_V7X = 56 * 1024 * 1024
ADAMW_BLOCK_ELEMS = 128 * 1024
MESH = pl.DeviceIdType.MESH
ANY = pl.BlockSpec(memory_space=pl.ANY)


def _params(sem=None):
    return pltpu.CompilerParams(dimension_semantics=sem, vmem_limit_bytes=VMEM_LIMIT_V7X)


def _tile(n, want):
    for t in range(min(want, n), 15, -1):
        if n % t == 0 and t % 16 == 0:
            return t
    return n


def _sds(shape, dtype):
    return jax.ShapeDtypeStruct(shape, dtype)


def _dot(a, b, dims):
    return lax.dot_general(a, b, (dims, ((), ())), preferred_element_type=F32)


NN = ((1,), (0,))
NT = ((1,), (1,))
TN = ((0,), (0,))


def _sigmoid(x):
    return 1.0 / (1.0 + jnp.exp(-x))


def _gelu(x):
    return 0.5 * x * (1.0 + lax.erf(x * 0.7071067811865476))


def _gelu_grad(x):
    cdf = 0.5 * (1.0 + lax.erf(x * 0.7071067811865476))
    pdf = 0.3989422804014327 * jnp.exp(-0.5 * x * x)
    return cdf + x * pdf


def rms_fwd(x, g):
    S, D = x.shape
    tr = _tile(S, 512)

    def body(x_ref, g_ref, o_ref):
        xv = x_ref[...]
        r = lax.rsqrt(jnp.mean(xv * xv, axis=-1, keepdims=True) + EPS)
        o_ref[...] = (xv * r * g_ref[...]).astype(o_ref.dtype)

    return pl.pallas_call(
        body, name="rms_fwd", grid=(S // tr,),
        in_specs=[pl.BlockSpec((tr, D), lambda i: (i, 0)), pl.BlockSpec((1, D), lambda i: (0, 0))],
        out_specs=pl.BlockSpec((tr, D), lambda i: (i, 0)),
        out_shape=_sds((S, D), BF16), compiler_params=_params(("parallel",)),
    )(x, g)


def _rms_bwd_math(xv, gv, dy):
    r = lax.rsqrt(jnp.mean(xv * xv, axis=-1, keepdims=True) + EPS)
    dyg = dy * gv
    dx = r * dyg - xv * (r * r * r) * jnp.mean(dyg * xv, axis=-1, keepdims=True)
    dg = jnp.sum(dy * xv * r, axis=0, keepdims=True)
    return dx, dg


def rms_bwd_res(x, g, dy, dres):
    S, D = x.shape
    tr = _tile(S, 512)

    def body(x_ref, g_ref, dy_ref, dres_ref, dx_ref, dxb_ref, dg_ref):
        dx, dg = _rms_bwd_math(x_ref[...], g_ref[...], dy_ref[...])
        tot = dres_ref[...] + dx
        dx_ref[...] = tot
        dxb_ref[...] = tot.astype(dxb_ref.dtype)

        @pl.when(pl.program_id(0) == 0)
        def _():
            dg_ref[...] = dg

        @pl.when(pl.program_id(0) > 0)
        def _():
            dg_ref[...] += dg

    row = pl.BlockSpec((tr, D), lambda i: (i, 0))
    vec = pl.BlockSpec((1, D), lambda i: (0, 0))
    return pl.pallas_call(
        body, name="rms_bwd_res", grid=(S // tr,),
        in_specs=[row, vec, row, row], out_specs=[row, row, vec],
        out_shape=[_sds((S, D), F32), _sds((S, D), BF16), _sds((1, D), F32)],
        compiler_params=_params(("arbitrary",)),
    )(x, g, dy, dres)


def loss_and_grad(x, g, target):
    S, D = x.shape
    tr = _tile(S, 512)

    def body(x_ref, g_ref, t_ref, loss_ref, dx_ref, dxb_ref, dg_ref):
        xv = x_ref[...]
        gv = g_ref[...]
        r = lax.rsqrt(jnp.mean(xv * xv, axis=-1, keepdims=True) + EPS)
        y = xv * r * gv
        diff = y - t_ref[...]
        part = 0.5 * jnp.sum(jnp.mean(diff * diff, axis=-1, keepdims=True), axis=0, keepdims=True)
        dy = diff * (1.0 / D)
        dx, dg = _rms_bwd_math(xv, gv, dy)
        dx_ref[...] = dx
        dxb_ref[...] = dx.astype(dxb_ref.dtype)

        @pl.when(pl.program_id(0) == 0)
        def _():
            dg_ref[...] = dg
            loss_ref[...] = jnp.broadcast_to(part, loss_ref.shape)

        @pl.when(pl.program_id(0) > 0)
        def _():
            dg_ref[...] += dg
            loss_ref[...] += jnp.broadcast_to(part, loss_ref.shape)

    row = pl.BlockSpec((tr, D), lambda i: (i, 0))
    vec = pl.BlockSpec((1, D), lambda i: (0, 0))
    return pl.pallas_call(
        body, name="loss_and_grad", grid=(S // tr,),
        in_specs=[row, vec, row],
        out_specs=[pl.BlockSpec((8, 128), lambda i: (0, 0)), row, row, vec],
        out_shape=[_sds((8, 128), F32), _sds((S, D), F32), _sds((S, D), BF16), _sds((1, D), F32)],
        compiler_params=_params(("arbitrary",)),
    )(x, g, target)


def _mm_body(n_pairs, dims, red_axis, n_red, has_res):
    def body(*refs):
        ins = refs[:2 * n_pairs]
        res_ref = refs[2 * n_pairs] if has_res else None
        o_ref = refs[2 * n_pairs + has_res]
        p = None
        for t in range(n_pairs):
            d = _dot(ins[2 * t][...], ins[2 * t + 1][...], dims)
            p = d if p is None else p + d
        if red_axis is None:
            if has_res:
                p = res_ref[...] + p
            o_ref[...] = p.astype(o_ref.dtype)
            return
        acc_ref = o_ref if o_ref.dtype == F32 else refs[2 * n_pairs + has_res + 1]
        r = pl.program_id(red_axis)

        @pl.when(r == 0)
        def _():
            acc_ref[...] = res_ref[...] + p if has_res else p

        @pl.when(r > 0)
        def _():
            acc_ref[...] += p

        if acc_ref is not o_ref:
            @pl.when(r == n_red - 1)
            def _():
                o_ref[...] = acc_ref[...].astype(o_ref.dtype)

    return body


def mm_in(h, w):
    S, D = h.shape
    N = w.shape[0]
    tm = _tile(S, 1024)
    tn = _tile(N, 1024)
    return pl.pallas_call(
        _mm_body(1, NT, None, 1, False), name="mm_in", grid=(S // tm, N // tn),
        in_specs=[pl.BlockSpec((tm, D), lambda i, j: (i, 0)), pl.BlockSpec((tn, D), lambda i, j: (j, 0))],
        out_specs=pl.BlockSpec((tm, tn), lambda i, j: (i, j)),
        out_shape=_sds((S, N), F32), compiler_params=_params(("parallel", "parallel")),
    )(h, w)


def mm_out(x, mix_a, mix_g, w):
    S, D = x.shape
    DA = mix_a.shape[1]
    tm = _tile(S, 1024)
    tn = _tile(D, 1024)
    act = pl.BlockSpec((tm, DA), lambda i, j: (i, 0))
    return pl.pallas_call(
        _mm_body(2, NN, None, 1, True), name="mm_out", grid=(S // tm, D // tn),
        in_specs=[act, pl.BlockSpec((None, DA, tn), lambda i, j: (0, 0, j)),
                  act, pl.BlockSpec((None, DA, tn), lambda i, j: (1, 0, j)),
                  pl.BlockSpec((tm, tn), lambda i, j: (i, j))],
        out_specs=pl.BlockSpec((tm, tn), lambda i, j: (i, j)),
        out_shape=_sds((S, D), F32), compiler_params=_params(("parallel", "parallel")),
    )(mix_a, w, mix_g, w, x)


def _lane_tile(n, want):
    for t in range(min(want, n) // 128 * 128, 127, -128):
        if n % t == 0:
            return t
    return n


def ff_fwd(h, wg, wu):
    S, D = h.shape
    F = wg.shape[0]
    tm = _tile(S, 1024)
    tn = _lane_tile(F, 512)

    def body(h_ref, wg_ref, wu_ref, dfdu_ref, dfdg_ref, ff_ref):
        hv = h_ref[...]
        g = _dot(hv, wg_ref[...], NT)
        u = _dot(hv, wu_ref[...], NT)
        sg = _sigmoid(g)
        act = g * sg
        dfdu_ref[...] = act.astype(dfdu_ref.dtype)
        dfdg_ref[...] = (u * (sg * (1.0 + g * (1.0 - sg)))).astype(dfdg_ref.dtype)
        ff_ref[...] = (act * u).astype(ff_ref.dtype)

    wspec = pl.BlockSpec((tn, D), lambda i, j: (j, 0))
    ospec = pl.BlockSpec((tm, tn), lambda i, j: (i, j))
    return pl.pallas_call(
        body, name="ff_fwd", grid=(S // tm, F // tn),
        in_specs=[pl.BlockSpec((tm, D), lambda i, j: (i, 0)), wspec, wspec],
        out_specs=[ospec, ospec, ospec],
        out_shape=[_sds((S, F), BF16), _sds((S, F), BF16), _sds((S, F), BF16)],
        compiler_params=_params(("parallel", "parallel")),
    )(h, wg, wu)


def mm_down(x, ff, wd):
    S, D = x.shape
    F = ff.shape[1]
    tm = _tile(S, 1024)
    tn = _lane_tile(D, 512)
    blk = pl.BlockSpec((tm, tn), lambda i, j: (i, j))
    return pl.pallas_call(
        _mm_body(1, NN, None, 1, True), name="mm_down", grid=(S // tm, D // tn),
        in_specs=[pl.BlockSpec((tm, F), lambda i, j: (i, 0)), pl.BlockSpec((F, tn), lambda i, j: (0, j)), blk],
        out_specs=blk, out_shape=_sds((S, D), F32), compiler_params=_params(("parallel", "parallel")),
    )(ff, wd, x)


def ff_bwd_act(dxb, wd, dfdg, dfdu, dep):
    S, D = dxb.shape
    F = wd.shape[0]
    tm = _tile(S, 1024)
    tn = _lane_tile(F, 512)

    def body(dx_ref, wd_ref, dfdg_ref, dfdu_ref, dep_ref, dgate_ref, dup_ref):
        dff = _dot(dx_ref[...], wd_ref[...], NT)
        dup_ref[...] = (dff * dfdu_ref[...].astype(F32)).astype(dup_ref.dtype)
        dgate_ref[...] = (dff * dfdg_ref[...].astype(F32)).astype(dgate_ref.dtype)

    aspec = pl.BlockSpec((tm, tn), lambda i, j: (i, j))
    return pl.pallas_call(
        body, name="ff_bwd_act", grid=(S // tm, F // tn),
        in_specs=[pl.BlockSpec((tm, D), lambda i, j: (i, 0)), pl.BlockSpec((tn, D), lambda i, j: (j, 0)), aspec, aspec,
                  pl.BlockSpec(TOKEN, lambda i, j: (0, 0))],
        out_specs=[aspec, aspec],
        out_shape=[_sds((S, F), BF16), _sds((S, F), BF16)],
        compiler_params=_params(("parallel", "parallel")),
    )(dxb, wd, dfdg, dfdu, dep)


def dw_down(ff, dxb):
    S, F = ff.shape
    D = dxb.shape[1]
    tf = _lane_tile(F, 256)
    return pl.pallas_call(
        _mm_body(1, TN, None, 1, False), name="dw_down", grid=(F // tf,),
        in_specs=[pl.BlockSpec((S, tf), lambda k: (0, k)), pl.BlockSpec((S, D), lambda k: (0, 0))],
        out_specs=pl.BlockSpec((tf, D), lambda k: (k, 0)),
        out_shape=_sds((F, D), BF16), compiler_params=_params(("parallel",)),
    )(ff, dxb)


def dw_gate_up(h, dgate, dup):
    S, D = h.shape
    F = dgate.shape[1]
    f = _lane_tile(F, 256)

    def body(h_ref, dg_ref, du_ref, og_ref, ou_ref):
        hv = h_ref[...]
        og_ref[...] = _dot(dg_ref[...], hv, TN).astype(og_ref.dtype)
        ou_ref[...] = _dot(du_ref[...], hv, TN).astype(ou_ref.dtype)

    aspec = pl.BlockSpec((S, f), lambda k: (0, k))
    ospec = pl.BlockSpec((f, D), lambda k: (k, 0))
    return pl.pallas_call(
        body, name="dw_gate_up", grid=(F // f,),
        in_specs=[pl.BlockSpec((S, D), lambda k: (0, 0)), aspec, aspec],
        out_specs=[ospec, ospec],
        out_shape=[_sds((F, D), BF16), _sds((F, D), BF16)],
        compiler_params=_params(("parallel",)),
    )(h, dgate, dup)


def dh_ff(dgate, dup, wg, wu, dep):
    S, F = dgate.shape
    D = wg.shape[1]
    tm = _tile(S, 512)
    tn = _lane_tile(D, 256)

    def body(dg_ref, wg_ref, du_ref, wu_ref, dep_ref, o_ref):
        o_ref[...] = _dot(dg_ref[...], wg_ref[...], NN) + _dot(du_ref[...], wu_ref[...], NN)

    aspec = pl.BlockSpec((tm, F), lambda i, j: (i, 0))
    wspec = pl.BlockSpec((F, tn), lambda i, j: (0, j))
    return pl.pallas_call(
        body, name="dh_ff", grid=(S // tm, D // tn),
        in_specs=[aspec, wspec, aspec, wspec, pl.BlockSpec(TOKEN, lambda i, j: (0, 0))],
        out_specs=pl.BlockSpec((tm, tn), lambda i, j: (i, j)),
        out_shape=_sds((S, D), F32), compiler_params=_params(("parallel", "parallel")),
    )(dgate, wg, dup, wu, dep)


def dmix_mm(dxb, w, dep):
    S, D = dxb.shape
    tm = _tile(S, 1024)
    tn = _tile(D, 1024)

    def body(dx_ref, w_ref, dep_ref, o_ref):
        o_ref[...] = _dot(dx_ref[...], w_ref[...], NT)

    return pl.pallas_call(
        body, name="dmix_mm", grid=(S // tm, D // tn),
        in_specs=[pl.BlockSpec((tm, D), lambda i, j: (i, 0)), pl.BlockSpec((tn, D), lambda i, j: (j, 0)),
                  pl.BlockSpec(TOKEN, lambda i, j: (0, 0))],
        out_specs=pl.BlockSpec((tm, tn), lambda i, j: (i, j)),
        out_shape=_sds((S, D), F32), compiler_params=_params(("parallel", "parallel")),
    )(dxb, w, dep)


def dw_out(mix_a, mix_g, dxb):
    S, DA = mix_a.shape
    D = dxb.shape[1]
    tn = _lane_tile(D, 512)

    def half(name, m):
        return pl.pallas_call(
            _mm_body(1, TN, None, 1, False), name=name, grid=(D // tn,),
            in_specs=[pl.BlockSpec((S, DA), lambda j: (0, 0)), pl.BlockSpec((S, tn), lambda j: (0, j))],
            out_specs=pl.BlockSpec((DA, tn), lambda j: (0, j)),
            out_shape=_sds((DA, D), BF16), compiler_params=_params(("parallel",)),
        )(m, dxb)

    return half("dw_out_a", mix_a), half("dw_out_g", mix_g)


def dw_in(h, dproj):
    S, D = h.shape
    N = dproj.shape[1]
    tn = _lane_tile(N, 256)
    return pl.pallas_call(
        _mm_body(1, TN, None, 1, False), name="dw_in", grid=(N // tn,),
        in_specs=[pl.BlockSpec((S, tn), lambda k: (0, k)), pl.BlockSpec((S, D), lambda k: (0, 0))],
        out_specs=pl.BlockSpec((tn, D), lambda k: (k, 0)),
        out_shape=_sds((N, D), BF16), compiler_params=_params(("parallel",)),
    )(dproj, h)


def dh_in(dproj, w, dep):
    S, N = dproj.shape
    D = w.shape[1]
    tm = _tile(S, 1024)
    tn = _lane_tile(D, 512)

    def body(dp_ref, w_ref, dep_ref, o_ref):
        o_ref[...] = _dot(dp_ref[...], w_ref[...], NN)

    return pl.pallas_call(
        body, name="dh_in", grid=(S // tm, D // tn),
        in_specs=[pl.BlockSpec((tm, N), lambda i, j: (i, 0)), pl.BlockSpec((N, tn), lambda i, j: (0, j)),
                  pl.BlockSpec(TOKEN, lambda i, j: (0, 0))],
        out_specs=pl.BlockSpec((tm, tn), lambda i, j: (i, j)),
        out_shape=_sds((S, D), F32), compiler_params=_params(("parallel", "parallel")),
    )(dproj, w, dep)


def rope_tables(S):
    pos = jnp.arange(S, dtype=F32)
    inv = ROPE_THETA ** (-jnp.arange(0, HEAD_DIM, 2, dtype=F32) / HEAD_DIM)
    ang = pos[:, None] * inv[None, :]
    cos, sin = jnp.cos(ang), jnp.sin(ang)
    return jnp.concatenate([cos, cos], axis=-1), jnp.concatenate([-sin, sin], axis=-1)


def _rot_half(t):
    return pltpu.roll(t, HEAD_DIM // 2, 1)


PERM_ROWS = 256


def _by_residue(y, d, out_ref):
    tr = y.shape[0]
    n = tr // d
    o = lax.broadcasted_iota(jnp.int32, (tr, tr), 0)
    i = lax.broadcasted_iota(jnp.int32, (tr, tr), 1)
    src = jnp.bitwise_and(o, n - 1) * d + lax.shift_right_logical(o, n.bit_length() - 1)
    z = _dot((src == i).astype(y.dtype), y, NN).astype(out_ref.dtype)
    for r in range(d):
        out_ref[r] = z[r * n:(r + 1) * n]


def _from_residue(ref, d, terms):
    z = jnp.concatenate([ref[r] for r in range(d)], axis=0)
    tr = z.shape[0]
    n = tr // d
    i = lax.broadcasted_iota(jnp.int32, (tr, tr), 0)
    o = lax.broadcasted_iota(jnp.int32, (tr, tr), 1)
    src = jnp.bitwise_and(i, d - 1) * n + lax.shift_right_logical(i, d.bit_length() - 1)
    pick = (src == o).astype(BF16)
    out = None
    rest = z
    for _ in range(terms):
        piece = rest.astype(BF16)
        got = _dot(pick, piece, NN)
        out = got if out is None else out + got
        rest = rest - piece.astype(F32)
    return out


def _residue_blocks(widths):
    return [pl.BlockSpec((d, PERM_ROWS // d, W), lambda i: (0, i, 0)) for d, W in zip(DILATIONS[1:], widths)]


def _residue_outputs(S, W, dtype):
    shapes = [_sds((d, S // d, W), dtype) for d in DILATIONS[1:]]
    specs = [pl.BlockSpec((d, PERM_ROWS // d, W), lambda i: (0, i, 0)) for d in DILATIONS[1:]]
    return shapes, specs


def rope_qkv(proj, cos2, sin2, DA):
    S = proj.shape[0]
    tr = PERM_ROWS
    nh = DA // HEAD_DIM

    def body(q_ref, k_ref, v_ref, cos_ref, sin_ref, o_ref, *by_res):
        c = cos_ref[...]
        s = sin_ref[...]
        for h in range(nh):
            sl = slice(h * HEAD_DIM, (h + 1) * HEAD_DIM)
            for j, ref in enumerate((q_ref, k_ref)):
                t = ref[:, sl]
                o_ref[:, j * DA + h * HEAD_DIM:j * DA + (h + 1) * HEAD_DIM] = (t * c + _rot_half(t) * s).astype(o_ref.dtype)
        o_ref[:, 2 * DA:3 * DA] = v_ref[...].astype(o_ref.dtype)
        y = o_ref[...]
        for d, ref in zip(DILATIONS[1:], by_res):
            _by_residue(y, d, ref)

    tab = pl.BlockSpec((tr, HEAD_DIM), lambda i: (i, 0))
    shapes, specs = _residue_outputs(S, 3 * DA, BF16)
    out = pl.pallas_call(
        body, name="rope_qkv", grid=(S // tr,),
        in_specs=[pl.BlockSpec((tr, DA), lambda i: (i, 0)), pl.BlockSpec((tr, DA), lambda i: (i, 1)),
                  pl.BlockSpec((tr, DA), lambda i: (i, 2)), tab, tab],
        out_specs=[pl.BlockSpec((tr, 3 * DA), lambda i: (i, 0))] + specs,
        out_shape=[_sds((S, 3 * DA), BF16)] + shapes, compiler_params=_params(("parallel",)),
    )(proj, proj, proj, cos2, sin2)
    return [out[0].reshape(1, S, 3 * DA)] + list(out[1:])


def assemble_dproj(dqs, dks, dvs, duv, cos2, sin2):
    S, DA = dqs[0].shape[1:]
    tr = PERM_ROWS
    nh = DA // HEAD_DIM

    def total(trio):
        t = trio[0][...]
        for d, ref in zip(DILATIONS[1:], trio[1:]):
            t = t + _from_residue(ref, d, 2)
        return t

    def body(*refs):
        dq_refs, dk_refs, dv_refs = refs[0:3], refs[3:6], refs[6:9]
        duv_ref, cos_ref, sin_ref, o_ref = refs[9:13]
        c = cos_ref[...]
        s = sin_ref[...]
        for j, trio in enumerate((dq_refs, dk_refs)):
            t_all = total(trio)
            for h in range(nh):
                t = t_all[:, h * HEAD_DIM:(h + 1) * HEAD_DIM]
                o_ref[:, j * DA + h * HEAD_DIM:j * DA + (h + 1) * HEAD_DIM] = (t * c - _rot_half(t) * s).astype(o_ref.dtype)
        o_ref[:, 2 * DA:3 * DA] = total(dv_refs).astype(o_ref.dtype)
        o_ref[:, 3 * DA:5 * DA] = duv_ref[...]

    trio_specs = [pl.BlockSpec((None, tr, DA), lambda i: (0, i, 0))] + _residue_blocks([DA, DA])
    tab = pl.BlockSpec((tr, HEAD_DIM), lambda i: (i, 0))
    return pl.pallas_call(
        body, name="assemble_dproj", grid=(S // tr,),
        in_specs=trio_specs * 3 + [pl.BlockSpec((tr, 2 * DA), lambda i: (i, 0)), tab, tab],
        out_specs=pl.BlockSpec((tr, 5 * DA), lambda i: (i, 0)),
        out_shape=_sds((S, 5 * DA), BF16), compiler_params=_params(("parallel",)),
    )(*dqs, *dks, *dvs, duv, cos2, sin2)


STAT_W = 128


def _lanes_per_head(DA):
    return STAT_W // (DA // HEAD_DIM)


def _halo_specs(L, width_blocks, col):
    per = TQ // HALO
    last = L // HALO - 1
    W = width_blocks
    return [
        pl.BlockSpec((HALO, W), lambda r, i: (jnp.maximum(i * per - 1, 0), col(r))),
        pl.BlockSpec((TQ, W), lambda r, i: (i, col(r))),
        pl.BlockSpec((HALO, W), lambda r, i: (jnp.minimum((i + 1) * per, last), col(r))),
    ]


def _halo_specs3(L, W, c):
    per = TQ // HALO
    last = L // HALO - 1
    return [
        pl.BlockSpec((None, HALO, W), lambda r, i: (r, jnp.maximum(i * per - 1, 0), c)),
        pl.BlockSpec((None, TQ, W), lambda r, i: (r, i, c)),
        pl.BlockSpec((None, HALO, W), lambda r, i: (r, jnp.minimum((i + 1) * per, last), c)),
    ]


def _cat3(refs, sl):
    return jnp.concatenate([refs[0][:, sl], refs[1][:, sl], refs[2][:, sl]], axis=0)


def attn_fwd(qkv, d, DA):
    L = qkv.shape[1]
    S = L * d
    nh = DA // HEAD_DIM
    lph = _lanes_per_head(DA)
    scale = HEAD_DIM ** -0.5
    TK = TQ + 2 * HALO

    def body(q_ref, kp_ref, kc_ref, kn_ref, vp_ref, vc_ref, vn_ref, o_ref, lse_ref):
        i = pl.program_id(1)
        row = lax.broadcasted_iota(jnp.int32, (TQ, TK), 0)
        col = lax.broadcasted_iota(jnp.int32, (TQ, TK), 1)
        kpos = i * TQ - HALO + col
        mask = (jnp.abs(col - HALO - row) <= N_SIDE) & (kpos >= 0) & (kpos < L)
        for h in range(nh):
            sl = slice(h * HEAD_DIM, (h + 1) * HEAD_DIM)
            k = _cat3((kp_ref, kc_ref, kn_ref), sl)
            v = _cat3((vp_ref, vc_ref, vn_ref), sl)
            s = _dot(q_ref[:, sl], k, NT) * scale
            s = jnp.where(mask, s, NEG)
            m = jnp.max(s, axis=-1, keepdims=True)
            p = jnp.exp(s - m)
            l = jnp.sum(p, axis=-1, keepdims=True)
            o = _dot(p.astype(v.dtype), v, NN) / l
            o_ref[:, sl] = o
            lse_ref[:, h * lph:(h + 1) * lph] = jnp.broadcast_to(m + jnp.log(l), (TQ, lph))

    out_spec = pl.BlockSpec((None, TQ, DA), lambda r, i: (r, i, 0))
    o, lse = pl.pallas_call(
        body, name="attn_fwd_d%d" % d, grid=(d, L // TQ),
        in_specs=[pl.BlockSpec((None, TQ, DA), lambda r, i: (r, i, 0))] + _halo_specs3(L, DA, 1) + _halo_specs3(L, DA, 2),
        out_specs=[out_spec, pl.BlockSpec((TQ, STAT_W), lambda r, i: (i, r))],
        out_shape=[_sds((d, L, DA), F32), _sds((L, d * STAT_W), F32)],
        compiler_params=_params(("parallel", "parallel")),
    )(*([qkv] * 7))
    return o, lse.reshape(S, STAT_W)


def attn_merge(os_, lses, g):
    S, DA = os_[0].shape[1:]
    tr = PERM_ROWS
    nh = DA // HEAD_DIM
    lph = _lanes_per_head(DA)

    def body(o0, o1, o2, l0, l1, l2, g_ref, a_ref, lse_ref, mix_ref):
        a0, a1, a2 = l0[...], l1[...], l2[...]
        m = jnp.maximum(jnp.maximum(a0, a1), a2)
        e0, e1, e2 = jnp.exp(a0 - m), jnp.exp(a1 - m), jnp.exp(a2 - m)
        den = e0 + e1 + e2
        w0, w1, w2 = e0 / den, e1 / den, e2 / den
        lse_ref[...] = m + jnp.log(den)
        v0 = o0[...]
        v1 = _from_residue(o1, DILATIONS[1], 2)
        v2 = _from_residue(o2, DILATIONS[2], 2)
        heads = []
        for h in range(nh):
            sl = slice(h * HEAD_DIM, (h + 1) * HEAD_DIM)
            c = slice(h * lph, h * lph + 1)
            heads.append(w0[:, c] * v0[:, sl] + w1[:, c] * v1[:, sl] + w2[:, c] * v2[:, sl])
        a = jnp.concatenate(heads, axis=1)
        a_ref[...] = a
        r = lax.rsqrt(jnp.mean(a * a, axis=-1, keepdims=True) + EPS)
        mix_ref[...] = (a * r * g_ref[...]).astype(mix_ref.dtype)

    blk = pl.BlockSpec((tr, DA), lambda i: (i, 0))
    stat = pl.BlockSpec((tr, STAT_W), lambda i: (i, 0))
    return pl.pallas_call(
        body, name="attn_merge", grid=(S // tr,),
        in_specs=[pl.BlockSpec((None, tr, DA), lambda i: (0, i, 0))] + _residue_blocks([DA, DA]) + [stat] * 3
        + [pl.BlockSpec((1, DA), lambda i: (0, 0))],
        out_specs=[blk, stat, blk],
        out_shape=[_sds((S, DA), F32), _sds((S, STAT_W), F32), _sds((S, DA), BF16)],
        compiler_params=_params(("parallel",)),
    )(*os_, *lses, g)


def attn_out_bwd(a, g, dmix):
    S, DA = a.shape
    tr = PERM_ROWS
    nh = DA // HEAD_DIM
    lph = _lanes_per_head(DA)

    def body(a_ref, g_ref, dy_ref, do_ref, dl_ref, dg_ref, *by_res):
        av = a_ref[...]
        dx, dg = _rms_bwd_math(av, g_ref[...], dy_ref[...])
        dob = dx.astype(do_ref.dtype)
        do_ref[...] = dob
        for d, ref in zip(DILATIONS[1:], by_res):
            _by_residue(dob, d, ref)
        prod = dx * av
        for h in range(nh):
            sl = slice(h * HEAD_DIM, (h + 1) * HEAD_DIM)
            dl_ref[:, h * lph:(h + 1) * lph] = jnp.broadcast_to(jnp.sum(prod[:, sl], axis=-1, keepdims=True), (tr, lph))

        @pl.when(pl.program_id(0) == 0)
        def _():
            dg_ref[...] = dg

        @pl.when(pl.program_id(0) > 0)
        def _():
            dg_ref[...] += dg

    blk = pl.BlockSpec((tr, DA), lambda i: (i, 0))
    vec = pl.BlockSpec((1, DA), lambda i: (0, 0))
    shapes, specs = _residue_outputs(S, DA, BF16)
    out = pl.pallas_call(
        body, name="attn_out_bwd", grid=(S // tr,),
        in_specs=[blk, vec, blk], out_specs=[blk, pl.BlockSpec((tr, STAT_W), lambda i: (i, 0)), vec] + specs,
        out_shape=[_sds((S, DA), BF16), _sds((S, STAT_W), F32), _sds((1, DA), F32)] + shapes,
        compiler_params=_params(("arbitrary",)),
    )(a, g, dmix)
    return [out[0].reshape(1, S, DA)] + list(out[3:]), out[1], out[2]


def attn_bwd_dq(qkv, dob, lse, dl, d, DA):
    L = qkv.shape[1]
    S = L * d
    nh = DA // HEAD_DIM
    lph = _lanes_per_head(DA)
    scale = HEAD_DIM ** -0.5
    TK = TQ + 2 * HALO

    def body(q_ref, kp_ref, kc_ref, kn_ref, vp_ref, vc_ref, vn_ref, do_ref, lse_ref, dl_ref, dq_ref):
        i = pl.program_id(1)
        row = lax.broadcasted_iota(jnp.int32, (TQ, TK), 0)
        col = lax.broadcasted_iota(jnp.int32, (TQ, TK), 1)
        kpos = i * TQ - HALO + col
        mask = (jnp.abs(col - HALO - row) <= N_SIDE) & (kpos >= 0) & (kpos < L)
        for h in range(nh):
            sl = slice(h * HEAD_DIM, (h + 1) * HEAD_DIM)
            k = _cat3((kp_ref, kc_ref, kn_ref), sl)
            v = _cat3((vp_ref, vc_ref, vn_ref), sl)
            s = _dot(q_ref[:, sl], k, NT) * scale
            s = jnp.where(mask, s, NEG)
            p = jnp.exp(s - lse_ref[:, h * lph:h * lph + 1])
            dp = _dot(do_ref[:, sl], v, NT)
            ds = p * (dp - dl_ref[:, h * lph:h * lph + 1])
            dq_ref[:, sl] = _dot(ds.astype(k.dtype), k, NN) * scale

    blk = pl.BlockSpec((None, TQ, DA), lambda r, i: (r, i, 0))
    stat = pl.BlockSpec((TQ, STAT_W), lambda r, i: (i, r))
    return pl.pallas_call(
        body, name="attn_bwd_dq_d%d" % d, grid=(d, L // TQ),
        in_specs=[blk] + _halo_specs3(L, DA, 1) + _halo_specs3(L, DA, 2) + [blk, stat, stat],
        out_specs=blk, out_shape=_sds((d, L, DA), F32),
        compiler_params=_params(("parallel", "parallel")),
    )(*([qkv] * 7), dob, lse.reshape(L, d * STAT_W), dl.reshape(L, d * STAT_W))


def attn_bwd_dkv(qkv, dob, lse, dl, d, DA):
    L = qkv.shape[1]
    S = L * d
    nh = DA // HEAD_DIM
    lph = _lanes_per_head(DA)
    scale = HEAD_DIM ** -0.5
    TR = TQ + 2 * HALO

    def body(k_ref, v_ref, qp, qc, qn, dop, doc, don, lp, lc, ln, dp_, dc_, dn_, dk_ref, dv_ref):
        j = pl.program_id(1)
        row = lax.broadcasted_iota(jnp.int32, (TR, TQ), 0)
        col = lax.broadcasted_iota(jnp.int32, (TR, TQ), 1)
        qpos = j * TQ - HALO + row
        mask = (jnp.abs(col - (row - HALO)) <= N_SIDE) & (qpos >= 0) & (qpos < L)
        for h in range(nh):
            sl = slice(h * HEAD_DIM, (h + 1) * HEAD_DIM)
            q = _cat3((qp, qc, qn), sl)
            do = _cat3((dop, doc, don), sl)
            stat = slice(h * lph, h * lph + 1)
            lse_q = _cat3((lp, lc, ln), stat)
            dl_q = _cat3((dp_, dc_, dn_), stat)
            k = k_ref[:, sl]
            v = v_ref[:, sl]
            s = _dot(q, k, NT) * scale
            s = jnp.where(mask, s, NEG)
            p = jnp.exp(s - lse_q)
            dv_ref[:, sl] = _dot(p.astype(do.dtype), do, TN)
            dp = _dot(do, v, NT)
            ds = p * (dp - dl_q)
            dk_ref[:, sl] = _dot(ds.astype(q.dtype), q, TN) * scale

    blk = pl.BlockSpec((None, TQ, DA), lambda r, i: (r, i, 0))
    ident = lambda r: r
    dk, dv = pl.pallas_call(
        body, name="attn_bwd_dkv_d%d" % d, grid=(d, L // TQ),
        in_specs=[pl.BlockSpec((None, TQ, DA), lambda r, i: (r, i, 1)), pl.BlockSpec((None, TQ, DA), lambda r, i: (r, i, 2))]
        + _halo_specs3(L, DA, 0) + _halo_specs3(L, DA, 0) + _halo_specs(L, STAT_W, ident) + _halo_specs(L, STAT_W, ident),
        out_specs=[blk, blk], out_shape=[_sds((d, L, DA), F32), _sds((d, L, DA), F32)],
        compiler_params=_params(("parallel", "parallel")),
    )(*([qkv] * 5), *([dob] * 3), *([lse.reshape(L, d * STAT_W)] * 3), *([dl.reshape(L, d * STAT_W)] * 3))
    return dk, dv


def _gmlp_fwd_math(u_raw, v_raw, ln_g, ws_ref, bcol_ref, n_chunks, ng):
    ug = _gelu(u_raw)
    vg = _gelu(v_raw)
    mu = jnp.mean(vg, axis=-1, keepdims=True)
    xc = vg - mu
    rstd = lax.rsqrt(jnp.mean(xc * xc, axis=-1, keepdims=True) + EPS)
    xhat = xc * rstd
    vn = xhat * ln_g
    rows = []
    for n in range(n_chunks):
        cols = []
        for g in range(ng):
            blk = vn[n * CHUNK:(n + 1) * CHUNK, g * GROUP:(g + 1) * GROUP]
            cols.append(_dot(ws_ref[g].astype(BF16), blk.astype(BF16), NN) + bcol_ref[g])
        rows.append(jnp.concatenate(cols, axis=1))
    mixed = jnp.concatenate(rows, axis=0)
    return ug, xhat, rstd, vn, mixed


def gmlp_fwd(proj, ln_g, w_s, bcol, g_out, DA):
    S = proj.shape[0]
    ng = DA // GROUP
    tr = _tile(S, 2 * CHUNK)

    def body(u_ref, v_ref, ln_ref, ws_ref, bcol_ref, g_ref, G_ref, mix_ref):
        ug, _, _, _, mixed = _gmlp_fwd_math(u_ref[...], v_ref[...], ln_ref[...], ws_ref, bcol_ref, tr // CHUNK, ng)
        G = ug * mixed
        G_ref[...] = G
        r = lax.rsqrt(jnp.mean(G * G, axis=-1, keepdims=True) + EPS)
        mix_ref[...] = (G * r * g_ref[...]).astype(mix_ref.dtype)

    vec = pl.BlockSpec((1, DA), lambda i: (0, 0))
    par = pl.BlockSpec((ng, CHUNK, CHUNK), lambda i: (0, 0, 0))
    blk = pl.BlockSpec((tr, DA), lambda i: (i, 0))
    return pl.pallas_call(
        body, name="gmlp_fwd", grid=(S // tr,),
        in_specs=[pl.BlockSpec((tr, DA), lambda i: (i, 3)), pl.BlockSpec((tr, DA), lambda i: (i, 4)), vec, par, par, vec],
        out_specs=[blk, blk], out_shape=[_sds((S, DA), F32), _sds((S, DA), BF16)],
        compiler_params=_params(("parallel",)),
    )(proj, proj, ln_g, w_s, bcol, g_out)


def gmlp_bwd(proj, ln_g, w_s, w_st, bcol, g_out, dmix, DA):
    S = proj.shape[0]
    ng = DA // GROUP
    tr = _tile(S, 2 * CHUNK)
    nc = tr // CHUNK

    def body(u_ref, v_ref, ln_ref, ws_ref, wst_ref, bcol_ref, g_ref, dy_ref, duv_ref, dg_ref, dln_ref, dws_ref, db_ref):
        u_raw = u_ref[...]
        v_raw = v_ref[...]
        ln_g_v = ln_ref[...]
        ug, xhat, rstd, vn, mixed = _gmlp_fwd_math(u_raw, v_raw, ln_g_v, ws_ref, bcol_ref, nc, ng)
        G = ug * mixed
        dG, dg = _rms_bwd_math(G, g_ref[...], dy_ref[...])
        du_g = dG * mixed
        dmixed = dG * ug
        dws, dbs, rows = [], [], []
        for g in range(ng):
            dws.append(None)
            dbs.append(None)
        for n in range(nc):
            cols = []
            for g in range(ng):
                dm = dmixed[n * CHUNK:(n + 1) * CHUNK, g * GROUP:(g + 1) * GROUP]
                vb = vn[n * CHUNK:(n + 1) * CHUNK, g * GROUP:(g + 1) * GROUP]
                dmb = dm.astype(BF16)
                w = _dot(dmb, vb.astype(BF16), NT)
                b = jnp.broadcast_to(jnp.sum(dm, axis=-1, keepdims=True), (CHUNK, GROUP))
                dws[g] = w if dws[g] is None else dws[g] + w
                dbs[g] = b if dbs[g] is None else dbs[g] + b
                cols.append(_dot(wst_ref[g].astype(BF16), dmb, NN))
            rows.append(jnp.concatenate(cols, axis=1))
        dvn = jnp.concatenate(rows, axis=0)
        dln = jnp.sum(dvn * xhat, axis=0, keepdims=True)
        dxh = dvn * ln_g_v
        dvg = rstd * (dxh - jnp.mean(dxh, axis=-1, keepdims=True) - xhat * jnp.mean(dxh * xhat, axis=-1, keepdims=True))
        duv_ref[:, 0:DA] = (du_g * _gelu_grad(u_raw)).astype(duv_ref.dtype)
        duv_ref[:, DA:2 * DA] = (dvg * _gelu_grad(v_raw)).astype(duv_ref.dtype)

        first = pl.program_id(0) == 0

        @pl.when(first)
        def _():
            dg_ref[...] = dg
            dln_ref[...] = dln
            for g in range(ng):
                dws_ref[g] = dws[g]
                db_ref[g] = dbs[g]

        @pl.when(jnp.logical_not(first))
        def _():
            dg_ref[...] += dg
            dln_ref[...] += dln
            for g in range(ng):
                dws_ref[g] += dws[g]
                db_ref[g] += dbs[g]

    vec = pl.BlockSpec((1, DA), lambda i: (0, 0))
    par = pl.BlockSpec((ng, CHUNK, CHUNK), lambda i: (0, 0, 0))
    return pl.pallas_call(
        body, name="gmlp_bwd", grid=(S // tr,),
        in_specs=[pl.BlockSpec((tr, DA), lambda i: (i, 3)), pl.BlockSpec((tr, DA), lambda i: (i, 4)), vec, par, par, par, vec,
                  pl.BlockSpec((tr, DA), lambda i: (i, 1))],
        out_specs=[pl.BlockSpec((tr, 2 * DA), lambda i: (i, 0)), vec, vec, par, par],
        out_shape=[_sds((S, 2 * DA), BF16), _sds((1, DA), F32), _sds((1, DA), F32),
                   _sds((ng, CHUNK, CHUNK), F32), _sds((ng, CHUNK, CHUNK), F32)],
        compiler_params=_params(("arbitrary",)),
    )(proj, proj, ln_g, w_s, w_st, bcol, g_out, dmix)


def mixer_fwd(x, sm, gw, tabs):
    D = x.shape[1]
    DA = D // 2
    cos2, sin2 = tabs
    h1 = rms_fwd(x, sm["norm1_g"])
    proj = mm_in(h1, gw["w_in"])
    qkv = rope_qkv(proj, cos2, sin2, DA)
    os_, lses = [], []
    for t, d in enumerate(DILATIONS):
        o, l = attn_fwd(qkv[t], d, DA)
        os_.append(o)
        lses.append(l)
    a, lse, mix_a = attn_merge(os_, lses, sm["mix_norm_attn_g"])
    _, mix_g = gmlp_fwd(proj, sm["gmlp_ln_g"], sm["w_spatial"], sm["bcol"], sm["mix_norm_gmlp_g"], DA)
    x2 = mm_out(x, mix_a, mix_g, gw["w_out"])
    return x2, dict(x=x, h1=h1, proj=proj, qkv=qkv, a=a, lse=lse, mix_a=mix_a, mix_g=mix_g)


def ffn_fwd(x2, sm, gw):
    h2 = rms_fwd(x2, sm["norm2_g"])
    dfdu, dfdg, ff = ff_fwd(h2, gw["w_gate"], gw["w_up"])
    x3 = mm_down(x2, ff, gw["w_down"])
    return x3, dict(x2=x2, h2=h2, dfdg=dfdg, dfdu=dfdu, ff=ff)


def ffn_bwd(dx, dxb, sv, sm, gw, dep, on_grads):
    dgate, dup = ff_bwd_act(dxb, gw["w_down"], sv["dfdg"], sv["dfdu"], dep)
    g_down = dw_down(sv["ff"], dxb)
    g_gate, g_up = dw_gate_up(sv["h2"], dgate, dup)
    token = on_grads(dict(w_gate=g_gate, w_up=g_up, w_down=g_down))
    dh2 = dh_ff(dgate, dup, gw["w_gate"], gw["w_up"], token)
    dx2, dx2b, d_norm2 = rms_bwd_res(sv["x2"], sm["norm2_g"], dh2, dx)
    return dx2, dx2b, dict(norm2_g=d_norm2)


def mixer_bwd(dx2, dx2b, sv, sm, gw, tabs, dep, on_grads):
    D = dx2.shape[1]
    DA = D // 2
    cos2, sin2 = tabs
    dmix = dmix_mm(dx2b, gw["w_out"].reshape(D, D), dep)
    g_out_a, g_out_g = dw_out(sv["mix_a"], sv["mix_g"], dx2b)
    dob, dl, d_mix_a = attn_out_bwd(sv["a"], sm["mix_norm_attn_g"], dmix)
    duv, d_mix_g, d_ln, d_ws, d_bs = gmlp_bwd(sv["proj"], sm["gmlp_ln_g"], sm["w_spatial"], sm["w_spatial_t"], sm["bcol"],
                                              sm["mix_norm_gmlp_g"], dmix, DA)
    dqs, dks, dvs = [], [], []
    for t, d in enumerate(DILATIONS):
        dqs.append(attn_bwd_dq(sv["qkv"][t], dob[t], sv["lse"], dl, d, DA))
        dk, dv = attn_bwd_dkv(sv["qkv"][t], dob[t], sv["lse"], dl, d, DA)
        dks.append(dk)
        dvs.append(dv)
    dproj = assemble_dproj(dqs, dks, dvs, duv, cos2, sin2)
    g_in = dw_in(sv["h1"], dproj)
    token = on_grads(dict(w_in=g_in, w_out=jnp.concatenate([g_out_a, g_out_g], axis=0)))
    dh1 = dh_in(dproj, gw["w_in"], token)
    dx0, dx0b, d_norm1 = rms_bwd_res(sv["x"], sm["norm1_g"], dh1, dx2)
    small = dict(norm1_g=d_norm1, gmlp_ln_g=d_ln, w_spatial=d_ws, b_spatial=d_bs[:, :, 0], mix_norm_attn_g=d_mix_a,
                 mix_norm_gmlp_g=d_mix_g)
    return dx0, dx0b, small


def _place():
    x, y, c = lax.axis_index("x"), lax.axis_index("y"), lax.axis_index("c")
    return x, y, c, [(1 - x, y), (x, 1 - y), (1 - x, 1 - y)]


def _remote(src, dst, send_sems, recv_sems, k, to):
    return pltpu.make_async_remote_copy(src_ref=src, dst_ref=dst, send_sem=send_sems.at[k], recv_sem=recv_sems.at[k],
                                        device_id=to, device_id_type=MESH)


def to_slot(src, layer, me_arr, dep, dtype, transpose=False):
    R, C = src.shape[-2:]
    tr = _lane_tile(R, 512) if transpose else _tile(R, max(16, 4 * ADAMW_BLOCK_ELEMS // C))

    def body(me_ref, src_ref, dep_ref, o_ref):
        v = src_ref[...]
        o_ref[...] = (v.T if transpose else v).astype(o_ref.dtype)

    if layer is None:
        src_spec = pl.BlockSpec((tr, C), lambda i, me_ref: (i, 0))
    else:
        src_spec = pl.BlockSpec((None, tr, C), lambda i, me_ref: (layer, i, 0))
    if transpose:
        out_spec = pl.BlockSpec((None, C, tr), lambda i, me_ref: (me_ref[0], 0, i))
    else:
        out_spec = pl.BlockSpec((None, tr, C), lambda i, me_ref: (me_ref[0], i, 0))
    return pl.pallas_call(
        body, name="to_slot",
        grid_spec=pltpu.PrefetchScalarGridSpec(
            num_scalar_prefetch=1, grid=(R // tr,),
            in_specs=[src_spec, pl.BlockSpec(TOKEN, lambda i, me_ref: (0, 0))], out_specs=out_spec),
        out_shape=_sds((N_DEV, C, R) if transpose else (N_DEV, R, C), dtype), compiler_params=_params(("parallel",)),
    )(me_arr, src, dep)


HBM_SPEC = pl.BlockSpec(memory_space=pltpu.HBM)
SEM_SPEC = pl.BlockSpec(memory_space=pltpu.SEMAPHORE)
DATAFLOW = pltpu.SideEffectType.DATAFLOW_SIDE_EFFECTING
TOKEN = (8, 128)


def _in_hbm(a):
    return pltpu.with_memory_space_constraint(a, pltpu.HBM)


PLAN_COPIES = dict(gather_ici=3, gather_d2d=N_CHIP, scatter_d2d=N_CHIP, scatter_ici=3)


def _plan(kind):
    x, y, c, chips = _place()
    sibling = (x, y, 1 - c)
    if kind == "gather_ici":
        me = 4 * x + 2 * y + c
        return [(me, me, 4 * px + 2 * py + c, (px, py, c)) for px, py in chips]
    if kind == "gather_d2d":
        return [(2 * j + c, 2 * j + c, 2 * j + 1 - c, sibling) for j in range(N_CHIP)]
    if kind == "scatter_d2d":
        return [(2 * j + 1 - c, j, j, sibling) for j in range(N_CHIP)]
    assert kind == "scatter_ici"
    return [(2 * px + py, 2 * x + y, 2 * px + py, (px, py, c)) for px, py in chips]


def split_start(kind, srcs, dsts):
    n = len(srcs)
    bufs = list(srcs) + ([] if dsts is None else list(dsts))
    nb = len(bufs)
    k = PLAN_COPIES[kind]

    def body(*refs):
        ins = refs[:n]
        outs = ins if dsts is None else refs[n:2 * n]
        send_sems, recv_sems = refs[nb], refs[nb + 1]
        token = refs[-1]
        for a in range(n):
            for j, (src, dst, _, peer) in enumerate(_plan(kind)):
                _remote(ins[a].at[src], outs[a].at[dst], send_sems, recv_sems, k * a + j, peer).start()
        token[...] = jnp.zeros_like(token)

    out = pl.pallas_call(
        body, name=kind + "_start",
        out_shape=(pltpu.SemaphoreType.DMA((k * n,)), pltpu.SemaphoreType.DMA((k * n,)),
                   *[pltpu.HBM(b.shape, b.dtype) for b in bufs], _sds(TOKEN, F32)),
        in_specs=[HBM_SPEC] * nb,
        out_specs=(SEM_SPEC, SEM_SPEC, *[HBM_SPEC] * nb, pl.BlockSpec(memory_space=pltpu.VMEM)),
        input_output_aliases={i: 2 + i for i in range(nb)},
        compiler_params=pltpu.CompilerParams(has_side_effects=DATAFLOW),
    )(*[_in_hbm(b) for b in bufs])
    return kind, out[0], out[1], list(out[2:2 + n]), None if dsts is None else list(out[2 + n:2 + nb]), out[-1]


def split_wait(pending, after):
    kind, send_sems, recv_sems, srcs, dsts, _ = pending
    n = len(srcs)
    bufs = list(srcs) + ([] if dsts is None else list(dsts))
    nb = len(bufs)
    k = PLAN_COPIES[kind]

    def body(*refs):
        ins = refs[:n]
        outs = ins if dsts is None else refs[n:2 * n]
        send_sems, recv_sems = refs[nb], refs[nb + 1]
        for a in range(n):
            for j, (src, _, landed, peer) in enumerate(_plan(kind)):
                cp = _remote(ins[a].at[src], outs[a].at[landed], send_sems, recv_sems, k * a + j, peer)
                cp.wait_send()
                cp.wait_recv()

    out = pl.pallas_call(
        body, name=kind + "_wait",
        out_shape=[pltpu.HBM(b.shape, b.dtype) for b in bufs],
        in_specs=[HBM_SPEC] * nb + [SEM_SPEC, SEM_SPEC, ANY], out_specs=[HBM_SPEC] * nb,
        input_output_aliases={i: i for i in range(nb)},
        compiler_params=pltpu.CompilerParams(has_side_effects=DATAFLOW),
    )(*bufs, send_sems, recv_sems, after)
    return list(out) if dsts is None else (list(out[:n]), list(out[n:]))


def gather_d2d(bufs):
    n = len(bufs)

    def body(*refs):
        outs = refs[n:2 * n]
        token = refs[2 * n]
        send_sems, recv_sems = refs[2 * n + 1:]
        x, y, c, _ = _place()
        sent = []
        for a in range(n):
            for j in range(N_CHIP):
                slot = outs[a].at[2 * j + c]
                cp = _remote(slot, slot, send_sems, recv_sems, N_CHIP * a + j, (x, y, 1 - c))
                cp.start()
                sent.append(cp)
        for a in range(n):
            for j in range(N_CHIP):
                slot = outs[a].at[2 * j + 1 - c]
                _remote(slot, slot, send_sems, recv_sems, N_CHIP * a + j, (x, y, 1 - c)).wait_recv()
        for cp in sent:
            cp.wait_send()
        token[...] = jnp.zeros_like(token)

    out = pl.pallas_call(
        body, name="gather_d2d", in_specs=[ANY] * n,
        out_specs=[ANY] * n + [pl.BlockSpec(memory_space=pltpu.VMEM)],
        out_shape=[_sds(b.shape, b.dtype) for b in bufs] + [_sds(TOKEN, F32)],
        input_output_aliases={a: a for a in range(n)},
        scratch_shapes=[pltpu.SemaphoreType.DMA((N_CHIP * n,)), pltpu.SemaphoreType.DMA((N_CHIP * n,))],
    )(*bufs)
    return list(out[:n]), out[n]


def pair_sum(part, got, c_arr, chip_arr):
    _, R, C = part.shape
    tr = _tile(R, 512)
    p4 = part.reshape(N_CHIP, 2, R, C)

    def body(c_ref, chip_ref, p_ref, g_ref, o_ref, own_ref):
        s = (p_ref[...].astype(F32) + g_ref[...].astype(F32)).astype(o_ref.dtype)
        o_ref[...] = s

        @pl.when(pl.program_id(1) == chip_ref[0])
        def _():
            own_ref[...] = s

    return pl.pallas_call(
        body, name="pair_sum",
        grid_spec=pltpu.PrefetchScalarGridSpec(
            num_scalar_prefetch=2, grid=(R // tr, N_CHIP),
            in_specs=[pl.BlockSpec((None, None, tr, C), lambda i, j, c_ref, chip_ref: (j, c_ref[0], i, 0)),
                      pl.BlockSpec((None, tr, C), lambda i, j, c_ref, chip_ref: (j, i, 0))],
            out_specs=[pl.BlockSpec((None, tr, C), lambda i, j, c_ref, chip_ref: (j, i, 0)),
                       pl.BlockSpec((None, tr, C), lambda i, j, c_ref, chip_ref: (chip_ref[0], i, 0))]),
        out_shape=[_sds((N_CHIP, R, C), part.dtype), _sds((N_CHIP, R, C), part.dtype)],
        compiler_params=_params(("parallel", "arbitrary")),
    )(c_arr, chip_arr, p4, got)


def adamw(w, m, v, parts_per_layer, transposed=False):
    NL, R, C = w.shape
    P = parts_per_layer[0].shape[0]
    want = max(16, ADAMW_BLOCK_ELEMS // C)
    tr = _lane_tile(R, max(want, 128)) if transposed else _tile(R, want)
    nb = R // tr

    def body(w_ref, m_ref, v_ref, *rest):
        part_refs = rest[:NL]
        g_ref, d_ref, nm_ref, nv_ref = rest[NL:]
        layer = pl.program_id(0)
        g = None
        for q in range(NL):
            s = part_refs[q][0].astype(F32)
            for t in range(1, P):
                s = s + part_refs[q][t].astype(F32)
            if transposed:
                s = s.T
            s = jnp.where(layer == q, s, 0.0)
            g = s if g is None else g + s
        wv = w_ref[...]
        nm = ADAM_B1 * m_ref[...] + (1.0 - ADAM_B1) * g
        nv = ADAM_B2 * v_ref[...] + (1.0 - ADAM_B2) * (g * g)
        m_hat = nm * (1.0 / (1.0 - ADAM_B1 ** ADAM_STEP))
        v_hat = nv * (1.0 / (1.0 - ADAM_B2 ** ADAM_STEP))
        g_ref[...] = g
        d_ref[...] = -ADAM_LR * (m_hat / (jnp.sqrt(v_hat) + ADAM_EPS) + ADAM_WD * wv)
        nm_ref[...] = nm
        nv_ref[...] = nv

    def part_spec(q):
        def block(l, i):
            return jnp.where(l == q, i, jnp.where(l < q, 0, nb - 1))

        if transposed:
            return pl.BlockSpec((P, C, tr), lambda l, i: (0, 0, block(l, i)))
        return pl.BlockSpec((P, tr, C), lambda l, i: (0, block(l, i), 0))

    blk = pl.BlockSpec((None, tr, C), lambda l, i: (l, i, 0))
    return pl.pallas_call(
        body, name="adamw", grid=(NL, nb),
        in_specs=[blk, blk, blk] + [part_spec(q) for q in range(NL)],
        out_specs=[blk, blk, blk, blk], out_shape=[_sds((NL, R, C), F32)] * 4,
        compiler_params=_params(("arbitrary", "arbitrary")),
    )(w, m, v, *parts_per_layer)


SMALL = ("norm1_g", "gmlp_ln_g", "w_spatial", "b_spatial", "mix_norm_attn_g", "mix_norm_gmlp_g", "norm2_g")
BIG = ("w_in", "w_out", "w_gate", "w_up", "w_down")
TRANSPOSED = ("w_gate", "w_up")
GATHERED_TRANSPOSED = ("w_in",)
GROUPS = (("w_in", "w_out"), ("w_gate", "w_up", "w_down"))
LANES = 128


def _pack(layers, final):
    flat = [layer[n].reshape(-1) for layer in layers for n in SMALL] + [final.reshape(-1)]
    return jnp.concatenate(flat).reshape(-1, LANES)


def _unpack(packed, like_layers, like_final):
    flat = packed.reshape(-1)
    out, off = [], 0
    for layer in like_layers:
        d = {}
        for n in SMALL:
            size = layer[n].size
            d[n] = flat[off:off + size].reshape(layer[n].shape)
            off += size
        out.append(d)
    return out, flat[off:off + like_final.size].reshape(like_final.shape)


def kernel(x, norm1_g, w_in, gmlp_ln_g, w_spatial, b_spatial, mix_norm_attn_g, mix_norm_gmlp_g, w_out, norm2_g, w_gate, w_up, w_down, final_g, loss_target, m_norm1_g, m_w_in, m_gmlp_ln_g, m_w_spatial, m_b_spatial, m_mix_norm_attn_g, m_mix_norm_gmlp_g, m_w_out, m_norm2_g, m_w_gate, m_w_up, m_w_down, m_final_g, v_norm1_g, v_w_in, v_gmlp_ln_g, v_w_spatial, v_b_spatial, v_mix_norm_attn_g, v_mix_norm_gmlp_g, v_w_out, v_norm2_g, v_w_gate, v_w_up, v_w_down, v_final_g):
    S, D = x.shape[1], x.shape[2]
    NL = norm1_g.shape[0]
    DA = D // 2
    xs = x.reshape(S, D)
    tabs = rope_tables(S)
    ax, ay, ac = lax.axis_index("x"), lax.axis_index("y"), lax.axis_index("c")
    me_arr = (4 * ax + 2 * ay + ac).astype(jnp.int32).reshape(1)
    c_arr = ac.astype(jnp.int32).reshape(1)
    chip_arr = (2 * ax + ay).astype(jnp.int32).reshape(1)
    small_w = dict(norm1_g=norm1_g, gmlp_ln_g=gmlp_ln_g, w_spatial=w_spatial, b_spatial=b_spatial,
                   mix_norm_attn_g=mix_norm_attn_g, mix_norm_gmlp_g=mix_norm_gmlp_g, norm2_g=norm2_g)
    small_m = dict(norm1_g=m_norm1_g, gmlp_ln_g=m_gmlp_ln_g, w_spatial=m_w_spatial, b_spatial=m_b_spatial,
                   mix_norm_attn_g=m_mix_norm_attn_g, mix_norm_gmlp_g=m_mix_norm_gmlp_g, norm2_g=m_norm2_g)
    small_v = dict(norm1_g=v_norm1_g, gmlp_ln_g=v_gmlp_ln_g, w_spatial=v_w_spatial, b_spatial=v_b_spatial,
                   mix_norm_attn_g=v_mix_norm_attn_g, mix_norm_gmlp_g=v_mix_norm_gmlp_g, norm2_g=v_norm2_g)
    def view(n, a):
        return jnp.swapaxes(a, 1, 2) if n in TRANSPOSED else a

    big_w = {n: view(n, a) for n, a in dict(w_in=w_in, w_out=w_out, w_gate=w_gate, w_up=w_up, w_down=w_down).items()}
    big_m = {n: view(n, a) for n, a in dict(w_in=m_w_in, w_out=m_w_out, w_gate=m_w_gate, w_up=m_w_up, w_down=m_w_down).items()}
    big_v = {n: view(n, a) for n, a in dict(w_in=v_w_in, w_out=v_w_out, w_gate=v_w_gate, w_up=v_w_up, w_down=v_w_down).items()}

    def layer_small(l):
        ws = w_spatial[l]
        return dict(norm1_g=norm1_g[l][None], gmlp_ln_g=gmlp_ln_g[l][None], mix_norm_attn_g=mix_norm_attn_g[l][None],
                    mix_norm_gmlp_g=mix_norm_gmlp_g[l][None], norm2_g=norm2_g[l][None], w_spatial=ws,
                    w_spatial_t=jnp.swapaxes(ws, 1, 2), bcol=jnp.broadcast_to(b_spatial[l][:, :, None], ws.shape))

    n_stages = 2 * NL
    zero = jnp.zeros(TOKEN, F32)
    TOK = 5

    def stage_bufs(s, dep):
        return [to_slot(big_w[n], s // 2, me_arr, dep, BF16, transpose=(n in GATHERED_TRANSPOSED)) for n in GROUPS[s % 2]]

    def stage_weights(s, bufs):
        gw = dict(zip(GROUPS[s % 2], bufs))
        if s % 2 == 0:
            gw["w_in"] = gw["w_in"].reshape(-1, D)
            gw["w_out"] = gw["w_out"].reshape(2, DA, D)
        else:
            gw = {n: b.reshape(-1, D) for n, b in gw.items()}
        return gw

    ici = {0: split_start("gather_ici", stage_bufs(0, zero), None)}
    ici[1] = split_start("gather_ici", stage_bufs(1, ici[0][TOK]), None)
    h = xs
    d2d = {0: split_start("gather_d2d", split_wait(ici[0], h), None)}
    ready = {0: split_wait(d2d[0], h)}
    saved, weights = [], []
    for s in range(n_stages):
        deps = [ici[1][TOK]] if s == 0 else []
        if 1 <= s < n_stages - 1:
            d2d[s + 1] = split_start("gather_d2d", split_wait(ici[s + 1], h), None)
            deps.append(d2d[s + 1][TOK])
        if s + 2 < n_stages:
            ici[s + 2] = split_start("gather_ici", stage_bufs(s + 2, deps[-1]), None)
            deps.append(ici[s + 2][TOK])
        dep = sum(deps[1:], deps[0]) if deps else zero
        gw = stage_weights(s, ready[s])
        sm = layer_small(s // 2)
        if s % 2 == 0:
            sm["norm1_g"] = sm["norm1_g"] + dep[0, 0]
            h, sv = mixer_fwd(h, sm, gw, tabs)
        else:
            sm["norm2_g"] = sm["norm2_g"] + dep[0, 0]
            h, sv = ffn_fwd(h, sm, gw)
        saved.append(sv)
        weights.append(gw)
        if s == 0:
            d2d[1] = split_start("gather_d2d", split_wait(ici[1], h), None)
        if s + 1 < n_stages:
            ready[s + 1] = split_wait(d2d[s + 1], h)
    loss_part, dx, dxb, d_final = loss_and_grad(h, final_g[None], loss_target.reshape(S, D))
    loss = lax.psum(loss_part[0, 0], ("x", "y", "c"))

    big_sums = [dict() for _ in range(NL)]
    small_grads = [dict() for _ in range(NL)]
    pending, pending_stage = None, None
    dep = zero
    for s in reversed(range(n_stages)):
        l = s // 2
        crossing = []

        def on_grads(big, s=s, crossing=crossing):
            parts = [big[n].reshape((N_DEV, -1, big[n].shape[-1])) for n in GROUPS[s % 2]]
            gots = [lax.empty((N_CHIP,) + p.shape[1:], p.dtype) for p in parts]
            crossing.append(split_start("scatter_d2d", parts, gots))
            return crossing[0][TOK]

        if s % 2 == 1:
            dx, dxb, small = ffn_bwd(dx, dxb, saved[s], layer_small(l), weights[s], dep, on_grads)
        else:
            dx, dxb, small = mixer_bwd(dx, dxb, saved[s], layer_small(l), weights[s], tabs, dep, on_grads)
        small_grads[l].update(small)
        if pending is not None:
            big_sums[pending_stage // 2].update(zip(GROUPS[pending_stage % 2], split_wait(pending, dx)[1]))
        parts, got = split_wait(crossing[0], dx)
        pairs = [pair_sum(p, g, c_arr, chip_arr) for p, g in zip(parts, got)]
        pending, pending_stage = split_start("scatter_ici", [t for t, _ in pairs], [o for _, o in pairs]), s
        dep = pending[TOK]

    def update_big(n):
        res = adamw(big_w[n], big_m[n], big_v[n], [big_sums[l][n] for l in range(NL)], n in GATHERED_TRANSPOSED)
        return [view(n, t) for t in res], res[0][0, :TOKEN[0], :TOKEN[1]]

    big_out = {}
    after_ffn = zero
    for n in GROUPS[1]:
        big_out[n], piece = update_big(n)
        after_ffn = after_ffn + piece

    packed = to_slot(_pack(small_grads, d_final), None, me_arr, dep + after_ffn, F32)
    small_pending = split_start("gather_ici", [packed], None)
    big_sums[pending_stage // 2].update(zip(GROUPS[pending_stage % 2], split_wait(pending, small_pending[TOK])[1]))
    after_mixer = zero
    for n in GROUPS[0]:
        big_out[n], piece = update_big(n)
        after_mixer = after_mixer + piece
    gathered_small = gather_d2d(split_wait(small_pending, after_mixer))[0][0]
    pw = _pack([{n: small_w[n][l] for n in SMALL} for l in range(NL)], final_g)[None]
    pm = _pack([{n: small_m[n][l] for n in SMALL} for l in range(NL)], m_final_g)[None]
    pv = _pack([{n: small_v[n][l] for n in SMALL} for l in range(NL)], v_final_g)[None]
    like_layers = [{n: small_w[n][l] for n in SMALL} for l in range(NL)]
    small_res = adamw(pw, pm, pv, [gathered_small])
    small_out = [_unpack(t[0], like_layers, final_g) for t in small_res]

    def small_stack(k, n):
        return jnp.stack([small_out[k][0][l][n] for l in range(NL)])

    order = ("norm1_g", "w_in", "gmlp_ln_g", "w_spatial", "b_spatial", "mix_norm_attn_g", "mix_norm_gmlp_g", "w_out",
             "norm2_g", "w_gate", "w_up", "w_down")
    outs = [loss, dx.reshape(x.shape)]
    for k in range(4):
        for n in order:
            outs.append(big_out[n][k] if n in BIG else small_stack(k, n))
        outs.append(small_out[k][1])
    return tuple(outs)
```
